```python
import math
import jax, jax.numpy as jnp
from jax import lax
import numpy as np

D_MODEL = 1024
BATCH = 2
SEQ = 16384
DEPTH = 1
DEC_BATCH = 16
DEC_SEQ = 16
PAST_LEN = 2048

CHUNK = 64
HEAD_DIM = 64
FOX_HEADS = 8
FOX_WIDTH = FOX_HEADS * HEAD_DIM
RWKV_HEADS = 8
RWKV_WIDTH = RWKV_HEADS * HEAD_DIM
MIX_WIDTH = FOX_WIDTH + RWKV_WIDTH
DECAY_LORA = 32
AAA_LORA = 32
GATE_LORA = 96
FOX_COLS = 3 * FOX_WIDTH + FOX_HEADS
RWKV_COLS = 3 * RWKV_WIDTH + DECAY_LORA + AAA_LORA + GATE_LORA
IN_COLS = FOX_COLS + RWKV_COLS
FOX_Q_BLOCK = 128
FOX_BF_INIT = 3.0
N_KEYS = 128
N_EXPERTS = N_KEYS * N_KEYS
PEER_HEADS = 8
PEER_TOPK = 16
PEER_QDIM = 128
PEER_BLOCK = 256
NORM_EPS = 1e-6
LNX_EPS = 64e-5

kernel_name = 'hymba_fox_rwkv7_peer_stream_step'

F32 = jnp.float32


def rms_norm(x, g):
    xf = x.astype(F32)
    y = xf * lax.rsqrt(jnp.mean(xf * xf, axis=-1, keepdims=True) + NORM_EPS)
    return (y * g.astype(F32)).astype(x.dtype)


def fox_attend(q, k, v, logf, q_off):
    B, T, H, dh = q.shape
    L = k.shape[1]
    C = jnp.cumsum(logf.astype(F32), axis=1)
    C_bhk = jnp.transpose(C, (0, 2, 1))
    kf = k.astype(F32)
    vf = v.astype(F32)
    qb = min(FOX_Q_BLOCK, T)
    nb = T // qb
    qs = jnp.transpose(q.reshape(B, nb, qb, H, dh), (1, 0, 2, 3, 4))
    key_pos = jnp.arange(L)
    scale = 1.0 / math.sqrt(dh)

    def one_block(args):
        qblk, bi = args
        start = q_off + bi * qb
        q_pos = start + jnp.arange(qb)
        Cq = jnp.transpose(lax.dynamic_slice_in_dim(C, start, qb, axis=1), (0, 2, 1))
        s = jnp.einsum('bqhd,bkhd->bhqk', qblk.astype(F32), kf) * scale
        s = s + Cq[..., :, None] - C_bhk[:, :, None, :]
        s = jnp.where(key_pos[None, None, None, :] <= q_pos[None, None, :, None], s, -jnp.inf)
        p = jax.nn.softmax(s, axis=-1)
        return jnp.einsum('bhqk,bkhd->bqhd', p, vf)

    out = lax.map(one_block, (qs, jnp.arange(nb)))
    return jnp.transpose(out, (1, 0, 2, 3, 4)).reshape(B, T, H, dh).astype(q.dtype)


def rwkv_time_mix(p, shift0, S0, mu, w0, w2, a0, a2, g2, k_k, k_a, r_k, lnx_w, lnx_b):
    B, T, _ = p.shape
    W = RWKV_WIDTH
    prev = jnp.concatenate([shift0.astype(p.dtype), p[:, :-1]], axis=1)
    ps = p + mu * (prev - p)
    r, k, v = ps[..., :W], ps[..., W:2 * W], ps[..., 2 * W:3 * W]
    o = 3 * W
    wl = ps[..., o:o + DECAY_LORA]
    al = ps[..., o + DECAY_LORA:o + DECAY_LORA + AAA_LORA]
    gl = ps[..., o + DECAY_LORA + AAA_LORA:]
    w_log = -jax.nn.softplus(-(w0 + jnp.tanh(wl) @ w2)) - 0.5
    decay = jnp.exp(-jnp.exp(w_log.astype(F32)))
    a = jax.nn.sigmoid(a0 + al @ a2)
    g = jax.nn.sigmoid(gl) @ g2

    def heads(z):
        return z.reshape(B, T, RWKV_HEADS, HEAD_DIM).astype(F32)

    kk = heads(k * k_k)
    kk = kk / jnp.maximum(jnp.sqrt(jnp.sum(kk * kk, axis=-1, keepdims=True)), 1e-12)
    k_mod = heads(k * (1 + (a - 1) * k_a))
    r_h, v_h, a_h, w_h = heads(r), heads(v), heads(a), heads(decay)
    seq = tuple(jnp.moveaxis(z, 1, 0) for z in (r_h, w_h, k_mod, v_h, kk, a_h))

    def step(S, inp):
        r_t, w_t, k_t, v_t, kk_t, a_t = inp
        Skk = jnp.einsum('bhij,bhj->bhi', S, -kk_t)
        S = (S * w_t[:, :, None, :]
             + jnp.einsum('bhi,bhj->bhij', Skk, kk_t * a_t)
             + jnp.einsum('bhi,bhj->bhij', v_t, k_t))
        return S, jnp.einsum('bhij,bhj->bhi', S, r_t)

    S_T, y = lax.scan(step, S0.astype(F32), seq)
    y = jnp.moveaxis(y, 0, 1)
    mean = jnp.mean(y, axis=-1, keepdims=True)
    var = jnp.mean(jnp.square(y - mean), axis=-1, keepdims=True)
    yn = ((y - mean) * lax.rsqrt(var + LNX_EPS)).reshape(B, T, W) * lnx_w.astype(F32) + lnx_b.astype(F32)
    bonus = (jnp.sum(r_h * k_mod * r_k.astype(F32), axis=-1, keepdims=True) * v_h).reshape(B, T, W)
    out = ((yn + bonus) * g.astype(F32)).astype(p.dtype)
    return out, S_T.astype(S0.dtype), p[:, -1:]


def peer_ffn(x, w_q, sub_keys, expert_u, expert_v):
    T, D = x.shape
    nb = -(-T // PEER_BLOCK)
    xp = jnp.pad(x, ((0, nb * PEER_BLOCK - T), (0, 0))).reshape(nb, PEER_BLOCK, D)

    def block(xb):
        q = (xb @ w_q).reshape(PEER_BLOCK, PEER_HEADS, 2, PEER_QDIM // 2).astype(F32)
        s = jnp.einsum('thcd,hcnd->thcn', q, sub_keys.astype(F32))
        s1, i1 = lax.top_k(s[:, :, 0], PEER_TOPK)
        s2, i2 = lax.top_k(s[:, :, 1], PEER_TOPK)
        cand = (s1[..., :, None] + s2[..., None, :]).reshape(PEER_BLOCK, PEER_HEADS, PEER_TOPK * PEER_TOPK)
        cidx = (i1[..., :, None] * N_KEYS + i2[..., None, :]).reshape(PEER_BLOCK, PEER_HEADS, PEER_TOPK * PEER_TOPK)
        top, pos = lax.top_k(cand, PEER_TOPK)
        eidx = jnp.take_along_axis(cidx, pos, axis=-1)
        gate = jax.nn.softmax(top, axis=-1)
        u = expert_u[eidx]
        act = jax.nn.gelu(jnp.einsum('thkd,td->thk', u, xb).astype(F32), approximate=False)
        return jnp.einsum('thk,thkd->td', (gate * act).astype(xb.dtype), expert_v[eidx])

    y = lax.map(block, xp).reshape(nb * PEER_BLOCK, D)
    return y[:T]


def layer_step(x, k_past, v_past, lf_past, S0, shift0,
               norm_mix_g, w_in, fox_b_f, rwkv_mu, rwkv_w0, rwkv_w2, rwkv_a0, rwkv_a2,
               rwkv_g2, rwkv_k_k, rwkv_k_a, rwkv_r_k, rwkv_lnx_w, rwkv_lnx_b, w_out,
               norm_ffn_g, peer_w_q, peer_sub_keys, peer_u, peer_v):
    B, T, D = x.shape
    h = rms_norm(x, norm_mix_g)
    proj = h @ w_in
    FW = FOX_WIDTH
    q = proj[..., :FW].reshape(B, T, FOX_HEADS, HEAD_DIM)
    k = proj[..., FW:2 * FW].reshape(B, T, FOX_HEADS, HEAD_DIM)
    v = proj[..., 2 * FW:3 * FW].reshape(B, T, FOX_HEADS, HEAD_DIM)
    logf = jax.nn.log_sigmoid((proj[..., 3 * FW:FOX_COLS] + fox_b_f).astype(F32))
    fox_out = fox_attend(q,
                         jnp.concatenate([k_past.astype(k.dtype), k], axis=1),
                         jnp.concatenate([v_past.astype(v.dtype), v], axis=1),
                         jnp.concatenate([lf_past.astype(F32), logf], axis=1),
                         k_past.shape[1])
    rw_out, S_T, shift_T = rwkv_time_mix(proj[..., FOX_COLS:], shift0, S0, rwkv_mu, rwkv_w0,
                                         rwkv_w2, rwkv_a0, rwkv_a2, rwkv_g2, rwkv_k_k,
                                         rwkv_k_a, rwkv_r_k, rwkv_lnx_w, rwkv_lnx_b)
    mix = jnp.concatenate([fox_out.reshape(B, T, FW), rw_out], axis=-1) @ w_out
    x = x + mix
    ff = peer_ffn(rms_norm(x, norm_ffn_g).reshape(B * T, D), peer_w_q, peer_sub_keys, peer_u, peer_v)
    x = x + ff.reshape(B, T, D)
    return x, k, v, logf.astype(lf_past.dtype), S_T, shift_T


def setup_inputs(seed: int = 0) -> dict:
    key = jax.random.key(seed)
    ks = jax.random.split(key, 32)

    def nrm(k, shape, scale):
        return jax.random.normal(k, shape, F32) * scale

    L = DEPTH
    return {
        'x_prompt': nrm(ks[0], (BATCH, SEQ, D_MODEL), 1.0),
        'x_sample': nrm(ks[1], (DEC_BATCH, DEC_SEQ, D_MODEL), 1.0),
        'cache_fox_k': nrm(ks[2], (L, DEC_BATCH, PAST_LEN, FOX_HEADS, HEAD_DIM), 1.0),
        'cache_fox_v': nrm(ks[3], (L, DEC_BATCH, PAST_LEN, FOX_HEADS, HEAD_DIM), 1.0),
        'cache_fox_logf': jax.nn.log_sigmoid(FOX_BF_INIT + nrm(ks[4], (L, DEC_BATCH, PAST_LEN, FOX_HEADS), 1.0)),
        'state_rwkv': nrm(ks[5], (L, DEC_BATCH, RWKV_HEADS, HEAD_DIM, HEAD_DIM), 0.1),
        'state_shift': nrm(ks[6], (L, DEC_BATCH, 1, RWKV_COLS), 1.0),
        'norm_mix_g': 1.0 + nrm(ks[7], (L, D_MODEL), 0.01),
        'w_in': nrm(ks[8], (L, D_MODEL, IN_COLS), D_MODEL ** -0.5),
        'fox_b_f': FOX_BF_INIT + nrm(ks[9], (L, FOX_HEADS), 0.1),
        'rwkv_mu': jax.random.uniform(ks[10], (L, RWKV_COLS), F32),
        'rwkv_w0': -2.0 + nrm(ks[11], (L, RWKV_WIDTH), 0.5),
        'rwkv_w2': nrm(ks[12], (L, DECAY_LORA, RWKV_WIDTH), 0.1),
        'rwkv_a0': nrm(ks[13], (L, RWKV_WIDTH), 0.1),
        'rwkv_a2': nrm(ks[14], (L, AAA_LORA, RWKV_WIDTH), 0.1),
        'rwkv_g2': nrm(ks[15], (L, GATE_LORA, RWKV_WIDTH), GATE_LORA ** -0.5),
        'rwkv_k_k': 0.85 + nrm(ks[16], (L, RWKV_WIDTH), 0.02),
        'rwkv_k_a': 1.0 + nrm(ks[17], (L, RWKV_WIDTH), 0.02),
        'rwkv_r_k': nrm(ks[18], (L, RWKV_HEADS, HEAD_DIM), 0.1),
        'rwkv_lnx_w': 1.0 + nrm(ks[19], (L, RWKV_WIDTH), 0.01),
        'rwkv_lnx_b': nrm(ks[20], (L, RWKV_WIDTH), 0.01),
        'w_out': nrm(ks[21], (L, MIX_WIDTH, D_MODEL), MIX_WIDTH ** -0.5),
        'norm_ffn_g': 1.0 + nrm(ks[22], (L, D_MODEL), 0.01),
        'peer_w_q': nrm(ks[23], (L, D_MODEL, PEER_HEADS * PEER_QDIM), D_MODEL ** -0.5),
        'peer_sub_keys': nrm(ks[24], (L, PEER_HEADS, 2, N_KEYS, PEER_QDIM // 2), (PEER_QDIM // 2) ** -0.5),
        'peer_u': nrm(ks[25], (L, N_EXPERTS, D_MODEL), D_MODEL ** -0.5),
        'peer_v': nrm(ks[26], (L, N_EXPERTS, D_MODEL), 0.1),
        'norm_final_g': 1.0 + nrm(ks[27], (D_MODEL,), 0.01),
    }


def reference(x_prompt, x_sample, cache_fox_k, cache_fox_v, cache_fox_logf, state_rwkv, state_shift,
              norm_mix_g, w_in, fox_b_f, rwkv_mu, rwkv_w0, rwkv_w2, rwkv_a0, rwkv_a2, rwkv_g2,
              rwkv_k_k, rwkv_k_a, rwkv_r_k, rwkv_lnx_w, rwkv_lnx_b, w_out, norm_ffn_g,
              peer_w_q, peer_sub_keys, peer_u, peer_v, norm_final_g):
    yp, ys = x_prompt, x_sample
    Bp = x_prompt.shape[0]
    dt = x_prompt.dtype
    kp_l, vp_l, lfp_l, Sp_l, shp_l = [], [], [], [], []
    ks_l, vs_l, lfs_l, Ss_l, shs_l = [], [], [], [], []
    for l in range(DEPTH):
        lp = (norm_mix_g[l], w_in[l], fox_b_f[l], rwkv_mu[l], rwkv_w0[l], rwkv_w2[l], rwkv_a0[l],
              rwkv_a2[l], rwkv_g2[l], rwkv_k_k[l], rwkv_k_a[l], rwkv_r_k[l], rwkv_lnx_w[l],
              rwkv_lnx_b[l], w_out[l], norm_ffn_g[l], peer_w_q[l], peer_sub_keys[l], peer_u[l], peer_v[l])
        empty_kv = jnp.zeros((Bp, 0, FOX_HEADS, HEAD_DIM), dt)
        empty_lf = jnp.zeros((Bp, 0, FOX_HEADS), dt)
        S_zero = jnp.zeros((Bp, RWKV_HEADS, HEAD_DIM, HEAD_DIM), dt)
        sh_zero = jnp.zeros((Bp, 1, RWKV_COLS), dt)
        yp, kp, vp, lfp, Sp, shp = layer_step(yp, empty_kv, empty_kv, empty_lf, S_zero, sh_zero, *lp)
        ys, kss, vss, lfs, Ss, shs = layer_step(ys, cache_fox_k[l], cache_fox_v[l], cache_fox_logf[l],
                                                state_rwkv[l], state_shift[l], *lp)
        kp_l.append(kp); vp_l.append(vp); lfp_l.append(lfp); Sp_l.append(Sp); shp_l.append(shp)
        ks_l.append(kss); vs_l.append(vss); lfs_l.append(lfs); Ss_l.append(Ss); shs_l.append(shs)
    y_prompt = rms_norm(yp, norm_final_g)
    y_sample = rms_norm(ys, norm_final_g)
    return (y_prompt, y_sample,
            jnp.stack(kp_l), jnp.stack(vp_l), jnp.stack(lfp_l), jnp.stack(Sp_l), jnp.stack(shp_l),
            jnp.stack(ks_l), jnp.stack(vs_l), jnp.stack(lfs_l), jnp.stack(Ss_l), jnp.stack(shs_l))
```

```python
import functools
import math

import jax
import jax.numpy as jnp
from jax import lax
from jax.experimental import pallas as pl
from jax.experimental.pallas import tpu as pltpu

F32 = jnp.float32
BF16 = jnp.bfloat16

HEAD_DIM = 64
LANES = 128
HEADS_PER_LANE_TILE = LANES // HEAD_DIM
TAIL_PAD = 2 * LANES
CHUNK = 64
NORM_EPS = 1e-6
LNX_EPS = 64e-5
NEG_BIG = -1e30
HIGHEST = lax.Precision.HIGHEST
VMEM_LIMIT = 48 * 1024 * 1024


def _cparams(sem):
    return pltpu.CompilerParams(dimension_semantics=sem, vmem_limit_bytes=VMEM_LIMIT)


def _row_tile(n, target):
    t = min(n, target)
    assert n % t == 0, (n, t)
    return t


def _inproj_kernel(x_ref, g_ref, wqkv_ref, wf_ref, wrw_ref, wtail_ref, bf_ref,
                   q_ref, k_ref, v_ref, kb_ref, vb_ref, lf_ref, rw_ref, tail_ref):
    x = x_ref[...]
    h = x * lax.rsqrt(jnp.mean(x * x, axis=-1, keepdims=True) + NORM_EPS) * g_ref[...]
    hb = h.astype(BF16)
    fw = wqkv_ref.shape[1] // 3
    qkv = jnp.dot(hb, wqkv_ref[...], preferred_element_type=F32)
    q_ref[...] = (qkv[:, :fw] * (1.0 / math.sqrt(HEAD_DIM))).astype(BF16)
    k = qkv[:, fw:2 * fw]
    v = qkv[:, 2 * fw:]
    k_ref[...] = k
    v_ref[...] = v
    kb_ref[...] = k.astype(BF16)
    vb_ref[...] = v.astype(BF16)
    f = jnp.dot(hb, wf_ref[...], preferred_element_type=F32) + bf_ref[...]
    lf_ref[...] = jax.nn.log_sigmoid(f)
    rw_ref[...] = jnp.dot(hb, wrw_ref[...], preferred_element_type=F32)
    tail_ref[...] = jnp.dot(hb, wtail_ref[...], preferred_element_type=F32)


def _inproj(x2d, g, w_in, b_f, fox_cols, fox_heads, rw_main):
    n, d = x2d.shape
    fw = fox_heads * HEAD_DIM
    wqkv = w_in[:, :3 * fw].astype(BF16)
    wf = w_in[:, 3 * fw:fox_cols].astype(BF16)
    wrw = w_in[:, fox_cols:fox_cols + rw_main].astype(BF16)
    wtail = w_in[:, fox_cols + rw_main:].astype(BF16)
    wtail = jnp.pad(wtail, ((0, 0), (0, TAIL_PAD - wtail.shape[1])))
    tm = _row_tile(n, 512)
    row = lambda c: pl.BlockSpec((tm, c), lambda i: (i, 0))
    full = lambda a: pl.BlockSpec(a.shape, lambda i: (0,) * a.ndim)
    g2 = g.reshape(1, d)
    bf2 = b_f.reshape(1, fox_heads)
    outs = (
        jax.ShapeDtypeStruct((n, fw), BF16),
        jax.ShapeDtypeStruct((n, fw), F32),
        jax.ShapeDtypeStruct((n, fw), F32),
        jax.ShapeDtypeStruct((n, fw), BF16),
        jax.ShapeDtypeStruct((n, fw), BF16),
        jax.ShapeDtypeStruct((n, fox_heads), F32),
        jax.ShapeDtypeStruct((n, rw_main), F32),
        jax.ShapeDtypeStruct((n, TAIL_PAD), F32),
    )
    return pl.pallas_call(
        _inproj_kernel,
        grid=(n // tm,),
        in_specs=[row(d), full(g2), full(wqkv), full(wf), full(wrw), full(wtail), full(bf2)],
        out_specs=[row(fw), row(fw), row(fw), row(fw), row(fw), row(fox_heads), row(rw_main),
                   row(TAIL_PAD)],
        out_shape=outs,
        compiler_params=_cparams(("parallel",)),
        name="inproj",
    )(x2d, g2, wqkv, wf, wrw, wtail, bf2)


def _cumsum_kernel(lf_ref, c_ref, ct_ref, carry_col, carry_row):
    @pl.when(pl.program_id(1) == 0)
    def _():
        carry_col[...] = jnp.zeros_like(carry_col)
        carry_row[...] = jnp.zeros_like(carry_row)

    lf = lf_ref[...]
    tc, nh = lf.shape
    r = lax.broadcasted_iota(jnp.int32, (tc, tc), 0)
    c = lax.broadcasted_iota(jnp.int32, (tc, tc), 1)
    lower = (c <= r).astype(F32)
    upper = (r <= c).astype(F32)
    eye = (lax.broadcasted_iota(jnp.int32, (nh, nh), 0)
           == lax.broadcasted_iota(jnp.int32, (nh, nh), 1)).astype(F32)
    cc = jnp.dot(lower, lf, precision=HIGHEST, preferred_element_type=F32) + carry_col[...]
    lft = lax.dot_general(eye, lf, (((1,), (1,)), ((), ())), precision=HIGHEST,
                          preferred_element_type=F32)
    cr = jnp.dot(lft, upper, precision=HIGHEST, preferred_element_type=F32) + carry_row[...]
    c_ref[...] = cc
    ct_ref[...] = cr
    carry_col[...] = cc[tc - 1:tc, :]
    carry_row[...] = cr[:, tc - 1:tc]


def _cumsum(lf, tc):
    b, l, nh = lf.shape
    assert l % tc == 0
    return pl.pallas_call(
        _cumsum_kernel,
        grid=(b, l // tc),
        in_specs=[pl.BlockSpec((None, tc, nh), lambda i, j: (i, j, 0))],
        out_specs=[pl.BlockSpec((None, tc, nh), lambda i, j: (i, j, 0)),
                   pl.BlockSpec((None, nh, tc), lambda i, j: (i, 0, j))],
        out_shape=(jax.ShapeDtypeStruct((b, l, nh), F32), jax.ShapeDtypeStruct((b, nh, l), F32)),
        scratch_shapes=[pltpu.VMEM((1, nh), F32), pltpu.VMEM((nh, 1), F32)],
        compiler_params=_cparams(("parallel", "arbitrary")),
        name="cumsum_logf",
    )(lf)


def _fox_kernel(q_ref, k_ref, v_ref, cq_ref, ck_ref, o_ref, m_sc, l_sc, acc_sc, cq_sc, *, q_off, tq, tk):
    i = pl.program_id(2)
    j = pl.program_id(3)
    p = pl.program_id(1)
    nk = pl.num_programs(3)

    @pl.when(j == 0)
    def _():
        m_sc[...] = jnp.full_like(m_sc, NEG_BIG)
        l_sc[...] = jnp.zeros_like(l_sc)
        acc_sc[...] = jnp.zeros_like(acc_sc)
        c_all = cq_ref[...]
        col = lax.broadcasted_iota(jnp.int32, c_all.shape, 1)
        for hh in range(HEADS_PER_LANE_TILE):
            h = p * HEADS_PER_LANE_TILE + hh
            cq_sc[hh] = jnp.sum(jnp.where(col == h, c_all, 0.0), axis=-1, keepdims=True)

    q_lo = q_off + i * tq
    k_lo = j * tk

    def body(masked):
        q = q_ref[...]
        k = k_ref[...]
        v = v_ref[...]
        lane = lax.broadcasted_iota(jnp.int32, (1, LANES), 1)
        if masked:
            qpos = q_lo + lax.broadcasted_iota(jnp.int32, (tq, tk), 0)
            kpos = k_lo + lax.broadcasted_iota(jnp.int32, (tq, tk), 1)
            visible = kpos <= qpos
        for hh in range(HEADS_PER_LANE_TILE):
            in_head = (lane >= hh * HEAD_DIM) & (lane < (hh + 1) * HEAD_DIM)
            qh = jnp.where(in_head, q, jnp.zeros_like(q))
            s = lax.dot_general(qh, k, (((1,), (1,)), ((), ())), preferred_element_type=F32)
            h = p * HEADS_PER_LANE_TILE + hh
            cq = cq_sc[hh]
            ck = ck_ref[pl.ds(h, 1), :]
            s = s + cq - ck
            if masked:
                s = jnp.where(visible, s, NEG_BIG)
            m_old = m_sc[hh]
            m_new = jnp.maximum(m_old, jnp.max(s, axis=-1, keepdims=True))
            alpha = jnp.exp(m_old - m_new)
            pr = jnp.exp(s - m_new)
            l_sc[hh] = alpha * l_sc[hh] + jnp.sum(pr, axis=-1, keepdims=True)
            pv = jnp.dot(pr.astype(BF16), v, preferred_element_type=F32)
            acc_sc[hh] = alpha * acc_sc[hh] + pv
            m_sc[hh] = m_new

    fully_visible = k_lo + tk - 1 <= q_lo
    any_visible = k_lo <= q_lo + tq - 1

    @pl.when(fully_visible)
    def _():
        body(False)

    @pl.when(jnp.logical_and(any_visible, jnp.logical_not(fully_visible)))
    def _():
        body(True)

    @pl.when(j == nk - 1)
    def _():
        lane = lax.broadcasted_iota(jnp.int32, (1, LANES), 1)
        out = jnp.zeros((tq, LANES), F32)
        for hh in range(HEADS_PER_LANE_TILE):
            in_head = (lane >= hh * HEAD_DIM) & (lane < (hh + 1) * HEAD_DIM)
            out = jnp.where(in_head, acc_sc[hh] / l_sc[hh], out)
        o_ref[...] = out


def _fox_attend(q, k, v, c_q, ct_k, *, batch, q_len, kv_len, q_off, tq, tk):
    n, w = q.shape
    npair = w // LANES
    nh = c_q.shape[1]
    nq, nk = q_len // tq, kv_len // tk
    assert q_len % tq == 0 and kv_len % tk == 0

    def last_kv(i):
        return (q_off + (i + 1) * tq - 1) // tk

    kv_map = lambda b, p, i, j: (b * nk + jnp.minimum(j, last_kv(i)), p)
    kern = functools.partial(_fox_kernel, q_off=q_off, tq=tq, tk=tk)
    return pl.pallas_call(
        kern,
        grid=(batch, npair, nq, nk),
        in_specs=[
            pl.BlockSpec((tq, LANES), lambda b, p, i, j: (b * nq + i, p)),
            pl.BlockSpec((tk, LANES), kv_map),
            pl.BlockSpec((tk, LANES), kv_map),
            pl.BlockSpec((tq, nh), lambda b, p, i, j: (b * nq + i, 0)),
            pl.BlockSpec((None, nh, tk), lambda b, p, i, j: (b, 0, jnp.minimum(j, last_kv(i)))),
        ],
        out_specs=pl.BlockSpec((tq, LANES), lambda b, p, i, j: (b * nq + i, p)),
        out_shape=jax.ShapeDtypeStruct((n, w), F32),
        scratch_shapes=[pltpu.VMEM((HEADS_PER_LANE_TILE, tq, 1), F32),
                        pltpu.VMEM((HEADS_PER_LANE_TILE, tq, 1), F32),
                        pltpu.VMEM((HEADS_PER_LANE_TILE, tq, LANES), F32),
                        pltpu.VMEM((HEADS_PER_LANE_TILE, tq, 1), F32)],
        compiler_params=_cparams(("parallel", "parallel", "parallel", "arbitrary")),
        name="fox_attention",
    )(q, k, v, c_q, ct_k)


def _fox_stream(qb, kb, vb, lf, k_past, v_past, lf_past, *, batch, q_len):
    n, w = qb.shape
    nh = lf.shape[1]
    past = k_past.shape[1]
    lf_new = lf.reshape(batch, q_len, nh)
    if past == 0:
        kv_len = q_len
        k_all, v_all, lf_all = kb, vb, lf_new
        tq = tk = _row_tile(q_len, 512)
        tc = tk
    else:
        kv_len = -(-(past + q_len) // LANES) * LANES
        pad = kv_len - past - q_len

        def cat(old, new):
            old = old.reshape(batch, past, -1).astype(new.dtype)
            new = new.reshape(batch, q_len, -1)
            z = jnp.zeros((batch, pad, new.shape[-1]), new.dtype)
            return jnp.concatenate([old, new, z], axis=1)

        k_all = cat(k_past, kb).reshape(batch * kv_len, w)
        v_all = cat(v_past, vb).reshape(batch * kv_len, w)
        lf_all = cat(lf_past, lf_new)
        tq, tk, tc = q_len, kv_len, LANES
    c, ct = _cumsum(lf_all, tc)
    c_q = c[:, past:past + q_len].reshape(n, nh)
    return _fox_attend(qb, k_all, v_all, c_q, ct, batch=batch, q_len=q_len, kv_len=kv_len,
                       q_off=past, tq=tq, tk=tk)


def _head_sum_matrix(width):
    r = lax.broadcasted_iota(jnp.int32, (width, width), 0) // HEAD_DIM
    c = lax.broadcasted_iota(jnp.int32, (width, width), 1) // HEAD_DIM
    return (r == c).astype(F32)


def _rwkv_pre_kernel(pm_ref, pt_ref, sm_ref, st_ref, mum_ref, mut_ref, wbig_ref, w0_ref, a0_ref,
                     kk_ref, ka_ref, rk_ref,
                     r_out, lw_out, km_out, v_out, kn_out, b_out, g_out, bonus_out,
                     carry_m, carry_t, *, lora_w, lora_a):
    @pl.when(pl.program_id(1) == 0)
    def _():
        carry_m[...] = sm_ref[...]
        carry_t[...] = st_ref[...]

    pm = pm_ref[...]
    pt = pt_ref[...]
    tm = pm.shape[0]
    w = pm.shape[1] // 3

    def shifted(p, carry):
        row = lax.broadcasted_iota(jnp.int32, p.shape, 0)
        return jnp.where(row == 0, carry[...], pltpu.roll(p, 1, 0))

    prev_m = shifted(pm, carry_m)
    prev_t = shifted(pt, carry_t)
    carry_m[...] = pm[tm - 1:tm, :]
    carry_t[...] = pt[tm - 1:tm, :]
    psm = pm + mum_ref[...] * (prev_m - pm)
    pst = pt + mut_ref[...] * (prev_t - pt)
    r = psm[:, :w]
    k = psm[:, w:2 * w]
    v = psm[:, 2 * w:]
    lane = lax.broadcasted_iota(jnp.int32, pst.shape, 1)
    z = jnp.where(lane < lora_w, jnp.tanh(pst),
                  jnp.where(lane < lora_w + lora_a, pst, jax.nn.sigmoid(pst)))
    lo = jnp.dot(z.astype(BF16), wbig_ref[...], preferred_element_type=F32)
    w_log = -jax.nn.softplus(-(w0_ref[...] + lo[:, :w])) - 0.5
    lw = -jnp.exp(w_log)
    a = jax.nn.sigmoid(a0_ref[...] + lo[:, w:2 * w])
    g = lo[:, 2 * w:]
    e = _head_sum_matrix(w)
    kk0 = k * kk_ref[...]
    n2 = jnp.dot(kk0 * kk0, e, precision=HIGHEST, preferred_element_type=F32)
    kn = kk0 / jnp.maximum(jnp.sqrt(n2), 1e-12)
    km = k * (1.0 + (a - 1.0) * ka_ref[...])
    rk = jnp.dot(r * km * rk_ref[...], e, precision=HIGHEST, preferred_element_type=F32)
    r_out[...] = r
    lw_out[...] = lw
    km_out[...] = km
    v_out[...] = v
    kn_out[...] = kn
    b_out[...] = kn * a
    g_out[...] = g
    bonus_out[...] = rk * v


def _rwkv_params(mu, w0, w2, a0, a2, g2, k_k, k_a, r_k):
    w = w0.shape[0]
    lora_w, lora_a, lora_g = w2.shape[0], a2.shape[0], g2.shape[0]
    w_lora = jnp.zeros((TAIL_PAD, 3 * w), F32)
    w_lora = w_lora.at[:lora_w, :w].set(w2)
    w_lora = w_lora.at[lora_w:lora_w + lora_a, w:2 * w].set(a2)
    w_lora = w_lora.at[lora_w + lora_a:lora_w + lora_a + lora_g, 2 * w:].set(g2)
    tail = mu.shape[0] - 3 * w
    return dict(
        mu_main=mu[:3 * w].reshape(1, 3 * w),
        mu_tail=jnp.pad(mu[3 * w:], (0, TAIL_PAD - tail)).reshape(1, TAIL_PAD),
        w_lora=w_lora.astype(BF16), w0=w0.reshape(1, w), a0=a0.reshape(1, w),
        k_k=k_k.reshape(1, w), k_a=k_a.reshape(1, w), r_k=r_k.reshape(1, w),
        lora_w=lora_w, lora_a=lora_a, tail=tail)


def _rwkv_pre(rw_main, rw_tail, shift_main, shift_tail, prm, *, batch, seq):
    n, w3 = rw_main.shape
    w = w3 // 3
    tm = _row_tile(seq, 512)
    nt = seq // tm
    row = lambda c: pl.BlockSpec((tm, c), lambda b, i: (b * nt + i, 0))
    per_b = lambda c: pl.BlockSpec((None, 1, c), lambda b, i: (b, 0, 0))
    full = lambda a: pl.BlockSpec(a.shape, lambda b, i: (0,) * a.ndim)
    consts = [prm["mu_main"], prm["mu_tail"], prm["w_lora"], prm["w0"], prm["a0"], prm["k_k"],
              prm["k_a"], prm["r_k"]]
    kern = functools.partial(_rwkv_pre_kernel, lora_w=prm["lora_w"], lora_a=prm["lora_a"])
    return pl.pallas_call(
        kern,
        grid=(batch, nt),
        in_specs=[row(w3), row(TAIL_PAD), per_b(w3), per_b(TAIL_PAD)] + [full(c) for c in consts],
        out_specs=[row(w)] * 8,
        out_shape=[jax.ShapeDtypeStruct((n, w), F32)] * 8,
        scratch_shapes=[pltpu.VMEM((1, w3), F32), pltpu.VMEM((1, TAIL_PAD), F32)],
        compiler_params=_cparams(("parallel", "arbitrary")),
        name="rwkv_pre",
    )(rw_main, rw_tail, shift_main, shift_tail, *consts)


def _rwkv_chunk(r, lw, km, v, kn, bb, s_blk, prec):
    c = r.shape[0]
    c2 = HEADS_PER_LANE_TILE * c
    dot = functools.partial(jnp.dot, precision=prec, preferred_element_type=F32)
    dot_nt = lambda a, b: lax.dot_general(a, b, (((1,), (1,)), ((), ())), precision=prec,
                                          preferred_element_type=F32)
    dot_tn = lambda a, b: lax.dot_general(a, b, (((0,), (0,)), ((), ())), precision=prec,
                                          preferred_element_type=F32)
    ti = lax.broadcasted_iota(jnp.int32, (c, c), 0)
    si = lax.broadcasted_iota(jnp.int32, (c, c), 1)
    cs = jnp.dot((si <= ti).astype(F32), lw, precision=HIGHEST, preferred_element_type=F32)
    e_pos = jnp.exp(cs)
    e_neg = jnp.exp(-cs)
    kt = kn * jnp.exp(cs - lw)
    bt = bb * e_neg
    kh = km * e_neg
    rt = r * e_pos
    g_end = e_pos[c - 1:c, :]

    lane = lax.broadcasted_iota(jnp.int32, (1, LANES), 1)
    head_of_lane = lane // HEAD_DIM

    def stack_masked(x):
        return jnp.concatenate(
            [jnp.where(head_of_lane == hh, x, 0.0) for hh in range(HEADS_PER_LANE_TILE)], axis=0)

    def stack(x):
        return jnp.concatenate([x] * HEADS_PER_LANE_TILE, axis=0)

    def pick(x):
        out = x[:c]
        for hh in range(1, HEADS_PER_LANE_TILE):
            out = jnp.where(head_of_lane == hh, x[hh * c:(hh + 1) * c], out)
        return out

    kt2 = stack_masked(kt)
    rt2 = stack_masked(rt)
    rr = lax.broadcasted_iota(jnp.int32, (c2, c2), 0)
    cc = lax.broadcasted_iota(jnp.int32, (c2, c2), 1)
    strict_blk = (rr // c == cc // c) & (cc < rr)
    x = jnp.where(strict_blk, -dot_nt(kt2, stack(bt)), 0.0)
    eye = (rr == cc).astype(F32)
    tinv = eye + x
    steps = max(int(math.ceil(math.log2(c))) - 1, 0)
    for _ in range(steps):
        x = dot(x, x)
        tinv = tinv + dot(tinv, x)
    tr = lax.broadcasted_iota(jnp.int32, (c2, c), 0) % c
    sr = lax.broadcasted_iota(jnp.int32, (c2, c), 1)
    kk_s = jnp.where(sr < tr, dot_nt(kt2, kh), 0.0)
    rb_s = jnp.where(sr <= tr, dot_nt(rt2, bt), 0.0)
    rk_s = jnp.where(sr <= tr, dot_nt(rt2, kh), 0.0)

    ks = dot(jnp.concatenate([kt, rt], axis=0), s_blk)
    rhs = ks[:c] + pick(dot(kk_s, v))
    z = pick(dot(tinv, stack(rhs)))
    y = ks[c:] - pick(dot(rb_s, z)) + pick(dot(rk_s, v))
    jr = lax.broadcasted_iota(jnp.int32, (LANES, LANES), 0)
    ic = lax.broadcasted_iota(jnp.int32, (LANES, LANES), 1)
    decay = jnp.where(jr == ic, jnp.broadcast_to(g_end, (LANES, LANES)), 0.0)
    upd = dot_tn(jnp.concatenate([bt * g_end, kh * g_end], axis=0),
                 jnp.concatenate([-z, v], axis=0))
    s_new = dot(decay, s_blk) + jnp.where(jr // HEAD_DIM == ic // HEAD_DIM, upd, 0.0)
    return y, s_new


def _rwkv_scan_kernel(r_ref, lw_ref, km_ref, v_ref, kn_ref, b_ref, s0_ref, y_ref, sT_ref, s_sc,
                      *, chunk, prec):
    it = pl.program_id(2)

    @pl.when(it == 0)
    def _():
        s_sc[...] = s0_ref[...]

    nchunk = r_ref.shape[0] // chunk

    def step(ci, carry):
        sl = pl.ds(pl.multiple_of(ci * chunk, chunk), chunk)
        y, s_new = _rwkv_chunk(r_ref[sl, :], lw_ref[sl, :], km_ref[sl, :], v_ref[sl, :],
                               kn_ref[sl, :], b_ref[sl, :], s_sc[...], prec)
        y_ref[sl, :] = y
        s_sc[...] = s_new
        return carry

    lax.fori_loop(0, nchunk, step, 0)

    @pl.when(it == pl.num_programs(2) - 1)
    def _():
        sT_ref[...] = s_sc[...]


def _rwkv_scan(r, lw, km, v, kn, bb, s0_blk, *, batch, seq, prec=HIGHEST):
    n, w = r.shape
    npair = w // LANES
    chunk = min(CHUNK, seq)
    tb = _row_tile(seq, 4 * chunk)
    nt = seq // tb
    row = pl.BlockSpec((tb, LANES), lambda b, p, i: (b * nt + i, p))
    st = pl.BlockSpec((None, None, LANES, LANES), lambda b, p, i: (b, p, 0, 0))
    kern = functools.partial(_rwkv_scan_kernel, chunk=chunk, prec=prec)
    return pl.pallas_call(
        kern,
        grid=(batch, npair, nt),
        in_specs=[row] * 6 + [st],
        out_specs=[row, st],
        out_shape=[jax.ShapeDtypeStruct((n, w), F32),
                   jax.ShapeDtypeStruct((batch, npair, LANES, LANES), F32)],
        scratch_shapes=[pltpu.VMEM((LANES, LANES), F32)],
        compiler_params=_cparams(("parallel", "parallel", "arbitrary")),
        name="rwkv_scan",
    )(r, lw, km, v, kn, bb, s0_blk)


def _outproj_kernel(x_ref, fox_ref, y_ref, bonus_ref, g_ref, lnw_ref, lnb_ref, wa_ref, wb_ref,
                    gf_ref, wqt_ref, keys_ref, x2_ref, xn_ref, sc_ref):
    y = y_ref[...]
    w = y.shape[1]
    em = _head_sum_matrix(w) * (1.0 / HEAD_DIM)
    mean = jnp.dot(y, em, precision=HIGHEST, preferred_element_type=F32)
    d = y - mean
    var = jnp.dot(d * d, em, precision=HIGHEST, preferred_element_type=F32)
    yn = d * lax.rsqrt(var + LNX_EPS) * lnw_ref[...] + lnb_ref[...]
    rw = (yn + bonus_ref[...]) * g_ref[...]
    mix = (jnp.dot(fox_ref[...].astype(BF16), wa_ref[...], preferred_element_type=F32)
           + jnp.dot(rw.astype(BF16), wb_ref[...], preferred_element_type=F32))
    x2 = x_ref[...] + mix
    xn = x2 * lax.rsqrt(jnp.mean(x2 * x2, axis=-1, keepdims=True) + NORM_EPS) * gf_ref[...]
    x2_ref[...] = x2
    xn_ref[...] = xn
    qt = lax.dot_general(wqt_ref[...], xn.astype(BF16), (((1,), (1,)), ((), ())),
                         preferred_element_type=F32)
    qh = keys_ref.shape[2]
    for hc in range(keys_ref.shape[0]):
        sc_ref[hc] = jnp.dot(keys_ref[hc], qt[hc * qh:(hc + 1) * qh, :].astype(BF16),
                             preferred_element_type=F32)


def _outproj(x2d, fox, y, bonus, g, lnx_w, lnx_b, w_out, g_ffn, w_q, sub_keys):
    n, d = x2d.shape
    w = y.shape[1]
    fw = fox.shape[1]
    wa = w_out[:fw].astype(BF16)
    wb = w_out[fw:].astype(BF16)
    wqt = w_q.T.astype(BF16)
    nkeys, qh = sub_keys.shape[-2:]
    keys = sub_keys.reshape(-1, nkeys, qh).astype(BF16)
    nhc = keys.shape[0]
    tm = _row_tile(n, 512)
    row = lambda c: pl.BlockSpec((tm, c), lambda i: (i, 0))
    full = lambda a: pl.BlockSpec(a.shape, lambda i: (0,) * a.ndim)
    consts = [lnx_w.reshape(1, w), lnx_b.reshape(1, w), wa, wb, g_ffn.reshape(1, d), wqt, keys]
    return pl.pallas_call(
        _outproj_kernel,
        grid=(n // tm,),
        in_specs=[row(d), row(fw), row(w), row(w), row(w)] + [full(c) for c in consts],
        out_specs=[row(d), row(d), pl.BlockSpec((nhc, nkeys, tm), lambda i: (0, 0, i))],
        out_shape=[jax.ShapeDtypeStruct((n, d), F32), jax.ShapeDtypeStruct((n, d), F32),
                   jax.ShapeDtypeStruct((nhc, nkeys, n), F32)],
        compiler_params=_cparams(("parallel",)),
        name="outproj_scores",
    )(x2d, fox, y, bonus, g, *consts)


def _topk_rows(s, payload, k):
    rows = lax.broadcasted_iota(jnp.int32, s.shape, 0)
    nrow = s.shape[0]
    vals, idxs, pays = [], [], []
    for _ in range(k):
        m = jnp.max(s, axis=0, keepdims=True)
        idx = jnp.min(jnp.where(s == m, rows, nrow), axis=0, keepdims=True)
        hit = rows == idx
        vals.append(m)
        idxs.append(idx)
        if payload is not None:
            pays.append(jnp.max(jnp.where(hit, payload, -1), axis=0, keepdims=True))
        s = jnp.where(hit, -jnp.inf, s)
    return vals, idxs, pays


def _retrieve_kernel(sc_ref, idx_ref, gate_ref, *, topk, nkeys):
    nhead = sc_ref.shape[0] // 2
    idx_rows, gate_rows = [], []
    for h in range(nhead):
        v1, i1, _ = _topk_rows(sc_ref[2 * h], None, topk)
        v2, i2, _ = _topk_rows(sc_ref[2 * h + 1], None, topk)
        v2a = jnp.concatenate(v2, axis=0)
        i2a = jnp.concatenate(i2, axis=0)
        cand = jnp.concatenate([v1[a] + v2a for a in range(topk)], axis=0)
        cidx = jnp.concatenate([i1[a] * nkeys + i2a for a in range(topk)], axis=0)
        top, _, eidx = _topk_rows(cand, cidx, topk)
        top = jnp.concatenate(top, axis=0)
        ex = jnp.exp(top - top[0:1])
        gate_rows.append(ex / jnp.sum(ex, axis=0, keepdims=True))
        idx_rows.append(jnp.concatenate(eidx, axis=0))
    idx_ref[...] = jnp.concatenate(idx_rows, axis=0).T
    gate_ref[...] = jnp.concatenate(gate_rows, axis=0).T


def _retrieve(scores, topk):
    nhc, nkeys, n = scores.shape
    slots = (nhc // 2) * topk
    tt = _row_tile(n, 256)
    kern = functools.partial(_retrieve_kernel, topk=topk, nkeys=nkeys)
    return pl.pallas_call(
        kern,
        grid=(n // tt,),
        in_specs=[pl.BlockSpec((nhc, nkeys, tt), lambda i: (0, 0, i))],
        out_specs=[pl.BlockSpec((tt, slots), lambda i: (i, 0))] * 2,
        out_shape=[jax.ShapeDtypeStruct((n, slots), jnp.int32),
                   jax.ShapeDtypeStruct((n, slots), F32)],
        compiler_params=_cparams(("parallel",)),
        name="peer_retrieve",
    )(scores)


ROW_TILE = 8


def _pack_table(t):
    e, d = t.shape
    assert d == ROW_TILE * LANES and e % 2 == 0
    bits = lax.bitcast_convert_type(t.astype(BF16), jnp.uint16).astype(jnp.uint32)
    word = (bits[:e // 2] << 16) | bits[e // 2:]
    return lax.bitcast_convert_type(word, jnp.int32).reshape(e // 2 * ROW_TILE, LANES)


def _expert_row(tab_ref, e, half_rows):
    upper = e >= half_rows
    m = jnp.where(upper, e - half_rows, e)
    sh = jnp.where(upper, 16, 0)
    word = tab_ref[pl.ds(pl.multiple_of(m * ROW_TILE, ROW_TILE), ROW_TILE), :]
    return lax.bitcast_convert_type(jnp.left_shift(word, sh) & jnp.int32(-65536), F32)


def _sublane_sums(tiles):
    sub = lax.broadcasted_iota(jnp.int32, (ROW_TILE, LANES), 0)
    lo4 = sub < 4
    lvl = []
    for a, b in zip(tiles[0::2], tiles[1::2]):
        lvl.append(jnp.where(lo4, a, b) + pltpu.roll(jnp.where(lo4, b, a), 4, 0))

    def fold(xs, span):
        keep = (sub % (2 * span)) < span
        out = []
        for a, b in zip(xs[0::2], xs[1::2]):
            other = jnp.where(keep, b, a)
            swapped = jnp.where(keep, pltpu.roll(other, ROW_TILE - span, 0), pltpu.roll(other, span, 0))
            out.append(jnp.where(keep, a, b) + swapped)
        return out

    lvl = fold(lvl, 2)
    lvl = fold(lvl, 1)
    return lvl[0]


_SUBLANE_SUM_ORDER = (0, 4, 2, 6, 1, 5, 3, 7)


def _peer_act_kernel(idx_ref, x_ref, tab_ref, gate_ref, w_ref, qs_ref, *, half_rows):
    tb, slots = gate_ref.shape
    inv_order = [_SUBLANE_SUM_ORDER.index(i) for i in range(ROW_TILE)]

    def group(gi, carry):
        t0 = gi * ROW_TILE
        xs = [x_ref[pl.ds(pl.multiple_of((t0 + i) * ROW_TILE, ROW_TILE), ROW_TILE), :]
              for i in range(ROW_TILE)]

        def slot(j, c2):
            prods = [_expert_row(tab_ref, idx_ref[t0 + i, j], half_rows) * xs[i]
                     for i in range(ROW_TILE)]
            qs_ref[j, pl.ds(pl.multiple_of(t0, ROW_TILE), ROW_TILE), :] = _sublane_sums(
                [prods[inv_order[r]] for r in range(ROW_TILE)])
            return c2

        lax.fori_loop(0, slots, slot, 0)
        return carry

    lax.fori_loop(0, tb // ROW_TILE, group, 0)

    lane = lax.broadcasted_iota(jnp.int32, (LANES, slots), 1)

    def lane_sum(j, acc):
        onehot = (lane == j).astype(F32)
        return acc + jnp.dot(qs_ref[j], onehot, precision=HIGHEST, preferred_element_type=F32)

    act = lax.fori_loop(0, slots, lane_sum, jnp.zeros((tb, slots), F32))
    gelu = 0.5 * act * (1.0 + lax.erf(act * math.sqrt(0.5)))
    w_ref[...] = gate_ref[...] * gelu


def _peer_act(idx, xn, tab, gate, *, half_rows, tb):
    n, slots = idx.shape
    x8 = xn.reshape(n * ROW_TILE, LANES)
    kern = functools.partial(_peer_act_kernel, half_rows=half_rows)
    return pl.pallas_call(
        kern,
        grid=(n // tb,),
        in_specs=[pl.BlockSpec((tb, slots), lambda i: (i, 0), memory_space=pltpu.SMEM),
                  pl.BlockSpec((tb * ROW_TILE, LANES), lambda i: (i, 0)),
                  pl.BlockSpec(memory_space=pltpu.VMEM),
                  pl.BlockSpec((tb, slots), lambda i: (i, 0))],
        out_specs=pl.BlockSpec((tb, slots), lambda i: (i, 0)),
        out_shape=jax.ShapeDtypeStruct((n, slots), F32),
        scratch_shapes=[pltpu.VMEM((slots, tb, LANES), F32)],
        compiler_params=_cparams(("arbitrary",)),
        name="peer_expert_act",
    )(idx, x8, tab, gate)


def _peer_mix_kernel(idx_ref, w_ref, tab_ref, x2_ref, gfin_ref, o_ref, *, half_rows, nacc,
                     final_norm):
    tb, slots = idx_ref.shape

    def token(t, carry):
        def chunk(c, accs):
            accs = list(accs)
            for u in range(nacc):
                j = c * nacc + u
                accs[u] = accs[u] + _expert_row(tab_ref, idx_ref[t, j], half_rows) * w_ref[t, j]
            return tuple(accs)

        zero = jnp.zeros((ROW_TILE, LANES), F32)
        accs = lax.fori_loop(0, slots // nacc, chunk, (zero,) * nacc)
        ff = accs[0]
        for a in accs[1:]:
            ff = ff + a
        rows = pl.ds(pl.multiple_of(t * ROW_TILE, ROW_TILE), ROW_TILE)
        x3 = x2_ref[rows, :] + ff
        sq = jnp.sum(x3 * x3, axis=1, keepdims=True)
        ms = jnp.sum(sq, axis=0, keepdims=True) * (1.0 / (ROW_TILE * LANES))
        if final_norm:
            o_ref[rows, :] = x3 * lax.rsqrt(ms + NORM_EPS) * gfin_ref[...]
        else:
            o_ref[rows, :] = x3
        return carry

    lax.fori_loop(0, tb, token, 0)


def _peer_mix(idx, wgt, tab, x2, g_final, *, half_rows, tb, final_norm):
    n, slots = idx.shape
    d = x2.shape[1]
    x8 = x2.reshape(n * ROW_TILE, LANES)
    g8 = g_final.reshape(ROW_TILE, LANES)
    kern = functools.partial(_peer_mix_kernel, half_rows=half_rows, nacc=4, final_norm=final_norm)
    smem = lambda: pl.BlockSpec((tb, slots), lambda i: (i, 0), memory_space=pltpu.SMEM)
    out = pl.pallas_call(
        kern,
        grid=(n // tb,),
        in_specs=[smem(), smem(), pl.BlockSpec(memory_space=pltpu.VMEM),
                  pl.BlockSpec((tb * ROW_TILE, LANES), lambda i: (i, 0)),
                  pl.BlockSpec((ROW_TILE, LANES), lambda i: (0, 0))],
        out_specs=pl.BlockSpec((tb * ROW_TILE, LANES), lambda i: (i, 0)),
        out_shape=jax.ShapeDtypeStruct((n * ROW_TILE, LANES), F32),
        compiler_params=_cparams(("arbitrary",)),
        name="peer_expert_mix",
    )(idx, wgt, tab, x8, g8)
    return out.reshape(n, d)


def _state_to_blocks(s):
    b, h, d, _ = s.shape
    st = jnp.swapaxes(s, -1, -2).reshape(b, h // HEADS_PER_LANE_TILE, HEADS_PER_LANE_TILE, d, d)
    eye = jnp.eye(HEADS_PER_LANE_TILE, dtype=s.dtype)
    blk = st[:, :, :, :, None, :] * eye[None, None, :, None, :, None]
    return blk.reshape(b, h // HEADS_PER_LANE_TILE, LANES, LANES)


def _blocks_to_state(blk, heads):
    b, npair = blk.shape[:2]
    x = blk.reshape(b, npair, HEADS_PER_LANE_TILE, HEAD_DIM, HEADS_PER_LANE_TILE, HEAD_DIM)
    diag = jnp.stack([x[:, :, hh, :, hh, :] for hh in range(HEADS_PER_LANE_TILE)], axis=2)
    return jnp.swapaxes(diag.reshape(b, heads, HEAD_DIM, HEAD_DIM), -1, -2)


PEER_TOPK = 16


def _layer(x, k_past, v_past, lf_past, s0, shift0, lp, g_final, final_norm):
    (norm_mix_g, w_in, fox_b_f, mu, w0, w2, a0, a2, g2, k_k, k_a, r_k, lnx_w, lnx_b, w_out,
     norm_ffn_g, peer_w_q, peer_sub_keys, tab_u, tab_v, half_rows) = lp
    b, t, d = x.shape
    n = b * t
    fox_heads = fox_b_f.shape[0]
    fw = fox_heads * HEAD_DIM
    fox_cols = 3 * fw + fox_heads
    rwkv_heads = r_k.shape[0]
    w = rwkv_heads * HEAD_DIM
    x2d = x.reshape(n, d)
    qb, k, v, kb, vb, lf, rw_main, rw_tail = _inproj(x2d, norm_mix_g, w_in, fox_b_f, fox_cols,
                                                     fox_heads, 3 * w)
    fox = _fox_stream(qb, kb, vb, lf, k_past, v_past, lf_past, batch=b, q_len=t)

    prm = _rwkv_params(mu, w0, w2, a0, a2, g2, k_k, k_a, r_k.reshape(-1))
    tail = prm["tail"]
    shift_main = shift0[..., :3 * w]
    shift_tail = jnp.pad(shift0[..., 3 * w:], ((0, 0), (0, 0), (0, TAIL_PAD - tail)))
    r, lw, km, vv, kn, bb, g, bonus = _rwkv_pre(rw_main, rw_tail, shift_main, shift_tail, prm,
                                                batch=b, seq=t)
    y, s_blk = _rwkv_scan(r, lw, km, vv, kn, bb, _state_to_blocks(s0), batch=b, seq=t)
    s_t = _blocks_to_state(s_blk, rwkv_heads)
    last = jnp.concatenate([rw_main.reshape(b, t, -1)[:, -1:], rw_tail.reshape(b, t, -1)[:, -1:, :tail]],
                           axis=-1)

    x2, xn, scores = _outproj(x2d, fox, y, bonus, g, lnx_w, lnx_b, w_out, norm_ffn_g, peer_w_q,
                              peer_sub_keys)
    idx, gate = _retrieve(scores, PEER_TOPK)
    tb = _row_tile(n, 128)
    wgt = _peer_act(idx, xn, tab_u, gate, half_rows=half_rows, tb=tb)
    out = _peer_mix(idx, wgt, tab_v, x2, g_final, half_rows=half_rows, tb=tb, final_norm=final_norm)
    return (out.reshape(b, t, d), k.reshape(b, t, fox_heads, HEAD_DIM),
            v.reshape(b, t, fox_heads, HEAD_DIM), lf.reshape(b, t, fox_heads), s_t, last)


def kernel(x_prompt, x_sample, cache_fox_k, cache_fox_v, cache_fox_logf, state_rwkv, state_shift,
           norm_mix_g, w_in, fox_b_f, rwkv_mu, rwkv_w0, rwkv_w2, rwkv_a0, rwkv_a2, rwkv_g2,
           rwkv_k_k, rwkv_k_a, rwkv_r_k, rwkv_lnx_w, rwkv_lnx_b, w_out, norm_ffn_g,
           peer_w_q, peer_sub_keys, peer_u, peer_v, norm_final_g):
    depth = w_in.shape[0]
    yp, ys = x_prompt, x_sample
    bp = x_prompt.shape[0]
    dt = x_prompt.dtype
    fox_heads = fox_b_f.shape[1]
    rwkv_heads = rwkv_r_k.shape[1]
    rwkv_cols = rwkv_mu.shape[1]
    outs_p, outs_s = [], []
    for l in range(depth):
        lp = (norm_mix_g[l], w_in[l], fox_b_f[l], rwkv_mu[l], rwkv_w0[l], rwkv_w2[l], rwkv_a0[l],
              rwkv_a2[l], rwkv_g2[l], rwkv_k_k[l], rwkv_k_a[l], rwkv_r_k[l], rwkv_lnx_w[l],
              rwkv_lnx_b[l], w_out[l], norm_ffn_g[l], peer_w_q[l], peer_sub_keys[l],
              _pack_table(peer_u[l]), _pack_table(peer_v[l]), peer_u.shape[1] // 2)
        last = l == depth - 1
        empty_kv = jnp.zeros((bp, 0, fox_heads, HEAD_DIM), dt)
        empty_lf = jnp.zeros((bp, 0, fox_heads), dt)
        s_zero = jnp.zeros((bp, rwkv_heads, HEAD_DIM, HEAD_DIM), dt)
        sh_zero = jnp.zeros((bp, 1, rwkv_cols), dt)
        yp, *rest_p = _layer(yp, empty_kv, empty_kv, empty_lf, s_zero, sh_zero, lp, norm_final_g, last)
        ys, *rest_s = _layer(ys, cache_fox_k[l], cache_fox_v[l], cache_fox_logf[l], state_rwkv[l],
                             state_shift[l], lp, norm_final_g, last)
        outs_p.append(rest_p)
        outs_s.append(rest_s)
    stack = lambda outs, i: jnp.stack([o[i] for o in outs])
    return ((yp, ys) + tuple(stack(outs_p, i) for i in range(5))
            + tuple(stack(outs_s, i) for i in range(5)))
```

```python
import functools
import math

import jax
import jax.numpy as jnp
from jax import lax
from jax.experimental import pallas as pl
from jax.experimental.pallas import tpu as pltpu

F32 = jnp.float32
BF16 = jnp.bfloat16

HEAD_DIM = 64
LANES = 128
HEADS_PER_LANE_TILE = LANES // HEAD_DIM
TAIL_PAD = 2 * LANES
CHUNK = 64
NORM_EPS = 1e-6
LNX_EPS = 64e-5
NEG_BIG = -1e30
HIGHEST = lax.Precision.HIGHEST
VMEM_LIMIT = 48 * 1024 * 1024


def _cparams(sem):
    return pltpu.CompilerParams(dimension_semantics=sem, vmem_limit_bytes=VMEM_LIMIT)


def _row_tile(n, target):
    t = min(n, target)
    assert n % t == 0, (n, t)
    return t


def _inproj_kernel(x_ref, g_ref, wqkv_ref, wf_ref, wrw_ref, wtail_ref, bf_ref,
                   q_ref, k_ref, v_ref, kb_ref, vb_ref, lf_ref, rw_ref, tail_ref):
    x = x_ref[...]
    h = x * lax.rsqrt(jnp.mean(x * x, axis=-1, keepdims=True) + NORM_EPS) * g_ref[...]
    hb = h.astype(BF16)
    fw = wqkv_ref.shape[1] // 3
    qkv = jnp.dot(hb, wqkv_ref[...], preferred_element_type=F32)
    q_ref[...] = (qkv[:, :fw] * (1.0 / math.sqrt(HEAD_DIM))).astype(BF16)
    k = qkv[:, fw:2 * fw]
    v = qkv[:, 2 * fw:]
    k_ref[...] = k
    v_ref[...] = v
    kb_ref[...] = k.astype(BF16)
    vb_ref[...] = v.astype(BF16)
    f = jnp.dot(hb, wf_ref[...], preferred_element_type=F32) + bf_ref[...]
    lf_ref[...] = jax.nn.log_sigmoid(f)
    rw_ref[...] = jnp.dot(hb, wrw_ref[...], preferred_element_type=F32)
    tail_ref[...] = jnp.dot(hb, wtail_ref[...], preferred_element_type=F32)


def _inproj(x2d, g, w_in, b_f, fox_cols, fox_heads, rw_main):
    n, d = x2d.shape
    fw = fox_heads * HEAD_DIM
    wqkv = w_in[:, :3 * fw].astype(BF16)
    wf = w_in[:, 3 * fw:fox_cols].astype(BF16)
    wrw = w_in[:, fox_cols:fox_cols + rw_main].astype(BF16)
    wtail = w_in[:, fox_cols + rw_main:].astype(BF16)
    wtail = jnp.pad(wtail, ((0, 0), (0, TAIL_PAD - wtail.shape[1])))
    tm = _row_tile(n, 512)
    row = lambda c: pl.BlockSpec((tm, c), lambda i: (i, 0))
    full = lambda a: pl.BlockSpec(a.shape, lambda i: (0,) * a.ndim)
    g2 = g.reshape(1, d)
    bf2 = b_f.reshape(1, fox_heads)
    outs = (
        jax.ShapeDtypeStruct((n, fw), BF16),
        jax.ShapeDtypeStruct((n, fw), F32),
        jax.ShapeDtypeStruct((n, fw), F32),
        jax.ShapeDtypeStruct((n, fw), BF16),
        jax.ShapeDtypeStruct((n, fw), BF16),
        jax.ShapeDtypeStruct((n, fox_heads), F32),
        jax.ShapeDtypeStruct((n, rw_main), F32),
        jax.ShapeDtypeStruct((n, TAIL_PAD), F32),
    )
    return pl.pallas_call(
        _inproj_kernel,
        grid=(n // tm,),
        in_specs=[row(d), full(g2), full(wqkv), full(wf), full(wrw), full(wtail), full(bf2)],
        out_specs=[row(fw), row(fw), row(fw), row(fw), row(fw), row(fox_heads), row(rw_main),
                   row(TAIL_PAD)],
        out_shape=outs,
        compiler_params=_cparams(("parallel",)),
        name="inproj",
    )(x2d, g2, wqkv, wf, wrw, wtail, bf2)


def _cumsum_kernel(lf_ref, c_ref, ct_ref, carry_col, carry_row):
    @pl.when(pl.program_id(1) == 0)
    def _():
        carry_col[...] = jnp.zeros_like(carry_col)
        carry_row[...] = jnp.zeros_like(carry_row)

    lf = lf_ref[...]
    tc, nh = lf.shape
    r = lax.broadcasted_iota(jnp.int32, (tc, tc), 0)
    c = lax.broadcasted_iota(jnp.int32, (tc, tc), 1)
    lower = (c <= r).astype(F32)
    upper = (r <= c).astype(F32)
    eye = (lax.broadcasted_iota(jnp.int32, (nh, nh), 0)
           == lax.broadcasted_iota(jnp.int32, (nh, nh), 1)).astype(F32)
    cc = jnp.dot(lower, lf, precision=HIGHEST, preferred_element_type=F32) + carry_col[...]
    lft = lax.dot_general(eye, lf, (((1,), (1,)), ((), ())), precision=HIGHEST,
                          preferred_element_type=F32)
    cr = jnp.dot(lft, upper, precision=HIGHEST, preferred_element_type=F32) + carry_row[...]
    c_ref[...] = cc
    ct_ref[...] = cr
    carry_col[...] = cc[tc - 1:tc, :]
    carry_row[...] = cr[:, tc - 1:tc]


def _cumsum(lf, tc):
    b, l, nh = lf.shape
    assert l % tc == 0
    return pl.pallas_call(
        _cumsum_kernel,
        grid=(b, l // tc),
        in_specs=[pl.BlockSpec((None, tc, nh), lambda i, j: (i, j, 0))],
        out_specs=[pl.BlockSpec((None, tc, nh), lambda i, j: (i, j, 0)),
                   pl.BlockSpec((None, nh, tc), lambda i, j: (i, 0, j))],
        out_shape=(jax.ShapeDtypeStruct((b, l, nh), F32), jax.ShapeDtypeStruct((b, nh, l), F32)),
        scratch_shapes=[pltpu.VMEM((1, nh), F32), pltpu.VMEM((nh, 1), F32)],
        compiler_params=_cparams(("parallel", "arbitrary")),
        name="cumsum_logf",
    )(lf)


def _fox_kernel(q_ref, k_ref, v_ref, cq_ref, ck_ref, o_ref, m_sc, l_sc, acc_sc, cq_sc, *, q_off, tq, tk):
    i = pl.program_id(2)
    j = pl.program_id(3)
    p = pl.program_id(1)
    nk = pl.num_programs(3)

    @pl.when(j == 0)
    def _():
        m_sc[...] = jnp.full_like(m_sc, NEG_BIG)
        l_sc[...] = jnp.zeros_like(l_sc)
        acc_sc[...] = jnp.zeros_like(acc_sc)
        c_all = cq_ref[...]
        col = lax.broadcasted_iota(jnp.int32, c_all.shape, 1)
        for hh in range(HEADS_PER_LANE_TILE):
            h = p * HEADS_PER_LANE_TILE + hh
            cq_sc[hh] = jnp.sum(jnp.where(col == h, c_all, 0.0), axis=-1, keepdims=True)

    q_lo = q_off + i * tq
    k_lo = j * tk

    def body(masked):
        q = q_ref[...]
        k = k_ref[...]
        v = v_ref[...]
        lane = lax.broadcasted_iota(jnp.int32, (1, LANES), 1)
        if masked:
            qpos = q_lo + lax.broadcasted_iota(jnp.int32, (tq, tk), 0)
            kpos = k_lo + lax.broadcasted_iota(jnp.int32, (tq, tk), 1)
            visible = kpos <= qpos
        for hh in range(HEADS_PER_LANE_TILE):
            in_head = (lane >= hh * HEAD_DIM) & (lane < (hh + 1) * HEAD_DIM)
            qh = jnp.where(in_head, q, jnp.zeros_like(q))
            s = lax.dot_general(qh, k, (((1,), (1,)), ((), ())), preferred_element_type=F32)
            h = p * HEADS_PER_LANE_TILE + hh
            cq = cq_sc[hh]
            ck = ck_ref[pl.ds(h, 1), :]
            s = s + cq - ck
            if masked:
                s = jnp.where(visible, s, NEG_BIG)
            m_old = m_sc[hh]
            m_new = jnp.maximum(m_old, jnp.max(s, axis=-1, keepdims=True))
            alpha = jnp.exp(m_old - m_new)
            pr = jnp.exp(s - m_new)
            l_sc[hh] = alpha * l_sc[hh] + jnp.sum(pr, axis=-1, keepdims=True)
            pv = jnp.dot(pr.astype(BF16), v, preferred_element_type=F32)
            acc_sc[hh] = alpha * acc_sc[hh] + pv
            m_sc[hh] = m_new

    fully_visible = k_lo + tk - 1 <= q_lo
    any_visible = k_lo <= q_lo + tq - 1

    @pl.when(fully_visible)
    def _():
        body(False)

    @pl.when(jnp.logical_and(any_visible, jnp.logical_not(fully_visible)))
    def _():
        body(True)

    @pl.when(j == nk - 1)
    def _():
        lane = lax.broadcasted_iota(jnp.int32, (1, LANES), 1)
        out = jnp.zeros((tq, LANES), F32)
        for hh in range(HEADS_PER_LANE_TILE):
            in_head = (lane >= hh * HEAD_DIM) & (lane < (hh + 1) * HEAD_DIM)
            out = jnp.where(in_head, acc_sc[hh] / l_sc[hh], out)
        o_ref[...] = out


def _fox_attend(q, k, v, c_q, ct_k, *, batch, q_len, kv_len, q_off, tq, tk):
    n, w = q.shape
    npair = w // LANES
    nh = c_q.shape[1]
    nq, nk = q_len // tq, kv_len // tk
    assert q_len % tq == 0 and kv_len % tk == 0

    def last_kv(i):
        return (q_off + (i + 1) * tq - 1) // tk

    kv_map = lambda b, p, i, j: (b * nk + jnp.minimum(j, last_kv(i)), p)
    kern = functools.partial(_fox_kernel, q_off=q_off, tq=tq, tk=tk)
    return pl.pallas_call(
        kern,
        grid=(batch, npair, nq, nk),
        in_specs=[
            pl.BlockSpec((tq, LANES), lambda b, p, i, j: (b * nq + i, p)),
            pl.BlockSpec((tk, LANES), kv_map),
            pl.BlockSpec((tk, LANES), kv_map),
            pl.BlockSpec((tq, nh), lambda b, p, i, j: (b * nq + i, 0)),
            pl.BlockSpec((None, nh, tk), lambda b, p, i, j: (b, 0, jnp.minimum(j, last_kv(i)))),
        ],
        out_specs=pl.BlockSpec((tq, LANES), lambda b, p, i, j: (b * nq + i, p)),
        out_shape=jax.ShapeDtypeStruct((n, w), F32),
        scratch_shapes=[pltpu.VMEM((HEADS_PER_LANE_TILE, tq, 1), F32),
                        pltpu.VMEM((HEADS_PER_LANE_TILE, tq, 1), F32),
                        pltpu.VMEM((HEADS_PER_LANE_TILE, tq, LANES), F32),
                        pltpu.VMEM((HEADS_PER_LANE_TILE, tq, 1), F32)],
        compiler_params=_cparams(("parallel", "parallel", "parallel", "arbitrary")),
        name="fox_attention",
    )(q, k, v, c_q, ct_k)


def _fox_stream(qb, kb, vb, lf, k_past, v_past, lf_past, *, batch, q_len):
    n, w = qb.shape
    nh = lf.shape[1]
    past = k_past.shape[1]
    lf_new = lf.reshape(batch, q_len, nh)
    if past == 0:
        kv_len = q_len
        k_all, v_all, lf_all = kb, vb, lf_new
        tq = tk = _row_tile(q_len, 512)
        tc = tk
    else:
        kv_len = -(-(past + q_len) // LANES) * LANES
        pad = kv_len - past - q_len

        def cat(old, new):
            old = old.reshape(batch, past, -1).astype(new.dtype)
            new = new.reshape(batch, q_len, -1)
            z = jnp.zeros((batch, pad, new.shape[-1]), new.dtype)
            return jnp.concatenate([old, new, z], axis=1)

        k_all = cat(k_past, kb).reshape(batch * kv_len, w)
        v_all = cat(v_past, vb).reshape(batch * kv_len, w)
        lf_all = cat(lf_past, lf_new)
        tq, tk, tc = q_len, kv_len, LANES
    c, ct = _cumsum(lf_all, tc)
    c_q = c[:, past:past + q_len].reshape(n, nh)
    return _fox_attend(qb, k_all, v_all, c_q, ct, batch=batch, q_len=q_len, kv_len=kv_len,
                       q_off=past, tq=tq, tk=tk)


def _head_sum_matrix(width):
    r = lax.broadcasted_iota(jnp.int32, (width, width), 0) // HEAD_DIM
    c = lax.broadcasted_iota(jnp.int32, (width, width), 1) // HEAD_DIM
    return (r == c).astype(F32)


def _rwkv_pre_kernel(pm_ref, pt_ref, sm_ref, st_ref, mum_ref, mut_ref, wbig_ref, w0_ref, a0_ref,
                     kk_ref, ka_ref, rk_ref,
                     r_out, lw_out, km_out, v_out, kn_out, b_out, g_out, bonus_out,
                     carry_m, carry_t, *, lora_w, lora_a):
    @pl.when(pl.program_id(1) == 0)
    def _():
        carry_m[...] = sm_ref[...]
        carry_t[...] = st_ref[...]

    pm = pm_ref[...]
    pt = pt_ref[...]
    tm = pm.shape[0]
    w = pm.shape[1] // 3

    def shifted(p, carry):
        row = lax.broadcasted_iota(jnp.int32, p.shape, 0)
        return jnp.where(row == 0, carry[...], pltpu.roll(p, 1, 0))

    prev_m = shifted(pm, carry_m)
    prev_t = shifted(pt, carry_t)
    carry_m[...] = pm[tm - 1:tm, :]
    carry_t[...] = pt[tm - 1:tm, :]
    psm = pm + mum_ref[...] * (prev_m - pm)
    pst = pt + mut_ref[...] * (prev_t - pt)
    r = psm[:, :w]
    k = psm[:, w:2 * w]
    v = psm[:, 2 * w:]
    lane = lax.broadcasted_iota(jnp.int32, pst.shape, 1)
    z = jnp.where(lane < lora_w, jnp.tanh(pst),
                  jnp.where(lane < lora_w + lora_a, pst, jax.nn.sigmoid(pst)))
    lo = jnp.dot(z.astype(BF16), wbig_ref[...], preferred_element_type=F32)
    w_log = -jax.nn.softplus(-(w0_ref[...] + lo[:, :w])) - 0.5
    lw = -jnp.exp(w_log)
    a = jax.nn.sigmoid(a0_ref[...] + lo[:, w:2 * w])
    g = lo[:, 2 * w:]
    e = _head_sum_matrix(w)
    kk0 = k * kk_ref[...]
    n2 = jnp.dot(kk0 * kk0, e, precision=HIGHEST, preferred_element_type=F32)
    kn = kk0 / jnp.maximum(jnp.sqrt(n2), 1e-12)
    km = k * (1.0 + (a - 1.0) * ka_ref[...])
    rk = jnp.dot(r * km * rk_ref[...], e, precision=HIGHEST, preferred_element_type=F32)
    r_out[...] = r
    lw_out[...] = lw
    km_out[...] = km
    v_out[...] = v
    kn_out[...] = kn
    b_out[...] = kn * a
    g_out[...] = g
    bonus_out[...] = rk * v


def _rwkv_params(mu, w0, w2, a0, a2, g2, k_k, k_a, r_k):
    w = w0.shape[0]
    lora_w, lora_a, lora_g = w2.shape[0], a2.shape[0], g2.shape[0]
    w_lora = jnp.zeros((TAIL_PAD, 3 * w), F32)
    w_lora = w_lora.at[:lora_w, :w].set(w2)
    w_lora = w_lora.at[lora_w:lora_w + lora_a, w:2 * w].set(a2)
    w_lora = w_lora.at[lora_w + lora_a:lora_w + lora_a + lora_g, 2 * w:].set(g2)
    tail = mu.shape[0] - 3 * w
    return dict(
        mu_main=mu[:3 * w].reshape(1, 3 * w),
        mu_tail=jnp.pad(mu[3 * w:], (0, TAIL_PAD - tail)).reshape(1, TAIL_PAD),
        w_lora=w_lora.astype(BF16), w0=w0.reshape(1, w), a0=a0.reshape(1, w),
        k_k=k_k.reshape(1, w), k_a=k_a.reshape(1, w), r_k=r_k.reshape(1, w),
        lora_w=lora_w, lora_a=lora_a, tail=tail)


def _rwkv_pre(rw_main, rw_tail, shift_main, shift_tail, prm, *, batch, seq):
    n, w3 = rw_main.shape
    w = w3 // 3
    tm = _row_tile(seq, 512)
    nt = seq // tm
    row = lambda c: pl.BlockSpec((tm, c), lambda b, i: (b * nt + i, 0))
    per_b = lambda c: pl.BlockSpec((None, 1, c), lambda b, i: (b, 0, 0))
    full = lambda a: pl.BlockSpec(a.shape, lambda b, i: (0,) * a.ndim)
    consts = [prm["mu_main"], prm["mu_tail"], prm["w_lora"], prm["w0"], prm["a0"], prm["k_k"],
              prm["k_a"], prm["r_k"]]
    kern = functools.partial(_rwkv_pre_kernel, lora_w=prm["lora_w"], lora_a=prm["lora_a"])
    return pl.pallas_call(
        kern,
        grid=(batch, nt),
        in_specs=[row(w3), row(TAIL_PAD), per_b(w3), per_b(TAIL_PAD)] + [full(c) for c in consts],
        out_specs=[row(w)] * 8,
        out_shape=[jax.ShapeDtypeStruct((n, w), F32)] * 8,
        scratch_shapes=[pltpu.VMEM((1, w3), F32), pltpu.VMEM((1, TAIL_PAD), F32)],
        compiler_params=_cparams(("parallel", "arbitrary")),
        name="rwkv_pre",
    )(rw_main, rw_tail, shift_main, shift_tail, *consts)


def _rwkv_chunk(r, lw, km, v, kn, bb, s_blk, prec):
    c = r.shape[0]
    c2 = HEADS_PER_LANE_TILE * c
    dot = functools.partial(jnp.dot, precision=prec, preferred_element_type=F32)
    dot_nt = lambda a, b: lax.dot_general(a, b, (((1,), (1,)), ((), ())), precision=prec,
                                          preferred_element_type=F32)
    dot_tn = lambda a, b: lax.dot_general(a, b, (((0,), (0,)), ((), ())), precision=prec,
                                          preferred_element_type=F32)
    ti = lax.broadcasted_iota(jnp.int32, (c, c), 0)
    si = lax.broadcasted_iota(jnp.int32, (c, c), 1)
    cs = jnp.dot((si <= ti).astype(F32), lw, precision=HIGHEST, preferred_element_type=F32)
    e_pos = jnp.exp(cs)
    e_neg = jnp.exp(-cs)
    kt = kn * jnp.exp(cs - lw)
    bt = bb * e_neg
    kh = km * e_neg
    rt = r * e_pos
    g_end = e_pos[c - 1:c, :]

    lane = lax.broadcasted_iota(jnp.int32, (1, LANES), 1)
    head_of_lane = lane // HEAD_DIM

    def stack_masked(x):
        return jnp.concatenate(
            [jnp.where(head_of_lane == hh, x, 0.0) for hh in range(HEADS_PER_LANE_TILE)], axis=0)

    def stack(x):
        return jnp.concatenate([x] * HEADS_PER_LANE_TILE, axis=0)

    def pick(x):
        out = x[:c]
        for hh in range(1, HEADS_PER_LANE_TILE):
            out = jnp.where(head_of_lane == hh, x[hh * c:(hh + 1) * c], out)
        return out

    kt2 = stack_masked(kt)
    rt2 = stack_masked(rt)
    rr = lax.broadcasted_iota(jnp.int32, (c2, c2), 0)
    cc = lax.broadcasted_iota(jnp.int32, (c2, c2), 1)
    strict_blk = (rr // c == cc // c) & (cc < rr)
    x = jnp.where(strict_blk, -dot_nt(kt2, stack(bt)), 0.0)
    eye = (rr == cc).astype(F32)
    tinv = eye + x
    steps = max(int(math.ceil(math.log2(c))) - 1, 0)
    for _ in range(steps):
        x = dot(x, x)
        tinv = tinv + dot(tinv, x)
    tr = lax.broadcasted_iota(jnp.int32, (c2, c), 0) % c
    sr = lax.broadcasted_iota(jnp.int32, (c2, c), 1)
    kk_s = jnp.where(sr < tr, dot_nt(kt2, kh), 0.0)
    rb_s = jnp.where(sr <= tr, dot_nt(rt2, bt), 0.0)
    rk_s = jnp.where(sr <= tr, dot_nt(rt2, kh), 0.0)

    ks = dot(jnp.concatenate([kt, rt], axis=0), s_blk)
    rhs = ks[:c] + pick(dot(kk_s, v))
    z = pick(dot(tinv, stack(rhs)))
    y = ks[c:] - pick(dot(rb_s, z)) + pick(dot(rk_s, v))
    jr = lax.broadcasted_iota(jnp.int32, (LANES, LANES), 0)
    ic = lax.broadcasted_iota(jnp.int32, (LANES, LANES), 1)
    decay = jnp.where(jr == ic, jnp.broadcast_to(g_end, (LANES, LANES)), 0.0)
    upd = dot_tn(jnp.concatenate([bt * g_end, kh * g_end], axis=0),
                 jnp.concatenate([-z, v], axis=0))
    s_new = dot(decay, s_blk) + jnp.where(jr // HEAD_DIM == ic // HEAD_DIM, upd, 0.0)
    return y, s_new


def _rwkv_scan_kernel(r_ref, lw_ref, km_ref, v_ref, kn_ref, b_ref, s0_ref, y_ref, sT_ref, s_sc,
                      *, chunk, prec):
    it = pl.program_id(2)

    @pl.when(it == 0)
    def _():
        s_sc[...] = s0_ref[...]

    nchunk = r_ref.shape[0] // chunk

    def step(ci, carry):
        sl = pl.ds(pl.multiple_of(ci * chunk, chunk), chunk)
        y, s_new = _rwkv_chunk(r_ref[sl, :], lw_ref[sl, :], km_ref[sl, :], v_ref[sl, :],
                               kn_ref[sl, :], b_ref[sl, :], s_sc[...], prec)
        y_ref[sl, :] = y
        s_sc[...] = s_new
        return carry

    lax.fori_loop(0, nchunk, step, 0)

    @pl.when(it == pl.num_programs(2) - 1)
    def _():
        sT_ref[...] = s_sc[...]


def _rwkv_scan(r, lw, km, v, kn, bb, s0_blk, *, batch, seq, prec=HIGHEST):
    n, w = r.shape
    npair = w // LANES
    chunk = min(CHUNK, seq)
    tb = _row_tile(seq, 4 * chunk)
    nt = seq // tb
    row = pl.BlockSpec((tb, LANES), lambda b, p, i: (b * nt + i, p))
    st = pl.BlockSpec((None, None, LANES, LANES), lambda b, p, i: (b, p, 0, 0))
    kern = functools.partial(_rwkv_scan_kernel, chunk=chunk, prec=prec)
    return pl.pallas_call(
        kern,
        grid=(batch, npair, nt),
        in_specs=[row] * 6 + [st],
        out_specs=[row, st],
        out_shape=[jax.ShapeDtypeStruct((n, w), F32),
                   jax.ShapeDtypeStruct((batch, npair, LANES, LANES), F32)],
        scratch_shapes=[pltpu.VMEM((LANES, LANES), F32)],
        compiler_params=_cparams(("parallel", "parallel", "arbitrary")),
        name="rwkv_scan",
    )(r, lw, km, v, kn, bb, s0_blk)


def _outproj_kernel(x_ref, fox_ref, y_ref, bonus_ref, g_ref, lnw_ref, lnb_ref, wa_ref, wb_ref,
                    gf_ref, wqt_ref, keys_ref, x2_ref, xn_ref, sc_ref):
    y = y_ref[...]
    w = y.shape[1]
    em = _head_sum_matrix(w) * (1.0 / HEAD_DIM)
    mean = jnp.dot(y, em, precision=HIGHEST, preferred_element_type=F32)
    d = y - mean
    var = jnp.dot(d * d, em, precision=HIGHEST, preferred_element_type=F32)
    yn = d * lax.rsqrt(var + LNX_EPS) * lnw_ref[...] + lnb_ref[...]
    rw = (yn + bonus_ref[...]) * g_ref[...]
    mix = (jnp.dot(fox_ref[...].astype(BF16), wa_ref[...], preferred_element_type=F32)
           + jnp.dot(rw.astype(BF16), wb_ref[...], preferred_element_type=F32))
    x2 = x_ref[...] + mix
    xn = x2 * lax.rsqrt(jnp.mean(x2 * x2, axis=-1, keepdims=True) + NORM_EPS) * gf_ref[...]
    x2_ref[...] = x2
    xn_ref[...] = xn
    qt = lax.dot_general(wqt_ref[...], xn.astype(BF16), (((1,), (1,)), ((), ())),
                         preferred_element_type=F32)
    qh = keys_ref.shape[2]
    for hc in range(keys_ref.shape[0]):
        sc_ref[hc] = jnp.dot(keys_ref[hc], qt[hc * qh:(hc + 1) * qh, :].astype(BF16),
                             preferred_element_type=F32)


def _outproj(x2d, fox, y, bonus, g, lnx_w, lnx_b, w_out, g_ffn, w_q, sub_keys):
    n, d = x2d.shape
    w = y.shape[1]
    fw = fox.shape[1]
    wa = w_out[:fw].astype(BF16)
    wb = w_out[fw:].astype(BF16)
    wqt = w_q.T.astype(BF16)
    nkeys, qh = sub_keys.shape[-2:]
    keys = sub_keys.reshape(-1, nkeys, qh).astype(BF16)
    nhc = keys.shape[0]
    tm = _row_tile(n, 512)
    row = lambda c: pl.BlockSpec((tm, c), lambda i: (i, 0))
    full = lambda a: pl.BlockSpec(a.shape, lambda i: (0,) * a.ndim)
    consts = [lnx_w.reshape(1, w), lnx_b.reshape(1, w), wa, wb, g_ffn.reshape(1, d), wqt, keys]
    return pl.pallas_call(
        _outproj_kernel,
        grid=(n // tm,),
        in_specs=[row(d), row(fw), row(w), row(w), row(w)] + [full(c) for c in consts],
        out_specs=[row(d), row(d), pl.BlockSpec((nhc, nkeys, tm), lambda i: (0, 0, i))],
        out_shape=[jax.ShapeDtypeStruct((n, d), F32), jax.ShapeDtypeStruct((n, d), F32),
                   jax.ShapeDtypeStruct((nhc, nkeys, n), F32)],
        compiler_params=_cparams(("parallel",)),
        name="outproj_scores",
    )(x2d, fox, y, bonus, g, *consts)


def _topk_rows(s, payload, k):
    rows = lax.broadcasted_iota(jnp.int32, s.shape, 0)
    nrow = s.shape[0]
    vals, idxs, pays = [], [], []
    for _ in range(k):
        m = jnp.max(s, axis=0, keepdims=True)
        idx = jnp.min(jnp.where(s == m, rows, nrow), axis=0, keepdims=True)
        hit = rows == idx
        vals.append(m)
        idxs.append(idx)
        if payload is not None:
            pays.append(jnp.max(jnp.where(hit, payload, -1), axis=0, keepdims=True))
        s = jnp.where(hit, -jnp.inf, s)
    return vals, idxs, pays


def _retrieve_kernel(sc_ref, idx_ref, gate_ref, *, topk, nkeys):
    nhead = sc_ref.shape[0] // 2
    idx_rows, gate_rows = [], []
    for h in range(nhead):
        v1, i1, _ = _topk_rows(sc_ref[2 * h], None, topk)
        v2, i2, _ = _topk_rows(sc_ref[2 * h + 1], None, topk)
        v2a = jnp.concatenate(v2, axis=0)
        i2a = jnp.concatenate(i2, axis=0)
        cand = jnp.concatenate([v1[a] + v2a for a in range(topk)], axis=0)
        cidx = jnp.concatenate([i1[a] * nkeys + i2a for a in range(topk)], axis=0)
        top, _, eidx = _topk_rows(cand, cidx, topk)
        top = jnp.concatenate(top, axis=0)
        ex = jnp.exp(top - top[0:1])
        gate_rows.append(ex / jnp.sum(ex, axis=0, keepdims=True))
        idx_rows.extend(eidx)
    idx_ref[...] = (jnp.concatenate(idx_rows, axis=0) * HALF_TILE).T
    gate_ref[...] = jnp.concatenate(gate_rows, axis=0).T


def _retrieve(scores, topk):
    nhc, nkeys, n = scores.shape
    slots = (nhc // 2) * topk
    tt = _row_tile(n, 256)
    kern = functools.partial(_retrieve_kernel, topk=topk, nkeys=nkeys)
    return pl.pallas_call(
        kern,
        grid=(n // tt,),
        in_specs=[pl.BlockSpec((nhc, nkeys, tt), lambda i: (0, 0, i))],
        out_specs=[pl.BlockSpec((tt, slots), lambda i: (i, 0))] * 2,
        out_shape=[jax.ShapeDtypeStruct((n, slots), jnp.int32),
                   jax.ShapeDtypeStruct((n, slots), F32)],
        compiler_params=_cparams(("parallel",)),
        name="peer_retrieve",
    )(scores)


ROW_TILE = 8
HALF_TILE = ROW_TILE // 2


def _pack_table(t):
    e, d = t.shape
    assert d == ROW_TILE * LANES
    bits = lax.bitcast_convert_type(t.astype(BF16), jnp.uint16).astype(jnp.uint32)
    word = (bits[:, :d // 2] << 16) | bits[:, d // 2:]
    return lax.bitcast_convert_type(word, jnp.int32).reshape(e * HALF_TILE, LANES)


def _expert_halves(tab_ref, off):
    word = tab_ref[pl.ds(pl.multiple_of(off, HALF_TILE), HALF_TILE), :]
    first = lax.bitcast_convert_type(word & jnp.int32(-65536), F32)
    second = lax.bitcast_convert_type(jnp.left_shift(word, 16), F32)
    return first, second


def _fold_pairs(xs, span, axis):
    size = xs[0].shape[axis]
    pos = lax.broadcasted_iota(jnp.int32, xs[0].shape, axis)
    keep = (pos % (2 * span)) < span
    out = []
    for a, b in zip(xs[0::2], xs[1::2]):
        other = jnp.where(keep, b, a)
        if 2 * span == size:
            swapped = pltpu.roll(other, span, axis)
        else:
            swapped = jnp.where(keep, pltpu.roll(other, size - span, axis),
                                pltpu.roll(other, span, axis))
        out.append(jnp.where(keep, a, b) + swapped)
    return out


def _bit_reverse(i, bits):
    return int(format(i, "0%db" % bits)[::-1], 2)


def _sublane_sums(halves):
    lvl = [jnp.concatenate([a, b], axis=0) for a, b in zip(halves[0::2], halves[1::2])]
    lvl = _fold_pairs(lvl, 2, 0)
    return _fold_pairs(lvl, 1, 0)[0]


def _lane_sums_as_row(q):
    ones = jnp.ones((ROW_TILE, LANES), BF16)
    nt = lambda b: lax.dot_general(ones, b, (((1,), (1,)), ((), ())), preferred_element_type=F32)
    hi = q.astype(BF16)
    r1 = q - hi.astype(F32)
    mid = r1.astype(BF16)
    lo = (r1 - mid.astype(F32)).astype(BF16)
    return nt(hi) + nt(mid) + nt(lo)


def _peer_act_kernel(off_ref, x_ref, tab_ref, gate_ref, w_ref, part_ref, act_ref, *,
                     blocks_per_iter):
    tb, slots = gate_ref.shape
    sub_rev = [_bit_reverse(i, 3) for i in range(ROW_TILE)]
    nblk = slots // ROW_TILE

    def group(gi, carry):
        t0 = gi * ROW_TILE

        def token(tl, c1):
            t = t0 + tl
            x = x_ref[pl.ds(pl.multiple_of(t * ROW_TILE, ROW_TILE), ROW_TILE), :]
            xa, xb = x[:HALF_TILE], x[HALF_TILE:]

            def slot_blocks(jb, c2):
                for u in range(blocks_per_iter):
                    sb = jb * blocks_per_iter + u
                    prods = []
                    for i in range(ROW_TILE):
                        first, second = _expert_halves(
                            tab_ref, off_ref[t * slots + sb * ROW_TILE + sub_rev[i]])
                        prods.append(first * xa + second * xb)
                    part_ref[tl, pl.ds(pl.multiple_of(sb * ROW_TILE, ROW_TILE), ROW_TILE), :] = (
                        _sublane_sums(prods))
                return c2

            lax.fori_loop(0, nblk // blocks_per_iter, slot_blocks, 0)
            return c1

        lax.fori_loop(0, ROW_TILE, token, 0)
        sub = lax.broadcasted_iota(jnp.int32, (ROW_TILE, slots), 0)
        tile = jnp.zeros((ROW_TILE, slots), F32)
        for tl in range(ROW_TILE):
            tile = jnp.where(sub == tl, _lane_sums_as_row(part_ref[tl]), tile)
        act_ref[pl.ds(pl.multiple_of(t0, ROW_TILE), ROW_TILE), :] = tile
        return carry

    lax.fori_loop(0, tb // ROW_TILE, group, 0)
    act = act_ref[...]
    gelu = 0.5 * act * (1.0 + lax.erf(act * math.sqrt(0.5)))
    w_ref[...] = gate_ref[...] * gelu


def _peer_act(off, xn, tab, gate, *, tb):
    n, slots = gate.shape
    x8 = xn.reshape(n * ROW_TILE, LANES)
    kern = functools.partial(_peer_act_kernel, blocks_per_iter=2)
    return pl.pallas_call(
        kern,
        grid=(n // tb,),
        in_specs=[pl.BlockSpec((tb * slots,), lambda i: (i,), memory_space=pltpu.SMEM),
                  pl.BlockSpec((tb * ROW_TILE, LANES), lambda i: (i, 0)),
                  pl.BlockSpec(memory_space=pltpu.VMEM),
                  pl.BlockSpec((tb, slots), lambda i: (i, 0))],
        out_specs=pl.BlockSpec((tb, slots), lambda i: (i, 0)),
        out_shape=jax.ShapeDtypeStruct((n, slots), F32),
        scratch_shapes=[pltpu.VMEM((ROW_TILE, slots, LANES), F32),
                        pltpu.VMEM((tb, slots), F32)],
        compiler_params=_cparams(("arbitrary",)),
        name="peer_expert_act",
    )(off, x8, tab, gate)


def _peer_mix_kernel(off_ref, w_ref, tab_ref, x2_ref, gfin_ref, o_ref, *, slots, nacc,
                     slots_per_iter, final_norm):
    tb = x2_ref.shape[0] // ROW_TILE

    def token(t, carry):
        base = t * slots

        def chunk(c, accs):
            accs = list(accs)
            for u in range(slots_per_iter):
                k = base + c * slots_per_iter + u
                first, second = _expert_halves(tab_ref, off_ref[k])
                wk = w_ref[k]
                a = 2 * (u % nacc)
                accs[a] = accs[a] + first * wk
                accs[a + 1] = accs[a + 1] + second * wk
            return tuple(accs)

        zero = jnp.zeros((HALF_TILE, LANES), F32)
        accs = lax.fori_loop(0, slots // slots_per_iter, chunk, (zero,) * (2 * nacc))
        first, second = accs[0], accs[1]
        for a in range(1, nacc):
            first = first + accs[2 * a]
            second = second + accs[2 * a + 1]
        ff = jnp.concatenate([first, second], axis=0)
        rows = pl.ds(pl.multiple_of(t * ROW_TILE, ROW_TILE), ROW_TILE)
        o_ref[rows, :] = x2_ref[rows, :] + ff
        return carry

    lax.fori_loop(0, tb, token, 0)
    if final_norm:
        x3 = o_ref[...].reshape(tb, ROW_TILE, LANES)
        sq = jnp.sum(jnp.sum(x3 * x3, axis=2, keepdims=True), axis=1, keepdims=True)
        scale = lax.rsqrt(sq * (1.0 / (ROW_TILE * LANES)) + NORM_EPS)
        o_ref[...] = (x3 * scale * gfin_ref[...][None]).reshape(tb * ROW_TILE, LANES)


def _peer_mix(off, wgt, tab, x2, g_final, *, slots, tb, final_norm):
    n, d = x2.shape
    x8 = x2.reshape(n * ROW_TILE, LANES)
    g8 = g_final.reshape(ROW_TILE, LANES)
    kern = functools.partial(_peer_mix_kernel, slots=slots, nacc=4, slots_per_iter=16,
                             final_norm=final_norm)
    smem = lambda: pl.BlockSpec((tb * slots,), lambda i: (i,), memory_space=pltpu.SMEM)
    out = pl.pallas_call(
        kern,
        grid=(n // tb,),
        in_specs=[smem(), smem(), pl.BlockSpec(memory_space=pltpu.VMEM),
                  pl.BlockSpec((tb * ROW_TILE, LANES), lambda i: (i, 0)),
                  pl.BlockSpec((ROW_TILE, LANES), lambda i: (0, 0))],
        out_specs=pl.BlockSpec((tb * ROW_TILE, LANES), lambda i: (i, 0)),
        out_shape=jax.ShapeDtypeStruct((n * ROW_TILE, LANES), F32),
        compiler_params=_cparams(("arbitrary",)),
        name="peer_expert_mix",
    )(off, wgt, tab, x8, g8)
    return out.reshape(n, d)


def _state_to_blocks(s):
    b, h, d, _ = s.shape
    st = jnp.swapaxes(s, -1, -2).reshape(b, h // HEADS_PER_LANE_TILE, HEADS_PER_LANE_TILE, d, d)
    eye = jnp.eye(HEADS_PER_LANE_TILE, dtype=s.dtype)
    blk = st[:, :, :, :, None, :] * eye[None, None, :, None, :, None]
    return blk.reshape(b, h // HEADS_PER_LANE_TILE, LANES, LANES)


def _blocks_to_state(blk, heads):
    b, npair = blk.shape[:2]
    x = blk.reshape(b, npair, HEADS_PER_LANE_TILE, HEAD_DIM, HEADS_PER_LANE_TILE, HEAD_DIM)
    diag = jnp.stack([x[:, :, hh, :, hh, :] for hh in range(HEADS_PER_LANE_TILE)], axis=2)
    return jnp.swapaxes(diag.reshape(b, heads, HEAD_DIM, HEAD_DIM), -1, -2)


PEER_TOPK = 16


def _layer(x, k_past, v_past, lf_past, s0, shift0, lp, g_final, final_norm):
    (norm_mix_g, w_in, fox_b_f, mu, w0, w2, a0, a2, g2, k_k, k_a, r_k, lnx_w, lnx_b, w_out,
     norm_ffn_g, peer_w_q, peer_sub_keys, tab_u, tab_v) = lp
    b, t, d = x.shape
    n = b * t
    fox_heads = fox_b_f.shape[0]
    fw = fox_heads * HEAD_DIM
    fox_cols = 3 * fw + fox_heads
    rwkv_heads = r_k.shape[0]
    w = rwkv_heads * HEAD_DIM
    x2d = x.reshape(n, d)
    qb, k, v, kb, vb, lf, rw_main, rw_tail = _inproj(x2d, norm_mix_g, w_in, fox_b_f, fox_cols,
                                                     fox_heads, 3 * w)
    fox = _fox_stream(qb, kb, vb, lf, k_past, v_past, lf_past, batch=b, q_len=t)

    prm = _rwkv_params(mu, w0, w2, a0, a2, g2, k_k, k_a, r_k.reshape(-1))
    tail = prm["tail"]
    shift_main = shift0[..., :3 * w]
    shift_tail = jnp.pad(shift0[..., 3 * w:], ((0, 0), (0, 0), (0, TAIL_PAD - tail)))
    r, lw, km, vv, kn, bb, g, bonus = _rwkv_pre(rw_main, rw_tail, shift_main, shift_tail, prm,
                                                batch=b, seq=t)
    y, s_blk = _rwkv_scan(r, lw, km, vv, kn, bb, _state_to_blocks(s0), batch=b, seq=t)
    s_t = _blocks_to_state(s_blk, rwkv_heads)
    last = jnp.concatenate([rw_main.reshape(b, t, -1)[:, -1:], rw_tail.reshape(b, t, -1)[:, -1:, :tail]],
                           axis=-1)

    x2, xn, scores = _outproj(x2d, fox, y, bonus, g, lnx_w, lnx_b, w_out, norm_ffn_g, peer_w_q,
                              peer_sub_keys)
    idx, gate = _retrieve(scores, PEER_TOPK)
    tb = _row_tile(n, 128)
    slots = gate.shape[1]
    off = idx.reshape(n * slots)
    wgt = _peer_act(off, xn, tab_u, gate, tb=tb)
    out = _peer_mix(off, wgt.reshape(n * slots), tab_v, x2, g_final, slots=slots, tb=tb,
                    final_norm=final_norm)
    return (out.reshape(b, t, d), k.reshape(b, t, fox_heads, HEAD_DIM),
            v.reshape(b, t, fox_heads, HEAD_DIM), lf.reshape(b, t, fox_heads), s_t, last)


def kernel(x_prompt, x_sample, cache_fox_k, cache_fox_v, cache_fox_logf, state_rwkv, state_shift,
           norm_mix_g, w_in, fox_b_f, rwkv_mu, rwkv_w0, rwkv_w2, rwkv_a0, rwkv_a2, rwkv_g2,
           rwkv_k_k, rwkv_k_a, rwkv_r_k, rwkv_lnx_w, rwkv_lnx_b, w_out, norm_ffn_g,
           peer_w_q, peer_sub_keys, peer_u, peer_v, norm_final_g):
    depth = w_in.shape[0]
    yp, ys = x_prompt, x_sample
    bp = x_prompt.shape[0]
    dt = x_prompt.dtype
    fox_heads = fox_b_f.shape[1]
    rwkv_heads = rwkv_r_k.shape[1]
    rwkv_cols = rwkv_mu.shape[1]
    outs_p, outs_s = [], []
    for l in range(depth):
        lp = (norm_mix_g[l], w_in[l], fox_b_f[l], rwkv_mu[l], rwkv_w0[l], rwkv_w2[l], rwkv_a0[l],
              rwkv_a2[l], rwkv_g2[l], rwkv_k_k[l], rwkv_k_a[l], rwkv_r_k[l], rwkv_lnx_w[l],
              rwkv_lnx_b[l], w_out[l], norm_ffn_g[l], peer_w_q[l], peer_sub_keys[l],
              _pack_table(peer_u[l]), _pack_table(peer_v[l]))
        last = l == depth - 1
        empty_kv = jnp.zeros((bp, 0, fox_heads, HEAD_DIM), dt)
        empty_lf = jnp.zeros((bp, 0, fox_heads), dt)
        s_zero = jnp.zeros((bp, rwkv_heads, HEAD_DIM, HEAD_DIM), dt)
        sh_zero = jnp.zeros((bp, 1, rwkv_cols), dt)
        yp, *rest_p = _layer(yp, empty_kv, empty_kv, empty_lf, s_zero, sh_zero, lp, norm_final_g, last)
        ys, *rest_s = _layer(ys, cache_fox_k[l], cache_fox_v[l], cache_fox_logf[l], state_rwkv[l],
                             state_shift[l], lp, norm_final_g, last)
        outs_p.append(rest_p)
        outs_s.append(rest_s)
    stack = lambda outs, i: jnp.stack([o[i] for o in outs])
    return ((yp, ys) + tuple(stack(outs_p, i) for i in range(5))
            + tuple(stack(outs_s, i) for i in range(5)))
```

```python
import functools
import math

import jax
import jax.numpy as jnp
from jax import lax
from jax.experimental import pallas as pl
from jax.experimental.pallas import tpu as pltpu

F32 = jnp.float32
BF16 = jnp.bfloat16

HEAD_DIM = 64
LANES = 128
HEADS_PER_LANE_TILE = LANES // HEAD_DIM
TAIL_PAD = 2 * LANES
CHUNK = 64
RWKV_PASSES = 1
NORM_EPS = 1e-6
LNX_EPS = 64e-5
NEG_BIG = -1e30
LOG2E = math.log2(math.e)
HIGHEST = lax.Precision.HIGHEST
VMEM_LIMIT = 48 * 1024 * 1024


def _cparams(sem):
    return pltpu.CompilerParams(dimension_semantics=sem, vmem_limit_bytes=VMEM_LIMIT)


def _row_tile(n, target):
    t = min(n, target)
    assert n % t == 0, (n, t)
    return t


def _inproj_kernel(x_ref, g_ref, wqkv_ref, wf_ref, wrw_ref, wtail_ref, bf_ref,
                   q_ref, k_ref, v_ref, kb_ref, vb_ref, lf_ref, rw_ref, tail_ref):
    x = x_ref[...]
    h = x * lax.rsqrt(jnp.mean(x * x, axis=-1, keepdims=True) + NORM_EPS) * g_ref[...]
    hb = h.astype(BF16)
    fw = wqkv_ref.shape[1] // 3
    qkv = jnp.dot(hb, wqkv_ref[...], preferred_element_type=F32)
    q_ref[...] = (qkv[:, :fw] * (LOG2E / math.sqrt(HEAD_DIM))).astype(BF16)
    k = qkv[:, fw:2 * fw]
    v = qkv[:, 2 * fw:]
    k_ref[...] = k
    v_ref[...] = v
    kb_ref[...] = k.astype(BF16)
    vb_ref[...] = v.astype(BF16)
    f = jnp.dot(hb, wf_ref[...], preferred_element_type=F32) + bf_ref[...]
    lf_ref[...] = jax.nn.log_sigmoid(f)
    rw_ref[...] = jnp.dot(hb, wrw_ref[...], preferred_element_type=F32)
    tail_ref[...] = jnp.dot(hb, wtail_ref[...], preferred_element_type=F32)


def _inproj(x2d, g, w_in, b_f, fox_cols, fox_heads, rw_main):
    n, d = x2d.shape
    fw = fox_heads * HEAD_DIM
    wqkv = w_in[:, :3 * fw].astype(BF16)
    wf = w_in[:, 3 * fw:fox_cols].astype(BF16)
    wrw = w_in[:, fox_cols:fox_cols + rw_main].astype(BF16)
    wtail = w_in[:, fox_cols + rw_main:].astype(BF16)
    wtail = jnp.pad(wtail, ((0, 0), (0, TAIL_PAD - wtail.shape[1])))
    tm = _row_tile(n, 512)
    row = lambda c: pl.BlockSpec((tm, c), lambda i: (i, 0))
    full = lambda a: pl.BlockSpec(a.shape, lambda i: (0,) * a.ndim)
    g2 = g.reshape(1, d)
    bf2 = b_f.reshape(1, fox_heads)
    outs = (
        jax.ShapeDtypeStruct((n, fw), BF16),
        jax.ShapeDtypeStruct((n, fw), F32),
        jax.ShapeDtypeStruct((n, fw), F32),
        jax.ShapeDtypeStruct((n, fw), BF16),
        jax.ShapeDtypeStruct((n, fw), BF16),
        jax.ShapeDtypeStruct((n, fox_heads), F32),
        jax.ShapeDtypeStruct((n, rw_main), F32),
        jax.ShapeDtypeStruct((n, TAIL_PAD), F32),
    )
    return pl.pallas_call(
        _inproj_kernel,
        grid=(n // tm,),
        in_specs=[row(d), full(g2), full(wqkv), full(wf), full(wrw), full(wtail), full(bf2)],
        out_specs=[row(fw), row(fw), row(fw), row(fw), row(fw), row(fox_heads), row(rw_main),
                   row(TAIL_PAD)],
        out_shape=outs,
        compiler_params=_cparams(("parallel",)),
        name="inproj",
    )(x2d, g2, wqkv, wf, wrw, wtail, bf2)


def _cumsum_kernel(lf_ref, c_ref, carry):
    @pl.when(pl.program_id(1) == 0)
    def _():
        carry[...] = jnp.zeros_like(carry)

    lf = lf_ref[...]
    tc = lf.shape[0]
    r = lax.broadcasted_iota(jnp.int32, (tc, tc), 0)
    c = lax.broadcasted_iota(jnp.int32, (tc, tc), 1)
    lower = (c <= r).astype(F32)
    cc = jnp.dot(lower, lf, precision=HIGHEST, preferred_element_type=F32) + carry[...]
    c_ref[...] = cc
    carry[...] = cc[tc - 1:tc, :]


def _cumsum(lf, tc):
    b, l, nh = lf.shape
    assert l % tc == 0
    return pl.pallas_call(
        _cumsum_kernel,
        grid=(b, l // tc),
        in_specs=[pl.BlockSpec((None, tc, nh), lambda i, j: (i, j, 0))],
        out_specs=pl.BlockSpec((None, tc, nh), lambda i, j: (i, j, 0)),
        out_shape=jax.ShapeDtypeStruct((b, l, nh), F32),
        scratch_shapes=[pltpu.VMEM((1, nh), F32)],
        compiler_params=_cparams(("parallel", "arbitrary")),
        name="cumsum_logf",
    )(lf)


def _split3(x):
    hi = x.astype(BF16)
    r = x - hi.astype(F32)
    mid = r.astype(BF16)
    lo = (r - mid.astype(F32)).astype(BF16)
    return hi.astype(F32), mid.astype(F32), lo.astype(F32)


def _augment_kernel(x_ref, c_ref, o_ref, *, role):
    tm = x_ref.shape[0]
    lane = lax.broadcasted_iota(jnp.int32, (tm, LANES), 1)
    for p in range(x_ref.shape[1] // LANES):
        xp = x_ref[:, p * LANES:(p + 1) * LANES].astype(F32)
        for hh in range(HEADS_PER_LANE_TILE):
            h = p * HEADS_PER_LANE_TILE + hh
            own = (lane >= hh * HEAD_DIM) & (lane < (hh + 1) * HEAD_DIM)
            e = (lane + (1 - hh) * HEAD_DIM) % LANES
            if role == "v":
                ext = jnp.where(e == 0, 1.0, 0.0)
            else:
                c = jnp.broadcast_to(c_ref[:, h:h + 1], (tm, LANES))
                hi, mid, lo = _split3(c * LOG2E)
                sgn = 1.0 if role == "q" else -1.0
                base = 0 if role == "q" else 3
                ext = jnp.where(e == base, sgn * hi,
                                jnp.where(e == base + 1, sgn * mid,
                                          jnp.where(e == base + 2, sgn * lo,
                                                    jnp.where(e < 6, 1.0, 0.0))))
            o_ref[:, h * LANES:(h + 1) * LANES] = jnp.where(own, xp, ext).astype(BF16)


def _augment(x, c, role):
    n, w = x.shape
    nh = w // HEAD_DIM
    tm = _row_tile(n, 512) if n % 512 == 0 else n
    kern = functools.partial(_augment_kernel, role=role)
    return pl.pallas_call(
        kern,
        grid=(n // tm,),
        in_specs=[pl.BlockSpec((tm, w), lambda i: (i, 0)),
                  pl.BlockSpec((tm, nh), lambda i: (i, 0))],
        out_specs=pl.BlockSpec((tm, nh * LANES), lambda i: (i, 0)),
        out_shape=jax.ShapeDtypeStruct((n, nh * LANES), BF16),
        compiler_params=_cparams(("parallel",)),
        name="fox_augment_" + role,
    )(x, c)


Q_SUB = 128
K_SUB = 256


def _fox_kernel(q_ref, k_ref, v_ref, o_ref, m_sc, acc_sc, *, q_off, tq, tk):
    i = pl.program_id(2)
    j = pl.program_id(3)
    nk = pl.num_programs(3)
    qs_n, ks_n = min(Q_SUB, tq), min(K_SUB, tk)

    @pl.when(j == 0)
    def _():
        m_sc[...] = jnp.full_like(m_sc, NEG_BIG)
        acc_sc[...] = jnp.zeros_like(acc_sc)

    q_lo = q_off + i * tq
    k_lo = j * tk

    def body(masked):
        if masked:
            diff = (lax.broadcasted_iota(jnp.int32, (qs_n, LANES), 1)
                    - lax.broadcasted_iota(jnp.int32, (qs_n, LANES), 0))
        for hh in range(HEADS_PER_LANE_TILE):
            cols = slice(hh * LANES, (hh + 1) * LANES)
            for ks in range(tk // ks_n):
                krows = slice(ks * ks_n, (ks + 1) * ks_n)
                k_sub = k_ref[krows, cols]
                v_sub = v_ref[krows, cols]
                for qs in range(tq // qs_n):
                    qrows = slice(qs * qs_n, (qs + 1) * qs_n)
                    s = lax.dot_general(q_ref[qrows, cols], k_sub, (((1,), (1,)), ((), ())),
                                        preferred_element_type=F32)
                    parts = [s[:, c * LANES:(c + 1) * LANES] for c in range(ks_n // LANES)]
                    if masked:
                        parts = [jnp.where(diff <= q_lo - k_lo + qs * qs_n - ks * ks_n - c * LANES,
                                           pc, NEG_BIG) for c, pc in enumerate(parts)]
                    mx = parts[0]
                    for pc in parts[1:]:
                        mx = jnp.maximum(mx, pc)
                    m_old = m_sc[hh, qrows, :]
                    m_new = jnp.maximum(m_old, jnp.max(mx, axis=-1, keepdims=True))
                    alpha = jnp.exp2(m_old - m_new)
                    pr = jnp.concatenate([jnp.exp2(pc - m_new).astype(BF16) for pc in parts], axis=1)
                    pv = jnp.dot(pr, v_sub, preferred_element_type=F32)
                    acc_sc[hh, qrows, :] = alpha * acc_sc[hh, qrows, :] + pv
                    m_sc[hh, qrows, :] = m_new

    fully_visible = k_lo + tk - 1 <= q_lo
    any_visible = k_lo <= q_lo + tq - 1

    @pl.when(fully_visible)
    def _():
        body(False)

    @pl.when(jnp.logical_and(any_visible, jnp.logical_not(fully_visible)))
    def _():
        body(True)

    @pl.when(j == nk - 1)
    def _():
        lane = lax.broadcasted_iota(jnp.int32, (1, LANES), 1)
        out = jnp.zeros((tq, LANES), F32)
        for hh in range(HEADS_PER_LANE_TILE):
            in_head = (lane >= hh * HEAD_DIM) & (lane < (hh + 1) * HEAD_DIM)
            acc = acc_sc[hh]
            ones_col = (1 - hh) * HEAD_DIM
            denom = jnp.broadcast_to(acc[:, ones_col:ones_col + 1], acc.shape)
            out = jnp.where(in_head, acc / denom, out)
        o_ref[...] = out


def _fox_attend(q_aug, k_aug, v_aug, *, batch, q_len, kv_len, q_off, tq, tk):
    n, wa = q_aug.shape
    pair_w = HEADS_PER_LANE_TILE * LANES
    npair = wa // pair_w
    nq, nk = q_len // tq, kv_len // tk
    assert q_len % tq == 0 and kv_len % tk == 0

    def last_kv(i):
        return (q_off + (i + 1) * tq - 1) // tk

    kv_map = lambda b, p, i, j: (b * nk + jnp.minimum(j, last_kv(i)), p)
    kern = functools.partial(_fox_kernel, q_off=q_off, tq=tq, tk=tk)
    return pl.pallas_call(
        kern,
        grid=(batch, npair, nq, nk),
        in_specs=[
            pl.BlockSpec((tq, pair_w), lambda b, p, i, j: (b * nq + i, p)),
            pl.BlockSpec((tk, pair_w), kv_map),
            pl.BlockSpec((tk, pair_w), kv_map),
        ],
        out_specs=pl.BlockSpec((tq, LANES), lambda b, p, i, j: (b * nq + i, p)),
        out_shape=jax.ShapeDtypeStruct((n, npair * LANES), F32),
        scratch_shapes=[pltpu.VMEM((HEADS_PER_LANE_TILE, tq, LANES), F32),
                        pltpu.VMEM((HEADS_PER_LANE_TILE, tq, LANES), F32)],
        compiler_params=_cparams(("parallel", "parallel", "parallel", "arbitrary")),
        name="fox_attention",
    )(q_aug, k_aug, v_aug)


def _fox_stream(qb, kb, vb, lf, k_past, v_past, lf_past, *, batch, q_len):
    n, w = qb.shape
    nh = lf.shape[1]
    past = k_past.shape[1]
    lf_new = lf.reshape(batch, q_len, nh)
    if past == 0:
        kv_len = q_len
        k_all, v_all, lf_all = kb, vb, lf_new
        tq = tk = _row_tile(q_len, 512)
        tc = tk
    else:
        kv_len = -(-(past + q_len) // K_SUB) * K_SUB
        pad = kv_len - past - q_len

        def cat(old, new):
            old = old.reshape(batch, past, -1).astype(new.dtype)
            new = new.reshape(batch, q_len, -1)
            z = jnp.zeros((batch, pad, new.shape[-1]), new.dtype)
            return jnp.concatenate([old, new, z], axis=1)

        k_all = cat(k_past, kb).reshape(batch * kv_len, w)
        v_all = cat(v_past, vb).reshape(batch * kv_len, w)
        lf_all = cat(lf_past, lf_new)
        tq, tk, tc = q_len, kv_len, LANES
    c = _cumsum(lf_all, tc)
    c_k = c.reshape(batch * kv_len, nh)
    c_q = c[:, past:past + q_len].reshape(n, nh)
    return _fox_attend(_augment(qb, c_q, "q"), _augment(k_all, c_k, "k"), _augment(v_all, c_k, "v"),
                       batch=batch, q_len=q_len, kv_len=kv_len, q_off=past, tq=tq, tk=tk)


def _head_sum_matrix(width):
    r = lax.broadcasted_iota(jnp.int32, (width, width), 0) // HEAD_DIM
    c = lax.broadcasted_iota(jnp.int32, (width, width), 1) // HEAD_DIM
    return (r == c).astype(F32)


def _rwkv_pre_kernel(pm_ref, pt_ref, sm_ref, st_ref, mum_ref, mut_ref, wbig_ref, w0_ref, a0_ref,
                     kk_ref, ka_ref, rk_ref,
                     r_out, lw_out, km_out, v_out, kn_out, b_out, g_out, bonus_out,
                     carry_m, carry_t, *, lora_w, lora_a):
    @pl.when(pl.program_id(1) == 0)
    def _():
        carry_m[...] = sm_ref[...]
        carry_t[...] = st_ref[...]

    pm = pm_ref[...]
    pt = pt_ref[...]
    tm = pm.shape[0]
    w = pm.shape[1] // 3

    def shifted(p, carry):
        row = lax.broadcasted_iota(jnp.int32, p.shape, 0)
        return jnp.where(row == 0, carry[...], pltpu.roll(p, 1, 0))

    prev_m = shifted(pm, carry_m)
    prev_t = shifted(pt, carry_t)
    carry_m[...] = pm[tm - 1:tm, :]
    carry_t[...] = pt[tm - 1:tm, :]
    psm = pm + mum_ref[...] * (prev_m - pm)
    pst = pt + mut_ref[...] * (prev_t - pt)
    r = psm[:, :w]
    k = psm[:, w:2 * w]
    v = psm[:, 2 * w:]
    lane = lax.broadcasted_iota(jnp.int32, pst.shape, 1)
    z = jnp.where(lane < lora_w, jnp.tanh(pst),
                  jnp.where(lane < lora_w + lora_a, pst, jax.nn.sigmoid(pst)))
    lo = jnp.dot(z.astype(BF16), wbig_ref[...], preferred_element_type=F32)
    w_log = -jax.nn.softplus(-(w0_ref[...] + lo[:, :w])) - 0.5
    lw = -jnp.exp(w_log)
    a = jax.nn.sigmoid(a0_ref[...] + lo[:, w:2 * w])
    g = lo[:, 2 * w:]
    e = _head_sum_matrix(w)
    kk0 = k * kk_ref[...]
    n2 = jnp.dot(kk0 * kk0, e, precision=HIGHEST, preferred_element_type=F32)
    kn = kk0 / jnp.maximum(jnp.sqrt(n2), 1e-12)
    km = k * (1.0 + (a - 1.0) * ka_ref[...])
    rk = jnp.dot(r * km * rk_ref[...], e, precision=HIGHEST, preferred_element_type=F32)
    r_out[...] = r
    lw_out[...] = lw
    km_out[...] = km
    v_out[...] = v
    kn_out[...] = kn
    b_out[...] = kn * a
    g_out[...] = g
    bonus_out[...] = rk * v


def _rwkv_params(mu, w0, w2, a0, a2, g2, k_k, k_a, r_k):
    w = w0.shape[0]
    lora_w, lora_a, lora_g = w2.shape[0], a2.shape[0], g2.shape[0]
    w_lora = jnp.zeros((TAIL_PAD, 3 * w), F32)
    w_lora = w_lora.at[:lora_w, :w].set(w2)
    w_lora = w_lora.at[lora_w:lora_w + lora_a, w:2 * w].set(a2)
    w_lora = w_lora.at[lora_w + lora_a:lora_w + lora_a + lora_g, 2 * w:].set(g2)
    tail = mu.shape[0] - 3 * w
    return dict(
        mu_main=mu[:3 * w].reshape(1, 3 * w),
        mu_tail=jnp.pad(mu[3 * w:], (0, TAIL_PAD - tail)).reshape(1, TAIL_PAD),
        w_lora=w_lora.astype(BF16), w0=w0.reshape(1, w), a0=a0.reshape(1, w),
        k_k=k_k.reshape(1, w), k_a=k_a.reshape(1, w), r_k=r_k.reshape(1, w),
        lora_w=lora_w, lora_a=lora_a, tail=tail)


def _rwkv_pre(rw_main, rw_tail, shift_main, shift_tail, prm, *, batch, seq):
    n, w3 = rw_main.shape
    w = w3 // 3
    tm = _row_tile(seq, 512)
    nt = seq // tm
    row = lambda c: pl.BlockSpec((tm, c), lambda b, i: (b * nt + i, 0))
    per_b = lambda c: pl.BlockSpec((None, 1, c), lambda b, i: (b, 0, 0))
    full = lambda a: pl.BlockSpec(a.shape, lambda b, i: (0,) * a.ndim)
    consts = [prm["mu_main"], prm["mu_tail"], prm["w_lora"], prm["w0"], prm["a0"], prm["k_k"],
              prm["k_a"], prm["r_k"]]
    kern = functools.partial(_rwkv_pre_kernel, lora_w=prm["lora_w"], lora_a=prm["lora_a"])
    return pl.pallas_call(
        kern,
        grid=(batch, nt),
        in_specs=[row(w3), row(TAIL_PAD), per_b(w3), per_b(TAIL_PAD)] + [full(c) for c in consts],
        out_specs=[row(w)] * 8,
        out_shape=[jax.ShapeDtypeStruct((n, w), F32)] * 8,
        scratch_shapes=[pltpu.VMEM((1, w3), F32), pltpu.VMEM((1, TAIL_PAD), F32)],
        compiler_params=_cparams(("parallel", "arbitrary")),
        name="rwkv_pre",
    )(rw_main, rw_tail, shift_main, shift_tail, *consts)


def _bmm(a, b, kind, passes):
    contract = {"nn": ((2,), (1,)), "nt": ((2,), (2,)), "tn": ((1,), (1,))}[kind]
    dims = (contract, ((0,), (0,)))
    if passes == 6:
        return lax.dot_general(a, b, dims, precision=HIGHEST, preferred_element_type=F32)
    dg = lambda x, y: lax.dot_general(x, y, dims, preferred_element_type=F32)
    ah, bh = a.astype(BF16), b.astype(BF16)
    out = dg(ah, bh)
    if passes == 3:
        al = (a - ah.astype(F32)).astype(BF16)
        bl = (b - bh.astype(F32)).astype(BF16)
        out = out + dg(ah, bl) + dg(al, bh)
    return out


def _rwkv_chunk(r, lw, km, v, kn, bb, s_blk, passes):
    g, c, _ = r.shape
    c2 = HEADS_PER_LANE_TILE * c
    ti = lax.broadcasted_iota(jnp.int32, (g, c, c), 1)
    si = lax.broadcasted_iota(jnp.int32, (g, c, c), 2)
    cs = _bmm((si <= ti).astype(F32), lw, "nn", 6)
    e_pos = jnp.exp(cs)
    e_neg = jnp.exp(-cs)
    kt = kn * jnp.exp(cs - lw)
    bt = bb * e_neg
    kh = km * e_neg
    rt = r * e_pos
    g_end = e_pos[:, c - 1:c, :]

    lane = lax.broadcasted_iota(jnp.int32, (1, 1, LANES), 2)
    head_of_lane = lane // HEAD_DIM

    def stack_masked(x):
        return jnp.concatenate(
            [jnp.where(head_of_lane == hh, x, 0.0) for hh in range(HEADS_PER_LANE_TILE)], axis=1)

    def stack(x):
        return jnp.concatenate([x] * HEADS_PER_LANE_TILE, axis=1)

    def pick(x):
        out = x[:, :c]
        for hh in range(1, HEADS_PER_LANE_TILE):
            out = jnp.where(head_of_lane == hh, x[:, hh * c:(hh + 1) * c], out)
        return out

    kt2 = stack_masked(kt)
    rt2 = stack_masked(rt)
    rr = lax.broadcasted_iota(jnp.int32, (1, c2, c2), 1)
    cc = lax.broadcasted_iota(jnp.int32, (1, c2, c2), 2)
    strict_blk = (rr // c == cc // c) & (cc < rr)
    x = jnp.where(strict_blk, -_bmm(kt2, stack(bt), "nt", passes), 0.0)
    tinv = (rr == cc).astype(F32) + x
    steps = max(int(math.ceil(math.log2(c))) - 1, 0)
    for _ in range(steps):
        x = _bmm(x, x, "nn", passes)
        tinv = tinv + _bmm(tinv, x, "nn", passes)
    tr = lax.broadcasted_iota(jnp.int32, (1, c2, c), 1) % c
    sr = lax.broadcasted_iota(jnp.int32, (1, c2, c), 2)
    kk_s = jnp.where(sr < tr, _bmm(kt2, kh, "nt", passes), 0.0)
    rb_s = jnp.where(sr <= tr, _bmm(rt2, bt, "nt", passes), 0.0)
    rk_s = jnp.where(sr <= tr, _bmm(rt2, kh, "nt", passes), 0.0)

    ks = _bmm(jnp.concatenate([kt, rt], axis=1), s_blk, "nn", passes)
    rhs = ks[:, :c] + pick(_bmm(kk_s, v, "nn", passes))
    z = pick(_bmm(tinv, stack(rhs), "nn", passes))
    y = ks[:, c:] - pick(_bmm(rb_s, z, "nn", passes)) + pick(_bmm(rk_s, v, "nn", passes))
    jr = lax.broadcasted_iota(jnp.int32, (1, LANES, LANES), 1)
    ic = lax.broadcasted_iota(jnp.int32, (1, LANES, LANES), 2)
    decay_rows = jnp.swapaxes(jnp.broadcast_to(g_end, (g, LANES, LANES)), 1, 2)
    upd = _bmm(jnp.concatenate([bt * g_end, kh * g_end], axis=1),
               jnp.concatenate([-z, v], axis=1), "tn", passes)
    s_new = decay_rows * s_blk + jnp.where(jr // HEAD_DIM == ic // HEAD_DIM, upd, 0.0)
    return y, s_new


def _rwkv_scan_kernel(r_ref, lw_ref, km_ref, v_ref, kn_ref, b_ref, s0_ref, y_ref, sT_ref, s_sc,
                      *, chunk, passes):
    it = pl.program_id(0)
    nb, tb, w = r_ref.shape
    npair = w // LANES

    @pl.when(it == 0)
    def _():
        s_sc[...] = s0_ref[...]

    def step(ci, carry):
        rows = pl.ds(pl.multiple_of(ci * chunk, chunk), chunk)

        def gather(ref):
            blk = ref[:, rows, :]
            return jnp.concatenate([blk[:, :, p * LANES:(p + 1) * LANES] for p in range(npair)],
                                   axis=0)

        y, s_new = _rwkv_chunk(gather(r_ref), gather(lw_ref), gather(km_ref), gather(v_ref),
                               gather(kn_ref), gather(b_ref), s_sc[...], passes)
        for p in range(npair):
            y_ref[:, rows, p * LANES:(p + 1) * LANES] = y[p * nb:(p + 1) * nb]
        s_sc[...] = s_new
        return carry

    lax.fori_loop(0, tb // chunk, step, 0)

    @pl.when(it == pl.num_programs(0) - 1)
    def _():
        sT_ref[...] = s_sc[...]


def _rwkv_scan(r, lw, km, v, kn, bb, s0_blk, *, batch, seq, passes=RWKV_PASSES):
    n, w = r.shape
    npair = w // LANES
    chunk = min(CHUNK, seq)
    tb = _row_tile(seq, 4 * chunk)
    row = pl.BlockSpec((batch, tb, w), lambda i: (0, i, 0))
    st = pl.BlockSpec((npair * batch, LANES, LANES), lambda i: (0, 0, 0))
    s0 = jnp.swapaxes(s0_blk, 0, 1).reshape(npair * batch, LANES, LANES)
    kern = functools.partial(_rwkv_scan_kernel, chunk=chunk, passes=passes)
    y, s_t = pl.pallas_call(
        kern,
        grid=(seq // tb,),
        in_specs=[row] * 6 + [st],
        out_specs=[row, st],
        out_shape=[jax.ShapeDtypeStruct((batch, seq, w), F32),
                   jax.ShapeDtypeStruct((npair * batch, LANES, LANES), F32)],
        scratch_shapes=[pltpu.VMEM((npair * batch, LANES, LANES), F32)],
        compiler_params=_cparams(("arbitrary",)),
        name="rwkv_scan",
    )(*(a.reshape(batch, seq, w) for a in (r, lw, km, v, kn, bb)), s0)
    s_t = jnp.swapaxes(s_t.reshape(npair, batch, LANES, LANES), 0, 1)
    return y.reshape(n, w), s_t


def _outproj_kernel(x_ref, fox_ref, y_ref, bonus_ref, g_ref, lnw_ref, lnb_ref, wa_ref, wb_ref,
                    gf_ref, wqt_ref, keys_ref, x2_ref, xn_ref, sc_ref):
    y = y_ref[...]
    w = y.shape[1]
    em = _head_sum_matrix(w) * (1.0 / HEAD_DIM)
    mean = jnp.dot(y, em, precision=HIGHEST, preferred_element_type=F32)
    d = y - mean
    var = jnp.dot(d * d, em, precision=HIGHEST, preferred_element_type=F32)
    yn = d * lax.rsqrt(var + LNX_EPS) * lnw_ref[...] + lnb_ref[...]
    rw = (yn + bonus_ref[...]) * g_ref[...]
    mix = (jnp.dot(fox_ref[...].astype(BF16), wa_ref[...], preferred_element_type=F32)
           + jnp.dot(rw.astype(BF16), wb_ref[...], preferred_element_type=F32))
    x2 = x_ref[...] + mix
    xn = x2 * lax.rsqrt(jnp.mean(x2 * x2, axis=-1, keepdims=True) + NORM_EPS) * gf_ref[...]
    x2_ref[...] = x2
    xn_ref[...] = xn
    qt = lax.dot_general(wqt_ref[...], xn.astype(BF16), (((1,), (1,)), ((), ())),
                         preferred_element_type=F32)
    qh = keys_ref.shape[2]
    for hc in range(keys_ref.shape[0]):
        sc_ref[hc] = jnp.dot(keys_ref[hc], qt[hc * qh:(hc + 1) * qh, :].astype(BF16),
                             preferred_element_type=F32)


def _outproj(x2d, fox, y, bonus, g, lnx_w, lnx_b, w_out, g_ffn, w_q, sub_keys):
    n, d = x2d.shape
    w = y.shape[1]
    fw = fox.shape[1]
    wa = w_out[:fw].astype(BF16)
    wb = w_out[fw:].astype(BF16)
    wqt = w_q.T.astype(BF16)
    nkeys, qh = sub_keys.shape[-2:]
    keys = sub_keys.reshape(-1, nkeys, qh).astype(BF16)
    nhc = keys.shape[0]
    tm = _row_tile(n, 512)
    row = lambda c: pl.BlockSpec((tm, c), lambda i: (i, 0))
    full = lambda a: pl.BlockSpec(a.shape, lambda i: (0,) * a.ndim)
    consts = [lnx_w.reshape(1, w), lnx_b.reshape(1, w), wa, wb, g_ffn.reshape(1, d), wqt, keys]
    return pl.pallas_call(
        _outproj_kernel,
        grid=(n // tm,),
        in_specs=[row(d), row(fw), row(w), row(w), row(w)] + [full(c) for c in consts],
        out_specs=[row(d), row(d), pl.BlockSpec((nhc, nkeys, tm), lambda i: (0, 0, i))],
        out_shape=[jax.ShapeDtypeStruct((n, d), F32), jax.ShapeDtypeStruct((n, d), F32),
                   jax.ShapeDtypeStruct((nhc, nkeys, n), F32)],
        compiler_params=_cparams(("parallel",)),
        name="outproj_scores",
    )(x2d, fox, y, bonus, g, *consts)


def _topk_rows(s, payload, k):
    rows = lax.broadcasted_iota(jnp.int32, s.shape, 0)
    nrow = s.shape[0]
    vals, idxs, pays = [], [], []
    for _ in range(k):
        m = jnp.max(s, axis=0, keepdims=True)
        idx = jnp.min(jnp.where(s == m, rows, nrow), axis=0, keepdims=True)
        hit = rows == idx
        vals.append(m)
        idxs.append(idx)
        if payload is not None:
            pays.append(jnp.max(jnp.where(hit, payload, -1), axis=0, keepdims=True))
        s = jnp.where(hit, -jnp.inf, s)
    return vals, idxs, pays


def _retrieve_kernel(sc_ref, idx_ref, gate_ref, *, topk, nkeys):
    nhead = sc_ref.shape[0] // 2
    idx_rows, gate_rows = [], []
    for h in range(nhead):
        v1, i1, _ = _topk_rows(sc_ref[2 * h], None, topk)
        v2, i2, _ = _topk_rows(sc_ref[2 * h + 1], None, topk)
        v2a = jnp.concatenate(v2, axis=0)
        i2a = jnp.concatenate(i2, axis=0)
        cand = jnp.concatenate([v1[a] + v2a for a in range(topk)], axis=0)
        cidx = jnp.concatenate([i1[a] * nkeys + i2a for a in range(topk)], axis=0)
        top, _, eidx = _topk_rows(cand, cidx, topk)
        top = jnp.concatenate(top, axis=0)
        ex = jnp.exp(top - top[0:1])
        gate_rows.append(ex / jnp.sum(ex, axis=0, keepdims=True))
        idx_rows.extend(eidx)
    idx_ref[...] = (jnp.concatenate(idx_rows, axis=0) * HALF_TILE).T
    gate_ref[...] = jnp.concatenate(gate_rows, axis=0).T


def _retrieve(scores, topk):
    nhc, nkeys, n = scores.shape
    slots = (nhc // 2) * topk
    tt = _row_tile(n, 256)
    kern = functools.partial(_retrieve_kernel, topk=topk, nkeys=nkeys)
    return pl.pallas_call(
        kern,
        grid=(n // tt,),
        in_specs=[pl.BlockSpec((nhc, nkeys, tt), lambda i: (0, 0, i))],
        out_specs=[pl.BlockSpec((tt, slots), lambda i: (i, 0))] * 2,
        out_shape=[jax.ShapeDtypeStruct((n, slots), jnp.int32),
                   jax.ShapeDtypeStruct((n, slots), F32)],
        compiler_params=_cparams(("parallel",)),
        name="peer_retrieve",
    )(scores)


ROW_TILE = 8
HALF_TILE = ROW_TILE // 2


def _pack_table(t):
    e, d = t.shape
    assert d == ROW_TILE * LANES
    bits = lax.bitcast_convert_type(t.astype(BF16), jnp.uint16).astype(jnp.uint32)
    word = (bits[:, :d // 2] << 16) | bits[:, d // 2:]
    return lax.bitcast_convert_type(word, jnp.int32).reshape(e * HALF_TILE, LANES)


def _expert_halves(tab_ref, off):
    word = tab_ref[pl.ds(pl.multiple_of(off, HALF_TILE), HALF_TILE), :]
    first = lax.bitcast_convert_type(word & jnp.int32(-65536), F32)
    second = lax.bitcast_convert_type(jnp.left_shift(word, 16), F32)
    return first, second


def _fold_pairs(xs, span, axis):
    size = xs[0].shape[axis]
    pos = lax.broadcasted_iota(jnp.int32, xs[0].shape, axis)
    keep = (pos % (2 * span)) < span
    out = []
    for a, b in zip(xs[0::2], xs[1::2]):
        other = jnp.where(keep, b, a)
        if 2 * span == size:
            swapped = pltpu.roll(other, span, axis)
        else:
            swapped = jnp.where(keep, pltpu.roll(other, size - span, axis),
                                pltpu.roll(other, span, axis))
        out.append(jnp.where(keep, a, b) + swapped)
    return out


def _bit_reverse(i, bits):
    return int(format(i, "0%db" % bits)[::-1], 2)


def _sublane_sums(halves):
    lvl = [jnp.concatenate([a, b], axis=0) for a, b in zip(halves[0::2], halves[1::2])]
    lvl = _fold_pairs(lvl, 2, 0)
    return _fold_pairs(lvl, 1, 0)[0]


def _lane_sums_as_row(q):
    ones = jnp.ones((ROW_TILE, LANES), BF16)
    nt = lambda b: lax.dot_general(ones, b, (((1,), (1,)), ((), ())), preferred_element_type=F32)
    hi = q.astype(BF16)
    r1 = q - hi.astype(F32)
    mid = r1.astype(BF16)
    lo = (r1 - mid.astype(F32)).astype(BF16)
    return nt(hi) + nt(mid) + nt(lo)


def _peer_act_kernel(off_ref, x_ref, tab_ref, gate_ref, w_ref, part_ref, act_ref, *,
                     blocks_per_iter):
    tb, slots = gate_ref.shape
    sub_rev = [_bit_reverse(i, 3) for i in range(ROW_TILE)]
    nblk = slots // ROW_TILE

    def group(gi, carry):
        t0 = gi * ROW_TILE

        def token(tl, c1):
            t = t0 + tl
            x = x_ref[pl.ds(pl.multiple_of(t * ROW_TILE, ROW_TILE), ROW_TILE), :]
            xa, xb = x[:HALF_TILE], x[HALF_TILE:]

            def slot_blocks(jb, c2):
                for u in range(blocks_per_iter):
                    sb = jb * blocks_per_iter + u
                    prods = []
                    for i in range(ROW_TILE):
                        first, second = _expert_halves(
                            tab_ref, off_ref[t * slots + sb * ROW_TILE + sub_rev[i]])
                        prods.append(first * xa + second * xb)
                    part_ref[tl, pl.ds(pl.multiple_of(sb * ROW_TILE, ROW_TILE), ROW_TILE), :] = (
                        _sublane_sums(prods))
                return c2

            lax.fori_loop(0, nblk // blocks_per_iter, slot_blocks, 0)
            return c1

        lax.fori_loop(0, ROW_TILE, token, 0)
        sub = lax.broadcasted_iota(jnp.int32, (ROW_TILE, slots), 0)
        tile = jnp.zeros((ROW_TILE, slots), F32)
        for tl in range(ROW_TILE):
            tile = jnp.where(sub == tl, _lane_sums_as_row(part_ref[tl]), tile)
        act_ref[pl.ds(pl.multiple_of(t0, ROW_TILE), ROW_TILE), :] = tile
        return carry

    lax.fori_loop(0, tb // ROW_TILE, group, 0)
    act = act_ref[...]
    gelu = 0.5 * act * (1.0 + lax.erf(act * math.sqrt(0.5)))
    w_ref[...] = gate_ref[...] * gelu


def _peer_act(off, xn, tab, gate, *, tb):
    n, slots = gate.shape
    x8 = xn.reshape(n * ROW_TILE, LANES)
    kern = functools.partial(_peer_act_kernel, blocks_per_iter=2)
    return pl.pallas_call(
        kern,
        grid=(n // tb,),
        in_specs=[pl.BlockSpec((tb * slots,), lambda i: (i,), memory_space=pltpu.SMEM),
                  pl.BlockSpec((tb * ROW_TILE, LANES), lambda i: (i, 0)),
                  pl.BlockSpec(memory_space=pltpu.VMEM),
                  pl.BlockSpec((tb, slots), lambda i: (i, 0))],
        out_specs=pl.BlockSpec((tb, slots), lambda i: (i, 0)),
        out_shape=jax.ShapeDtypeStruct((n, slots), F32),
        scratch_shapes=[pltpu.VMEM((ROW_TILE, slots, LANES), F32),
                        pltpu.VMEM((tb, slots), F32)],
        compiler_params=_cparams(("arbitrary",)),
        name="peer_expert_act",
    )(off, x8, tab, gate)


def _peer_mix_kernel(off_ref, w_ref, tab_ref, x2_ref, gfin_ref, o_ref, *, slots, nacc,
                     slots_per_iter, final_norm):
    tb = x2_ref.shape[0] // ROW_TILE

    def token(t, carry):
        base = t * slots

        def chunk(c, accs):
            accs = list(accs)
            for u in range(slots_per_iter):
                k = base + c * slots_per_iter + u
                first, second = _expert_halves(tab_ref, off_ref[k])
                wk = w_ref[k]
                a = 2 * (u % nacc)
                accs[a] = accs[a] + first * wk
                accs[a + 1] = accs[a + 1] + second * wk
            return tuple(accs)

        zero = jnp.zeros((HALF_TILE, LANES), F32)
        accs = lax.fori_loop(0, slots // slots_per_iter, chunk, (zero,) * (2 * nacc))
        first, second = accs[0], accs[1]
        for a in range(1, nacc):
            first = first + accs[2 * a]
            second = second + accs[2 * a + 1]
        ff = jnp.concatenate([first, second], axis=0)
        rows = pl.ds(pl.multiple_of(t * ROW_TILE, ROW_TILE), ROW_TILE)
        o_ref[rows, :] = x2_ref[rows, :] + ff
        return carry

    lax.fori_loop(0, tb, token, 0)
    if final_norm:
        x3 = o_ref[...].reshape(tb, ROW_TILE, LANES)
        sq = jnp.sum(jnp.sum(x3 * x3, axis=2, keepdims=True), axis=1, keepdims=True)
        scale = lax.rsqrt(sq * (1.0 / (ROW_TILE * LANES)) + NORM_EPS)
        o_ref[...] = (x3 * scale * gfin_ref[...][None]).reshape(tb * ROW_TILE, LANES)


def _peer_mix(off, wgt, tab, x2, g_final, *, slots, tb, final_norm):
    n, d = x2.shape
    x8 = x2.reshape(n * ROW_TILE, LANES)
    g8 = g_final.reshape(ROW_TILE, LANES)
    kern = functools.partial(_peer_mix_kernel, slots=slots, nacc=4, slots_per_iter=16,
                             final_norm=final_norm)
    smem = lambda: pl.BlockSpec((tb * slots,), lambda i: (i,), memory_space=pltpu.SMEM)
    out = pl.pallas_call(
        kern,
        grid=(n // tb,),
        in_specs=[smem(), smem(), pl.BlockSpec(memory_space=pltpu.VMEM),
                  pl.BlockSpec((tb * ROW_TILE, LANES), lambda i: (i, 0)),
                  pl.BlockSpec((ROW_TILE, LANES), lambda i: (0, 0))],
        out_specs=pl.BlockSpec((tb * ROW_TILE, LANES), lambda i: (i, 0)),
        out_shape=jax.ShapeDtypeStruct((n * ROW_TILE, LANES), F32),
        compiler_params=_cparams(("arbitrary",)),
        name="peer_expert_mix",
    )(off, wgt, tab, x8, g8)
    return out.reshape(n, d)


def _state_to_blocks(s):
    b, h, d, _ = s.shape
    st = jnp.swapaxes(s, -1, -2).reshape(b, h // HEADS_PER_LANE_TILE, HEADS_PER_LANE_TILE, d, d)
    eye = jnp.eye(HEADS_PER_LANE_TILE, dtype=s.dtype)
    blk = st[:, :, :, :, None, :] * eye[None, None, :, None, :, None]
    return blk.reshape(b, h // HEADS_PER_LANE_TILE, LANES, LANES)


def _blocks_to_state(blk, heads):
    b, npair = blk.shape[:2]
    x = blk.reshape(b, npair, HEADS_PER_LANE_TILE, HEAD_DIM, HEADS_PER_LANE_TILE, HEAD_DIM)
    diag = jnp.stack([x[:, :, hh, :, hh, :] for hh in range(HEADS_PER_LANE_TILE)], axis=2)
    return jnp.swapaxes(diag.reshape(b, heads, HEAD_DIM, HEAD_DIM), -1, -2)


PEER_TOPK = 16


def _layer(x, k_past, v_past, lf_past, s0, shift0, lp, g_final, final_norm):
    (norm_mix_g, w_in, fox_b_f, mu, w0, w2, a0, a2, g2, k_k, k_a, r_k, lnx_w, lnx_b, w_out,
     norm_ffn_g, peer_w_q, peer_sub_keys, tab_u, tab_v) = lp
    b, t, d = x.shape
    n = b * t
    fox_heads = fox_b_f.shape[0]
    fw = fox_heads * HEAD_DIM
    fox_cols = 3 * fw + fox_heads
    rwkv_heads = r_k.shape[0]
    w = rwkv_heads * HEAD_DIM
    x2d = x.reshape(n, d)
    qb, k, v, kb, vb, lf, rw_main, rw_tail = _inproj(x2d, norm_mix_g, w_in, fox_b_f, fox_cols,
                                                     fox_heads, 3 * w)
    fox = _fox_stream(qb, kb, vb, lf, k_past, v_past, lf_past, batch=b, q_len=t)

    prm = _rwkv_params(mu, w0, w2, a0, a2, g2, k_k, k_a, r_k.reshape(-1))
    tail = prm["tail"]
    shift_main = shift0[..., :3 * w]
    shift_tail = jnp.pad(shift0[..., 3 * w:], ((0, 0), (0, 0), (0, TAIL_PAD - tail)))
    r, lw, km, vv, kn, bb, g, bonus = _rwkv_pre(rw_main, rw_tail, shift_main, shift_tail, prm,
                                                batch=b, seq=t)
    y, s_blk = _rwkv_scan(r, lw, km, vv, kn, bb, _state_to_blocks(s0), batch=b, seq=t)
    s_t = _blocks_to_state(s_blk, rwkv_heads)
    last = jnp.concatenate([rw_main.reshape(b, t, -1)[:, -1:], rw_tail.reshape(b, t, -1)[:, -1:, :tail]],
                           axis=-1)

    x2, xn, scores = _outproj(x2d, fox, y, bonus, g, lnx_w, lnx_b, w_out, norm_ffn_g, peer_w_q,
                              peer_sub_keys)
    idx, gate = _retrieve(scores, PEER_TOPK)
    tb = _row_tile(n, 128)
    slots = gate.shape[1]
    off = idx.reshape(n * slots)
    wgt = _peer_act(off, xn, tab_u, gate, tb=tb)
    out = _peer_mix(off, wgt.reshape(n * slots), tab_v, x2, g_final, slots=slots, tb=tb,
                    final_norm=final_norm)
    return (out.reshape(b, t, d), k.reshape(b, t, fox_heads, HEAD_DIM),
            v.reshape(b, t, fox_heads, HEAD_DIM), lf.reshape(b, t, fox_heads), s_t, last)


def kernel(x_prompt, x_sample, cache_fox_k, cache_fox_v, cache_fox_logf, state_rwkv, state_shift,
           norm_mix_g, w_in, fox_b_f, rwkv_mu, rwkv_w0, rwkv_w2, rwkv_a0, rwkv_a2, rwkv_g2,
           rwkv_k_k, rwkv_k_a, rwkv_r_k, rwkv_lnx_w, rwkv_lnx_b, w_out, norm_ffn_g,
           peer_w_q, peer_sub_keys, peer_u, peer_v, norm_final_g):
    depth = w_in.shape[0]
    yp, ys = x_prompt, x_sample
    bp = x_prompt.shape[0]
    dt = x_prompt.dtype
    fox_heads = fox_b_f.shape[1]
    rwkv_heads = rwkv_r_k.shape[1]
    rwkv_cols = rwkv_mu.shape[1]
    outs_p, outs_s = [], []
    for l in range(depth):
        lp = (norm_mix_g[l], w_in[l], fox_b_f[l], rwkv_mu[l], rwkv_w0[l], rwkv_w2[l], rwkv_a0[l],
              rwkv_a2[l], rwkv_g2[l], rwkv_k_k[l], rwkv_k_a[l], rwkv_r_k[l], rwkv_lnx_w[l],
              rwkv_lnx_b[l], w_out[l], norm_ffn_g[l], peer_w_q[l], peer_sub_keys[l],
              _pack_table(peer_u[l]), _pack_table(peer_v[l]))
        last = l == depth - 1
        empty_kv = jnp.zeros((bp, 0, fox_heads, HEAD_DIM), dt)
        empty_lf = jnp.zeros((bp, 0, fox_heads), dt)
        s_zero = jnp.zeros((bp, rwkv_heads, HEAD_DIM, HEAD_DIM), dt)
        sh_zero = jnp.zeros((bp, 1, rwkv_cols), dt)
        yp, *rest_p = _layer(yp, empty_kv, empty_kv, empty_lf, s_zero, sh_zero, lp, norm_final_g, last)
        ys, *rest_s = _layer(ys, cache_fox_k[l], cache_fox_v[l], cache_fox_logf[l], state_rwkv[l],
                             state_shift[l], lp, norm_final_g, last)
        outs_p.append(rest_p)
        outs_s.append(rest_s)
    stack = lambda outs, i: jnp.stack([o[i] for o in outs])
    return ((yp, ys) + tuple(stack(outs_p, i) for i in range(5))
            + tuple(stack(outs_s, i) for i in range(5)))
```

```python
import functools
import math

import jax
import jax.numpy as jnp
from jax import lax
from jax.experimental import pallas as pl
from jax.experimental.pallas import tpu as pltpu

F32 = jnp.float32
BF16 = jnp.bfloat16

HEAD_DIM = 64
LANES = 128
HEADS_PER_LANE_TILE = LANES // HEAD_DIM
TAIL_PAD = 2 * LANES
CHUNK = 64
RWKV_PASSES = 1
NORM_EPS = 1e-6
LNX_EPS = 64e-5
NEG_BIG = -1e30
LOG2E = math.log2(math.e)
HIGHEST = lax.Precision.HIGHEST
VMEM_LIMIT = 48 * 1024 * 1024


def _cparams(sem):
    return pltpu.CompilerParams(dimension_semantics=sem, vmem_limit_bytes=VMEM_LIMIT)


def _row_tile(n, target):
    t = min(n, target)
    assert n % t == 0, (n, t)
    return t


def _inproj_kernel(x_ref, g_ref, wqkv_ref, wf_ref, wrw_ref, wtail_ref, bf_ref,
                   q_ref, k_ref, v_ref, kb_ref, vb_ref, lf_ref, rw_ref, tail_ref):
    x = x_ref[...]
    h = x * lax.rsqrt(jnp.mean(x * x, axis=-1, keepdims=True) + NORM_EPS) * g_ref[...]
    hb = h.astype(BF16)
    fw = wqkv_ref.shape[1] // 3
    qkv = jnp.dot(hb, wqkv_ref[...], preferred_element_type=F32)
    q_ref[...] = (qkv[:, :fw] * (LOG2E / math.sqrt(HEAD_DIM))).astype(BF16)
    k = qkv[:, fw:2 * fw]
    v = qkv[:, 2 * fw:]
    k_ref[...] = k
    v_ref[...] = v
    kb_ref[...] = k.astype(BF16)
    vb_ref[...] = v.astype(BF16)
    f = jnp.dot(hb, wf_ref[...], preferred_element_type=F32) + bf_ref[...]
    lf_ref[...] = jax.nn.log_sigmoid(f)
    rw_ref[...] = jnp.dot(hb, wrw_ref[...], preferred_element_type=F32)
    tail_ref[...] = jnp.dot(hb, wtail_ref[...], preferred_element_type=F32)


def _inproj(x2d, g, w_in, b_f, fox_cols, fox_heads, rw_main):
    n, d = x2d.shape
    fw = fox_heads * HEAD_DIM
    wqkv = w_in[:, :3 * fw].astype(BF16)
    wf = w_in[:, 3 * fw:fox_cols].astype(BF16)
    wrw = w_in[:, fox_cols:fox_cols + rw_main].astype(BF16)
    wtail = w_in[:, fox_cols + rw_main:].astype(BF16)
    wtail = jnp.pad(wtail, ((0, 0), (0, TAIL_PAD - wtail.shape[1])))
    tm = _row_tile(n, 512)
    row = lambda c: pl.BlockSpec((tm, c), lambda i: (i, 0))
    full = lambda a: pl.BlockSpec(a.shape, lambda i: (0,) * a.ndim)
    g2 = g.reshape(1, d)
    bf2 = b_f.reshape(1, fox_heads)
    outs = (
        jax.ShapeDtypeStruct((n, fw), BF16),
        jax.ShapeDtypeStruct((n, fw), F32),
        jax.ShapeDtypeStruct((n, fw), F32),
        jax.ShapeDtypeStruct((n, fw), BF16),
        jax.ShapeDtypeStruct((n, fw), BF16),
        jax.ShapeDtypeStruct((n, fox_heads), F32),
        jax.ShapeDtypeStruct((n, rw_main), F32),
        jax.ShapeDtypeStruct((n, TAIL_PAD), F32),
    )
    return pl.pallas_call(
        _inproj_kernel,
        grid=(n // tm,),
        in_specs=[row(d), full(g2), full(wqkv), full(wf), full(wrw), full(wtail), full(bf2)],
        out_specs=[row(fw), row(fw), row(fw), row(fw), row(fw), row(fox_heads), row(rw_main),
                   row(TAIL_PAD)],
        out_shape=outs,
        compiler_params=_cparams(("parallel",)),
        name="inproj",
    )(x2d, g2, wqkv, wf, wrw, wtail, bf2)


def _cumsum_kernel(lf_ref, c_ref, carry):
    @pl.when(pl.program_id(1) == 0)
    def _():
        carry[...] = jnp.zeros_like(carry)

    lf = lf_ref[...]
    tc = lf.shape[0]
    r = lax.broadcasted_iota(jnp.int32, (tc, tc), 0)
    c = lax.broadcasted_iota(jnp.int32, (tc, tc), 1)
    lower = (c <= r).astype(F32)
    cc = jnp.dot(lower, lf, precision=HIGHEST, preferred_element_type=F32) + carry[...]
    c_ref[...] = cc
    carry[...] = cc[tc - 1:tc, :]


def _cumsum(lf, tc):
    b, l, nh = lf.shape
    assert l % tc == 0
    return pl.pallas_call(
        _cumsum_kernel,
        grid=(b, l // tc),
        in_specs=[pl.BlockSpec((None, tc, nh), lambda i, j: (i, j, 0))],
        out_specs=pl.BlockSpec((None, tc, nh), lambda i, j: (i, j, 0)),
        out_shape=jax.ShapeDtypeStruct((b, l, nh), F32),
        scratch_shapes=[pltpu.VMEM((1, nh), F32)],
        compiler_params=_cparams(("parallel", "arbitrary")),
        name="cumsum_logf",
    )(lf)


def _split3(x):
    hi = x.astype(BF16)
    r = x - hi.astype(F32)
    mid = r.astype(BF16)
    lo = (r - mid.astype(F32)).astype(BF16)
    return hi.astype(F32), mid.astype(F32), lo.astype(F32)


def _augment_kernel(x_ref, c_ref, o_ref, *, role):
    tm = x_ref.shape[0]
    lane = lax.broadcasted_iota(jnp.int32, (tm, LANES), 1)
    for p in range(x_ref.shape[1] // LANES):
        xp = x_ref[:, p * LANES:(p + 1) * LANES].astype(F32)
        for hh in range(HEADS_PER_LANE_TILE):
            h = p * HEADS_PER_LANE_TILE + hh
            own = (lane >= hh * HEAD_DIM) & (lane < (hh + 1) * HEAD_DIM)
            e = (lane + (1 - hh) * HEAD_DIM) % LANES
            if role == "v":
                ext = jnp.where(e == 0, 1.0, 0.0)
            else:
                c = jnp.broadcast_to(c_ref[:, h:h + 1], (tm, LANES))
                hi, mid, lo = _split3(c * LOG2E)
                sgn = 1.0 if role == "q" else -1.0
                base = 0 if role == "q" else 3
                ext = jnp.where(e == base, sgn * hi,
                                jnp.where(e == base + 1, sgn * mid,
                                          jnp.where(e == base + 2, sgn * lo,
                                                    jnp.where(e < 6, 1.0, 0.0))))
            o_ref[:, h * LANES:(h + 1) * LANES] = jnp.where(own, xp, ext).astype(BF16)


def _augment(x, c, role):
    n, w = x.shape
    nh = w // HEAD_DIM
    tm = _row_tile(n, 512) if n % 512 == 0 else n
    kern = functools.partial(_augment_kernel, role=role)
    return pl.pallas_call(
        kern,
        grid=(n // tm,),
        in_specs=[pl.BlockSpec((tm, w), lambda i: (i, 0)),
                  pl.BlockSpec((tm, nh), lambda i: (i, 0))],
        out_specs=pl.BlockSpec((tm, nh * LANES), lambda i: (i, 0)),
        out_shape=jax.ShapeDtypeStruct((n, nh * LANES), BF16),
        compiler_params=_cparams(("parallel",)),
        name="fox_augment_" + role,
    )(x, c)


Q_SUB = 128
K_SUB = 256


def _fox_kernel(q_ref, k_ref, v_ref, o_ref, m_sc, acc_sc, *, q_off, tq, tk):
    i = pl.program_id(2)
    j = pl.program_id(3)
    nk = pl.num_programs(3)
    qs_n, ks_n = min(Q_SUB, tq), min(K_SUB, tk)

    @pl.when(j == 0)
    def _():
        m_sc[...] = jnp.full_like(m_sc, NEG_BIG)
        acc_sc[...] = jnp.zeros_like(acc_sc)

    q_lo = q_off + i * tq
    k_lo = j * tk

    def body(masked):
        if masked:
            diff = (lax.broadcasted_iota(jnp.int32, (qs_n, LANES), 1)
                    - lax.broadcasted_iota(jnp.int32, (qs_n, LANES), 0))
        for hh in range(HEADS_PER_LANE_TILE):
            cols = slice(hh * LANES, (hh + 1) * LANES)
            for ks in range(tk // ks_n):
                krows = slice(ks * ks_n, (ks + 1) * ks_n)
                k_sub = k_ref[krows, cols]
                v_sub = v_ref[krows, cols]
                for qs in range(tq // qs_n):
                    qrows = slice(qs * qs_n, (qs + 1) * qs_n)
                    s = lax.dot_general(q_ref[qrows, cols], k_sub, (((1,), (1,)), ((), ())),
                                        preferred_element_type=F32)
                    parts = [s[:, c * LANES:(c + 1) * LANES] for c in range(ks_n // LANES)]
                    if masked:
                        parts = [jnp.where(diff <= q_lo - k_lo + qs * qs_n - ks * ks_n - c * LANES,
                                           pc, NEG_BIG) for c, pc in enumerate(parts)]
                    mx = parts[0]
                    for pc in parts[1:]:
                        mx = jnp.maximum(mx, pc)
                    m_old = m_sc[hh, qrows, :]
                    m_new = jnp.maximum(m_old, jnp.max(mx, axis=-1, keepdims=True))
                    alpha = jnp.exp2(m_old - m_new)
                    pr = jnp.concatenate([jnp.exp2(pc - m_new).astype(BF16) for pc in parts], axis=1)
                    pv = jnp.dot(pr, v_sub, preferred_element_type=F32)
                    acc_sc[hh, qrows, :] = alpha * acc_sc[hh, qrows, :] + pv
                    m_sc[hh, qrows, :] = m_new

    fully_visible = k_lo + tk - 1 <= q_lo
    any_visible = k_lo <= q_lo + tq - 1

    @pl.when(fully_visible)
    def _():
        body(False)

    @pl.when(jnp.logical_and(any_visible, jnp.logical_not(fully_visible)))
    def _():
        body(True)

    @pl.when(j == nk - 1)
    def _():
        lane = lax.broadcasted_iota(jnp.int32, (1, LANES), 1)
        out = jnp.zeros((tq, LANES), F32)
        for hh in range(HEADS_PER_LANE_TILE):
            in_head = (lane >= hh * HEAD_DIM) & (lane < (hh + 1) * HEAD_DIM)
            acc = acc_sc[hh]
            ones_col = (1 - hh) * HEAD_DIM
            denom = jnp.broadcast_to(acc[:, ones_col:ones_col + 1], acc.shape)
            out = jnp.where(in_head, acc / denom, out)
        o_ref[...] = out


def _fox_attend(q_aug, k_aug, v_aug, *, batch, q_len, kv_len, q_off, tq, tk):
    n, wa = q_aug.shape
    pair_w = HEADS_PER_LANE_TILE * LANES
    npair = wa // pair_w
    nq, nk = q_len // tq, kv_len // tk
    assert q_len % tq == 0 and kv_len % tk == 0

    def last_kv(i):
        return (q_off + (i + 1) * tq - 1) // tk

    kv_map = lambda b, p, i, j: (b * nk + jnp.minimum(j, last_kv(i)), p)
    kern = functools.partial(_fox_kernel, q_off=q_off, tq=tq, tk=tk)
    return pl.pallas_call(
        kern,
        grid=(batch, npair, nq, nk),
        in_specs=[
            pl.BlockSpec((tq, pair_w), lambda b, p, i, j: (b * nq + i, p)),
            pl.BlockSpec((tk, pair_w), kv_map),
            pl.BlockSpec((tk, pair_w), kv_map),
        ],
        out_specs=pl.BlockSpec((tq, LANES), lambda b, p, i, j: (b * nq + i, p)),
        out_shape=jax.ShapeDtypeStruct((n, npair * LANES), F32),
        scratch_shapes=[pltpu.VMEM((HEADS_PER_LANE_TILE, tq, LANES), F32),
                        pltpu.VMEM((HEADS_PER_LANE_TILE, tq, LANES), F32)],
        compiler_params=_cparams(("parallel", "parallel", "parallel", "arbitrary")),
        name="fox_attention",
    )(q_aug, k_aug, v_aug)


def _fox_stream(qb, kb, vb, lf, k_past, v_past, lf_past, *, batch, q_len):
    n, w = qb.shape
    nh = lf.shape[1]
    past = k_past.shape[1]
    lf_new = lf.reshape(batch, q_len, nh)
    if past == 0:
        kv_len = q_len
        k_all, v_all, lf_all = kb, vb, lf_new
        tq = tk = _row_tile(q_len, 512)
        tc = tk
    else:
        kv_len = -(-(past + q_len) // K_SUB) * K_SUB
        pad = kv_len - past - q_len

        def cat(old, new):
            old = old.reshape(batch, past, -1).astype(new.dtype)
            new = new.reshape(batch, q_len, -1)
            z = jnp.zeros((batch, pad, new.shape[-1]), new.dtype)
            return jnp.concatenate([old, new, z], axis=1)

        k_all = cat(k_past, kb).reshape(batch * kv_len, w)
        v_all = cat(v_past, vb).reshape(batch * kv_len, w)
        lf_all = cat(lf_past, lf_new)
        tq, tk, tc = q_len, kv_len, LANES
    c = _cumsum(lf_all, tc)
    c_k = c.reshape(batch * kv_len, nh)
    c_q = c[:, past:past + q_len].reshape(n, nh)
    return _fox_attend(_augment(qb, c_q, "q"), _augment(k_all, c_k, "k"), _augment(v_all, c_k, "v"),
                       batch=batch, q_len=q_len, kv_len=kv_len, q_off=past, tq=tq, tk=tk)


def _head_sum_matrix(width):
    r = lax.broadcasted_iota(jnp.int32, (width, width), 0) // HEAD_DIM
    c = lax.broadcasted_iota(jnp.int32, (width, width), 1) // HEAD_DIM
    return (r == c).astype(F32)


def _rwkv_pre_kernel(pm_ref, pt_ref, sm_ref, st_ref, mum_ref, mut_ref, wbig_ref, w0_ref, a0_ref,
                     kk_ref, ka_ref, rk_ref,
                     r_out, lw_out, km_out, v_out, kn_out, b_out, g_out, bonus_out,
                     carry_m, carry_t, *, lora_w, lora_a):
    @pl.when(pl.program_id(1) == 0)
    def _():
        carry_m[...] = sm_ref[...]
        carry_t[...] = st_ref[...]

    pm = pm_ref[...]
    pt = pt_ref[...]
    tm = pm.shape[0]
    w = pm.shape[1] // 3

    def shifted(p, carry):
        row = lax.broadcasted_iota(jnp.int32, p.shape, 0)
        return jnp.where(row == 0, carry[...], pltpu.roll(p, 1, 0))

    prev_m = shifted(pm, carry_m)
    prev_t = shifted(pt, carry_t)
    carry_m[...] = pm[tm - 1:tm, :]
    carry_t[...] = pt[tm - 1:tm, :]
    psm = pm + mum_ref[...] * (prev_m - pm)
    pst = pt + mut_ref[...] * (prev_t - pt)
    r = psm[:, :w]
    k = psm[:, w:2 * w]
    v = psm[:, 2 * w:]
    lane = lax.broadcasted_iota(jnp.int32, pst.shape, 1)
    z = jnp.where(lane < lora_w, jnp.tanh(pst),
                  jnp.where(lane < lora_w + lora_a, pst, jax.nn.sigmoid(pst)))
    lo = jnp.dot(z.astype(BF16), wbig_ref[...], preferred_element_type=F32)
    w_log = -jax.nn.softplus(-(w0_ref[...] + lo[:, :w])) - 0.5
    lw = -jnp.exp(w_log)
    a = jax.nn.sigmoid(a0_ref[...] + lo[:, w:2 * w])
    g = lo[:, 2 * w:]
    e = _head_sum_matrix(w)
    kk0 = k * kk_ref[...]
    n2 = jnp.dot(kk0 * kk0, e, precision=HIGHEST, preferred_element_type=F32)
    kn = kk0 / jnp.maximum(jnp.sqrt(n2), 1e-12)
    km = k * (1.0 + (a - 1.0) * ka_ref[...])
    rk = jnp.dot(r * km * rk_ref[...], e, precision=HIGHEST, preferred_element_type=F32)
    r_out[...] = r
    lw_out[...] = lw
    km_out[...] = km
    v_out[...] = v
    kn_out[...] = kn
    b_out[...] = kn * a
    g_out[...] = g
    bonus_out[...] = rk * v


def _rwkv_params(mu, w0, w2, a0, a2, g2, k_k, k_a, r_k):
    w = w0.shape[0]
    lora_w, lora_a, lora_g = w2.shape[0], a2.shape[0], g2.shape[0]
    w_lora = jnp.zeros((TAIL_PAD, 3 * w), F32)
    w_lora = w_lora.at[:lora_w, :w].set(w2)
    w_lora = w_lora.at[lora_w:lora_w + lora_a, w:2 * w].set(a2)
    w_lora = w_lora.at[lora_w + lora_a:lora_w + lora_a + lora_g, 2 * w:].set(g2)
    tail = mu.shape[0] - 3 * w
    return dict(
        mu_main=mu[:3 * w].reshape(1, 3 * w),
        mu_tail=jnp.pad(mu[3 * w:], (0, TAIL_PAD - tail)).reshape(1, TAIL_PAD),
        w_lora=w_lora.astype(BF16), w0=w0.reshape(1, w), a0=a0.reshape(1, w),
        k_k=k_k.reshape(1, w), k_a=k_a.reshape(1, w), r_k=r_k.reshape(1, w),
        lora_w=lora_w, lora_a=lora_a, tail=tail)


def _rwkv_pre(rw_main, rw_tail, shift_main, shift_tail, prm, *, batch, seq):
    n, w3 = rw_main.shape
    w = w3 // 3
    tm = _row_tile(seq, 512)
    nt = seq // tm
    row = lambda c: pl.BlockSpec((tm, c), lambda b, i: (b * nt + i, 0))
    per_b = lambda c: pl.BlockSpec((None, 1, c), lambda b, i: (b, 0, 0))
    full = lambda a: pl.BlockSpec(a.shape, lambda b, i: (0,) * a.ndim)
    consts = [prm["mu_main"], prm["mu_tail"], prm["w_lora"], prm["w0"], prm["a0"], prm["k_k"],
              prm["k_a"], prm["r_k"]]
    kern = functools.partial(_rwkv_pre_kernel, lora_w=prm["lora_w"], lora_a=prm["lora_a"])
    return pl.pallas_call(
        kern,
        grid=(batch, nt),
        in_specs=[row(w3), row(TAIL_PAD), per_b(w3), per_b(TAIL_PAD)] + [full(c) for c in consts],
        out_specs=[row(w)] * 8,
        out_shape=[jax.ShapeDtypeStruct((n, w), F32)] * 8,
        scratch_shapes=[pltpu.VMEM((1, w3), F32), pltpu.VMEM((1, TAIL_PAD), F32)],
        compiler_params=_cparams(("parallel", "arbitrary")),
        name="rwkv_pre",
    )(rw_main, rw_tail, shift_main, shift_tail, *consts)


def _bmm(a, b, kind, passes):
    contract = {"nn": ((2,), (1,)), "nt": ((2,), (2,)), "tn": ((1,), (1,))}[kind]
    dims = (contract, ((0,), (0,)))
    if passes == 6:
        return lax.dot_general(a, b, dims, precision=HIGHEST, preferred_element_type=F32)
    dg = lambda x, y: lax.dot_general(x, y, dims, preferred_element_type=F32)
    ah, bh = a.astype(BF16), b.astype(BF16)
    out = dg(ah, bh)
    if passes == 3:
        al = (a - ah.astype(F32)).astype(BF16)
        bl = (b - bh.astype(F32)).astype(BF16)
        out = out + dg(ah, bl) + dg(al, bh)
    return out


def _rwkv_chunk(r, lw, km, v, kn, bb, s_blk, passes):
    g, c, _ = r.shape
    c2 = HEADS_PER_LANE_TILE * c
    ti = lax.broadcasted_iota(jnp.int32, (g, c, c), 1)
    si = lax.broadcasted_iota(jnp.int32, (g, c, c), 2)
    cs = _bmm((si <= ti).astype(F32), lw, "nn", 6)
    e_pos = jnp.exp(cs)
    e_neg = jnp.exp(-cs)
    kt = kn * jnp.exp(cs - lw)
    bt = bb * e_neg
    kh = km * e_neg
    rt = r * e_pos
    g_end = e_pos[:, c - 1:c, :]

    lane = lax.broadcasted_iota(jnp.int32, (1, 1, LANES), 2)
    head_of_lane = lane // HEAD_DIM

    def stack_masked(x):
        return jnp.concatenate(
            [jnp.where(head_of_lane == hh, x, 0.0) for hh in range(HEADS_PER_LANE_TILE)], axis=1)

    def stack(x):
        return jnp.concatenate([x] * HEADS_PER_LANE_TILE, axis=1)

    def pick(x):
        out = x[:, :c]
        for hh in range(1, HEADS_PER_LANE_TILE):
            out = jnp.where(head_of_lane == hh, x[:, hh * c:(hh + 1) * c], out)
        return out

    kt2 = stack_masked(kt)
    rt2 = stack_masked(rt)
    rr = lax.broadcasted_iota(jnp.int32, (1, c2, c2), 1)
    cc = lax.broadcasted_iota(jnp.int32, (1, c2, c2), 2)
    strict_blk = (rr // c == cc // c) & (cc < rr)
    x = jnp.where(strict_blk, -_bmm(kt2, stack(bt), "nt", passes), 0.0)
    tinv = (rr == cc).astype(F32) + x
    steps = max(int(math.ceil(math.log2(c))) - 1, 0)
    for _ in range(steps):
        x = _bmm(x, x, "nn", passes)
        tinv = tinv + _bmm(tinv, x, "nn", passes)
    tr = lax.broadcasted_iota(jnp.int32, (1, c2, c), 1) % c
    sr = lax.broadcasted_iota(jnp.int32, (1, c2, c), 2)
    kk_s = jnp.where(sr < tr, _bmm(kt2, kh, "nt", passes), 0.0)
    rb_s = jnp.where(sr <= tr, _bmm(rt2, bt, "nt", passes), 0.0)
    rk_s = jnp.where(sr <= tr, _bmm(rt2, kh, "nt", passes), 0.0)

    ks = _bmm(jnp.concatenate([kt, rt], axis=1), s_blk, "nn", passes)
    rhs = ks[:, :c] + pick(_bmm(kk_s, v, "nn", passes))
    z = pick(_bmm(tinv, stack(rhs), "nn", passes))
    y = ks[:, c:] - pick(_bmm(rb_s, z, "nn", passes)) + pick(_bmm(rk_s, v, "nn", passes))
    jr = lax.broadcasted_iota(jnp.int32, (1, LANES, LANES), 1)
    ic = lax.broadcasted_iota(jnp.int32, (1, LANES, LANES), 2)
    decay_rows = jnp.swapaxes(jnp.broadcast_to(g_end, (g, LANES, LANES)), 1, 2)
    upd = _bmm(jnp.concatenate([bt * g_end, kh * g_end], axis=1),
               jnp.concatenate([-z, v], axis=1), "tn", passes)
    s_new = decay_rows * s_blk + jnp.where(jr // HEAD_DIM == ic // HEAD_DIM, upd, 0.0)
    return y, s_new


def _rwkv_scan_kernel(r_ref, lw_ref, km_ref, v_ref, kn_ref, b_ref, s0_ref, y_ref, sT_ref, s_sc,
                      *, chunk, passes):
    it = pl.program_id(0)
    nb, tb, w = r_ref.shape
    npair = w // LANES

    @pl.when(it == 0)
    def _():
        s_sc[...] = s0_ref[...]

    def step(ci, carry):
        rows = pl.ds(pl.multiple_of(ci * chunk, chunk), chunk)

        def gather(ref):
            blk = ref[:, rows, :]
            return jnp.concatenate([blk[:, :, p * LANES:(p + 1) * LANES] for p in range(npair)],
                                   axis=0)

        y, s_new = _rwkv_chunk(gather(r_ref), gather(lw_ref), gather(km_ref), gather(v_ref),
                               gather(kn_ref), gather(b_ref), s_sc[...], passes)
        for p in range(npair):
            y_ref[:, rows, p * LANES:(p + 1) * LANES] = y[p * nb:(p + 1) * nb]
        s_sc[...] = s_new
        return carry

    lax.fori_loop(0, tb // chunk, step, 0)

    @pl.when(it == pl.num_programs(0) - 1)
    def _():
        sT_ref[...] = s_sc[...]


def _rwkv_scan(r, lw, km, v, kn, bb, s0_blk, *, batch, seq, passes=RWKV_PASSES):
    n, w = r.shape
    npair = w // LANES
    chunk = min(CHUNK, seq)
    tb = _row_tile(seq, 4 * chunk)
    row = pl.BlockSpec((batch, tb, w), lambda i: (0, i, 0))
    st = pl.BlockSpec((npair * batch, LANES, LANES), lambda i: (0, 0, 0))
    s0 = jnp.swapaxes(s0_blk, 0, 1).reshape(npair * batch, LANES, LANES)
    kern = functools.partial(_rwkv_scan_kernel, chunk=chunk, passes=passes)
    y, s_t = pl.pallas_call(
        kern,
        grid=(seq // tb,),
        in_specs=[row] * 6 + [st],
        out_specs=[row, st],
        out_shape=[jax.ShapeDtypeStruct((batch, seq, w), F32),
                   jax.ShapeDtypeStruct((npair * batch, LANES, LANES), F32)],
        scratch_shapes=[pltpu.VMEM((npair * batch, LANES, LANES), F32)],
        compiler_params=_cparams(("arbitrary",)),
        name="rwkv_scan",
    )(*(a.reshape(batch, seq, w) for a in (r, lw, km, v, kn, bb)), s0)
    s_t = jnp.swapaxes(s_t.reshape(npair, batch, LANES, LANES), 0, 1)
    return y.reshape(n, w), s_t


def _outproj_kernel(x_ref, fox_ref, y_ref, bonus_ref, g_ref, lnw_ref, lnb_ref, wa_ref, wb_ref,
                    gf_ref, wqt_ref, keys_ref, x2_ref, xn_ref, sc_ref):
    y = y_ref[...]
    w = y.shape[1]
    em = _head_sum_matrix(w) * (1.0 / HEAD_DIM)
    mean = jnp.dot(y, em, precision=HIGHEST, preferred_element_type=F32)
    d = y - mean
    var = jnp.dot(d * d, em, precision=HIGHEST, preferred_element_type=F32)
    yn = d * lax.rsqrt(var + LNX_EPS) * lnw_ref[...] + lnb_ref[...]
    rw = (yn + bonus_ref[...]) * g_ref[...]
    mix = (jnp.dot(fox_ref[...].astype(BF16), wa_ref[...], preferred_element_type=F32)
           + jnp.dot(rw.astype(BF16), wb_ref[...], preferred_element_type=F32))
    x2 = x_ref[...] + mix
    xn = x2 * lax.rsqrt(jnp.mean(x2 * x2, axis=-1, keepdims=True) + NORM_EPS) * gf_ref[...]
    x2_ref[...] = x2
    xn_ref[...] = xn
    qt = lax.dot_general(wqt_ref[...], xn.astype(BF16), (((1,), (1,)), ((), ())),
                         preferred_element_type=F32)
    qh = keys_ref.shape[2]
    for hc in range(keys_ref.shape[0]):
        sc_ref[hc] = jnp.dot(keys_ref[hc], qt[hc * qh:(hc + 1) * qh, :].astype(BF16),
                             preferred_element_type=F32)


def _outproj(x2d, fox, y, bonus, g, lnx_w, lnx_b, w_out, g_ffn, w_q, sub_keys):
    n, d = x2d.shape
    w = y.shape[1]
    fw = fox.shape[1]
    wa = w_out[:fw].astype(BF16)
    wb = w_out[fw:].astype(BF16)
    wqt = w_q.T.astype(BF16)
    nkeys, qh = sub_keys.shape[-2:]
    keys = sub_keys.reshape(-1, nkeys, qh).astype(BF16)
    nhc = keys.shape[0]
    tm = _row_tile(n, 512)
    row = lambda c: pl.BlockSpec((tm, c), lambda i: (i, 0))
    full = lambda a: pl.BlockSpec(a.shape, lambda i: (0,) * a.ndim)
    consts = [lnx_w.reshape(1, w), lnx_b.reshape(1, w), wa, wb, g_ffn.reshape(1, d), wqt, keys]
    return pl.pallas_call(
        _outproj_kernel,
        grid=(n // tm,),
        in_specs=[row(d), row(fw), row(w), row(w), row(w)] + [full(c) for c in consts],
        out_specs=[row(d), row(d), pl.BlockSpec((nhc, nkeys, tm), lambda i: (0, 0, i))],
        out_shape=[jax.ShapeDtypeStruct((n, d), F32), jax.ShapeDtypeStruct((n, d), F32),
                   jax.ShapeDtypeStruct((nhc, nkeys, n), F32)],
        compiler_params=_cparams(("parallel",)),
        name="outproj_scores",
    )(x2d, fox, y, bonus, g, *consts)


def _topk_rows(s, payload, k):
    rows = lax.broadcasted_iota(jnp.int32, s.shape, 0)
    nrow = s.shape[0]
    vals, idxs, pays = [], [], []
    for _ in range(k):
        m = jnp.max(s, axis=0, keepdims=True)
        idx = jnp.min(jnp.where(s == m, rows, nrow), axis=0, keepdims=True)
        hit = rows == idx
        vals.append(m)
        idxs.append(idx)
        if payload is not None:
            pays.append(jnp.max(jnp.where(hit, payload, -1), axis=0, keepdims=True))
        s = jnp.where(hit, -jnp.inf, s)
    return vals, idxs, pays


def _retrieve_kernel(sc_ref, idx_ref, gate_ref, *, topk, nkeys):
    nhead = sc_ref.shape[0] // 2
    idx_rows, gate_rows = [], []
    for h in range(nhead):
        v1, i1, _ = _topk_rows(sc_ref[2 * h], None, topk)
        v2, i2, _ = _topk_rows(sc_ref[2 * h + 1], None, topk)
        v2a = jnp.concatenate(v2, axis=0)
        i2a = jnp.concatenate(i2, axis=0)
        cand = jnp.concatenate([v1[a] + v2a for a in range(topk)], axis=0)
        cidx = jnp.concatenate([i1[a] * nkeys + i2a for a in range(topk)], axis=0)
        top, _, eidx = _topk_rows(cand, cidx, topk)
        top = jnp.concatenate(top, axis=0)
        ex = jnp.exp(top - top[0:1])
        gate_rows.append(ex / jnp.sum(ex, axis=0, keepdims=True))
        idx_rows.extend(eidx)
    idx_ref[...] = (jnp.concatenate(idx_rows, axis=0) * HALF_TILE).T
    gate_ref[...] = jnp.concatenate(gate_rows, axis=0).T


def _retrieve(scores, topk):
    nhc, nkeys, n = scores.shape
    slots = (nhc // 2) * topk
    tt = _row_tile(n, 256)
    kern = functools.partial(_retrieve_kernel, topk=topk, nkeys=nkeys)
    return pl.pallas_call(
        kern,
        grid=(n // tt,),
        in_specs=[pl.BlockSpec((nhc, nkeys, tt), lambda i: (0, 0, i))],
        out_specs=[pl.BlockSpec((tt, slots), lambda i: (i, 0))] * 2,
        out_shape=[jax.ShapeDtypeStruct((n, slots), jnp.int32),
                   jax.ShapeDtypeStruct((n, slots), F32)],
        compiler_params=_cparams(("parallel",)),
        name="peer_retrieve",
    )(scores)


ROW_TILE = 8
HALF_TILE = ROW_TILE // 2


def _pack_table(t):
    e, d = t.shape
    assert d == ROW_TILE * LANES
    bits = lax.bitcast_convert_type(t.astype(BF16), jnp.uint16).astype(jnp.uint32)
    bits = bits.reshape(e, HALF_TILE, 2, LANES)
    word = bits[:, :, 0, :] | (bits[:, :, 1, :] << 16)
    return lax.bitcast_convert_type(word, jnp.int32).reshape(e * HALF_TILE, LANES)


def _gather_rows(tab_ref, off_ref, base, stack_ref, slots):
    for j in range(slots):
        off = pl.multiple_of(off_ref[base + j], HALF_TILE)
        stack_ref[j * HALF_TILE:(j + 1) * HALF_TILE, :] = tab_ref[pl.ds(off, HALF_TILE), :]


def _diag_mask(slots):
    shape = (ROW_TILE, slots * ROW_TILE)
    return (lax.broadcasted_iota(jnp.int32, shape, 1) % ROW_TILE
            == lax.broadcasted_iota(jnp.int32, shape, 0))


def _split2(x):
    hi = x.astype(BF16)
    return hi, (x - hi.astype(F32)).astype(BF16)


def _dot3(x, w01):
    hi = x.astype(BF16)
    r1 = x - hi.astype(F32)
    mid = r1.astype(BF16)
    lo = (r1 - mid.astype(F32)).astype(BF16)
    d = lambda a: jnp.dot(a, w01, preferred_element_type=F32)
    return d(hi) + d(mid) + d(lo)


def _peer_act_kernel(off_ref, x_ref, tab_ref, gate_ref, w_ref, stack_a, stack_b, part_ref):
    tb, slots = gate_ref.shape
    kdim = slots * ROW_TILE
    diag = _diag_mask(slots)

    def one(t, stack_ref):
        _gather_rows(tab_ref, off_ref, t * slots, stack_ref, slots)
        g = pltpu.bitcast(stack_ref[...], BF16)
        rows = pl.ds(pl.multiple_of(t * ROW_TILE, ROW_TILE), ROW_TILE)
        xh, xl = _split2(x_ref[rows, :])
        p2 = lax.dot_general(jnp.concatenate([xh, xl], axis=0), g, (((1,), (1,)), ((), ())),
                             preferred_element_type=F32)
        part_ref[rows, :] = jnp.where(diag, p2[:ROW_TILE] + p2[ROW_TILE:], 0.0)

    def pair(i, carry):
        one(2 * i, stack_a)
        one(2 * i + 1, stack_b)
        return carry

    lax.fori_loop(0, tb // 2, pair, 0)
    fold = (lax.broadcasted_iota(jnp.int32, (kdim, slots), 0) // ROW_TILE
            == lax.broadcasted_iota(jnp.int32, (kdim, slots), 1)).astype(BF16)
    act = jnp.sum(_dot3(part_ref[...], fold).reshape(tb, ROW_TILE, slots), axis=1)
    gelu = 0.5 * act * (1.0 + lax.erf(act * math.sqrt(0.5)))
    w_ref[...] = gate_ref[...] * gelu


def _peer_act(off, xn, tab, gate, *, tb):
    n, slots = gate.shape
    x8 = xn.reshape(n * ROW_TILE, LANES)
    stack = pltpu.VMEM((slots * HALF_TILE, LANES), jnp.int32)
    return pl.pallas_call(
        _peer_act_kernel,
        grid=(n // tb,),
        in_specs=[pl.BlockSpec((tb * slots,), lambda i: (i,), memory_space=pltpu.SMEM),
                  pl.BlockSpec((tb * ROW_TILE, LANES), lambda i: (i, 0)),
                  pl.BlockSpec(memory_space=pltpu.VMEM),
                  pl.BlockSpec((tb, slots), lambda i: (i, 0))],
        out_specs=pl.BlockSpec((tb, slots), lambda i: (i, 0)),
        out_shape=jax.ShapeDtypeStruct((n, slots), F32),
        scratch_shapes=[stack, stack, pltpu.VMEM((tb * ROW_TILE, slots * ROW_TILE), F32)],
        compiler_params=_cparams(("arbitrary",)),
        name="peer_expert_act",
    )(off, x8, tab, gate)


def _peer_mix_kernel(off_ref, w_ref, tab_ref, x2_ref, gfin_ref, o_ref, stack_a, stack_b, wexp_ref,
                     *, final_norm):
    tb, slots = w_ref.shape
    kdim = slots * ROW_TILE
    diag = _diag_mask(slots)
    spread = (lax.broadcasted_iota(jnp.int32, (slots, kdim), 1) // ROW_TILE
              == lax.broadcasted_iota(jnp.int32, (slots, kdim), 0)).astype(BF16)
    wexp_ref[...] = _dot3(w_ref[...], spread)

    def one(t, stack_ref):
        _gather_rows(tab_ref, off_ref, t * slots, stack_ref, slots)
        g = pltpu.bitcast(stack_ref[...], BF16)
        w8 = jnp.where(diag, jnp.broadcast_to(wexp_ref[pl.ds(t, 1), :], (ROW_TILE, kdim)), 0.0)
        wh, wl = _split2(w8)
        ff2 = jnp.dot(jnp.concatenate([wh, wl], axis=0), g, preferred_element_type=F32)
        rows = pl.ds(pl.multiple_of(t * ROW_TILE, ROW_TILE), ROW_TILE)
        o_ref[rows, :] = x2_ref[rows, :] + ff2[:ROW_TILE] + ff2[ROW_TILE:]

    def pair(i, carry):
        one(2 * i, stack_a)
        one(2 * i + 1, stack_b)
        return carry

    lax.fori_loop(0, tb // 2, pair, 0)
    if final_norm:
        x3 = o_ref[...].reshape(tb, ROW_TILE, LANES)
        sq = jnp.sum(jnp.sum(x3 * x3, axis=2, keepdims=True), axis=1, keepdims=True)
        scale = lax.rsqrt(sq * (1.0 / (ROW_TILE * LANES)) + NORM_EPS)
        o_ref[...] = (x3 * scale * gfin_ref[...][None]).reshape(tb * ROW_TILE, LANES)


def _peer_mix(off, wgt, tab, x2, g_final, *, tb, final_norm):
    n, d = x2.shape
    slots = wgt.shape[1]
    x8 = x2.reshape(n * ROW_TILE, LANES)
    g8 = g_final.reshape(ROW_TILE, LANES)
    kern = functools.partial(_peer_mix_kernel, final_norm=final_norm)
    stack = pltpu.VMEM((slots * HALF_TILE, LANES), jnp.int32)
    out = pl.pallas_call(
        kern,
        grid=(n // tb,),
        in_specs=[pl.BlockSpec((tb * slots,), lambda i: (i,), memory_space=pltpu.SMEM),
                  pl.BlockSpec((tb, slots), lambda i: (i, 0)),
                  pl.BlockSpec(memory_space=pltpu.VMEM),
                  pl.BlockSpec((tb * ROW_TILE, LANES), lambda i: (i, 0)),
                  pl.BlockSpec((ROW_TILE, LANES), lambda i: (0, 0))],
        out_specs=pl.BlockSpec((tb * ROW_TILE, LANES), lambda i: (i, 0)),
        out_shape=jax.ShapeDtypeStruct((n * ROW_TILE, LANES), F32),
        scratch_shapes=[stack, stack, pltpu.VMEM((tb, slots * ROW_TILE), F32)],
        compiler_params=_cparams(("arbitrary",)),
        name="peer_expert_mix",
    )(off, wgt, tab, x8, g8)
    return out.reshape(n, d)


def _state_to_blocks(s):
    b, h, d, _ = s.shape
    st = jnp.swapaxes(s, -1, -2).reshape(b, h // HEADS_PER_LANE_TILE, HEADS_PER_LANE_TILE, d, d)
    eye = jnp.eye(HEADS_PER_LANE_TILE, dtype=s.dtype)
    blk = st[:, :, :, :, None, :] * eye[None, None, :, None, :, None]
    return blk.reshape(b, h // HEADS_PER_LANE_TILE, LANES, LANES)


def _blocks_to_state(blk, heads):
    b, npair = blk.shape[:2]
    x = blk.reshape(b, npair, HEADS_PER_LANE_TILE, HEAD_DIM, HEADS_PER_LANE_TILE, HEAD_DIM)
    diag = jnp.stack([x[:, :, hh, :, hh, :] for hh in range(HEADS_PER_LANE_TILE)], axis=2)
    return jnp.swapaxes(diag.reshape(b, heads, HEAD_DIM, HEAD_DIM), -1, -2)


PEER_TOPK = 16


def _layer(x, k_past, v_past, lf_past, s0, shift0, lp, g_final, final_norm):
    (norm_mix_g, w_in, fox_b_f, mu, w0, w2, a0, a2, g2, k_k, k_a, r_k, lnx_w, lnx_b, w_out,
     norm_ffn_g, peer_w_q, peer_sub_keys, tab_u, tab_v) = lp
    b, t, d = x.shape
    n = b * t
    fox_heads = fox_b_f.shape[0]
    fw = fox_heads * HEAD_DIM
    fox_cols = 3 * fw + fox_heads
    rwkv_heads = r_k.shape[0]
    w = rwkv_heads * HEAD_DIM
    x2d = x.reshape(n, d)
    qb, k, v, kb, vb, lf, rw_main, rw_tail = _inproj(x2d, norm_mix_g, w_in, fox_b_f, fox_cols,
                                                     fox_heads, 3 * w)
    fox = _fox_stream(qb, kb, vb, lf, k_past, v_past, lf_past, batch=b, q_len=t)

    prm = _rwkv_params(mu, w0, w2, a0, a2, g2, k_k, k_a, r_k.reshape(-1))
    tail = prm["tail"]
    shift_main = shift0[..., :3 * w]
    shift_tail = jnp.pad(shift0[..., 3 * w:], ((0, 0), (0, 0), (0, TAIL_PAD - tail)))
    r, lw, km, vv, kn, bb, g, bonus = _rwkv_pre(rw_main, rw_tail, shift_main, shift_tail, prm,
                                                batch=b, seq=t)
    y, s_blk = _rwkv_scan(r, lw, km, vv, kn, bb, _state_to_blocks(s0), batch=b, seq=t)
    s_t = _blocks_to_state(s_blk, rwkv_heads)
    last = jnp.concatenate([rw_main.reshape(b, t, -1)[:, -1:], rw_tail.reshape(b, t, -1)[:, -1:, :tail]],
                           axis=-1)

    x2, xn, scores = _outproj(x2d, fox, y, bonus, g, lnx_w, lnx_b, w_out, norm_ffn_g, peer_w_q,
                              peer_sub_keys)
    idx, gate = _retrieve(scores, PEER_TOPK)
    tb = _row_tile(n, 128)
    slots = gate.shape[1]
    off = idx.reshape(n * slots)
    wgt = _peer_act(off, xn, tab_u, gate, tb=tb)
    out = _peer_mix(off, wgt, tab_v, x2, g_final, tb=tb, final_norm=final_norm)
    return (out.reshape(b, t, d), k.reshape(b, t, fox_heads, HEAD_DIM),
            v.reshape(b, t, fox_heads, HEAD_DIM), lf.reshape(b, t, fox_heads), s_t, last)


def kernel(x_prompt, x_sample, cache_fox_k, cache_fox_v, cache_fox_logf, state_rwkv, state_shift,
           norm_mix_g, w_in, fox_b_f, rwkv_mu, rwkv_w0, rwkv_w2, rwkv_a0, rwkv_a2, rwkv_g2,
           rwkv_k_k, rwkv_k_a, rwkv_r_k, rwkv_lnx_w, rwkv_lnx_b, w_out, norm_ffn_g,
           peer_w_q, peer_sub_keys, peer_u, peer_v, norm_final_g):
    depth = w_in.shape[0]
    yp, ys = x_prompt, x_sample
    bp = x_prompt.shape[0]
    dt = x_prompt.dtype
    fox_heads = fox_b_f.shape[1]
    rwkv_heads = rwkv_r_k.shape[1]
    rwkv_cols = rwkv_mu.shape[1]
    outs_p, outs_s = [], []
    for l in range(depth):
        lp = (norm_mix_g[l], w_in[l], fox_b_f[l], rwkv_mu[l], rwkv_w0[l], rwkv_w2[l], rwkv_a0[l],
              rwkv_a2[l], rwkv_g2[l], rwkv_k_k[l], rwkv_k_a[l], rwkv_r_k[l], rwkv_lnx_w[l],
              rwkv_lnx_b[l], w_out[l], norm_ffn_g[l], peer_w_q[l], peer_sub_keys[l],
              _pack_table(peer_u[l]), _pack_table(peer_v[l]))
        last = l == depth - 1
        empty_kv = jnp.zeros((bp, 0, fox_heads, HEAD_DIM), dt)
        empty_lf = jnp.zeros((bp, 0, fox_heads), dt)
        s_zero = jnp.zeros((bp, rwkv_heads, HEAD_DIM, HEAD_DIM), dt)
        sh_zero = jnp.zeros((bp, 1, rwkv_cols), dt)
        yp, *rest_p = _layer(yp, empty_kv, empty_kv, empty_lf, s_zero, sh_zero, lp, norm_final_g, last)
        ys, *rest_s = _layer(ys, cache_fox_k[l], cache_fox_v[l], cache_fox_logf[l], state_rwkv[l],
                             state_shift[l], lp, norm_final_g, last)
        outs_p.append(rest_p)
        outs_s.append(rest_s)
    stack = lambda outs, i: jnp.stack([o[i] for o in outs])
    return ((yp, ys) + tuple(stack(outs_p, i) for i in range(5))
            + tuple(stack(outs_s, i) for i in range(5)))
```

```python
import functools
import math

import jax
import jax.numpy as jnp
from jax import lax
from jax.experimental import pallas as pl
from jax.experimental.pallas import tpu as pltpu

F32 = jnp.float32
BF16 = jnp.bfloat16

HEAD_DIM = 64
LANES = 128
HEADS_PER_LANE_TILE = LANES // HEAD_DIM
TAIL_PAD = 2 * LANES
CHUNK = 64
RWKV_PASSES = 1
NORM_EPS = 1e-6
LNX_EPS = 64e-5
NEG_BIG = -1e30
LOG2E = math.log2(math.e)
HIGHEST = lax.Precision.HIGHEST
VMEM_LIMIT = 48 * 1024 * 1024


def _cparams(sem):
    return pltpu.CompilerParams(dimension_semantics=sem, vmem_limit_bytes=VMEM_LIMIT)


def _row_tile(n, target):
    t = min(n, target)
    assert n % t == 0, (n, t)
    return t


def _inproj_kernel(x_ref, g_ref, wqkv_ref, wf_ref, wrw_ref, wtail_ref, bf_ref,
                   q_ref, k_ref, v_ref, kb_ref, vb_ref, lf_ref, rw_ref, tail_ref):
    x = x_ref[...]
    h = x * lax.rsqrt(jnp.mean(x * x, axis=-1, keepdims=True) + NORM_EPS) * g_ref[...]
    hb = h.astype(BF16)
    fw = wqkv_ref.shape[1] // 3
    qkv = jnp.dot(hb, wqkv_ref[...], preferred_element_type=F32)
    q_ref[...] = (qkv[:, :fw] * (LOG2E / math.sqrt(HEAD_DIM))).astype(BF16)
    k = qkv[:, fw:2 * fw]
    v = qkv[:, 2 * fw:]
    k_ref[...] = k
    v_ref[...] = v
    kb_ref[...] = k.astype(BF16)
    vb_ref[...] = v.astype(BF16)
    f = jnp.dot(hb, wf_ref[...], preferred_element_type=F32) + bf_ref[...]
    lf_ref[...] = jax.nn.log_sigmoid(f)
    rw_ref[...] = jnp.dot(hb, wrw_ref[...], preferred_element_type=F32)
    tail_ref[...] = jnp.dot(hb, wtail_ref[...], preferred_element_type=F32)


def _inproj(x2d, g, w_in, b_f, fox_cols, fox_heads, rw_main):
    n, d = x2d.shape
    fw = fox_heads * HEAD_DIM
    wqkv = w_in[:, :3 * fw].astype(BF16)
    wf = w_in[:, 3 * fw:fox_cols].astype(BF16)
    wrw = w_in[:, fox_cols:fox_cols + rw_main].astype(BF16)
    wtail = w_in[:, fox_cols + rw_main:].astype(BF16)
    wtail = jnp.pad(wtail, ((0, 0), (0, TAIL_PAD - wtail.shape[1])))
    tm = _row_tile(n, 512)
    row = lambda c: pl.BlockSpec((tm, c), lambda i: (i, 0))
    full = lambda a: pl.BlockSpec(a.shape, lambda i: (0,) * a.ndim)
    g2 = g.reshape(1, d)
    bf2 = b_f.reshape(1, fox_heads)
    outs = (
        jax.ShapeDtypeStruct((n, fw), BF16),
        jax.ShapeDtypeStruct((n, fw), F32),
        jax.ShapeDtypeStruct((n, fw), F32),
        jax.ShapeDtypeStruct((n, fw), BF16),
        jax.ShapeDtypeStruct((n, fw), BF16),
        jax.ShapeDtypeStruct((n, fox_heads), F32),
        jax.ShapeDtypeStruct((n, rw_main), F32),
        jax.ShapeDtypeStruct((n, TAIL_PAD), F32),
    )
    return pl.pallas_call(
        _inproj_kernel,
        grid=(n // tm,),
        in_specs=[row(d), full(g2), full(wqkv), full(wf), full(wrw), full(wtail), full(bf2)],
        out_specs=[row(fw), row(fw), row(fw), row(fw), row(fw), row(fox_heads), row(rw_main),
                   row(TAIL_PAD)],
        out_shape=outs,
        compiler_params=_cparams(("parallel",)),
        name="inproj",
    )(x2d, g2, wqkv, wf, wrw, wtail, bf2)


def _cumsum_kernel(lf_ref, c_ref, carry):
    @pl.when(pl.program_id(1) == 0)
    def _():
        carry[...] = jnp.zeros_like(carry)

    lf = lf_ref[...]
    tc = lf.shape[0]
    r = lax.broadcasted_iota(jnp.int32, (tc, tc), 0)
    c = lax.broadcasted_iota(jnp.int32, (tc, tc), 1)
    lower = (c <= r).astype(F32)
    cc = jnp.dot(lower, lf, precision=HIGHEST, preferred_element_type=F32) + carry[...]
    c_ref[...] = cc
    carry[...] = cc[tc - 1:tc, :]


def _cumsum(lf, tc):
    b, l, nh = lf.shape
    assert l % tc == 0
    return pl.pallas_call(
        _cumsum_kernel,
        grid=(b, l // tc),
        in_specs=[pl.BlockSpec((None, tc, nh), lambda i, j: (i, j, 0))],
        out_specs=pl.BlockSpec((None, tc, nh), lambda i, j: (i, j, 0)),
        out_shape=jax.ShapeDtypeStruct((b, l, nh), F32),
        scratch_shapes=[pltpu.VMEM((1, nh), F32)],
        compiler_params=_cparams(("parallel", "arbitrary")),
        name="cumsum_logf",
    )(lf)


def _split3(x):
    hi = x.astype(BF16)
    r = x - hi.astype(F32)
    mid = r.astype(BF16)
    lo = (r - mid.astype(F32)).astype(BF16)
    return hi.astype(F32), mid.astype(F32), lo.astype(F32)


def _augment_kernel(x_ref, c_ref, o_ref, *, role):
    tm = x_ref.shape[0]
    lane = lax.broadcasted_iota(jnp.int32, (tm, LANES), 1)
    for p in range(x_ref.shape[1] // LANES):
        xp = x_ref[:, p * LANES:(p + 1) * LANES].astype(F32)
        for hh in range(HEADS_PER_LANE_TILE):
            h = p * HEADS_PER_LANE_TILE + hh
            own = (lane >= hh * HEAD_DIM) & (lane < (hh + 1) * HEAD_DIM)
            e = (lane + (1 - hh) * HEAD_DIM) % LANES
            if role == "v":
                ext = jnp.where(e == 0, 1.0, 0.0)
            else:
                c = jnp.broadcast_to(c_ref[:, h:h + 1], (tm, LANES))
                hi, mid, lo = _split3(c * LOG2E)
                sgn = 1.0 if role == "q" else -1.0
                base = 0 if role == "q" else 3
                ext = jnp.where(e == base, sgn * hi,
                                jnp.where(e == base + 1, sgn * mid,
                                          jnp.where(e == base + 2, sgn * lo,
                                                    jnp.where(e < 6, 1.0, 0.0))))
            o_ref[:, h * LANES:(h + 1) * LANES] = jnp.where(own, xp, ext).astype(BF16)


def _augment(x, c, role):
    n, w = x.shape
    nh = w // HEAD_DIM
    tm = _row_tile(n, 512) if n % 512 == 0 else n
    kern = functools.partial(_augment_kernel, role=role)
    return pl.pallas_call(
        kern,
        grid=(n // tm,),
        in_specs=[pl.BlockSpec((tm, w), lambda i: (i, 0)),
                  pl.BlockSpec((tm, nh), lambda i: (i, 0))],
        out_specs=pl.BlockSpec((tm, nh * LANES), lambda i: (i, 0)),
        out_shape=jax.ShapeDtypeStruct((n, nh * LANES), BF16),
        compiler_params=_cparams(("parallel",)),
        name="fox_augment_" + role,
    )(x, c)


Q_SUB = 128
K_SUB = 256


def _fox_kernel(q_ref, k_ref, v_ref, o_ref, m_sc, acc_sc, *, q_off, tq, tk):
    i = pl.program_id(2)
    j = pl.program_id(3)
    nk = pl.num_programs(3)
    qs_n, ks_n = min(Q_SUB, tq), min(K_SUB, tk)

    @pl.when(j == 0)
    def _():
        m_sc[...] = jnp.full_like(m_sc, NEG_BIG)
        acc_sc[...] = jnp.zeros_like(acc_sc)

    q_lo = q_off + i * tq
    k_lo = j * tk

    def body(masked):
        if masked:
            diff = (lax.broadcasted_iota(jnp.int32, (qs_n, LANES), 1)
                    - lax.broadcasted_iota(jnp.int32, (qs_n, LANES), 0))
        for hh in range(HEADS_PER_LANE_TILE):
            cols = slice(hh * LANES, (hh + 1) * LANES)
            for ks in range(tk // ks_n):
                krows = slice(ks * ks_n, (ks + 1) * ks_n)
                k_sub = k_ref[krows, cols]
                v_sub = v_ref[krows, cols]
                for qs in range(tq // qs_n):
                    qrows = slice(qs * qs_n, (qs + 1) * qs_n)
                    s = lax.dot_general(q_ref[qrows, cols], k_sub, (((1,), (1,)), ((), ())),
                                        preferred_element_type=F32)
                    parts = [s[:, c * LANES:(c + 1) * LANES] for c in range(ks_n // LANES)]
                    if masked:
                        parts = [jnp.where(diff <= q_lo - k_lo + qs * qs_n - ks * ks_n - c * LANES,
                                           pc, NEG_BIG) for c, pc in enumerate(parts)]
                    mx = parts[0]
                    for pc in parts[1:]:
                        mx = jnp.maximum(mx, pc)
                    m_old = m_sc[hh, qrows, :]
                    m_new = jnp.maximum(m_old, jnp.max(mx, axis=-1, keepdims=True))
                    alpha = jnp.exp2(m_old - m_new)
                    pr = jnp.concatenate([jnp.exp2(pc - m_new).astype(BF16) for pc in parts], axis=1)
                    pv = jnp.dot(pr, v_sub, preferred_element_type=F32)
                    acc_sc[hh, qrows, :] = alpha * acc_sc[hh, qrows, :] + pv
                    m_sc[hh, qrows, :] = m_new

    fully_visible = k_lo + tk - 1 <= q_lo
    any_visible = k_lo <= q_lo + tq - 1

    @pl.when(fully_visible)
    def _():
        body(False)

    @pl.when(jnp.logical_and(any_visible, jnp.logical_not(fully_visible)))
    def _():
        body(True)

    @pl.when(j == nk - 1)
    def _():
        lane = lax.broadcasted_iota(jnp.int32, (1, LANES), 1)
        out = jnp.zeros((tq, LANES), F32)
        for hh in range(HEADS_PER_LANE_TILE):
            in_head = (lane >= hh * HEAD_DIM) & (lane < (hh + 1) * HEAD_DIM)
            acc = acc_sc[hh]
            ones_col = (1 - hh) * HEAD_DIM
            denom = jnp.broadcast_to(acc[:, ones_col:ones_col + 1], acc.shape)
            out = jnp.where(in_head, acc / denom, out)
        o_ref[...] = out


def _fox_attend(q_aug, k_aug, v_aug, *, batch, q_len, kv_len, q_off, tq, tk):
    n, wa = q_aug.shape
    pair_w = HEADS_PER_LANE_TILE * LANES
    npair = wa // pair_w
    nq, nk = q_len // tq, kv_len // tk
    assert q_len % tq == 0 and kv_len % tk == 0

    def last_kv(i):
        return (q_off + (i + 1) * tq - 1) // tk

    kv_map = lambda b, p, i, j: (b * nk + jnp.minimum(j, last_kv(i)), p)
    kern = functools.partial(_fox_kernel, q_off=q_off, tq=tq, tk=tk)
    return pl.pallas_call(
        kern,
        grid=(batch, npair, nq, nk),
        in_specs=[
            pl.BlockSpec((tq, pair_w), lambda b, p, i, j: (b * nq + i, p)),
            pl.BlockSpec((tk, pair_w), kv_map),
            pl.BlockSpec((tk, pair_w), kv_map),
        ],
        out_specs=pl.BlockSpec((tq, LANES), lambda b, p, i, j: (b * nq + i, p)),
        out_shape=jax.ShapeDtypeStruct((n, npair * LANES), F32),
        scratch_shapes=[pltpu.VMEM((HEADS_PER_LANE_TILE, tq, LANES), F32),
                        pltpu.VMEM((HEADS_PER_LANE_TILE, tq, LANES), F32)],
        compiler_params=_cparams(("parallel", "parallel", "parallel", "arbitrary")),
        name="fox_attention",
    )(q_aug, k_aug, v_aug)


def _fox_stream(qb, kb, vb, lf, k_past, v_past, lf_past, *, batch, q_len):
    n, w = qb.shape
    nh = lf.shape[1]
    past = k_past.shape[1]
    lf_new = lf.reshape(batch, q_len, nh)
    if past == 0:
        kv_len = q_len
        k_all, v_all, lf_all = kb, vb, lf_new
        tq = tk = _row_tile(q_len, 512)
        tc = tk
    else:
        kv_len = -(-(past + q_len) // K_SUB) * K_SUB
        pad = kv_len - past - q_len

        def cat(old, new):
            old = old.reshape(batch, past, -1).astype(new.dtype)
            new = new.reshape(batch, q_len, -1)
            z = jnp.zeros((batch, pad, new.shape[-1]), new.dtype)
            return jnp.concatenate([old, new, z], axis=1)

        k_all = cat(k_past, kb).reshape(batch * kv_len, w)
        v_all = cat(v_past, vb).reshape(batch * kv_len, w)
        lf_all = cat(lf_past, lf_new)
        tc = max(t for t in range(LANES, 1024 + 1, LANES) if kv_len % t == 0)
        tq, tk = q_len, kv_len
    c = _cumsum(lf_all, tc)
    c_k = c.reshape(batch * kv_len, nh)
    c_q = c[:, past:past + q_len].reshape(n, nh)
    return _fox_attend(_augment(qb, c_q, "q"), _augment(k_all, c_k, "k"), _augment(v_all, c_k, "v"),
                       batch=batch, q_len=q_len, kv_len=kv_len, q_off=past, tq=tq, tk=tk)


def _head_sum_matrix(width):
    r = lax.broadcasted_iota(jnp.int32, (width, width), 0) // HEAD_DIM
    c = lax.broadcasted_iota(jnp.int32, (width, width), 1) // HEAD_DIM
    return (r == c).astype(F32)


def _rwkv_pre_kernel(pm_ref, pt_ref, sm_ref, st_ref, mum_ref, mut_ref, wbig_ref, w0_ref, a0_ref,
                     kk_ref, ka_ref, rk_ref,
                     r_out, lw_out, km_out, v_out, kn_out, b_out, g_out, bonus_out,
                     carry_m, carry_t, *, lora_w, lora_a):
    @pl.when(pl.program_id(1) == 0)
    def _():
        carry_m[...] = sm_ref[...]
        carry_t[...] = st_ref[...]

    pm = pm_ref[...]
    pt = pt_ref[...]
    tm = pm.shape[0]
    w = pm.shape[1] // 3

    def shifted(p, carry):
        row = lax.broadcasted_iota(jnp.int32, p.shape, 0)
        return jnp.where(row == 0, carry[...], pltpu.roll(p, 1, 0))

    prev_m = shifted(pm, carry_m)
    prev_t = shifted(pt, carry_t)
    carry_m[...] = pm[tm - 1:tm, :]
    carry_t[...] = pt[tm - 1:tm, :]
    psm = pm + mum_ref[...] * (prev_m - pm)
    pst = pt + mut_ref[...] * (prev_t - pt)
    r = psm[:, :w]
    k = psm[:, w:2 * w]
    v = psm[:, 2 * w:]
    lane = lax.broadcasted_iota(jnp.int32, pst.shape, 1)
    z = jnp.where(lane < lora_w, jnp.tanh(pst),
                  jnp.where(lane < lora_w + lora_a, pst, jax.nn.sigmoid(pst)))
    lo = jnp.dot(z.astype(BF16), wbig_ref[...], preferred_element_type=F32)
    w_log = -jax.nn.softplus(-(w0_ref[...] + lo[:, :w])) - 0.5
    lw = -jnp.exp(w_log)
    a = jax.nn.sigmoid(a0_ref[...] + lo[:, w:2 * w])
    g = lo[:, 2 * w:]
    e = _head_sum_matrix(w)
    kk0 = k * kk_ref[...]
    n2 = jnp.dot(kk0 * kk0, e, precision=HIGHEST, preferred_element_type=F32)
    kn = kk0 / jnp.maximum(jnp.sqrt(n2), 1e-12)
    km = k * (1.0 + (a - 1.0) * ka_ref[...])
    rk = jnp.dot(r * km * rk_ref[...], e, precision=HIGHEST, preferred_element_type=F32)
    r_out[...] = r
    lw_out[...] = lw
    km_out[...] = km
    v_out[...] = v
    kn_out[...] = kn
    b_out[...] = kn * a
    g_out[...] = g
    bonus_out[...] = rk * v


def _rwkv_params(mu, w0, w2, a0, a2, g2, k_k, k_a, r_k):
    w = w0.shape[0]
    lora_w, lora_a, lora_g = w2.shape[0], a2.shape[0], g2.shape[0]
    w_lora = jnp.zeros((TAIL_PAD, 3 * w), F32)
    w_lora = w_lora.at[:lora_w, :w].set(w2)
    w_lora = w_lora.at[lora_w:lora_w + lora_a, w:2 * w].set(a2)
    w_lora = w_lora.at[lora_w + lora_a:lora_w + lora_a + lora_g, 2 * w:].set(g2)
    tail = mu.shape[0] - 3 * w
    return dict(
        mu_main=mu[:3 * w].reshape(1, 3 * w),
        mu_tail=jnp.pad(mu[3 * w:], (0, TAIL_PAD - tail)).reshape(1, TAIL_PAD),
        w_lora=w_lora.astype(BF16), w0=w0.reshape(1, w), a0=a0.reshape(1, w),
        k_k=k_k.reshape(1, w), k_a=k_a.reshape(1, w), r_k=r_k.reshape(1, w),
        lora_w=lora_w, lora_a=lora_a, tail=tail)


def _rwkv_pre(rw_main, rw_tail, shift_main, shift_tail, prm, *, batch, seq):
    n, w3 = rw_main.shape
    w = w3 // 3
    tm = _row_tile(seq, 512)
    nt = seq // tm
    row = lambda c: pl.BlockSpec((tm, c), lambda b, i: (b * nt + i, 0))
    per_b = lambda c: pl.BlockSpec((None, 1, c), lambda b, i: (b, 0, 0))
    full = lambda a: pl.BlockSpec(a.shape, lambda b, i: (0,) * a.ndim)
    consts = [prm["mu_main"], prm["mu_tail"], prm["w_lora"], prm["w0"], prm["a0"], prm["k_k"],
              prm["k_a"], prm["r_k"]]
    kern = functools.partial(_rwkv_pre_kernel, lora_w=prm["lora_w"], lora_a=prm["lora_a"])
    return pl.pallas_call(
        kern,
        grid=(batch, nt),
        in_specs=[row(w3), row(TAIL_PAD), per_b(w3), per_b(TAIL_PAD)] + [full(c) for c in consts],
        out_specs=[row(w)] * 8,
        out_shape=[jax.ShapeDtypeStruct((n, w), F32)] * 8,
        scratch_shapes=[pltpu.VMEM((1, w3), F32), pltpu.VMEM((1, TAIL_PAD), F32)],
        compiler_params=_cparams(("parallel", "arbitrary")),
        name="rwkv_pre",
    )(rw_main, rw_tail, shift_main, shift_tail, *consts)


def _bmm(a, b, kind, passes):
    contract = {"nn": ((2,), (1,)), "nt": ((2,), (2,)), "tn": ((1,), (1,))}[kind]
    dims = (contract, ((0,), (0,)))
    if passes == 6:
        return lax.dot_general(a, b, dims, precision=HIGHEST, preferred_element_type=F32)
    dg = lambda x, y: lax.dot_general(x, y, dims, preferred_element_type=F32)
    ah, bh = a.astype(BF16), b.astype(BF16)
    out = dg(ah, bh)
    if passes == 3:
        al = (a - ah.astype(F32)).astype(BF16)
        bl = (b - bh.astype(F32)).astype(BF16)
        out = out + dg(ah, bl) + dg(al, bh)
    return out


def _rwkv_chunk(r, lw, km, v, kn, bb, s_blk, passes):
    g, c, _ = r.shape
    c2 = HEADS_PER_LANE_TILE * c
    ti = lax.broadcasted_iota(jnp.int32, (g, c, c), 1)
    si = lax.broadcasted_iota(jnp.int32, (g, c, c), 2)
    cs = _bmm((si <= ti).astype(F32), lw, "nn", 6)
    e_pos = jnp.exp(cs)
    e_neg = jnp.exp(-cs)
    kt = kn * jnp.exp(cs - lw)
    bt = bb * e_neg
    kh = km * e_neg
    rt = r * e_pos
    g_end = e_pos[:, c - 1:c, :]

    lane = lax.broadcasted_iota(jnp.int32, (1, 1, LANES), 2)
    head_of_lane = lane // HEAD_DIM

    def stack_masked(x):
        return jnp.concatenate(
            [jnp.where(head_of_lane == hh, x, 0.0) for hh in range(HEADS_PER_LANE_TILE)], axis=1)

    def stack(x):
        return jnp.concatenate([x] * HEADS_PER_LANE_TILE, axis=1)

    def pick(x):
        out = x[:, :c]
        for hh in range(1, HEADS_PER_LANE_TILE):
            out = jnp.where(head_of_lane == hh, x[:, hh * c:(hh + 1) * c], out)
        return out

    kt2 = stack_masked(kt)
    rt2 = stack_masked(rt)
    rr = lax.broadcasted_iota(jnp.int32, (1, c2, c2), 1)
    cc = lax.broadcasted_iota(jnp.int32, (1, c2, c2), 2)
    strict_blk = (rr // c == cc // c) & (cc < rr)
    x = jnp.where(strict_blk, -_bmm(kt2, stack(bt), "nt", passes), 0.0)
    tinv = (rr == cc).astype(F32) + x
    steps = max(int(math.ceil(math.log2(c))) - 1, 0)
    for _ in range(steps):
        x = _bmm(x, x, "nn", passes)
        tinv = tinv + _bmm(tinv, x, "nn", passes)
    tr = lax.broadcasted_iota(jnp.int32, (1, c2, c), 1) % c
    sr = lax.broadcasted_iota(jnp.int32, (1, c2, c), 2)
    kk_s = jnp.where(sr < tr, _bmm(kt2, kh, "nt", passes), 0.0)
    rb_s = jnp.where(sr <= tr, _bmm(rt2, bt, "nt", passes), 0.0)
    rk_s = jnp.where(sr <= tr, _bmm(rt2, kh, "nt", passes), 0.0)

    ks = _bmm(jnp.concatenate([kt, rt], axis=1), s_blk, "nn", passes)
    rhs = ks[:, :c] + pick(_bmm(kk_s, v, "nn", passes))
    z = pick(_bmm(tinv, stack(rhs), "nn", passes))
    y = ks[:, c:] - pick(_bmm(rb_s, z, "nn", passes)) + pick(_bmm(rk_s, v, "nn", passes))
    jr = lax.broadcasted_iota(jnp.int32, (1, LANES, LANES), 1)
    ic = lax.broadcasted_iota(jnp.int32, (1, LANES, LANES), 2)
    decay_rows = jnp.swapaxes(jnp.broadcast_to(g_end, (g, LANES, LANES)), 1, 2)
    upd = _bmm(jnp.concatenate([bt * g_end, kh * g_end], axis=1),
               jnp.concatenate([-z, v], axis=1), "tn", passes)
    s_new = decay_rows * s_blk + jnp.where(jr // HEAD_DIM == ic // HEAD_DIM, upd, 0.0)
    return y, s_new


def _rwkv_scan_kernel(r_ref, lw_ref, km_ref, v_ref, kn_ref, b_ref, s0_ref, y_ref, sT_ref, s_sc,
                      *, chunk, passes):
    it = pl.program_id(0)
    nb, tb, w = r_ref.shape
    npair = w // LANES

    @pl.when(it == 0)
    def _():
        s_sc[...] = s0_ref[...]

    def step(ci, carry):
        rows = pl.ds(pl.multiple_of(ci * chunk, chunk), chunk)

        def gather(ref):
            blk = ref[:, rows, :]
            return jnp.concatenate([blk[:, :, p * LANES:(p + 1) * LANES] for p in range(npair)],
                                   axis=0)

        y, s_new = _rwkv_chunk(gather(r_ref), gather(lw_ref), gather(km_ref), gather(v_ref),
                               gather(kn_ref), gather(b_ref), s_sc[...], passes)
        for p in range(npair):
            y_ref[:, rows, p * LANES:(p + 1) * LANES] = y[p * nb:(p + 1) * nb]
        s_sc[...] = s_new
        return carry

    lax.fori_loop(0, tb // chunk, step, 0)

    @pl.when(it == pl.num_programs(0) - 1)
    def _():
        sT_ref[...] = s_sc[...]


def _rwkv_scan(r, lw, km, v, kn, bb, s0_blk, *, batch, seq, passes=RWKV_PASSES):
    n, w = r.shape
    npair = w // LANES
    chunk = min(CHUNK, seq)
    tb = _row_tile(seq, 4 * chunk)
    row = pl.BlockSpec((batch, tb, w), lambda i: (0, i, 0))
    st = pl.BlockSpec((npair * batch, LANES, LANES), lambda i: (0, 0, 0))
    s0 = jnp.swapaxes(s0_blk, 0, 1).reshape(npair * batch, LANES, LANES)
    kern = functools.partial(_rwkv_scan_kernel, chunk=chunk, passes=passes)
    y, s_t = pl.pallas_call(
        kern,
        grid=(seq // tb,),
        in_specs=[row] * 6 + [st],
        out_specs=[row, st],
        out_shape=[jax.ShapeDtypeStruct((batch, seq, w), F32),
                   jax.ShapeDtypeStruct((npair * batch, LANES, LANES), F32)],
        scratch_shapes=[pltpu.VMEM((npair * batch, LANES, LANES), F32)],
        compiler_params=_cparams(("arbitrary",)),
        name="rwkv_scan",
    )(*(a.reshape(batch, seq, w) for a in (r, lw, km, v, kn, bb)), s0)
    s_t = jnp.swapaxes(s_t.reshape(npair, batch, LANES, LANES), 0, 1)
    return y.reshape(n, w), s_t


def _outproj_kernel(x_ref, fox_ref, y_ref, bonus_ref, g_ref, lnw_ref, lnb_ref, wa_ref, wb_ref,
                    gf_ref, wqt_ref, keys_ref, x2_ref, xn_ref, sc_ref):
    y = y_ref[...]
    w = y.shape[1]
    em = _head_sum_matrix(w) * (1.0 / HEAD_DIM)
    mean = jnp.dot(y, em, precision=HIGHEST, preferred_element_type=F32)
    d = y - mean
    var = jnp.dot(d * d, em, precision=HIGHEST, preferred_element_type=F32)
    yn = d * lax.rsqrt(var + LNX_EPS) * lnw_ref[...] + lnb_ref[...]
    rw = (yn + bonus_ref[...]) * g_ref[...]
    mix = (jnp.dot(fox_ref[...].astype(BF16), wa_ref[...], preferred_element_type=F32)
           + jnp.dot(rw.astype(BF16), wb_ref[...], preferred_element_type=F32))
    x2 = x_ref[...] + mix
    xn = x2 * lax.rsqrt(jnp.mean(x2 * x2, axis=-1, keepdims=True) + NORM_EPS) * gf_ref[...]
    x2_ref[...] = x2
    xn_ref[...] = xn
    qt = lax.dot_general(wqt_ref[...], xn.astype(BF16), (((1,), (1,)), ((), ())),
                         preferred_element_type=F32)
    qh = keys_ref.shape[2]
    for hc in range(keys_ref.shape[0]):
        sc_ref[hc] = jnp.dot(keys_ref[hc], qt[hc * qh:(hc + 1) * qh, :].astype(BF16),
                             preferred_element_type=F32)


def _outproj(x2d, fox, y, bonus, g, lnx_w, lnx_b, w_out, g_ffn, w_q, sub_keys):
    n, d = x2d.shape
    w = y.shape[1]
    fw = fox.shape[1]
    wa = w_out[:fw].astype(BF16)
    wb = w_out[fw:].astype(BF16)
    wqt = w_q.T.astype(BF16)
    nkeys, qh = sub_keys.shape[-2:]
    keys = sub_keys.reshape(-1, nkeys, qh).astype(BF16)
    nhc = keys.shape[0]
    tm = _row_tile(n, 512)
    row = lambda c: pl.BlockSpec((tm, c), lambda i: (i, 0))
    full = lambda a: pl.BlockSpec(a.shape, lambda i: (0,) * a.ndim)
    consts = [lnx_w.reshape(1, w), lnx_b.reshape(1, w), wa, wb, g_ffn.reshape(1, d), wqt, keys]
    return pl.pallas_call(
        _outproj_kernel,
        grid=(n // tm,),
        in_specs=[row(d), row(fw), row(w), row(w), row(w)] + [full(c) for c in consts],
        out_specs=[row(d), row(d), pl.BlockSpec((nhc, nkeys, tm), lambda i: (0, 0, i))],
        out_shape=[jax.ShapeDtypeStruct((n, d), F32), jax.ShapeDtypeStruct((n, d), F32),
                   jax.ShapeDtypeStruct((nhc, nkeys, n), F32)],
        compiler_params=_cparams(("parallel",)),
        name="outproj_scores",
    )(x2d, fox, y, bonus, g, *consts)


def _topk_rows(s, payload, k):
    rows = lax.broadcasted_iota(jnp.int32, s.shape, 0)
    nrow = s.shape[0]
    vals, idxs, pays = [], [], []
    for _ in range(k):
        m = jnp.max(s, axis=0, keepdims=True)
        idx = jnp.min(jnp.where(s == m, rows, nrow), axis=0, keepdims=True)
        hit = rows == idx
        vals.append(m)
        idxs.append(idx)
        if payload is not None:
            pays.append(jnp.max(jnp.where(hit, payload, -1), axis=0, keepdims=True))
        s = jnp.where(hit, -jnp.inf, s)
    return vals, idxs, pays


def _retrieve_kernel(sc_ref, idx_ref, gate_ref, *, topk, nkeys):
    nhead = sc_ref.shape[0] // 2
    idx_rows, gate_rows = [], []
    for h in range(nhead):
        v1, i1, _ = _topk_rows(sc_ref[2 * h], None, topk)
        v2, i2, _ = _topk_rows(sc_ref[2 * h + 1], None, topk)
        v2a = jnp.concatenate(v2, axis=0)
        i2a = jnp.concatenate(i2, axis=0)
        nb = [topk // (a + 1) for a in range(topk)]
        pad = -sum(nb) % 8
        cand = jnp.concatenate([v1[a] + v2a[:nb[a]] for a in range(topk)]
                               + [jnp.full((pad, v2a.shape[1]), -jnp.inf, F32)], axis=0)
        cidx = jnp.concatenate([i1[a] * nkeys + i2a[:nb[a]] for a in range(topk)]
                               + [jnp.full((pad, v2a.shape[1]), -1, jnp.int32)], axis=0)
        top, _, eidx = _topk_rows(cand, cidx, topk)
        top = jnp.concatenate(top, axis=0)
        ex = jnp.exp(top - top[0:1])
        gate_rows.append(ex / jnp.sum(ex, axis=0, keepdims=True))
        idx_rows.extend(eidx)
    idx_ref[...] = (jnp.concatenate(idx_rows, axis=0) * HALF_TILE).T
    gate_ref[...] = jnp.concatenate(gate_rows, axis=0).T


def _retrieve(scores, topk):
    nhc, nkeys, n = scores.shape
    slots = (nhc // 2) * topk
    tt = _row_tile(n, 256)
    kern = functools.partial(_retrieve_kernel, topk=topk, nkeys=nkeys)
    return pl.pallas_call(
        kern,
        grid=(n // tt,),
        in_specs=[pl.BlockSpec((nhc, nkeys, tt), lambda i: (0, 0, i))],
        out_specs=[pl.BlockSpec((tt, slots), lambda i: (i, 0))] * 2,
        out_shape=[jax.ShapeDtypeStruct((n, slots), jnp.int32),
                   jax.ShapeDtypeStruct((n, slots), F32)],
        compiler_params=_cparams(("parallel",)),
        name="peer_retrieve",
    )(scores)


ROW_TILE = 8
HALF_TILE = ROW_TILE // 2


def _pack_table(t):
    e, d = t.shape
    assert d == ROW_TILE * LANES
    bits = lax.bitcast_convert_type(t.astype(BF16), jnp.uint16).astype(jnp.uint32)
    bits = bits.reshape(e, HALF_TILE, 2, LANES)
    word = bits[:, :, 0, :] | (bits[:, :, 1, :] << 16)
    return lax.bitcast_convert_type(word, jnp.int32).reshape(e * HALF_TILE, LANES)


def _gather_rows(tab_ref, off_ref, base, stack_ref, slots):
    tok = off_ref.at[pl.ds(base, slots)]
    for j in range(slots):
        off = pl.multiple_of(tok[j], HALF_TILE)
        stack_ref[j * HALF_TILE:(j + 1) * HALF_TILE, :] = tab_ref[pl.ds(off, HALF_TILE), :]


def _diag_mask(slots):
    shape = (ROW_TILE, slots * ROW_TILE)
    return (lax.broadcasted_iota(jnp.int32, shape, 1) % ROW_TILE
            == lax.broadcasted_iota(jnp.int32, shape, 0))


def _split2(x):
    hi = x.astype(BF16)
    return hi, (x - hi.astype(F32)).astype(BF16)


def _dot3(x, w01):
    hi = x.astype(BF16)
    r1 = x - hi.astype(F32)
    mid = r1.astype(BF16)
    lo = (r1 - mid.astype(F32)).astype(BF16)
    d = lambda a: jnp.dot(a, w01, preferred_element_type=F32)
    return d(hi) + d(mid) + d(lo)


def _peer_act_kernel(off_ref, x_ref, tab_ref, gate_ref, w_ref, stack_a, stack_b, part_ref):
    tb, slots = gate_ref.shape
    kdim = slots * ROW_TILE
    diag = _diag_mask(slots)

    def one(t, stack_ref):
        _gather_rows(tab_ref, off_ref, t * slots, stack_ref, slots)
        g = pltpu.bitcast(stack_ref[...], BF16)
        rows = pl.ds(pl.multiple_of(t * ROW_TILE, ROW_TILE), ROW_TILE)
        xh, xl = _split2(x_ref[rows, :])
        p2 = lax.dot_general(jnp.concatenate([xh, xl], axis=0), g, (((1,), (1,)), ((), ())),
                             preferred_element_type=F32)
        part_ref[rows, :] = jnp.where(diag, p2[:ROW_TILE] + p2[ROW_TILE:], 0.0)

    def pair(i, carry):
        one(2 * i, stack_a)
        one(2 * i + 1, stack_b)
        return carry

    lax.fori_loop(0, tb // 2, pair, 0)
    fold = (lax.broadcasted_iota(jnp.int32, (kdim, slots), 0) // ROW_TILE
            == lax.broadcasted_iota(jnp.int32, (kdim, slots), 1)).astype(BF16)
    act = jnp.sum(_dot3(part_ref[...], fold).reshape(tb, ROW_TILE, slots), axis=1)
    gelu = 0.5 * act * (1.0 + lax.erf(act * math.sqrt(0.5)))
    w_ref[...] = gate_ref[...] * gelu


def _peer_act(off, xn, tab, gate, *, tb):
    n, slots = gate.shape
    x8 = xn.reshape(n * ROW_TILE, LANES)
    stack = pltpu.VMEM((slots * HALF_TILE, LANES), jnp.int32)
    return pl.pallas_call(
        _peer_act_kernel,
        grid=(n // tb,),
        in_specs=[pl.BlockSpec((tb * slots,), lambda i: (i,), memory_space=pltpu.SMEM),
                  pl.BlockSpec((tb * ROW_TILE, LANES), lambda i: (i, 0)),
                  pl.BlockSpec(memory_space=pltpu.VMEM),
                  pl.BlockSpec((tb, slots), lambda i: (i, 0))],
        out_specs=pl.BlockSpec((tb, slots), lambda i: (i, 0)),
        out_shape=jax.ShapeDtypeStruct((n, slots), F32),
        scratch_shapes=[stack, stack, pltpu.VMEM((tb * ROW_TILE, slots * ROW_TILE), F32)],
        compiler_params=_cparams(("arbitrary",)),
        name="peer_expert_act",
    )(off, x8, tab, gate)


def _peer_mix_kernel(off_ref, w_ref, tab_ref, x2_ref, gfin_ref, o_ref, stack_a, stack_b, wexp_ref,
                     *, final_norm):
    tb, slots = w_ref.shape
    kdim = slots * ROW_TILE
    diag = _diag_mask(slots)
    spread = (lax.broadcasted_iota(jnp.int32, (slots, kdim), 1) // ROW_TILE
              == lax.broadcasted_iota(jnp.int32, (slots, kdim), 0)).astype(BF16)
    wexp_ref[...] = _dot3(w_ref[...], spread)

    def one(t, stack_ref):
        _gather_rows(tab_ref, off_ref, t * slots, stack_ref, slots)
        g = pltpu.bitcast(stack_ref[...], BF16)
        w8 = jnp.where(diag, jnp.broadcast_to(wexp_ref[pl.ds(t, 1), :], (ROW_TILE, kdim)), 0.0)
        wh, wl = _split2(w8)
        ff2 = jnp.dot(jnp.concatenate([wh, wl], axis=0), g, preferred_element_type=F32)
        rows = pl.ds(pl.multiple_of(t * ROW_TILE, ROW_TILE), ROW_TILE)
        o_ref[rows, :] = x2_ref[rows, :] + ff2[:ROW_TILE] + ff2[ROW_TILE:]

    def pair(i, carry):
        one(2 * i, stack_a)
        one(2 * i + 1, stack_b)
        return carry

    lax.fori_loop(0, tb // 2, pair, 0)
    if final_norm:
        x3 = o_ref[...].reshape(tb, ROW_TILE, LANES)
        sq = jnp.sum(jnp.sum(x3 * x3, axis=2, keepdims=True), axis=1, keepdims=True)
        scale = lax.rsqrt(sq * (1.0 / (ROW_TILE * LANES)) + NORM_EPS)
        o_ref[...] = (x3 * scale * gfin_ref[...][None]).reshape(tb * ROW_TILE, LANES)


def _peer_mix(off, wgt, tab, x2, g_final, *, tb, final_norm):
    n, d = x2.shape
    slots = wgt.shape[1]
    x8 = x2.reshape(n * ROW_TILE, LANES)
    g8 = g_final.reshape(ROW_TILE, LANES)
    kern = functools.partial(_peer_mix_kernel, final_norm=final_norm)
    stack = pltpu.VMEM((slots * HALF_TILE, LANES), jnp.int32)
    out = pl.pallas_call(
        kern,
        grid=(n // tb,),
        in_specs=[pl.BlockSpec((tb * slots,), lambda i: (i,), memory_space=pltpu.SMEM),
                  pl.BlockSpec((tb, slots), lambda i: (i, 0)),
                  pl.BlockSpec(memory_space=pltpu.VMEM),
                  pl.BlockSpec((tb * ROW_TILE, LANES), lambda i: (i, 0)),
                  pl.BlockSpec((ROW_TILE, LANES), lambda i: (0, 0))],
        out_specs=pl.BlockSpec((tb * ROW_TILE, LANES), lambda i: (i, 0)),
        out_shape=jax.ShapeDtypeStruct((n * ROW_TILE, LANES), F32),
        scratch_shapes=[stack, stack, pltpu.VMEM((tb, slots * ROW_TILE), F32)],
        compiler_params=_cparams(("arbitrary",)),
        name="peer_expert_mix",
    )(off, wgt, tab, x8, g8)
    return out.reshape(n, d)


def _state_to_blocks(s):
    b, h, d, _ = s.shape
    st = jnp.swapaxes(s, -1, -2).reshape(b, h // HEADS_PER_LANE_TILE, HEADS_PER_LANE_TILE, d, d)
    eye = jnp.eye(HEADS_PER_LANE_TILE, dtype=s.dtype)
    blk = st[:, :, :, :, None, :] * eye[None, None, :, None, :, None]
    return blk.reshape(b, h // HEADS_PER_LANE_TILE, LANES, LANES)


def _blocks_to_state(blk, heads):
    b, npair = blk.shape[:2]
    x = blk.reshape(b, npair, HEADS_PER_LANE_TILE, HEAD_DIM, HEADS_PER_LANE_TILE, HEAD_DIM)
    diag = jnp.stack([x[:, :, hh, :, hh, :] for hh in range(HEADS_PER_LANE_TILE)], axis=2)
    return jnp.swapaxes(diag.reshape(b, heads, HEAD_DIM, HEAD_DIM), -1, -2)


PEER_TOPK = 16


def _layer(x, k_past, v_past, lf_past, s0, shift0, lp, g_final, final_norm):
    (norm_mix_g, w_in, fox_b_f, mu, w0, w2, a0, a2, g2, k_k, k_a, r_k, lnx_w, lnx_b, w_out,
     norm_ffn_g, peer_w_q, peer_sub_keys, tab_u, tab_v) = lp
    b, t, d = x.shape
    n = b * t
    fox_heads = fox_b_f.shape[0]
    fw = fox_heads * HEAD_DIM
    fox_cols = 3 * fw + fox_heads
    rwkv_heads = r_k.shape[0]
    w = rwkv_heads * HEAD_DIM
    x2d = x.reshape(n, d)
    qb, k, v, kb, vb, lf, rw_main, rw_tail = _inproj(x2d, norm_mix_g, w_in, fox_b_f, fox_cols,
                                                     fox_heads, 3 * w)
    fox = _fox_stream(qb, kb, vb, lf, k_past, v_past, lf_past, batch=b, q_len=t)

    prm = _rwkv_params(mu, w0, w2, a0, a2, g2, k_k, k_a, r_k.reshape(-1))
    tail = prm["tail"]
    shift_main = shift0[..., :3 * w]
    shift_tail = jnp.pad(shift0[..., 3 * w:], ((0, 0), (0, 0), (0, TAIL_PAD - tail)))
    r, lw, km, vv, kn, bb, g, bonus = _rwkv_pre(rw_main, rw_tail, shift_main, shift_tail, prm,
                                                batch=b, seq=t)
    y, s_blk = _rwkv_scan(r, lw, km, vv, kn, bb, _state_to_blocks(s0), batch=b, seq=t)
    s_t = _blocks_to_state(s_blk, rwkv_heads)
    last = jnp.concatenate([rw_main.reshape(b, t, -1)[:, -1:], rw_tail.reshape(b, t, -1)[:, -1:, :tail]],
                           axis=-1)

    x2, xn, scores = _outproj(x2d, fox, y, bonus, g, lnx_w, lnx_b, w_out, norm_ffn_g, peer_w_q,
                              peer_sub_keys)
    idx, gate = _retrieve(scores, PEER_TOPK)
    tb = _row_tile(n, 128)
    slots = gate.shape[1]
    off = idx.reshape(n * slots)
    wgt = _peer_act(off, xn, tab_u, gate, tb=tb)
    out = _peer_mix(off, wgt, tab_v, x2, g_final, tb=tb, final_norm=final_norm)
    return (out.reshape(b, t, d), k.reshape(b, t, fox_heads, HEAD_DIM),
            v.reshape(b, t, fox_heads, HEAD_DIM), lf.reshape(b, t, fox_heads), s_t, last)


def kernel(x_prompt, x_sample, cache_fox_k, cache_fox_v, cache_fox_logf, state_rwkv, state_shift,
           norm_mix_g, w_in, fox_b_f, rwkv_mu, rwkv_w0, rwkv_w2, rwkv_a0, rwkv_a2, rwkv_g2,
           rwkv_k_k, rwkv_k_a, rwkv_r_k, rwkv_lnx_w, rwkv_lnx_b, w_out, norm_ffn_g,
           peer_w_q, peer_sub_keys, peer_u, peer_v, norm_final_g):
    depth = w_in.shape[0]
    yp, ys = x_prompt, x_sample
    bp = x_prompt.shape[0]
    dt = x_prompt.dtype
    fox_heads = fox_b_f.shape[1]
    rwkv_heads = rwkv_r_k.shape[1]
    rwkv_cols = rwkv_mu.shape[1]
    outs_p, outs_s = [], []
    for l in range(depth):
        lp = (norm_mix_g[l], w_in[l], fox_b_f[l], rwkv_mu[l], rwkv_w0[l], rwkv_w2[l], rwkv_a0[l],
              rwkv_a2[l], rwkv_g2[l], rwkv_k_k[l], rwkv_k_a[l], rwkv_r_k[l], rwkv_lnx_w[l],
              rwkv_lnx_b[l], w_out[l], norm_ffn_g[l], peer_w_q[l], peer_sub_keys[l],
              _pack_table(peer_u[l]), _pack_table(peer_v[l]))
        last = l == depth - 1
        empty_kv = jnp.zeros((bp, 0, fox_heads, HEAD_DIM), dt)
        empty_lf = jnp.zeros((bp, 0, fox_heads), dt)
        s_zero = jnp.zeros((bp, rwkv_heads, HEAD_DIM, HEAD_DIM), dt)
        sh_zero = jnp.zeros((bp, 1, rwkv_cols), dt)
        yp, *rest_p = _layer(yp, empty_kv, empty_kv, empty_lf, s_zero, sh_zero, lp, norm_final_g, last)
        ys, *rest_s = _layer(ys, cache_fox_k[l], cache_fox_v[l], cache_fox_logf[l], state_rwkv[l],
                             state_shift[l], lp, norm_final_g, last)
        outs_p.append(rest_p)
        outs_s.append(rest_s)
    stack = lambda outs, i: jnp.stack([o[i] for o in outs])
    return ((yp, ys) + tuple(stack(outs_p, i) for i in range(5))
            + tuple(stack(outs_s, i) for i in range(5)))
```

```python
import functools
import math

import jax
import jax.numpy as jnp
from jax import lax
from jax.experimental import pallas as pl
from jax.experimental.pallas import tpu as pltpu

F32 = jnp.float32
BF16 = jnp.bfloat16

HEAD_DIM = 64
LANES = 128
HEADS_PER_LANE_TILE = LANES // HEAD_DIM
TAIL_PAD = 2 * LANES
CHUNK = 64
RWKV_PASSES = 1
NORM_EPS = 1e-6
LNX_EPS = 64e-5
NEG_BIG = -1e30
LOG2E = math.log2(math.e)
HIGHEST = lax.Precision.HIGHEST
VMEM_LIMIT = 48 * 1024 * 1024


def _cparams(sem):
    return pltpu.CompilerParams(dimension_semantics=sem, vmem_limit_bytes=VMEM_LIMIT)


def _row_tile(n, target):
    t = min(n, target)
    assert n % t == 0, (n, t)
    return t


def _inproj_kernel(x_ref, g_ref, wqkv_ref, wf_ref, wrw_ref, wtail_ref, bf_ref,
                   q_ref, k_ref, v_ref, kb_ref, vb_ref, lf_ref, rw_ref, tail_ref):
    x = x_ref[...]
    h = x * lax.rsqrt(jnp.mean(x * x, axis=-1, keepdims=True) + NORM_EPS) * g_ref[...]
    hb = h.astype(BF16)
    fw = wqkv_ref.shape[1] // 3
    qkv = jnp.dot(hb, wqkv_ref[...], preferred_element_type=F32)
    q_ref[...] = (qkv[:, :fw] * (LOG2E / math.sqrt(HEAD_DIM))).astype(BF16)
    k = qkv[:, fw:2 * fw]
    v = qkv[:, 2 * fw:]
    k_ref[...] = k
    v_ref[...] = v
    kb_ref[...] = k.astype(BF16)
    vb_ref[...] = v.astype(BF16)
    f = jnp.dot(hb, wf_ref[...], preferred_element_type=F32) + bf_ref[...]
    lf_ref[...] = jax.nn.log_sigmoid(f)
    rw_ref[...] = jnp.dot(hb, wrw_ref[...], preferred_element_type=F32)
    tail_ref[...] = jnp.dot(hb, wtail_ref[...], preferred_element_type=F32)


def _inproj(x2d, g, w_in, b_f, fox_cols, fox_heads, rw_main):
    n, d = x2d.shape
    fw = fox_heads * HEAD_DIM
    wqkv = w_in[:, :3 * fw].astype(BF16)
    wf = w_in[:, 3 * fw:fox_cols].astype(BF16)
    wrw = w_in[:, fox_cols:fox_cols + rw_main].astype(BF16)
    wtail = w_in[:, fox_cols + rw_main:].astype(BF16)
    wtail = jnp.pad(wtail, ((0, 0), (0, TAIL_PAD - wtail.shape[1])))
    tm = _row_tile(n, 512)
    row = lambda c: pl.BlockSpec((tm, c), lambda i: (i, 0))
    full = lambda a: pl.BlockSpec(a.shape, lambda i: (0,) * a.ndim)
    g2 = g.reshape(1, d)
    bf2 = b_f.reshape(1, fox_heads)
    outs = (
        jax.ShapeDtypeStruct((n, fw), BF16),
        jax.ShapeDtypeStruct((n, fw), F32),
        jax.ShapeDtypeStruct((n, fw), F32),
        jax.ShapeDtypeStruct((n, fw), BF16),
        jax.ShapeDtypeStruct((n, fw), BF16),
        jax.ShapeDtypeStruct((n, fox_heads), F32),
        jax.ShapeDtypeStruct((n, rw_main), F32),
        jax.ShapeDtypeStruct((n, TAIL_PAD), F32),
    )
    return pl.pallas_call(
        _inproj_kernel,
        grid=(n // tm,),
        in_specs=[row(d), full(g2), full(wqkv), full(wf), full(wrw), full(wtail), full(bf2)],
        out_specs=[row(fw), row(fw), row(fw), row(fw), row(fw), row(fox_heads), row(rw_main),
                   row(TAIL_PAD)],
        out_shape=outs,
        compiler_params=_cparams(("parallel",)),
        name="inproj",
    )(x2d, g2, wqkv, wf, wrw, wtail, bf2)


def _cumsum_kernel(lf_ref, c_ref, carry):
    @pl.when(pl.program_id(1) == 0)
    def _():
        carry[...] = jnp.zeros_like(carry)

    lf = lf_ref[...]
    tc = lf.shape[0]
    r = lax.broadcasted_iota(jnp.int32, (tc, tc), 0)
    c = lax.broadcasted_iota(jnp.int32, (tc, tc), 1)
    lower = (c <= r).astype(F32)
    cc = jnp.dot(lower, lf, precision=HIGHEST, preferred_element_type=F32) + carry[...]
    c_ref[...] = cc
    carry[...] = cc[tc - 1:tc, :]


def _cumsum(lf, tc):
    b, l, nh = lf.shape
    assert l % tc == 0
    return pl.pallas_call(
        _cumsum_kernel,
        grid=(b, l // tc),
        in_specs=[pl.BlockSpec((None, tc, nh), lambda i, j: (i, j, 0))],
        out_specs=pl.BlockSpec((None, tc, nh), lambda i, j: (i, j, 0)),
        out_shape=jax.ShapeDtypeStruct((b, l, nh), F32),
        scratch_shapes=[pltpu.VMEM((1, nh), F32)],
        compiler_params=_cparams(("parallel", "arbitrary")),
        name="cumsum_logf",
    )(lf)


def _split3(x):
    hi = x.astype(BF16)
    r = x - hi.astype(F32)
    mid = r.astype(BF16)
    lo = (r - mid.astype(F32)).astype(BF16)
    return hi.astype(F32), mid.astype(F32), lo.astype(F32)


def _augment_kernel(x_ref, c_ref, o_ref, *, role):
    tm = x_ref.shape[0]
    lane = lax.broadcasted_iota(jnp.int32, (tm, LANES), 1)
    for p in range(x_ref.shape[1] // LANES):
        xp = x_ref[:, p * LANES:(p + 1) * LANES].astype(F32)
        for hh in range(HEADS_PER_LANE_TILE):
            h = p * HEADS_PER_LANE_TILE + hh
            own = (lane >= hh * HEAD_DIM) & (lane < (hh + 1) * HEAD_DIM)
            e = (lane + (1 - hh) * HEAD_DIM) % LANES
            if role == "v":
                ext = jnp.where(e == 0, 1.0, 0.0)
            else:
                c = jnp.broadcast_to(c_ref[:, h:h + 1], (tm, LANES))
                hi, mid, lo = _split3(c * LOG2E)
                sgn = 1.0 if role == "q" else -1.0
                base = 0 if role == "q" else 3
                ext = jnp.where(e == base, sgn * hi,
                                jnp.where(e == base + 1, sgn * mid,
                                          jnp.where(e == base + 2, sgn * lo,
                                                    jnp.where(e < 6, 1.0, 0.0))))
            o_ref[:, h * LANES:(h + 1) * LANES] = jnp.where(own, xp, ext).astype(BF16)


def _augment(x, c, role):
    n, w = x.shape
    nh = w // HEAD_DIM
    tm = _row_tile(n, 512) if n % 512 == 0 else n
    kern = functools.partial(_augment_kernel, role=role)
    return pl.pallas_call(
        kern,
        grid=(n // tm,),
        in_specs=[pl.BlockSpec((tm, w), lambda i: (i, 0)),
                  pl.BlockSpec((tm, nh), lambda i: (i, 0))],
        out_specs=pl.BlockSpec((tm, nh * LANES), lambda i: (i, 0)),
        out_shape=jax.ShapeDtypeStruct((n, nh * LANES), BF16),
        compiler_params=_cparams(("parallel",)),
        name="fox_augment_" + role,
    )(x, c)


Q_SUB = 128
K_SUB = 256


def _fox_kernel(qblk_ref, kblk_ref, last_ref, q_ref, k_ref, v_ref, o_ref, m_sc, acc_sc,
                *, q_off, tq, tk):
    t = pl.program_id(2)
    i = qblk_ref[t]
    j = kblk_ref[t]
    qs_n, ks_n = min(Q_SUB, tq), min(K_SUB, tk)

    @pl.when(j == 0)
    def _():
        m_sc[...] = jnp.full_like(m_sc, NEG_BIG)
        acc_sc[...] = jnp.zeros_like(acc_sc)

    q_lo = q_off + i * tq
    k_lo = j * tk

    def body(masked):
        if masked:
            diff = (lax.broadcasted_iota(jnp.int32, (qs_n, LANES), 1)
                    - lax.broadcasted_iota(jnp.int32, (qs_n, LANES), 0))
        nqs = tq // qs_n
        qrow = [slice(qs * qs_n, (qs + 1) * qs_n) for qs in range(nqs)]
        m_run = [[m_sc[hh, qrow[qs], :] for qs in range(nqs)] for hh in range(HEADS_PER_LANE_TILE)]
        a_run = [[acc_sc[hh, qrow[qs], :] for qs in range(nqs)] for hh in range(HEADS_PER_LANE_TILE)]
        for hh in range(HEADS_PER_LANE_TILE):
            cols = slice(hh * LANES, (hh + 1) * LANES)
            for ks in range(tk // ks_n):
                krows = slice(ks * ks_n, (ks + 1) * ks_n)
                k_sub = k_ref[krows, cols]
                v_sub = v_ref[krows, cols]
                for qs in range(nqs):
                    qrows = qrow[qs]
                    s = lax.dot_general(q_ref[qrows, cols], k_sub, (((1,), (1,)), ((), ())),
                                        preferred_element_type=F32)
                    parts = [s[:, c * LANES:(c + 1) * LANES] for c in range(ks_n // LANES)]
                    if masked:
                        parts = [jnp.where(diff <= q_lo - k_lo + qs * qs_n - ks * ks_n - c * LANES,
                                           pc, NEG_BIG) for c, pc in enumerate(parts)]
                    mx = parts[0]
                    for pc in parts[1:]:
                        mx = jnp.maximum(mx, pc)
                    m_old = m_run[hh][qs]
                    m_new = jnp.maximum(m_old, jnp.max(mx, axis=-1, keepdims=True))
                    alpha = jnp.exp2(m_old - m_new)
                    pr = jnp.concatenate([jnp.exp2(pc - m_new).astype(BF16) for pc in parts], axis=1)
                    pv = jnp.dot(pr, v_sub, preferred_element_type=F32)
                    a_run[hh][qs] = alpha * a_run[hh][qs] + pv
                    m_run[hh][qs] = m_new
        for hh in range(HEADS_PER_LANE_TILE):
            for qs in range(nqs):
                m_sc[hh, qrow[qs], :] = m_run[hh][qs]
                acc_sc[hh, qrow[qs], :] = a_run[hh][qs]

    fully_visible = k_lo + tk - 1 <= q_lo

    @pl.when(fully_visible)
    def _():
        body(False)

    @pl.when(jnp.logical_not(fully_visible))
    def _():
        body(True)

    @pl.when(last_ref[t] == 1)
    def _():
        lane = lax.broadcasted_iota(jnp.int32, (1, LANES), 1)
        out = jnp.zeros((tq, LANES), F32)
        for hh in range(HEADS_PER_LANE_TILE):
            in_head = (lane >= hh * HEAD_DIM) & (lane < (hh + 1) * HEAD_DIM)
            acc = acc_sc[hh]
            ones_col = (1 - hh) * HEAD_DIM
            denom = jnp.broadcast_to(acc[:, ones_col:ones_col + 1], acc.shape)
            out = jnp.where(in_head, acc / denom, out)
        o_ref[...] = out


def _fox_attend(q_aug, k_aug, v_aug, *, batch, q_len, kv_len, q_off, tq, tk):
    n, wa = q_aug.shape
    pair_w = HEADS_PER_LANE_TILE * LANES
    npair = wa // pair_w
    nq, nk = q_len // tq, kv_len // tk
    assert q_len % tq == 0 and kv_len % tk == 0

    pairs = [(i, j) for i in range(nq) for j in range(min(nk, (q_off + (i + 1) * tq - 1) // tk + 1))]
    qblk = jnp.asarray([i for i, _ in pairs], jnp.int32)
    kblk = jnp.asarray([j for _, j in pairs], jnp.int32)
    last = jnp.asarray([int(t + 1 == len(pairs) or pairs[t + 1][0] != i)
                        for t, (i, _) in enumerate(pairs)], jnp.int32)
    q_map = lambda b, p, t, qb, kb, lt: (b * nq + qb[t], p)
    kv_map = lambda b, p, t, qb, kb, lt: (b * nk + kb[t], p)
    kern = functools.partial(_fox_kernel, q_off=q_off, tq=tq, tk=tk)
    return pl.pallas_call(
        kern,
        grid_spec=pltpu.PrefetchScalarGridSpec(
            num_scalar_prefetch=3,
            grid=(batch, npair, len(pairs)),
            in_specs=[pl.BlockSpec((tq, pair_w), q_map),
                      pl.BlockSpec((tk, pair_w), kv_map),
                      pl.BlockSpec((tk, pair_w), kv_map)],
            out_specs=pl.BlockSpec((tq, LANES), q_map),
            scratch_shapes=[pltpu.VMEM((HEADS_PER_LANE_TILE, tq, LANES), F32),
                            pltpu.VMEM((HEADS_PER_LANE_TILE, tq, LANES), F32)]),
        out_shape=jax.ShapeDtypeStruct((n, npair * LANES), F32),
        compiler_params=_cparams(("parallel", "parallel", "arbitrary")),
        name="fox_attention",
    )(qblk, kblk, last, q_aug, k_aug, v_aug)


def _fox_stream(qb, kb, vb, lf, k_past, v_past, lf_past, *, batch, q_len):
    n, w = qb.shape
    nh = lf.shape[1]
    past = k_past.shape[1]
    lf_new = lf.reshape(batch, q_len, nh)
    if past == 0:
        kv_len = q_len
        k_all, v_all, lf_all = kb, vb, lf_new
        tq = tk = _row_tile(q_len, 512)
        tc = tk
    else:
        kv_len = -(-(past + q_len) // K_SUB) * K_SUB
        pad = kv_len - past - q_len

        def cat(old, new):
            old = old.reshape(batch, past, -1).astype(new.dtype)
            new = new.reshape(batch, q_len, -1)
            z = jnp.zeros((batch, pad, new.shape[-1]), new.dtype)
            return jnp.concatenate([old, new, z], axis=1)

        k_all = cat(k_past, kb).reshape(batch * kv_len, w)
        v_all = cat(v_past, vb).reshape(batch * kv_len, w)
        lf_all = cat(lf_past, lf_new)
        tc = max(t for t in range(LANES, 1024 + 1, LANES) if kv_len % t == 0)
        tq, tk = q_len, kv_len
    c = _cumsum(lf_all, tc)
    c_k = c.reshape(batch * kv_len, nh)
    c_q = c[:, past:past + q_len].reshape(n, nh)
    return _fox_attend(_augment(qb, c_q, "q"), _augment(k_all, c_k, "k"), _augment(v_all, c_k, "v"),
                       batch=batch, q_len=q_len, kv_len=kv_len, q_off=past, tq=tq, tk=tk)


def _head_sum_matrix(width):
    r = lax.broadcasted_iota(jnp.int32, (width, width), 0) // HEAD_DIM
    c = lax.broadcasted_iota(jnp.int32, (width, width), 1) // HEAD_DIM
    return (r == c).astype(BF16)


def _rwkv_pre_kernel(pm_ref, pt_ref, sm_ref, st_ref, mum_ref, mut_ref, wbig_ref, w0_ref, a0_ref,
                     kk_ref, ka_ref, rk_ref,
                     r_out, lw_out, km_out, v_out, kn_out, b_out, g_out, bonus_out,
                     carry_m, carry_t, *, lora_w, lora_a):
    @pl.when(pl.program_id(1) == 0)
    def _():
        carry_m[...] = sm_ref[...]
        carry_t[...] = st_ref[...]

    pm = pm_ref[...]
    pt = pt_ref[...]
    tm = pm.shape[0]
    w = pm.shape[1] // 3

    def shifted(p, carry):
        row = lax.broadcasted_iota(jnp.int32, p.shape, 0)
        return jnp.where(row == 0, carry[...], pltpu.roll(p, 1, 0))

    prev_m = shifted(pm, carry_m)
    prev_t = shifted(pt, carry_t)
    carry_m[...] = pm[tm - 1:tm, :]
    carry_t[...] = pt[tm - 1:tm, :]
    psm = pm + mum_ref[...] * (prev_m - pm)
    pst = pt + mut_ref[...] * (prev_t - pt)
    r = psm[:, :w]
    k = psm[:, w:2 * w]
    v = psm[:, 2 * w:]
    lane = lax.broadcasted_iota(jnp.int32, pst.shape, 1)
    z = jnp.where(lane < lora_w, jnp.tanh(pst),
                  jnp.where(lane < lora_w + lora_a, pst, jax.nn.sigmoid(pst)))
    lo = jnp.dot(z.astype(BF16), wbig_ref[...], preferred_element_type=F32)
    w_log = -jax.nn.softplus(-(w0_ref[...] + lo[:, :w])) - 0.5
    lw = -jnp.exp(w_log)
    a = jax.nn.sigmoid(a0_ref[...] + lo[:, w:2 * w])
    g = lo[:, 2 * w:]
    e = _head_sum_matrix(w)
    kk0 = k * kk_ref[...]
    n2 = _dot3(kk0 * kk0, e)
    kn = kk0 / jnp.maximum(jnp.sqrt(n2), 1e-12)
    km = k * (1.0 + (a - 1.0) * ka_ref[...])
    rk = _dot3(r * km * rk_ref[...], e)
    r_out[...] = r
    lw_out[...] = lw
    km_out[...] = km
    v_out[...] = v
    kn_out[...] = kn
    b_out[...] = kn * a
    g_out[...] = g
    bonus_out[...] = rk * v


def _rwkv_params(mu, w0, w2, a0, a2, g2, k_k, k_a, r_k):
    w = w0.shape[0]
    lora_w, lora_a, lora_g = w2.shape[0], a2.shape[0], g2.shape[0]
    w_lora = jnp.zeros((TAIL_PAD, 3 * w), F32)
    w_lora = w_lora.at[:lora_w, :w].set(w2)
    w_lora = w_lora.at[lora_w:lora_w + lora_a, w:2 * w].set(a2)
    w_lora = w_lora.at[lora_w + lora_a:lora_w + lora_a + lora_g, 2 * w:].set(g2)
    tail = mu.shape[0] - 3 * w
    return dict(
        mu_main=mu[:3 * w].reshape(1, 3 * w),
        mu_tail=jnp.pad(mu[3 * w:], (0, TAIL_PAD - tail)).reshape(1, TAIL_PAD),
        w_lora=w_lora.astype(BF16), w0=w0.reshape(1, w), a0=a0.reshape(1, w),
        k_k=k_k.reshape(1, w), k_a=k_a.reshape(1, w), r_k=r_k.reshape(1, w),
        lora_w=lora_w, lora_a=lora_a, tail=tail)


def _rwkv_pre(rw_main, rw_tail, shift_main, shift_tail, prm, *, batch, seq):
    n, w3 = rw_main.shape
    w = w3 // 3
    tm = _row_tile(seq, 512)
    nt = seq // tm
    row = lambda c: pl.BlockSpec((tm, c), lambda b, i: (b * nt + i, 0))
    per_b = lambda c: pl.BlockSpec((None, 1, c), lambda b, i: (b, 0, 0))
    full = lambda a: pl.BlockSpec(a.shape, lambda b, i: (0,) * a.ndim)
    consts = [prm["mu_main"], prm["mu_tail"], prm["w_lora"], prm["w0"], prm["a0"], prm["k_k"],
              prm["k_a"], prm["r_k"]]
    kern = functools.partial(_rwkv_pre_kernel, lora_w=prm["lora_w"], lora_a=prm["lora_a"])
    return pl.pallas_call(
        kern,
        grid=(batch, nt),
        in_specs=[row(w3), row(TAIL_PAD), per_b(w3), per_b(TAIL_PAD)] + [full(c) for c in consts],
        out_specs=[row(w)] * 8,
        out_shape=[jax.ShapeDtypeStruct((n, w), F32)] * 8,
        scratch_shapes=[pltpu.VMEM((1, w3), F32), pltpu.VMEM((1, TAIL_PAD), F32)],
        compiler_params=_cparams(("parallel", "arbitrary")),
        name="rwkv_pre",
    )(rw_main, rw_tail, shift_main, shift_tail, *consts)


def _bmm(a, b, kind, passes):
    contract = {"nn": ((2,), (1,)), "nt": ((2,), (2,)), "tn": ((1,), (1,))}[kind]
    dims = (contract, ((0,), (0,)))
    if passes == 6:
        return lax.dot_general(a, b, dims, precision=HIGHEST, preferred_element_type=F32)
    dg = lambda x, y: lax.dot_general(x, y, dims, preferred_element_type=F32)
    ah, bh = a.astype(BF16), b.astype(BF16)
    out = dg(ah, bh)
    if passes == 3:
        al = (a - ah.astype(F32)).astype(BF16)
        bl = (b - bh.astype(F32)).astype(BF16)
        out = out + dg(ah, bl) + dg(al, bh)
    return out


def _rwkv_chunk(r, lw, km, v, kn, bb, s_blk, passes):
    g, c, _ = r.shape
    c2 = HEADS_PER_LANE_TILE * c
    ti = lax.broadcasted_iota(jnp.int32, (g, c, c), 1)
    si = lax.broadcasted_iota(jnp.int32, (g, c, c), 2)
    cs = _bmm((si <= ti).astype(F32), lw, "nn", 6)
    e_pos = jnp.exp(cs)
    e_neg = jnp.exp(-cs)
    kt = kn * jnp.exp(cs - lw)
    bt = bb * e_neg
    kh = km * e_neg
    rt = r * e_pos
    g_end = e_pos[:, c - 1:c, :]

    lane = lax.broadcasted_iota(jnp.int32, (1, 1, LANES), 2)
    head_of_lane = lane // HEAD_DIM

    def stack_masked(x):
        return jnp.concatenate(
            [jnp.where(head_of_lane == hh, x, 0.0) for hh in range(HEADS_PER_LANE_TILE)], axis=1)

    def stack(x):
        return jnp.concatenate([x] * HEADS_PER_LANE_TILE, axis=1)

    def pick(x):
        out = x[:, :c]
        for hh in range(1, HEADS_PER_LANE_TILE):
            out = jnp.where(head_of_lane == hh, x[:, hh * c:(hh + 1) * c], out)
        return out

    kt2 = stack_masked(kt)
    rt2 = stack_masked(rt)
    rr = lax.broadcasted_iota(jnp.int32, (1, c2, c2), 1)
    cc = lax.broadcasted_iota(jnp.int32, (1, c2, c2), 2)
    strict_blk = (rr // c == cc // c) & (cc < rr)
    x = jnp.where(strict_blk, -_bmm(kt2, stack(bt), "nt", passes), 0.0)
    tinv = (rr == cc).astype(F32) + x
    steps = max(int(math.ceil(math.log2(c))) - 1, 0)
    for _ in range(steps):
        x = _bmm(x, x, "nn", passes)
        tinv = tinv + _bmm(tinv, x, "nn", passes)
    tr = lax.broadcasted_iota(jnp.int32, (1, c2, c), 1) % c
    sr = lax.broadcasted_iota(jnp.int32, (1, c2, c), 2)
    kk_s = jnp.where(sr < tr, _bmm(kt2, kh, "nt", passes), 0.0)
    rb_s = jnp.where(sr <= tr, _bmm(rt2, bt, "nt", passes), 0.0)
    rk_s = jnp.where(sr <= tr, _bmm(rt2, kh, "nt", passes), 0.0)

    ks = _bmm(jnp.concatenate([kt, rt], axis=1), s_blk, "nn", passes)
    rhs = ks[:, :c] + pick(_bmm(kk_s, v, "nn", passes))
    z = pick(_bmm(tinv, stack(rhs), "nn", passes))
    y = ks[:, c:] - pick(_bmm(rb_s, z, "nn", passes)) + pick(_bmm(rk_s, v, "nn", passes))
    jr = lax.broadcasted_iota(jnp.int32, (1, LANES, LANES), 1)
    ic = lax.broadcasted_iota(jnp.int32, (1, LANES, LANES), 2)
    decay_rows = jnp.swapaxes(jnp.broadcast_to(g_end, (g, LANES, LANES)), 1, 2)
    upd = _bmm(jnp.concatenate([bt * g_end, kh * g_end], axis=1),
               jnp.concatenate([-z, v], axis=1), "tn", passes)
    s_new = decay_rows * s_blk + jnp.where(jr // HEAD_DIM == ic // HEAD_DIM, upd, 0.0)
    return y, s_new


def _rwkv_scan_kernel(r_ref, lw_ref, km_ref, v_ref, kn_ref, b_ref, s0_ref, y_ref, sT_ref, s_sc,
                      *, chunk, passes):
    it = pl.program_id(0)
    nb, tb, w = r_ref.shape
    npair = w // LANES

    @pl.when(it == 0)
    def _():
        s_sc[...] = s0_ref[...]

    def step(ci, carry):
        rows = pl.ds(pl.multiple_of(ci * chunk, chunk), chunk)

        def gather(ref):
            blk = ref[:, rows, :]
            return jnp.concatenate([blk[:, :, p * LANES:(p + 1) * LANES] for p in range(npair)],
                                   axis=0)

        y, s_new = _rwkv_chunk(gather(r_ref), gather(lw_ref), gather(km_ref), gather(v_ref),
                               gather(kn_ref), gather(b_ref), s_sc[...], passes)
        for p in range(npair):
            y_ref[:, rows, p * LANES:(p + 1) * LANES] = y[p * nb:(p + 1) * nb]
        s_sc[...] = s_new
        return carry

    lax.fori_loop(0, tb // chunk, step, 0)

    @pl.when(it == pl.num_programs(0) - 1)
    def _():
        sT_ref[...] = s_sc[...]


def _rwkv_scan(r, lw, km, v, kn, bb, s0_blk, *, batch, seq, passes=RWKV_PASSES):
    n, w = r.shape
    npair = w // LANES
    chunk = min(CHUNK, seq)
    tb = _row_tile(seq, 4 * chunk)
    row = pl.BlockSpec((batch, tb, w), lambda i: (0, i, 0))
    st = pl.BlockSpec((npair * batch, LANES, LANES), lambda i: (0, 0, 0))
    s0 = jnp.swapaxes(s0_blk, 0, 1).reshape(npair * batch, LANES, LANES)
    kern = functools.partial(_rwkv_scan_kernel, chunk=chunk, passes=passes)
    y, s_t = pl.pallas_call(
        kern,
        grid=(seq // tb,),
        in_specs=[row] * 6 + [st],
        out_specs=[row, st],
        out_shape=[jax.ShapeDtypeStruct((batch, seq, w), F32),
                   jax.ShapeDtypeStruct((npair * batch, LANES, LANES), F32)],
        scratch_shapes=[pltpu.VMEM((npair * batch, LANES, LANES), F32)],
        compiler_params=_cparams(("arbitrary",)),
        name="rwkv_scan",
    )(*(a.reshape(batch, seq, w) for a in (r, lw, km, v, kn, bb)), s0)
    s_t = jnp.swapaxes(s_t.reshape(npair, batch, LANES, LANES), 0, 1)
    return y.reshape(n, w), s_t


def _outproj_kernel(x_ref, fox_ref, y_ref, bonus_ref, g_ref, lnw_ref, lnb_ref, wa_ref, wb_ref,
                    gf_ref, wqt_ref, keys_ref, x2_ref, xn_ref, sc_ref):
    y = y_ref[...]
    w = y.shape[1]
    e = _head_sum_matrix(w)
    mean = _dot3(y, e) * (1.0 / HEAD_DIM)
    d = y - mean
    var = _dot3(d * d, e) * (1.0 / HEAD_DIM)
    yn = d * lax.rsqrt(var + LNX_EPS) * lnw_ref[...] + lnb_ref[...]
    rw = (yn + bonus_ref[...]) * g_ref[...]
    mix = (jnp.dot(fox_ref[...].astype(BF16), wa_ref[...], preferred_element_type=F32)
           + jnp.dot(rw.astype(BF16), wb_ref[...], preferred_element_type=F32))
    x2 = x_ref[...] + mix
    xn = x2 * lax.rsqrt(jnp.mean(x2 * x2, axis=-1, keepdims=True) + NORM_EPS) * gf_ref[...]
    x2_ref[...] = x2.reshape(x2_ref.shape)
    xn_ref[...] = xn.reshape(xn_ref.shape)
    qt = lax.dot_general(wqt_ref[...], xn.astype(BF16), (((1,), (1,)), ((), ())),
                         preferred_element_type=F32)
    qh = keys_ref.shape[2]
    for hc in range(keys_ref.shape[0]):
        sc_ref[hc] = jnp.dot(keys_ref[hc], qt[hc * qh:(hc + 1) * qh, :].astype(BF16),
                             preferred_element_type=F32)


def _outproj(x2d, fox, y, bonus, g, lnx_w, lnx_b, w_out, g_ffn, w_q, sub_keys):
    n, d = x2d.shape
    w = y.shape[1]
    fw = fox.shape[1]
    wa = w_out[:fw].astype(BF16)
    wb = w_out[fw:].astype(BF16)
    wqt = w_q.T.astype(BF16)
    nkeys, qh = sub_keys.shape[-2:]
    keys = sub_keys.reshape(-1, nkeys, qh).astype(BF16)
    nhc = keys.shape[0]
    tm = _row_tile(n, 512)
    row = lambda c: pl.BlockSpec((tm, c), lambda i: (i, 0))
    tiles = pl.BlockSpec((tm * d // LANES, LANES), lambda i: (i, 0))
    full = lambda a: pl.BlockSpec(a.shape, lambda i: (0,) * a.ndim)
    consts = [lnx_w.reshape(1, w), lnx_b.reshape(1, w), wa, wb, g_ffn.reshape(1, d), wqt, keys]
    return pl.pallas_call(
        _outproj_kernel,
        grid=(n // tm,),
        in_specs=[row(d), row(fw), row(w), row(w), row(w)] + [full(c) for c in consts],
        out_specs=[tiles, tiles, pl.BlockSpec((nhc, nkeys, tm), lambda i: (0, 0, i))],
        out_shape=[jax.ShapeDtypeStruct((n * d // LANES, LANES), F32),
                   jax.ShapeDtypeStruct((n * d // LANES, LANES), F32),
                   jax.ShapeDtypeStruct((nhc, nkeys, n), F32)],
        compiler_params=_cparams(("parallel",)),
        name="outproj_scores",
    )(x2d, fox, y, bonus, g, *consts)


def _topk_rows(s, payload, k):
    rows = lax.broadcasted_iota(jnp.int32, s.shape, 0)
    nrow = s.shape[0]
    vals, idxs, pays = [], [], []
    for _ in range(k):
        m = jnp.max(s, axis=0, keepdims=True)
        idx = jnp.min(jnp.where(s == m, rows, nrow), axis=0, keepdims=True)
        hit = rows == idx
        vals.append(m)
        idxs.append(idx)
        if payload is not None:
            pays.append(jnp.max(jnp.where(hit, payload, -1), axis=0, keepdims=True))
        s = jnp.where(hit, -jnp.inf, s)
    return vals, idxs, pays


def _retrieve_kernel(sc_ref, idx_ref, gate_ref, *, topk, nkeys):
    nhead = sc_ref.shape[0] // 2
    idx_rows, gate_rows = [], []
    for h in range(nhead):
        v1, i1, _ = _topk_rows(sc_ref[2 * h], None, topk)
        v2, i2, _ = _topk_rows(sc_ref[2 * h + 1], None, topk)
        v2a = jnp.concatenate(v2, axis=0)
        i2a = jnp.concatenate(i2, axis=0)
        nb = [topk // (a + 1) for a in range(topk)]
        pad = -sum(nb) % 8
        cand = jnp.concatenate([v1[a] + v2a[:nb[a]] for a in range(topk)]
                               + [jnp.full((pad, v2a.shape[1]), -jnp.inf, F32)], axis=0)
        cidx = jnp.concatenate([i1[a] * nkeys + i2a[:nb[a]] for a in range(topk)]
                               + [jnp.full((pad, v2a.shape[1]), -1, jnp.int32)], axis=0)
        top, _, eidx = _topk_rows(cand, cidx, topk)
        top = jnp.concatenate(top, axis=0)
        ex = jnp.exp(top - top[0:1])
        gate_rows.append(ex / jnp.sum(ex, axis=0, keepdims=True))
        idx_rows.extend(eidx)
    idx_ref[...] = (jnp.concatenate(idx_rows, axis=0) * HALF_TILE).T
    gate_ref[...] = jnp.concatenate(gate_rows, axis=0).T


def _retrieve(scores, topk):
    nhc, nkeys, n = scores.shape
    slots = (nhc // 2) * topk
    tt = _row_tile(n, 256)
    kern = functools.partial(_retrieve_kernel, topk=topk, nkeys=nkeys)
    return pl.pallas_call(
        kern,
        grid=(n // tt,),
        in_specs=[pl.BlockSpec((nhc, nkeys, tt), lambda i: (0, 0, i))],
        out_specs=[pl.BlockSpec((tt, slots), lambda i: (i, 0))] * 2,
        out_shape=[jax.ShapeDtypeStruct((n, slots), jnp.int32),
                   jax.ShapeDtypeStruct((n, slots), F32)],
        compiler_params=_cparams(("parallel",)),
        name="peer_retrieve",
    )(scores)


ROW_TILE = 8
HALF_TILE = ROW_TILE // 2


def _pack_table(t):
    e, d = t.shape
    assert d == ROW_TILE * LANES
    bits = lax.bitcast_convert_type(t.astype(BF16), jnp.uint16).astype(jnp.uint32)
    bits = bits.reshape(e, HALF_TILE, 2, LANES)
    word = bits[:, :, 0, :] | (bits[:, :, 1, :] << 16)
    return lax.bitcast_convert_type(word, jnp.int32).reshape(e * HALF_TILE, LANES)


def _gather_rows(tab_ref, off_ref, base, stack_ref, slots):
    tok = off_ref.at[pl.ds(base, slots)]
    for j in range(slots):
        off = pl.multiple_of(tok[j], HALF_TILE)
        stack_ref[j * HALF_TILE:(j + 1) * HALF_TILE, :] = tab_ref[pl.ds(off, HALF_TILE), :]


def _diag_mask(slots):
    shape = (ROW_TILE, slots * ROW_TILE)
    return (lax.broadcasted_iota(jnp.int32, shape, 1) % ROW_TILE
            == lax.broadcasted_iota(jnp.int32, shape, 0))


def _split2(x):
    hi = x.astype(BF16)
    return hi, (x - hi.astype(F32)).astype(BF16)


def _dot3(x, w01):
    hi = x.astype(BF16)
    r1 = x - hi.astype(F32)
    mid = r1.astype(BF16)
    lo = (r1 - mid.astype(F32)).astype(BF16)
    d = lambda a: jnp.dot(a, w01, preferred_element_type=F32)
    return d(hi) + d(mid) + d(lo)


def _peer_act_kernel(off_ref, x_ref, tab_ref, gate_ref, w_ref, stack_a, stack_b, part_ref):
    tb, slots = gate_ref.shape
    kdim = slots * ROW_TILE
    diag = _diag_mask(slots)

    def one(t, stack_ref):
        _gather_rows(tab_ref, off_ref, t * slots, stack_ref, slots)
        g = pltpu.bitcast(stack_ref[...], BF16)
        rows = pl.ds(pl.multiple_of(t * ROW_TILE, ROW_TILE), ROW_TILE)
        xh, xl = _split2(x_ref[rows, :])
        p2 = lax.dot_general(jnp.concatenate([xh, xl], axis=0), g, (((1,), (1,)), ((), ())),
                             preferred_element_type=F32)
        part_ref[rows, :] = jnp.where(diag, p2[:ROW_TILE] + p2[ROW_TILE:], 0.0)

    def pair(i, carry):
        one(2 * i, stack_a)
        one(2 * i + 1, stack_b)
        return carry

    lax.fori_loop(0, tb // 2, pair, 0)
    fold = (lax.broadcasted_iota(jnp.int32, (kdim, slots), 0) // ROW_TILE
            == lax.broadcasted_iota(jnp.int32, (kdim, slots), 1)).astype(BF16)
    act = jnp.sum(_dot3(part_ref[...], fold).reshape(tb, ROW_TILE, slots), axis=1)
    gelu = 0.5 * act * (1.0 + lax.erf(act * math.sqrt(0.5)))
    w_ref[...] = gate_ref[...] * gelu


def _peer_act(off, x8, tab, gate, *, tb):
    n, slots = gate.shape
    stack = pltpu.VMEM((slots * HALF_TILE, LANES), jnp.int32)
    return pl.pallas_call(
        _peer_act_kernel,
        grid=(n // tb,),
        in_specs=[pl.BlockSpec((tb * slots,), lambda i: (i,), memory_space=pltpu.SMEM),
                  pl.BlockSpec((tb * ROW_TILE, LANES), lambda i: (i, 0)),
                  pl.BlockSpec(memory_space=pltpu.VMEM),
                  pl.BlockSpec((tb, slots), lambda i: (i, 0))],
        out_specs=pl.BlockSpec((tb, slots), lambda i: (i, 0)),
        out_shape=jax.ShapeDtypeStruct((n, slots), F32),
        scratch_shapes=[stack, stack, pltpu.VMEM((tb * ROW_TILE, slots * ROW_TILE), F32)],
        compiler_params=_cparams(("arbitrary",)),
        name="peer_expert_act",
    )(off, x8, tab, gate)


def _peer_mix_kernel(off_ref, w_ref, tab_ref, x2_ref, gfin_ref, o_ref, stack_a, stack_b, wexp_ref,
                     *, final_norm):
    tb, slots = w_ref.shape
    kdim = slots * ROW_TILE
    diag = _diag_mask(slots)
    spread = (lax.broadcasted_iota(jnp.int32, (slots, kdim), 1) // ROW_TILE
              == lax.broadcasted_iota(jnp.int32, (slots, kdim), 0)).astype(BF16)
    wexp_ref[...] = _dot3(w_ref[...], spread)

    def one(t, stack_ref):
        _gather_rows(tab_ref, off_ref, t * slots, stack_ref, slots)
        g = pltpu.bitcast(stack_ref[...], BF16)
        w8 = jnp.where(diag, jnp.broadcast_to(wexp_ref[pl.ds(t, 1), :], (ROW_TILE, kdim)), 0.0)
        wh, wl = _split2(w8)
        ff2 = jnp.dot(jnp.concatenate([wh, wl], axis=0), g, preferred_element_type=F32)
        rows = pl.ds(pl.multiple_of(t * ROW_TILE, ROW_TILE), ROW_TILE)
        o_ref[rows, :] = x2_ref[rows, :] + ff2[:ROW_TILE] + ff2[ROW_TILE:]

    def pair(i, carry):
        one(2 * i, stack_a)
        one(2 * i + 1, stack_b)
        return carry

    lax.fori_loop(0, tb // 2, pair, 0)
    if final_norm:
        x3 = o_ref[...].reshape(tb, ROW_TILE, LANES)
        sq = jnp.sum(jnp.sum(x3 * x3, axis=2, keepdims=True), axis=1, keepdims=True)
        scale = lax.rsqrt(sq * (1.0 / (ROW_TILE * LANES)) + NORM_EPS)
        o_ref[...] = (x3 * scale * gfin_ref[...][None]).reshape(tb * ROW_TILE, LANES)


def _peer_mix(off, wgt, tab, x8, g_final, *, tb, final_norm):
    n, slots = wgt.shape
    d = ROW_TILE * LANES
    g8 = g_final.reshape(ROW_TILE, LANES)
    kern = functools.partial(_peer_mix_kernel, final_norm=final_norm)
    stack = pltpu.VMEM((slots * HALF_TILE, LANES), jnp.int32)
    out = pl.pallas_call(
        kern,
        grid=(n // tb,),
        in_specs=[pl.BlockSpec((tb * slots,), lambda i: (i,), memory_space=pltpu.SMEM),
                  pl.BlockSpec((tb, slots), lambda i: (i, 0)),
                  pl.BlockSpec(memory_space=pltpu.VMEM),
                  pl.BlockSpec((tb * ROW_TILE, LANES), lambda i: (i, 0)),
                  pl.BlockSpec((ROW_TILE, LANES), lambda i: (0, 0))],
        out_specs=pl.BlockSpec((tb * ROW_TILE, LANES), lambda i: (i, 0)),
        out_shape=jax.ShapeDtypeStruct((n * ROW_TILE, LANES), F32),
        scratch_shapes=[stack, stack, pltpu.VMEM((tb, slots * ROW_TILE), F32)],
        compiler_params=_cparams(("arbitrary",)),
        name="peer_expert_mix",
    )(off, wgt, tab, x8, g8)
    return out.reshape(n, d)


def _state_to_blocks(s):
    b, h, d, _ = s.shape
    st = jnp.swapaxes(s, -1, -2).reshape(b, h // HEADS_PER_LANE_TILE, HEADS_PER_LANE_TILE, d, d)
    eye = jnp.eye(HEADS_PER_LANE_TILE, dtype=s.dtype)
    blk = st[:, :, :, :, None, :] * eye[None, None, :, None, :, None]
    return blk.reshape(b, h // HEADS_PER_LANE_TILE, LANES, LANES)


def _blocks_to_state(blk, heads):
    b, npair = blk.shape[:2]
    x = blk.reshape(b, npair, HEADS_PER_LANE_TILE, HEAD_DIM, HEADS_PER_LANE_TILE, HEAD_DIM)
    diag = jnp.stack([x[:, :, hh, :, hh, :] for hh in range(HEADS_PER_LANE_TILE)], axis=2)
    return jnp.swapaxes(diag.reshape(b, heads, HEAD_DIM, HEAD_DIM), -1, -2)


PEER_TOPK = 16


def _layer(x, k_past, v_past, lf_past, s0, shift0, lp, g_final, final_norm):
    (norm_mix_g, w_in, fox_b_f, mu, w0, w2, a0, a2, g2, k_k, k_a, r_k, lnx_w, lnx_b, w_out,
     norm_ffn_g, peer_w_q, peer_sub_keys, tab_u, tab_v) = lp
    b, t, d = x.shape
    n = b * t
    fox_heads = fox_b_f.shape[0]
    fw = fox_heads * HEAD_DIM
    fox_cols = 3 * fw + fox_heads
    rwkv_heads = r_k.shape[0]
    w = rwkv_heads * HEAD_DIM
    x2d = x.reshape(n, d)
    qb, k, v, kb, vb, lf, rw_main, rw_tail = _inproj(x2d, norm_mix_g, w_in, fox_b_f, fox_cols,
                                                     fox_heads, 3 * w)
    fox = _fox_stream(qb, kb, vb, lf, k_past, v_past, lf_past, batch=b, q_len=t)

    prm = _rwkv_params(mu, w0, w2, a0, a2, g2, k_k, k_a, r_k.reshape(-1))
    tail = prm["tail"]
    shift_main = shift0[..., :3 * w]
    shift_tail = jnp.pad(shift0[..., 3 * w:], ((0, 0), (0, 0), (0, TAIL_PAD - tail)))
    r, lw, km, vv, kn, bb, g, bonus = _rwkv_pre(rw_main, rw_tail, shift_main, shift_tail, prm,
                                                batch=b, seq=t)
    y, s_blk = _rwkv_scan(r, lw, km, vv, kn, bb, _state_to_blocks(s0), batch=b, seq=t)
    s_t = _blocks_to_state(s_blk, rwkv_heads)
    last = jnp.concatenate([rw_main.reshape(b, t, -1)[:, -1:], rw_tail.reshape(b, t, -1)[:, -1:, :tail]],
                           axis=-1)

    x2, xn, scores = _outproj(x2d, fox, y, bonus, g, lnx_w, lnx_b, w_out, norm_ffn_g, peer_w_q,
                              peer_sub_keys)
    idx, gate = _retrieve(scores, PEER_TOPK)
    tb = _row_tile(n, 128)
    slots = gate.shape[1]
    off = idx.reshape(n * slots)
    wgt = _peer_act(off, xn, tab_u, gate, tb=tb)
    out = _peer_mix(off, wgt, tab_v, x2, g_final, tb=tb, final_norm=final_norm)
    return (out.reshape(b, t, d), k.reshape(b, t, fox_heads, HEAD_DIM),
            v.reshape(b, t, fox_heads, HEAD_DIM), lf.reshape(b, t, fox_heads), s_t, last)


def kernel(x_prompt, x_sample, cache_fox_k, cache_fox_v, cache_fox_logf, state_rwkv, state_shift,
           norm_mix_g, w_in, fox_b_f, rwkv_mu, rwkv_w0, rwkv_w2, rwkv_a0, rwkv_a2, rwkv_g2,
           rwkv_k_k, rwkv_k_a, rwkv_r_k, rwkv_lnx_w, rwkv_lnx_b, w_out, norm_ffn_g,
           peer_w_q, peer_sub_keys, peer_u, peer_v, norm_final_g):
    depth = w_in.shape[0]
    yp, ys = x_prompt, x_sample
    bp = x_prompt.shape[0]
    dt = x_prompt.dtype
    fox_heads = fox_b_f.shape[1]
    rwkv_heads = rwkv_r_k.shape[1]
    rwkv_cols = rwkv_mu.shape[1]
    outs_p, outs_s = [], []
    for l in range(depth):
        lp = (norm_mix_g[l], w_in[l], fox_b_f[l], rwkv_mu[l], rwkv_w0[l], rwkv_w2[l], rwkv_a0[l],
              rwkv_a2[l], rwkv_g2[l], rwkv_k_k[l], rwkv_k_a[l], rwkv_r_k[l], rwkv_lnx_w[l],
              rwkv_lnx_b[l], w_out[l], norm_ffn_g[l], peer_w_q[l], peer_sub_keys[l],
              _pack_table(peer_u[l]), _pack_table(peer_v[l]))
        last = l == depth - 1
        empty_kv = jnp.zeros((bp, 0, fox_heads, HEAD_DIM), dt)
        empty_lf = jnp.zeros((bp, 0, fox_heads), dt)
        s_zero = jnp.zeros((bp, rwkv_heads, HEAD_DIM, HEAD_DIM), dt)
        sh_zero = jnp.zeros((bp, 1, rwkv_cols), dt)
        yp, *rest_p = _layer(yp, empty_kv, empty_kv, empty_lf, s_zero, sh_zero, lp, norm_final_g, last)
        ys, *rest_s = _layer(ys, cache_fox_k[l], cache_fox_v[l], cache_fox_logf[l], state_rwkv[l],
                             state_shift[l], lp, norm_final_g, last)
        outs_p.append(rest_p)
        outs_s.append(rest_s)
    stack = lambda outs, i: jnp.stack([o[i] for o in outs])
    return ((yp, ys) + tuple(stack(outs_p, i) for i in range(5))
            + tuple(stack(outs_s, i) for i in range(5)))
```

```python
import functools
import math

import jax
import jax.numpy as jnp
from jax import lax
from jax.experimental import pallas as pl
from jax.experimental.pallas import tpu as pltpu

F32 = jnp.float32
BF16 = jnp.bfloat16

HEAD_DIM = 64
LANES = 128
HEADS_PER_LANE_TILE = LANES // HEAD_DIM
TAIL_PAD = 2 * LANES
CHUNK = 64
RWKV_PASSES = 1
NORM_EPS = 1e-6
LNX_EPS = 64e-5
NEG_BIG = -1e30
LOG2E = math.log2(math.e)
HIGHEST = lax.Precision.HIGHEST
VMEM_LIMIT = 48 * 1024 * 1024


def _cparams(sem):
    return pltpu.CompilerParams(dimension_semantics=sem, vmem_limit_bytes=VMEM_LIMIT)


def _row_tile(n, target):
    t = min(n, target)
    assert n % t == 0, (n, t)
    return t


def _inproj_kernel(x_ref, g_ref, wqkv_ref, wf_ref, wrw_ref, wtail_ref, bf_ref,
                   q_ref, k_ref, v_ref, kb_ref, vb_ref, lf_ref, rw_ref, tail_ref):
    x = x_ref[...]
    h = x * lax.rsqrt(jnp.mean(x * x, axis=-1, keepdims=True) + NORM_EPS) * g_ref[...]
    hb = h.astype(BF16)
    fw = wqkv_ref.shape[1] // 3
    qkv = jnp.dot(hb, wqkv_ref[...], preferred_element_type=F32)
    q_ref[...] = (qkv[:, :fw] * (LOG2E / math.sqrt(HEAD_DIM))).astype(BF16)
    k = qkv[:, fw:2 * fw]
    v = qkv[:, 2 * fw:]
    k_ref[...] = k
    v_ref[...] = v
    kb_ref[...] = k.astype(BF16)
    vb_ref[...] = v.astype(BF16)
    f = jnp.dot(hb, wf_ref[...], preferred_element_type=F32) + bf_ref[...]
    lf_ref[...] = jax.nn.log_sigmoid(f)
    rw_ref[...] = jnp.dot(hb, wrw_ref[...], preferred_element_type=F32)
    tail_ref[...] = jnp.dot(hb, wtail_ref[...], preferred_element_type=F32)


def _inproj(x2d, g, w_in, b_f, fox_cols, fox_heads, rw_main):
    n, d = x2d.shape
    fw = fox_heads * HEAD_DIM
    wqkv = w_in[:, :3 * fw].astype(BF16)
    wf = w_in[:, 3 * fw:fox_cols].astype(BF16)
    wrw = w_in[:, fox_cols:fox_cols + rw_main].astype(BF16)
    wtail = w_in[:, fox_cols + rw_main:].astype(BF16)
    wtail = jnp.pad(wtail, ((0, 0), (0, TAIL_PAD - wtail.shape[1])))
    tm = _row_tile(n, 512)
    row = lambda c: pl.BlockSpec((tm, c), lambda i: (i, 0))
    full = lambda a: pl.BlockSpec(a.shape, lambda i: (0,) * a.ndim)
    g2 = g.reshape(1, d)
    bf2 = b_f.reshape(1, fox_heads)
    outs = (
        jax.ShapeDtypeStruct((n, fw), BF16),
        jax.ShapeDtypeStruct((n, fw), F32),
        jax.ShapeDtypeStruct((n, fw), F32),
        jax.ShapeDtypeStruct((n, fw), BF16),
        jax.ShapeDtypeStruct((n, fw), BF16),
        jax.ShapeDtypeStruct((n, fox_heads), F32),
        jax.ShapeDtypeStruct((n, rw_main), F32),
        jax.ShapeDtypeStruct((n, TAIL_PAD), F32),
    )
    return pl.pallas_call(
        _inproj_kernel,
        grid=(n // tm,),
        in_specs=[row(d), full(g2), full(wqkv), full(wf), full(wrw), full(wtail), full(bf2)],
        out_specs=[row(fw), row(fw), row(fw), row(fw), row(fw), row(fox_heads), row(rw_main),
                   row(TAIL_PAD)],
        out_shape=outs,
        compiler_params=_cparams(("parallel",)),
        name="inproj",
    )(x2d, g2, wqkv, wf, wrw, wtail, bf2)


def _cumsum_kernel(lf_ref, c_ref, carry):
    @pl.when(pl.program_id(1) == 0)
    def _():
        carry[...] = jnp.zeros_like(carry)

    lf = lf_ref[...]
    tc = lf.shape[0]
    r = lax.broadcasted_iota(jnp.int32, (tc, tc), 0)
    c = lax.broadcasted_iota(jnp.int32, (tc, tc), 1)
    lower = (c <= r).astype(F32)
    cc = jnp.dot(lower, lf, precision=HIGHEST, preferred_element_type=F32) + carry[...]
    c_ref[...] = cc
    carry[...] = cc[tc - 1:tc, :]


def _cumsum(lf, tc):
    b, l, nh = lf.shape
    assert l % tc == 0
    return pl.pallas_call(
        _cumsum_kernel,
        grid=(b, l // tc),
        in_specs=[pl.BlockSpec((None, tc, nh), lambda i, j: (i, j, 0))],
        out_specs=pl.BlockSpec((None, tc, nh), lambda i, j: (i, j, 0)),
        out_shape=jax.ShapeDtypeStruct((b, l, nh), F32),
        scratch_shapes=[pltpu.VMEM((1, nh), F32)],
        compiler_params=_cparams(("parallel", "arbitrary")),
        name="cumsum_logf",
    )(lf)


def _split3(x):
    hi = x.astype(BF16)
    r = x - hi.astype(F32)
    mid = r.astype(BF16)
    lo = (r - mid.astype(F32)).astype(BF16)
    return hi.astype(F32), mid.astype(F32), lo.astype(F32)


def _augment_kernel(x_ref, c_ref, o_ref, *, role):
    tm = x_ref.shape[0]
    lane = lax.broadcasted_iota(jnp.int32, (tm, LANES), 1)
    for p in range(x_ref.shape[1] // LANES):
        xp = x_ref[:, p * LANES:(p + 1) * LANES].astype(F32)
        for hh in range(HEADS_PER_LANE_TILE):
            h = p * HEADS_PER_LANE_TILE + hh
            own = (lane >= hh * HEAD_DIM) & (lane < (hh + 1) * HEAD_DIM)
            e = (lane + (1 - hh) * HEAD_DIM) % LANES
            if role == "v":
                ext = jnp.where(e == 0, 1.0, 0.0)
            else:
                c = jnp.broadcast_to(c_ref[:, h:h + 1], (tm, LANES))
                hi, mid, lo = _split3(c * LOG2E)
                sgn = 1.0 if role == "q" else -1.0
                base = 0 if role == "q" else 3
                ext = jnp.where(e == base, sgn * hi,
                                jnp.where(e == base + 1, sgn * mid,
                                          jnp.where(e == base + 2, sgn * lo,
                                                    jnp.where(e < 6, 1.0, 0.0))))
            o_ref[:, h * LANES:(h + 1) * LANES] = jnp.where(own, xp, ext).astype(BF16)


def _augment(x, c, role):
    n, w = x.shape
    nh = w // HEAD_DIM
    tm = _row_tile(n, 512) if n % 512 == 0 else n
    kern = functools.partial(_augment_kernel, role=role)
    return pl.pallas_call(
        kern,
        grid=(n // tm,),
        in_specs=[pl.BlockSpec((tm, w), lambda i: (i, 0)),
                  pl.BlockSpec((tm, nh), lambda i: (i, 0))],
        out_specs=pl.BlockSpec((tm, nh * LANES), lambda i: (i, 0)),
        out_shape=jax.ShapeDtypeStruct((n, nh * LANES), BF16),
        compiler_params=_cparams(("parallel",)),
        name="fox_augment_" + role,
    )(x, c)


Q_SUB = 128
K_SUB = 256


def _fox_kernel(qblk_ref, kblk_ref, last_ref, q_ref, k_ref, v_ref, o_ref, m_sc, acc_sc,
                *, q_off, tq, tk):
    t = pl.program_id(2)
    i = qblk_ref[t]
    j = kblk_ref[t]
    qs_n, ks_n = min(Q_SUB, tq), min(K_SUB, tk)

    @pl.when(j == 0)
    def _():
        m_sc[...] = jnp.full_like(m_sc, NEG_BIG)
        acc_sc[...] = jnp.zeros_like(acc_sc)

    q_lo = q_off + i * tq
    k_lo = j * tk

    def body(masked):
        if masked:
            diff = (lax.broadcasted_iota(jnp.int32, (qs_n, LANES), 1)
                    - lax.broadcasted_iota(jnp.int32, (qs_n, LANES), 0))
        nqs = tq // qs_n
        qrow = [slice(qs * qs_n, (qs + 1) * qs_n) for qs in range(nqs)]
        m_run = [[m_sc[hh, qrow[qs], :] for qs in range(nqs)] for hh in range(HEADS_PER_LANE_TILE)]
        a_run = [[acc_sc[hh, qrow[qs], :] for qs in range(nqs)] for hh in range(HEADS_PER_LANE_TILE)]
        for hh in range(HEADS_PER_LANE_TILE):
            cols = slice(hh * LANES, (hh + 1) * LANES)
            for ks in range(tk // ks_n):
                krows = slice(ks * ks_n, (ks + 1) * ks_n)
                k_sub = k_ref[krows, cols]
                v_sub = v_ref[krows, cols]
                for qs in range(nqs):
                    qrows = qrow[qs]
                    s = lax.dot_general(q_ref[qrows, cols], k_sub, (((1,), (1,)), ((), ())),
                                        preferred_element_type=F32)
                    parts = [s[:, c * LANES:(c + 1) * LANES] for c in range(ks_n // LANES)]
                    if masked:
                        parts = [jnp.where(diff <= q_lo - k_lo + qs * qs_n - ks * ks_n - c * LANES,
                                           pc, NEG_BIG) for c, pc in enumerate(parts)]
                    mx = parts[0]
                    for pc in parts[1:]:
                        mx = jnp.maximum(mx, pc)
                    m_old = m_run[hh][qs]
                    m_new = jnp.maximum(m_old, jnp.max(mx, axis=-1, keepdims=True))
                    alpha = jnp.exp2(m_old - m_new)
                    pr = jnp.concatenate([jnp.exp2(pc - m_new).astype(BF16) for pc in parts], axis=1)
                    pv = jnp.dot(pr, v_sub, preferred_element_type=F32)
                    a_run[hh][qs] = alpha * a_run[hh][qs] + pv
                    m_run[hh][qs] = m_new
        for hh in range(HEADS_PER_LANE_TILE):
            for qs in range(nqs):
                m_sc[hh, qrow[qs], :] = m_run[hh][qs]
                acc_sc[hh, qrow[qs], :] = a_run[hh][qs]

    fully_visible = k_lo + tk - 1 <= q_lo

    @pl.when(fully_visible)
    def _():
        body(False)

    @pl.when(jnp.logical_not(fully_visible))
    def _():
        body(True)

    @pl.when(last_ref[t] == 1)
    def _():
        lane = lax.broadcasted_iota(jnp.int32, (1, LANES), 1)
        out = jnp.zeros((tq, LANES), F32)
        for hh in range(HEADS_PER_LANE_TILE):
            in_head = (lane >= hh * HEAD_DIM) & (lane < (hh + 1) * HEAD_DIM)
            acc = acc_sc[hh]
            ones_col = (1 - hh) * HEAD_DIM
            denom = jnp.broadcast_to(acc[:, ones_col:ones_col + 1], acc.shape)
            out = jnp.where(in_head, acc / denom, out)
        o_ref[...] = out


def _fox_attend(q_aug, k_aug, v_aug, *, batch, q_len, kv_len, q_off, tq, tk):
    n, wa = q_aug.shape
    pair_w = HEADS_PER_LANE_TILE * LANES
    npair = wa // pair_w
    nq, nk = q_len // tq, kv_len // tk
    assert q_len % tq == 0 and kv_len % tk == 0

    pairs = [(i, j) for i in range(nq) for j in range(min(nk, (q_off + (i + 1) * tq - 1) // tk + 1))]
    qblk = jnp.asarray([i for i, _ in pairs], jnp.int32)
    kblk = jnp.asarray([j for _, j in pairs], jnp.int32)
    last = jnp.asarray([int(t + 1 == len(pairs) or pairs[t + 1][0] != i)
                        for t, (i, _) in enumerate(pairs)], jnp.int32)
    q_map = lambda b, p, t, qb, kb, lt: (b * nq + qb[t], p)
    kv_map = lambda b, p, t, qb, kb, lt: (b * nk + kb[t], p)
    kern = functools.partial(_fox_kernel, q_off=q_off, tq=tq, tk=tk)
    return pl.pallas_call(
        kern,
        grid_spec=pltpu.PrefetchScalarGridSpec(
            num_scalar_prefetch=3,
            grid=(batch, npair, len(pairs)),
            in_specs=[pl.BlockSpec((tq, pair_w), q_map),
                      pl.BlockSpec((tk, pair_w), kv_map),
                      pl.BlockSpec((tk, pair_w), kv_map)],
            out_specs=pl.BlockSpec((tq, LANES), q_map),
            scratch_shapes=[pltpu.VMEM((HEADS_PER_LANE_TILE, tq, LANES), F32),
                            pltpu.VMEM((HEADS_PER_LANE_TILE, tq, LANES), F32)]),
        out_shape=jax.ShapeDtypeStruct((n, npair * LANES), F32),
        compiler_params=_cparams(("parallel", "parallel", "arbitrary")),
        name="fox_attention",
    )(qblk, kblk, last, q_aug, k_aug, v_aug)


def _fox_stream(qb, kb, vb, lf, k_past, v_past, lf_past, *, batch, q_len):
    n, w = qb.shape
    nh = lf.shape[1]
    past = k_past.shape[1]
    lf_new = lf.reshape(batch, q_len, nh)
    if past == 0:
        kv_len = q_len
        k_all, v_all, lf_all = kb, vb, lf_new
        tq = tk = _row_tile(q_len, 512)
        tc = tk
    else:
        kv_len = -(-(past + q_len) // K_SUB) * K_SUB
        pad = kv_len - past - q_len

        def cat(old, new):
            old = old.reshape(batch, past, -1).astype(new.dtype)
            new = new.reshape(batch, q_len, -1)
            z = jnp.zeros((batch, pad, new.shape[-1]), new.dtype)
            return jnp.concatenate([old, new, z], axis=1)

        k_all = cat(k_past, kb).reshape(batch * kv_len, w)
        v_all = cat(v_past, vb).reshape(batch * kv_len, w)
        lf_all = cat(lf_past, lf_new)
        tc = max(t for t in range(LANES, 1024 + 1, LANES) if kv_len % t == 0)
        tq, tk = q_len, kv_len
    c = _cumsum(lf_all, tc)
    c_k = c.reshape(batch * kv_len, nh)
    c_q = c[:, past:past + q_len].reshape(n, nh)
    return _fox_attend(_augment(qb, c_q, "q"), _augment(k_all, c_k, "k"), _augment(v_all, c_k, "v"),
                       batch=batch, q_len=q_len, kv_len=kv_len, q_off=past, tq=tq, tk=tk)


def _head_sum_matrix(width):
    r = lax.broadcasted_iota(jnp.int32, (width, width), 0) // HEAD_DIM
    c = lax.broadcasted_iota(jnp.int32, (width, width), 1) // HEAD_DIM
    return (r == c).astype(BF16)


def _rwkv_pre_kernel(pm_ref, pt_ref, sm_ref, st_ref, mum_ref, mut_ref, wbig_ref, w0_ref, a0_ref,
                     kk_ref, ka_ref, rk_ref,
                     r_out, lw_out, km_out, v_out, kn_out, b_out, g_out, bonus_out,
                     carry_m, carry_t, *, lora_w, lora_a):
    @pl.when(pl.program_id(1) == 0)
    def _():
        carry_m[...] = sm_ref[...]
        carry_t[...] = st_ref[...]

    pm = pm_ref[...]
    pt = pt_ref[...]
    tm = pm.shape[0]
    w = pm.shape[1] // 3

    def shifted(p, carry):
        row = lax.broadcasted_iota(jnp.int32, p.shape, 0)
        return jnp.where(row == 0, carry[...], pltpu.roll(p, 1, 0))

    prev_m = shifted(pm, carry_m)
    prev_t = shifted(pt, carry_t)
    carry_m[...] = pm[tm - 1:tm, :]
    carry_t[...] = pt[tm - 1:tm, :]
    psm = pm + mum_ref[...] * (prev_m - pm)
    pst = pt + mut_ref[...] * (prev_t - pt)
    r = psm[:, :w]
    k = psm[:, w:2 * w]
    v = psm[:, 2 * w:]
    lane = lax.broadcasted_iota(jnp.int32, pst.shape, 1)
    z = jnp.where(lane < lora_w, jnp.tanh(pst),
                  jnp.where(lane < lora_w + lora_a, pst, jax.nn.sigmoid(pst)))
    lo = jnp.dot(z.astype(BF16), wbig_ref[...], preferred_element_type=F32)
    w_log = -jax.nn.softplus(-(w0_ref[...] + lo[:, :w])) - 0.5
    lw = -jnp.exp(w_log)
    a = jax.nn.sigmoid(a0_ref[...] + lo[:, w:2 * w])
    g = lo[:, 2 * w:]
    e = _head_sum_matrix(w)
    kk0 = k * kk_ref[...]
    n2 = _dot3(kk0 * kk0, e)
    kn = kk0 / jnp.maximum(jnp.sqrt(n2), 1e-12)
    km = k * (1.0 + (a - 1.0) * ka_ref[...])
    rk = _dot3(r * km * rk_ref[...], e)
    r_out[...] = r
    lw_out[...] = lw
    km_out[...] = km
    v_out[...] = v
    kn_out[...] = kn
    b_out[...] = kn * a
    g_out[...] = g
    bonus_out[...] = rk * v


def _rwkv_params(mu, w0, w2, a0, a2, g2, k_k, k_a, r_k):
    w = w0.shape[0]
    lora_w, lora_a, lora_g = w2.shape[0], a2.shape[0], g2.shape[0]
    w_lora = jnp.zeros((TAIL_PAD, 3 * w), F32)
    w_lora = w_lora.at[:lora_w, :w].set(w2)
    w_lora = w_lora.at[lora_w:lora_w + lora_a, w:2 * w].set(a2)
    w_lora = w_lora.at[lora_w + lora_a:lora_w + lora_a + lora_g, 2 * w:].set(g2)
    tail = mu.shape[0] - 3 * w
    return dict(
        mu_main=mu[:3 * w].reshape(1, 3 * w),
        mu_tail=jnp.pad(mu[3 * w:], (0, TAIL_PAD - tail)).reshape(1, TAIL_PAD),
        w_lora=w_lora.astype(BF16), w0=w0.reshape(1, w), a0=a0.reshape(1, w),
        k_k=k_k.reshape(1, w), k_a=k_a.reshape(1, w), r_k=r_k.reshape(1, w),
        lora_w=lora_w, lora_a=lora_a, tail=tail)


def _rwkv_pre(rw_main, rw_tail, shift_main, shift_tail, prm, *, batch, seq):
    n, w3 = rw_main.shape
    w = w3 // 3
    tm = _row_tile(seq, 512)
    nt = seq // tm
    row = lambda c: pl.BlockSpec((tm, c), lambda b, i: (b * nt + i, 0))
    per_b = lambda c: pl.BlockSpec((None, 1, c), lambda b, i: (b, 0, 0))
    full = lambda a: pl.BlockSpec(a.shape, lambda b, i: (0,) * a.ndim)
    consts = [prm["mu_main"], prm["mu_tail"], prm["w_lora"], prm["w0"], prm["a0"], prm["k_k"],
              prm["k_a"], prm["r_k"]]
    kern = functools.partial(_rwkv_pre_kernel, lora_w=prm["lora_w"], lora_a=prm["lora_a"])
    return pl.pallas_call(
        kern,
        grid=(batch, nt),
        in_specs=[row(w3), row(TAIL_PAD), per_b(w3), per_b(TAIL_PAD)] + [full(c) for c in consts],
        out_specs=[row(w)] * 8,
        out_shape=[jax.ShapeDtypeStruct((n, w), F32)] * 8,
        scratch_shapes=[pltpu.VMEM((1, w3), F32), pltpu.VMEM((1, TAIL_PAD), F32)],
        compiler_params=_cparams(("parallel", "arbitrary")),
        name="rwkv_pre",
    )(rw_main, rw_tail, shift_main, shift_tail, *consts)


def _bmm(a, b, kind, passes):
    contract = {"nn": ((2,), (1,)), "nt": ((2,), (2,)), "tn": ((1,), (1,))}[kind]
    dims = (contract, ((0,), (0,)))
    if passes == 6:
        return lax.dot_general(a, b, dims, precision=HIGHEST, preferred_element_type=F32)
    dg = lambda x, y: lax.dot_general(x, y, dims, preferred_element_type=F32)
    ah, bh = a.astype(BF16), b.astype(BF16)
    out = dg(ah, bh)
    if passes == 3:
        al = (a - ah.astype(F32)).astype(BF16)
        bl = (b - bh.astype(F32)).astype(BF16)
        out = out + dg(ah, bl) + dg(al, bh)
    return out


def _rwkv_chunk(r, lw, km, v, kn, bb, s_blk, passes):
    g, c, _ = r.shape
    c2 = HEADS_PER_LANE_TILE * c
    ti = lax.broadcasted_iota(jnp.int32, (g, c, c), 1)
    si = lax.broadcasted_iota(jnp.int32, (g, c, c), 2)
    cs = _bmm((si <= ti).astype(F32), lw, "nn", 6)
    e_pos = jnp.exp(cs)
    e_neg = jnp.exp(-cs)
    kt = kn * jnp.exp(cs - lw)
    bt = bb * e_neg
    kh = km * e_neg
    rt = r * e_pos
    g_end = e_pos[:, c - 1:c, :]

    lane = lax.broadcasted_iota(jnp.int32, (1, 1, LANES), 2)
    head_of_lane = lane // HEAD_DIM

    def stack_masked(x):
        return jnp.concatenate(
            [jnp.where(head_of_lane == hh, x, 0.0) for hh in range(HEADS_PER_LANE_TILE)], axis=1)

    def stack(x):
        return jnp.concatenate([x] * HEADS_PER_LANE_TILE, axis=1)

    def pick(x):
        out = x[:, :c]
        for hh in range(1, HEADS_PER_LANE_TILE):
            out = jnp.where(head_of_lane == hh, x[:, hh * c:(hh + 1) * c], out)
        return out

    kt2 = stack_masked(kt)
    rt2 = stack_masked(rt)
    rr = lax.broadcasted_iota(jnp.int32, (1, c2, c2), 1)
    cc = lax.broadcasted_iota(jnp.int32, (1, c2, c2), 2)
    strict_blk = (rr // c == cc // c) & (cc < rr)
    x = jnp.where(strict_blk, -_bmm(kt2, stack(bt), "nt", passes), 0.0)
    tinv = (rr == cc).astype(F32) + x
    steps = max(int(math.ceil(math.log2(c))) - 1, 0)
    for _ in range(steps):
        x = _bmm(x, x, "nn", passes)
        tinv = tinv + _bmm(tinv, x, "nn", passes)
    tr = lax.broadcasted_iota(jnp.int32, (1, c2, c), 1) % c
    sr = lax.broadcasted_iota(jnp.int32, (1, c2, c), 2)
    kk_s = jnp.where(sr < tr, _bmm(kt2, kh, "nt", passes), 0.0)
    rb_s = jnp.where(sr <= tr, _bmm(rt2, bt, "nt", passes), 0.0)
    rk_s = jnp.where(sr <= tr, _bmm(rt2, kh, "nt", passes), 0.0)

    ks = _bmm(jnp.concatenate([kt, rt], axis=1), s_blk, "nn", passes)
    rhs = ks[:, :c] + pick(_bmm(kk_s, v, "nn", passes))
    z = pick(_bmm(tinv, stack(rhs), "nn", passes))
    y = ks[:, c:] - pick(_bmm(rb_s, z, "nn", passes)) + pick(_bmm(rk_s, v, "nn", passes))
    jr = lax.broadcasted_iota(jnp.int32, (1, LANES, LANES), 1)
    ic = lax.broadcasted_iota(jnp.int32, (1, LANES, LANES), 2)
    decay_rows = jnp.swapaxes(jnp.broadcast_to(g_end, (g, LANES, LANES)), 1, 2)
    upd = _bmm(jnp.concatenate([bt * g_end, kh * g_end], axis=1),
               jnp.concatenate([-z, v], axis=1), "tn", passes)
    s_new = decay_rows * s_blk + jnp.where(jr // HEAD_DIM == ic // HEAD_DIM, upd, 0.0)
    return y, s_new


def _rwkv_scan_kernel(r_ref, lw_ref, km_ref, v_ref, kn_ref, b_ref, s0_ref, y_ref, sT_ref, s_sc,
                      *, chunk, passes):
    it = pl.program_id(0)
    nb, tb, w = r_ref.shape
    npair = w // LANES

    @pl.when(it == 0)
    def _():
        s_sc[...] = s0_ref[...]

    def step(ci, carry):
        rows = pl.ds(pl.multiple_of(ci * chunk, chunk), chunk)

        def gather(ref):
            blk = ref[:, rows, :]
            return jnp.concatenate([blk[:, :, p * LANES:(p + 1) * LANES] for p in range(npair)],
                                   axis=0)

        y, s_new = _rwkv_chunk(gather(r_ref), gather(lw_ref), gather(km_ref), gather(v_ref),
                               gather(kn_ref), gather(b_ref), s_sc[...], passes)
        for p in range(npair):
            y_ref[:, rows, p * LANES:(p + 1) * LANES] = y[p * nb:(p + 1) * nb]
        s_sc[...] = s_new
        return carry

    lax.fori_loop(0, tb // chunk, step, 0)

    @pl.when(it == pl.num_programs(0) - 1)
    def _():
        sT_ref[...] = s_sc[...]


def _rwkv_scan(r, lw, km, v, kn, bb, s0_blk, *, batch, seq, passes=RWKV_PASSES):
    n, w = r.shape
    npair = w // LANES
    chunk = min(CHUNK, seq)
    tb = _row_tile(seq, 4 * chunk)
    row = pl.BlockSpec((batch, tb, w), lambda i: (0, i, 0))
    st = pl.BlockSpec((npair * batch, LANES, LANES), lambda i: (0, 0, 0))
    s0 = jnp.swapaxes(s0_blk, 0, 1).reshape(npair * batch, LANES, LANES)
    kern = functools.partial(_rwkv_scan_kernel, chunk=chunk, passes=passes)
    y, s_t = pl.pallas_call(
        kern,
        grid=(seq // tb,),
        in_specs=[row] * 6 + [st],
        out_specs=[row, st],
        out_shape=[jax.ShapeDtypeStruct((batch, seq, w), F32),
                   jax.ShapeDtypeStruct((npair * batch, LANES, LANES), F32)],
        scratch_shapes=[pltpu.VMEM((npair * batch, LANES, LANES), F32)],
        compiler_params=_cparams(("arbitrary",)),
        name="rwkv_scan",
    )(*(a.reshape(batch, seq, w) for a in (r, lw, km, v, kn, bb)), s0)
    s_t = jnp.swapaxes(s_t.reshape(npair, batch, LANES, LANES), 0, 1)
    return y.reshape(n, w), s_t


def _outproj_kernel(x_ref, fox_ref, y_ref, bonus_ref, g_ref, lnw_ref, lnb_ref, wa_ref, wb_ref,
                    gf_ref, wqt_ref, keys_ref, x2_ref, xn_ref, sc_ref):
    y = y_ref[...]
    w = y.shape[1]
    e = _head_sum_matrix(w)
    mean = _dot3(y, e) * (1.0 / HEAD_DIM)
    d = y - mean
    var = _dot3(d * d, e) * (1.0 / HEAD_DIM)
    yn = d * lax.rsqrt(var + LNX_EPS) * lnw_ref[...] + lnb_ref[...]
    rw = (yn + bonus_ref[...]) * g_ref[...]
    mix = (jnp.dot(fox_ref[...].astype(BF16), wa_ref[...], preferred_element_type=F32)
           + jnp.dot(rw.astype(BF16), wb_ref[...], preferred_element_type=F32))
    x2 = x_ref[...] + mix
    xn = x2 * lax.rsqrt(jnp.mean(x2 * x2, axis=-1, keepdims=True) + NORM_EPS) * gf_ref[...]
    x2_ref[...] = x2.reshape(x2_ref.shape)
    xn_ref[...] = xn.reshape(xn_ref.shape)
    qt = lax.dot_general(wqt_ref[...], xn.astype(BF16), (((1,), (1,)), ((), ())),
                         preferred_element_type=F32)
    qh = keys_ref.shape[2]
    for hc in range(keys_ref.shape[0]):
        sc_ref[hc] = jnp.dot(keys_ref[hc], qt[hc * qh:(hc + 1) * qh, :].astype(BF16),
                             preferred_element_type=F32)


def _outproj(x2d, fox, y, bonus, g, lnx_w, lnx_b, w_out, g_ffn, w_q, sub_keys):
    n, d = x2d.shape
    w = y.shape[1]
    fw = fox.shape[1]
    wa = w_out[:fw].astype(BF16)
    wb = w_out[fw:].astype(BF16)
    wqt = w_q.T.astype(BF16)
    nkeys, qh = sub_keys.shape[-2:]
    keys = sub_keys.reshape(-1, nkeys, qh).astype(BF16)
    nhc = keys.shape[0]
    tm = _row_tile(n, 512)
    row = lambda c: pl.BlockSpec((tm, c), lambda i: (i, 0))
    tiles = pl.BlockSpec((tm * d // LANES, LANES), lambda i: (i, 0))
    full = lambda a: pl.BlockSpec(a.shape, lambda i: (0,) * a.ndim)
    consts = [lnx_w.reshape(1, w), lnx_b.reshape(1, w), wa, wb, g_ffn.reshape(1, d), wqt, keys]
    return pl.pallas_call(
        _outproj_kernel,
        grid=(n // tm,),
        in_specs=[row(d), row(fw), row(w), row(w), row(w)] + [full(c) for c in consts],
        out_specs=[tiles, tiles, pl.BlockSpec((nhc, nkeys, tm), lambda i: (0, 0, i))],
        out_shape=[jax.ShapeDtypeStruct((n * d // LANES, LANES), F32),
                   jax.ShapeDtypeStruct((n * d // LANES, LANES), F32),
                   jax.ShapeDtypeStruct((nhc, nkeys, n), F32)],
        compiler_params=_cparams(("parallel",)),
        name="outproj_scores",
    )(x2d, fox, y, bonus, g, *consts)


def _topk_rows(s, payload, k):
    rows = lax.broadcasted_iota(jnp.int32, s.shape, 0)
    nrow = s.shape[0]
    vals, idxs, pays = [], [], []
    for _ in range(k):
        m = jnp.max(s, axis=0, keepdims=True)
        idx = jnp.min(jnp.where(s == m, rows, nrow), axis=0, keepdims=True)
        hit = rows == idx
        vals.append(m)
        idxs.append(idx)
        if payload is not None:
            pays.append(jnp.max(jnp.where(hit, payload, -1), axis=0, keepdims=True))
        s = jnp.where(hit, -jnp.inf, s)
    return vals, idxs, pays


def _retrieve_kernel(sc_ref, idx_ref, gate_ref, *, topk, nkeys):
    nhead = sc_ref.shape[0] // 2
    idx_rows, gate_rows = [], []
    for h in range(nhead):
        v1, i1, _ = _topk_rows(sc_ref[2 * h], None, topk)
        v2, i2, _ = _topk_rows(sc_ref[2 * h + 1], None, topk)
        v2a = jnp.concatenate(v2, axis=0)
        i2a = jnp.concatenate(i2, axis=0)
        nb = [topk // (a + 1) for a in range(topk)]
        pad = -sum(nb) % 8
        cand = jnp.concatenate([v1[a] + v2a[:nb[a]] for a in range(topk)]
                               + [jnp.full((pad, v2a.shape[1]), -jnp.inf, F32)], axis=0)
        cidx = jnp.concatenate([i1[a] * nkeys + i2a[:nb[a]] for a in range(topk)]
                               + [jnp.full((pad, v2a.shape[1]), -1, jnp.int32)], axis=0)
        top, _, eidx = _topk_rows(cand, cidx, topk)
        top = jnp.concatenate(top, axis=0)
        ex = jnp.exp(top - top[0:1])
        gate_rows.append(ex / jnp.sum(ex, axis=0, keepdims=True))
        idx_rows.extend(eidx)
    idx_ref[...] = (jnp.concatenate(idx_rows, axis=0) * HALF_TILE).T
    gate_ref[...] = jnp.concatenate(gate_rows, axis=0).T


def _retrieve(scores, topk):
    nhc, nkeys, n = scores.shape
    slots = (nhc // 2) * topk
    tt = _row_tile(n, 256)
    kern = functools.partial(_retrieve_kernel, topk=topk, nkeys=nkeys)
    return pl.pallas_call(
        kern,
        grid=(n // tt,),
        in_specs=[pl.BlockSpec((nhc, nkeys, tt), lambda i: (0, 0, i))],
        out_specs=[pl.BlockSpec((tt, slots), lambda i: (i, 0))] * 2,
        out_shape=[jax.ShapeDtypeStruct((n, slots), jnp.int32),
                   jax.ShapeDtypeStruct((n, slots), F32)],
        compiler_params=_cparams(("parallel",)),
        name="peer_retrieve",
    )(scores)


ROW_TILE = 8
HALF_TILE = ROW_TILE // 2


def _pack_table(t, split):
    e, d = t.shape
    assert d == ROW_TILE * LANES
    bits = lax.bitcast_convert_type(t.astype(BF16), jnp.uint16).astype(jnp.uint32)
    if split:
        bits = bits.reshape(e, 2, HALF_TILE, LANES)
        word = bits[:, 0] | (bits[:, 1] << 16)
    else:
        bits = bits.reshape(e, HALF_TILE, 2, LANES)
        word = bits[:, :, 0, :] | (bits[:, :, 1, :] << 16)
    return lax.bitcast_convert_type(word, jnp.int32).reshape(e * HALF_TILE, LANES)


def _gather_rows(tab_ref, off_ref, base, stack_ref, slots):
    tok = off_ref.at[pl.ds(base, slots)]
    for j in range(slots):
        off = pl.multiple_of(tok[j], HALF_TILE)
        stack_ref[j * HALF_TILE:(j + 1) * HALF_TILE, :] = tab_ref[pl.ds(off, HALF_TILE), :]


def _diag_mask(slots):
    shape = (ROW_TILE, slots * ROW_TILE)
    return (lax.broadcasted_iota(jnp.int32, shape, 1) % ROW_TILE
            == lax.broadcasted_iota(jnp.int32, shape, 0))


def _split2(x):
    hi = x.astype(BF16)
    return hi, (x - hi.astype(F32)).astype(BF16)


def _dot3(x, w01):
    hi = x.astype(BF16)
    r1 = x - hi.astype(F32)
    mid = r1.astype(BF16)
    lo = (r1 - mid.astype(F32)).astype(BF16)
    d = lambda a: jnp.dot(a, w01, preferred_element_type=F32)
    return d(hi) + d(mid) + d(lo)


def _peer_act_kernel(off_ref, x_ref, tab_ref, gate_ref, w_ref, stack_a, stack_b, part_ref):
    tb, slots = gate_ref.shape
    kdim = slots * ROW_TILE
    diag = _diag_mask(slots)

    def one(t, stack_ref):
        _gather_rows(tab_ref, off_ref, t * slots, stack_ref, slots)
        g = pltpu.bitcast(stack_ref[...], BF16)
        rows = pl.ds(pl.multiple_of(t * ROW_TILE, ROW_TILE), ROW_TILE)
        xh, xl = _split2(x_ref[rows, :])
        p2 = lax.dot_general(jnp.concatenate([xh, xl], axis=0), g, (((1,), (1,)), ((), ())),
                             preferred_element_type=F32)
        d = jnp.where(diag, p2[:ROW_TILE] + p2[ROW_TILE:], 0.0)
        part_ref[pl.ds(t, 1), :] = jnp.sum(d, axis=0, keepdims=True)

    def pair(i, carry):
        one(2 * i, stack_a)
        one(2 * i + 1, stack_b)
        return carry

    lax.fori_loop(0, tb // 2, pair, 0)
    fold = (lax.broadcasted_iota(jnp.int32, (kdim, slots), 0) // ROW_TILE
            == lax.broadcasted_iota(jnp.int32, (kdim, slots), 1)).astype(BF16)
    act = _dot3(part_ref[...], fold)
    gelu = 0.5 * act * (1.0 + lax.erf(act * math.sqrt(0.5)))
    w_ref[...] = gate_ref[...] * gelu


def _peer_act(off, x8, tab, gate, *, tb):
    n, slots = gate.shape
    stack = pltpu.VMEM((slots * HALF_TILE, LANES), jnp.int32)
    return pl.pallas_call(
        _peer_act_kernel,
        grid=(n // tb,),
        in_specs=[pl.BlockSpec((tb * slots,), lambda i: (i,), memory_space=pltpu.SMEM),
                  pl.BlockSpec((tb * ROW_TILE, LANES), lambda i: (i, 0)),
                  pl.BlockSpec(memory_space=pltpu.VMEM),
                  pl.BlockSpec((tb, slots), lambda i: (i, 0))],
        out_specs=pl.BlockSpec((tb, slots), lambda i: (i, 0)),
        out_shape=jax.ShapeDtypeStruct((n, slots), F32),
        scratch_shapes=[stack, stack, pltpu.VMEM((tb, slots * ROW_TILE), F32)],
        compiler_params=_cparams(("arbitrary",)),
        name="peer_expert_act",
    )(off, x8, tab, gate)


def _peer_mix_kernel(off_ref, w_ref, tab_ref, x2_ref, gfin_ref, o_ref, stack_a, stack_b, wbc_a,
                     wbc_b, wt_ref, *, final_norm):
    tb, slots = w_ref.shape
    nacc = 4
    wt = w_ref[...].T
    hi = wt.astype(BF16)
    r1 = wt - hi.astype(F32)
    mid = r1.astype(BF16)
    wt_ref[0] = hi
    wt_ref[1] = mid
    wt_ref[2] = (r1 - mid.astype(F32)).astype(BF16)
    sub = lax.broadcasted_iota(jnp.int32, (ROW_TILE, LANES), 0)
    tok = lax.broadcasted_iota(jnp.int32, (tb, LANES), 0)

    def one(t, stack_ref, wbc_ref):
        _gather_rows(tab_ref, off_ref, t * slots, stack_ref, slots)
        onehot = (tok == t).astype(BF16)
        wbc_ref[...] = sum(jnp.dot(wt_ref[c], onehot, preferred_element_type=F32)
                           for c in range(3))
        lo = [jnp.zeros((ROW_TILE, LANES), F32) for _ in range(nacc)]
        hi_acc = [jnp.zeros((ROW_TILE, LANES), F32) for _ in range(nacc)]
        for k in range(slots // 2):
            word = stack_ref[k * ROW_TILE:(k + 1) * ROW_TILE, :]
            wv = jnp.where(sub < HALF_TILE,
                           jnp.broadcast_to(wbc_ref[2 * k:2 * k + 1, :], (ROW_TILE, LANES)),
                           jnp.broadcast_to(wbc_ref[2 * k + 1:2 * k + 2, :], (ROW_TILE, LANES)))
            a = k % nacc
            lo[a] = lo[a] + lax.bitcast_convert_type(jnp.left_shift(word, 16), F32) * wv
            hi_acc[a] = hi_acc[a] + lax.bitcast_convert_type(word & jnp.int32(-65536), F32) * wv
        lo_sum = (lo[0] + lo[1]) + (lo[2] + lo[3])
        hi_sum = (hi_acc[0] + hi_acc[1]) + (hi_acc[2] + hi_acc[3])
        ff = jnp.concatenate([lo_sum[:HALF_TILE] + lo_sum[HALF_TILE:],
                              hi_sum[:HALF_TILE] + hi_sum[HALF_TILE:]], axis=0)
        rows = pl.ds(pl.multiple_of(t * ROW_TILE, ROW_TILE), ROW_TILE)
        o_ref[rows, :] = x2_ref[rows, :] + ff

    def pair(i, carry):
        one(2 * i, stack_a, wbc_a)
        one(2 * i + 1, stack_b, wbc_b)
        return carry

    lax.fori_loop(0, tb // 2, pair, 0)
    if final_norm:
        x3 = o_ref[...].reshape(tb, ROW_TILE, LANES)
        sq = jnp.sum(jnp.sum(x3 * x3, axis=2, keepdims=True), axis=1, keepdims=True)
        scale = lax.rsqrt(sq * (1.0 / (ROW_TILE * LANES)) + NORM_EPS)
        o_ref[...] = (x3 * scale * gfin_ref[...][None]).reshape(tb * ROW_TILE, LANES)


def _peer_mix(off, wgt, tab, x8, g_final, *, tb, final_norm):
    n, slots = wgt.shape
    d = ROW_TILE * LANES
    g8 = g_final.reshape(ROW_TILE, LANES)
    kern = functools.partial(_peer_mix_kernel, final_norm=final_norm)
    stack = pltpu.VMEM((slots * HALF_TILE, LANES), jnp.int32)
    out = pl.pallas_call(
        kern,
        grid=(n // tb,),
        in_specs=[pl.BlockSpec((tb * slots,), lambda i: (i,), memory_space=pltpu.SMEM),
                  pl.BlockSpec((tb, slots), lambda i: (i, 0)),
                  pl.BlockSpec(memory_space=pltpu.VMEM),
                  pl.BlockSpec((tb * ROW_TILE, LANES), lambda i: (i, 0)),
                  pl.BlockSpec((ROW_TILE, LANES), lambda i: (0, 0))],
        out_specs=pl.BlockSpec((tb * ROW_TILE, LANES), lambda i: (i, 0)),
        out_shape=jax.ShapeDtypeStruct((n * ROW_TILE, LANES), F32),
        scratch_shapes=[stack, stack, pltpu.VMEM((slots, LANES), F32),
                        pltpu.VMEM((slots, LANES), F32), pltpu.VMEM((3, slots, tb), BF16)],
        compiler_params=_cparams(("arbitrary",)),
        name="peer_expert_mix",
    )(off, wgt, tab, x8, g8)
    return out.reshape(n, d)


def _state_to_blocks(s):
    b, h, d, _ = s.shape
    st = jnp.swapaxes(s, -1, -2).reshape(b, h // HEADS_PER_LANE_TILE, HEADS_PER_LANE_TILE, d, d)
    eye = jnp.eye(HEADS_PER_LANE_TILE, dtype=s.dtype)
    blk = st[:, :, :, :, None, :] * eye[None, None, :, None, :, None]
    return blk.reshape(b, h // HEADS_PER_LANE_TILE, LANES, LANES)


def _blocks_to_state(blk, heads):
    b, npair = blk.shape[:2]
    x = blk.reshape(b, npair, HEADS_PER_LANE_TILE, HEAD_DIM, HEADS_PER_LANE_TILE, HEAD_DIM)
    diag = jnp.stack([x[:, :, hh, :, hh, :] for hh in range(HEADS_PER_LANE_TILE)], axis=2)
    return jnp.swapaxes(diag.reshape(b, heads, HEAD_DIM, HEAD_DIM), -1, -2)


PEER_TOPK = 16


def _layer(x, k_past, v_past, lf_past, s0, shift0, lp, g_final, final_norm):
    (norm_mix_g, w_in, fox_b_f, mu, w0, w2, a0, a2, g2, k_k, k_a, r_k, lnx_w, lnx_b, w_out,
     norm_ffn_g, peer_w_q, peer_sub_keys, tab_u, tab_v) = lp
    b, t, d = x.shape
    n = b * t
    fox_heads = fox_b_f.shape[0]
    fw = fox_heads * HEAD_DIM
    fox_cols = 3 * fw + fox_heads
    rwkv_heads = r_k.shape[0]
    w = rwkv_heads * HEAD_DIM
    x2d = x.reshape(n, d)
    qb, k, v, kb, vb, lf, rw_main, rw_tail = _inproj(x2d, norm_mix_g, w_in, fox_b_f, fox_cols,
                                                     fox_heads, 3 * w)
    fox = _fox_stream(qb, kb, vb, lf, k_past, v_past, lf_past, batch=b, q_len=t)

    prm = _rwkv_params(mu, w0, w2, a0, a2, g2, k_k, k_a, r_k.reshape(-1))
    tail = prm["tail"]
    shift_main = shift0[..., :3 * w]
    shift_tail = jnp.pad(shift0[..., 3 * w:], ((0, 0), (0, 0), (0, TAIL_PAD - tail)))
    r, lw, km, vv, kn, bb, g, bonus = _rwkv_pre(rw_main, rw_tail, shift_main, shift_tail, prm,
                                                batch=b, seq=t)
    y, s_blk = _rwkv_scan(r, lw, km, vv, kn, bb, _state_to_blocks(s0), batch=b, seq=t)
    s_t = _blocks_to_state(s_blk, rwkv_heads)
    last = jnp.concatenate([rw_main.reshape(b, t, -1)[:, -1:], rw_tail.reshape(b, t, -1)[:, -1:, :tail]],
                           axis=-1)

    x2, xn, scores = _outproj(x2d, fox, y, bonus, g, lnx_w, lnx_b, w_out, norm_ffn_g, peer_w_q,
                              peer_sub_keys)
    idx, gate = _retrieve(scores, PEER_TOPK)
    tb = _row_tile(n, 128)
    slots = gate.shape[1]
    off = idx.reshape(n * slots)
    wgt = _peer_act(off, xn, tab_u, gate, tb=tb)
    out = _peer_mix(off, wgt, tab_v, x2, g_final, tb=tb, final_norm=final_norm)
    return (out.reshape(b, t, d), k.reshape(b, t, fox_heads, HEAD_DIM),
            v.reshape(b, t, fox_heads, HEAD_DIM), lf.reshape(b, t, fox_heads), s_t, last)


def kernel(x_prompt, x_sample, cache_fox_k, cache_fox_v, cache_fox_logf, state_rwkv, state_shift,
           norm_mix_g, w_in, fox_b_f, rwkv_mu, rwkv_w0, rwkv_w2, rwkv_a0, rwkv_a2, rwkv_g2,
           rwkv_k_k, rwkv_k_a, rwkv_r_k, rwkv_lnx_w, rwkv_lnx_b, w_out, norm_ffn_g,
           peer_w_q, peer_sub_keys, peer_u, peer_v, norm_final_g):
    depth = w_in.shape[0]
    yp, ys = x_prompt, x_sample
    bp = x_prompt.shape[0]
    dt = x_prompt.dtype
    fox_heads = fox_b_f.shape[1]
    rwkv_heads = rwkv_r_k.shape[1]
    rwkv_cols = rwkv_mu.shape[1]
    outs_p, outs_s = [], []
    for l in range(depth):
        lp = (norm_mix_g[l], w_in[l], fox_b_f[l], rwkv_mu[l], rwkv_w0[l], rwkv_w2[l], rwkv_a0[l],
              rwkv_a2[l], rwkv_g2[l], rwkv_k_k[l], rwkv_k_a[l], rwkv_r_k[l], rwkv_lnx_w[l],
              rwkv_lnx_b[l], w_out[l], norm_ffn_g[l], peer_w_q[l], peer_sub_keys[l],
              _pack_table(peer_u[l], False), _pack_table(peer_v[l], True))
        last = l == depth - 1
        empty_kv = jnp.zeros((bp, 0, fox_heads, HEAD_DIM), dt)
        empty_lf = jnp.zeros((bp, 0, fox_heads), dt)
        s_zero = jnp.zeros((bp, rwkv_heads, HEAD_DIM, HEAD_DIM), dt)
        sh_zero = jnp.zeros((bp, 1, rwkv_cols), dt)
        yp, *rest_p = _layer(yp, empty_kv, empty_kv, empty_lf, s_zero, sh_zero, lp, norm_final_g, last)
        ys, *rest_s = _layer(ys, cache_fox_k[l], cache_fox_v[l], cache_fox_logf[l], state_rwkv[l],
                             state_shift[l], lp, norm_final_g, last)
        outs_p.append(rest_p)
        outs_s.append(rest_s)
    stack = lambda outs, i: jnp.stack([o[i] for o in outs])
    return ((yp, ys) + tuple(stack(outs_p, i) for i in range(5))
            + tuple(stack(outs_s, i) for i in range(5)))
```

```python
import functools
import math

import jax
import jax.numpy as jnp
from jax import lax
from jax.experimental import pallas as pl
from jax.experimental.pallas import tpu as pltpu

F32 = jnp.float32
BF16 = jnp.bfloat16

HEAD_DIM = 64
LANES = 128
HEADS_PER_LANE_TILE = LANES // HEAD_DIM
TAIL_PAD = 2 * LANES
CHUNK = 64
RWKV_PASSES = 1
NORM_EPS = 1e-6
LNX_EPS = 64e-5
NEG_BIG = -1e30
LOG2E = math.log2(math.e)
HIGHEST = lax.Precision.HIGHEST
VMEM_LIMIT = 48 * 1024 * 1024


def _cparams(sem):
    return pltpu.CompilerParams(dimension_semantics=sem, vmem_limit_bytes=VMEM_LIMIT)


def _row_tile(n, target):
    t = min(n, target)
    assert n % t == 0, (n, t)
    return t


def _inproj_kernel(x_ref, g_ref, wqkv_ref, wf_ref, wrw_ref, wtail_ref, bf_ref,
                   q_ref, k_ref, v_ref, kb_ref, vb_ref, lf_ref, rw_ref, tail_ref):
    x = x_ref[...]
    h = x * lax.rsqrt(jnp.mean(x * x, axis=-1, keepdims=True) + NORM_EPS) * g_ref[...]
    hb = h.astype(BF16)
    fw = wqkv_ref.shape[1] // 3
    qkv = jnp.dot(hb, wqkv_ref[...], preferred_element_type=F32)
    q_ref[...] = (qkv[:, :fw] * (LOG2E / math.sqrt(HEAD_DIM))).astype(BF16)
    k = qkv[:, fw:2 * fw]
    v = qkv[:, 2 * fw:]
    k_ref[...] = k
    v_ref[...] = v
    kb_ref[...] = k.astype(BF16)
    vb_ref[...] = v.astype(BF16)
    f = jnp.dot(hb, wf_ref[...], preferred_element_type=F32) + bf_ref[...]
    lf_ref[...] = jax.nn.log_sigmoid(f)
    rw_ref[...] = jnp.dot(hb, wrw_ref[...], preferred_element_type=F32)
    tail_ref[...] = jnp.dot(hb, wtail_ref[...], preferred_element_type=F32)


def _inproj(x2d, g, w_in, b_f, fox_cols, fox_heads, rw_main):
    n, d = x2d.shape
    fw = fox_heads * HEAD_DIM
    wqkv = w_in[:, :3 * fw].astype(BF16)
    wf = w_in[:, 3 * fw:fox_cols].astype(BF16)
    wrw = w_in[:, fox_cols:fox_cols + rw_main].astype(BF16)
    wtail = w_in[:, fox_cols + rw_main:].astype(BF16)
    wtail = jnp.pad(wtail, ((0, 0), (0, TAIL_PAD - wtail.shape[1])))
    tm = _row_tile(n, 512)
    row = lambda c: pl.BlockSpec((tm, c), lambda i: (i, 0))
    full = lambda a: pl.BlockSpec(a.shape, lambda i: (0,) * a.ndim)
    g2 = g.reshape(1, d)
    bf2 = b_f.reshape(1, fox_heads)
    outs = (
        jax.ShapeDtypeStruct((n, fw), BF16),
        jax.ShapeDtypeStruct((n, fw), F32),
        jax.ShapeDtypeStruct((n, fw), F32),
        jax.ShapeDtypeStruct((n, fw), BF16),
        jax.ShapeDtypeStruct((n, fw), BF16),
        jax.ShapeDtypeStruct((n, fox_heads), F32),
        jax.ShapeDtypeStruct((n, rw_main), F32),
        jax.ShapeDtypeStruct((n, TAIL_PAD), F32),
    )
    return pl.pallas_call(
        _inproj_kernel,
        grid=(n // tm,),
        in_specs=[row(d), full(g2), full(wqkv), full(wf), full(wrw), full(wtail), full(bf2)],
        out_specs=[row(fw), row(fw), row(fw), row(fw), row(fw), row(fox_heads), row(rw_main),
                   row(TAIL_PAD)],
        out_shape=outs,
        compiler_params=_cparams(("parallel",)),
        name="inproj",
    )(x2d, g2, wqkv, wf, wrw, wtail, bf2)


def _cumsum_kernel(lf_ref, c_ref, carry):
    @pl.when(pl.program_id(1) == 0)
    def _():
        carry[...] = jnp.zeros_like(carry)

    lf = lf_ref[...]
    tc = lf.shape[0]
    r = lax.broadcasted_iota(jnp.int32, (tc, tc), 0)
    c = lax.broadcasted_iota(jnp.int32, (tc, tc), 1)
    lower = (c <= r).astype(F32)
    cc = jnp.dot(lower, lf, precision=HIGHEST, preferred_element_type=F32) + carry[...]
    c_ref[...] = cc
    carry[...] = cc[tc - 1:tc, :]


def _cumsum(lf, tc):
    b, l, nh = lf.shape
    assert l % tc == 0
    return pl.pallas_call(
        _cumsum_kernel,
        grid=(b, l // tc),
        in_specs=[pl.BlockSpec((None, tc, nh), lambda i, j: (i, j, 0))],
        out_specs=pl.BlockSpec((None, tc, nh), lambda i, j: (i, j, 0)),
        out_shape=jax.ShapeDtypeStruct((b, l, nh), F32),
        scratch_shapes=[pltpu.VMEM((1, nh), F32)],
        compiler_params=_cparams(("parallel", "arbitrary")),
        name="cumsum_logf",
    )(lf)


def _split3(x):
    hi = x.astype(BF16)
    r = x - hi.astype(F32)
    mid = r.astype(BF16)
    lo = (r - mid.astype(F32)).astype(BF16)
    return hi.astype(F32), mid.astype(F32), lo.astype(F32)


def _augment_kernel(x_ref, c_ref, o_ref, *, role):
    tm = x_ref.shape[0]
    lane = lax.broadcasted_iota(jnp.int32, (tm, LANES), 1)
    for p in range(x_ref.shape[1] // LANES):
        xp = x_ref[:, p * LANES:(p + 1) * LANES].astype(F32)
        for hh in range(HEADS_PER_LANE_TILE):
            h = p * HEADS_PER_LANE_TILE + hh
            own = (lane >= hh * HEAD_DIM) & (lane < (hh + 1) * HEAD_DIM)
            e = (lane + (1 - hh) * HEAD_DIM) % LANES
            if role == "v":
                ext = jnp.where(e == 0, 1.0, 0.0)
            else:
                c = jnp.broadcast_to(c_ref[:, h:h + 1], (tm, LANES))
                hi, mid, lo = _split3(c * LOG2E)
                sgn = 1.0 if role == "q" else -1.0
                base = 0 if role == "q" else 3
                ext = jnp.where(e == base, sgn * hi,
                                jnp.where(e == base + 1, sgn * mid,
                                          jnp.where(e == base + 2, sgn * lo,
                                                    jnp.where(e < 6, 1.0, 0.0))))
            o_ref[:, h * LANES:(h + 1) * LANES] = jnp.where(own, xp, ext).astype(BF16)


def _augment(x, c, role):
    n, w = x.shape
    nh = w // HEAD_DIM
    tm = _row_tile(n, 512) if n % 512 == 0 else n
    kern = functools.partial(_augment_kernel, role=role)
    return pl.pallas_call(
        kern,
        grid=(n // tm,),
        in_specs=[pl.BlockSpec((tm, w), lambda i: (i, 0)),
                  pl.BlockSpec((tm, nh), lambda i: (i, 0))],
        out_specs=pl.BlockSpec((tm, nh * LANES), lambda i: (i, 0)),
        out_shape=jax.ShapeDtypeStruct((n, nh * LANES), BF16),
        compiler_params=_cparams(("parallel",)),
        name="fox_augment_" + role,
    )(x, c)


Q_SUB = 128
K_SUB = 256


def _fox_kernel(qblk_ref, kblk_ref, last_ref, q_ref, k_ref, v_ref, o_ref, m_sc, acc_sc,
                *, q_off, tq, tk):
    t = pl.program_id(2)
    i = qblk_ref[t]
    j = kblk_ref[t]
    qs_n, ks_n = min(Q_SUB, tq), min(K_SUB, tk)

    @pl.when(j == 0)
    def _():
        m_sc[...] = jnp.full_like(m_sc, NEG_BIG)
        acc_sc[...] = jnp.zeros_like(acc_sc)

    q_lo = q_off + i * tq
    k_lo = j * tk

    def body(masked):
        if masked:
            diff = (lax.broadcasted_iota(jnp.int32, (qs_n, LANES), 1)
                    - lax.broadcasted_iota(jnp.int32, (qs_n, LANES), 0))
        nqs = tq // qs_n
        qrow = [slice(qs * qs_n, (qs + 1) * qs_n) for qs in range(nqs)]
        m_run = [[m_sc[hh, qrow[qs], :] for qs in range(nqs)] for hh in range(HEADS_PER_LANE_TILE)]
        a_run = [[acc_sc[hh, qrow[qs], :] for qs in range(nqs)] for hh in range(HEADS_PER_LANE_TILE)]
        for hh in range(HEADS_PER_LANE_TILE):
            cols = slice(hh * LANES, (hh + 1) * LANES)
            for ks in range(tk // ks_n):
                krows = slice(ks * ks_n, (ks + 1) * ks_n)
                k_sub = k_ref[krows, cols]
                v_sub = v_ref[krows, cols]
                for qs in range(nqs):
                    qrows = qrow[qs]
                    s = lax.dot_general(q_ref[qrows, cols], k_sub, (((1,), (1,)), ((), ())),
                                        preferred_element_type=F32)
                    parts = [s[:, c * LANES:(c + 1) * LANES] for c in range(ks_n // LANES)]
                    if masked:
                        parts = [jnp.where(diff <= q_lo - k_lo + qs * qs_n - ks * ks_n - c * LANES,
                                           pc, NEG_BIG) for c, pc in enumerate(parts)]
                    mx = parts[0]
                    for pc in parts[1:]:
                        mx = jnp.maximum(mx, pc)
                    m_old = m_run[hh][qs]
                    m_new = jnp.maximum(m_old, jnp.max(mx, axis=-1, keepdims=True))
                    alpha = jnp.exp2(m_old - m_new)
                    pr = jnp.concatenate([jnp.exp2(pc - m_new).astype(BF16) for pc in parts], axis=1)
                    pv = jnp.dot(pr, v_sub, preferred_element_type=F32)
                    a_run[hh][qs] = alpha * a_run[hh][qs] + pv
                    m_run[hh][qs] = m_new
        for hh in range(HEADS_PER_LANE_TILE):
            for qs in range(nqs):
                m_sc[hh, qrow[qs], :] = m_run[hh][qs]
                acc_sc[hh, qrow[qs], :] = a_run[hh][qs]

    fully_visible = k_lo + tk - 1 <= q_lo

    @pl.when(fully_visible)
    def _():
        body(False)

    @pl.when(jnp.logical_not(fully_visible))
    def _():
        body(True)

    @pl.when(last_ref[t] == 1)
    def _():
        lane = lax.broadcasted_iota(jnp.int32, (1, LANES), 1)
        out = jnp.zeros((tq, LANES), F32)
        for hh in range(HEADS_PER_LANE_TILE):
            in_head = (lane >= hh * HEAD_DIM) & (lane < (hh + 1) * HEAD_DIM)
            acc = acc_sc[hh]
            ones_col = (1 - hh) * HEAD_DIM
            denom = jnp.broadcast_to(acc[:, ones_col:ones_col + 1], acc.shape)
            out = jnp.where(in_head, acc / denom, out)
        o_ref[...] = out


def _fox_attend(q_aug, k_aug, v_aug, *, batch, q_len, kv_len, q_off, tq, tk):
    n, wa = q_aug.shape
    pair_w = HEADS_PER_LANE_TILE * LANES
    npair = wa // pair_w
    nq, nk = q_len // tq, kv_len // tk
    assert q_len % tq == 0 and kv_len % tk == 0

    pairs = [(i, j) for i in range(nq) for j in range(min(nk, (q_off + (i + 1) * tq - 1) // tk + 1))]
    qblk = jnp.asarray([i for i, _ in pairs], jnp.int32)
    kblk = jnp.asarray([j for _, j in pairs], jnp.int32)
    last = jnp.asarray([int(t + 1 == len(pairs) or pairs[t + 1][0] != i)
                        for t, (i, _) in enumerate(pairs)], jnp.int32)
    q_map = lambda b, p, t, qb, kb, lt: (b * nq + qb[t], p)
    kv_map = lambda b, p, t, qb, kb, lt: (b * nk + kb[t], p)
    kern = functools.partial(_fox_kernel, q_off=q_off, tq=tq, tk=tk)
    return pl.pallas_call(
        kern,
        grid_spec=pltpu.PrefetchScalarGridSpec(
            num_scalar_prefetch=3,
            grid=(batch, npair, len(pairs)),
            in_specs=[pl.BlockSpec((tq, pair_w), q_map),
                      pl.BlockSpec((tk, pair_w), kv_map),
                      pl.BlockSpec((tk, pair_w), kv_map)],
            out_specs=pl.BlockSpec((tq, LANES), q_map),
            scratch_shapes=[pltpu.VMEM((HEADS_PER_LANE_TILE, tq, LANES), F32),
                            pltpu.VMEM((HEADS_PER_LANE_TILE, tq, LANES), F32)]),
        out_shape=jax.ShapeDtypeStruct((n, npair * LANES), F32),
        compiler_params=_cparams(("parallel", "parallel", "arbitrary")),
        name="fox_attention",
    )(qblk, kblk, last, q_aug, k_aug, v_aug)


def _fox_stream(qb, kb, vb, lf, k_past, v_past, lf_past, *, batch, q_len):
    n, w = qb.shape
    nh = lf.shape[1]
    past = k_past.shape[1]
    lf_new = lf.reshape(batch, q_len, nh)
    if past == 0:
        kv_len = q_len
        k_all, v_all, lf_all = kb, vb, lf_new
        tq = tk = _row_tile(q_len, 512)
        tc = tk
    else:
        kv_len = -(-(past + q_len) // K_SUB) * K_SUB
        pad = kv_len - past - q_len

        def cat(old, new):
            old = old.reshape(batch, past, -1).astype(new.dtype)
            new = new.reshape(batch, q_len, -1)
            z = jnp.zeros((batch, pad, new.shape[-1]), new.dtype)
            return jnp.concatenate([old, new, z], axis=1)

        k_all = cat(k_past, kb).reshape(batch * kv_len, w)
        v_all = cat(v_past, vb).reshape(batch * kv_len, w)
        lf_all = cat(lf_past, lf_new)
        tc = max(t for t in range(LANES, 1024 + 1, LANES) if kv_len % t == 0)
        tq, tk = q_len, kv_len
    c = _cumsum(lf_all, tc)
    c_k = c.reshape(batch * kv_len, nh)
    c_q = c[:, past:past + q_len].reshape(n, nh)
    return _fox_attend(_augment(qb, c_q, "q"), _augment(k_all, c_k, "k"), _augment(v_all, c_k, "v"),
                       batch=batch, q_len=q_len, kv_len=kv_len, q_off=past, tq=tq, tk=tk)


def _head_sum_matrix(width):
    r = lax.broadcasted_iota(jnp.int32, (width, width), 0) // HEAD_DIM
    c = lax.broadcasted_iota(jnp.int32, (width, width), 1) // HEAD_DIM
    return (r == c).astype(BF16)


def _rwkv_pre_kernel(pm_ref, pt_ref, sm_ref, st_ref, mum_ref, mut_ref, wbig_ref, w0_ref, a0_ref,
                     kk_ref, ka_ref, rk_ref,
                     r_out, lw_out, km_out, v_out, kn_out, b_out, g_out, bonus_out,
                     carry_m, carry_t, *, lora_w, lora_a):
    @pl.when(pl.program_id(1) == 0)
    def _():
        carry_m[...] = sm_ref[...]
        carry_t[...] = st_ref[...]

    pm = pm_ref[...]
    pt = pt_ref[...]
    tm = pm.shape[0]
    w = pm.shape[1] // 3

    def shifted(p, carry):
        row = lax.broadcasted_iota(jnp.int32, p.shape, 0)
        return jnp.where(row == 0, carry[...], pltpu.roll(p, 1, 0))

    prev_m = shifted(pm, carry_m)
    prev_t = shifted(pt, carry_t)
    carry_m[...] = pm[tm - 1:tm, :]
    carry_t[...] = pt[tm - 1:tm, :]
    psm = pm + mum_ref[...] * (prev_m - pm)
    pst = pt + mut_ref[...] * (prev_t - pt)
    r = psm[:, :w]
    k = psm[:, w:2 * w]
    v = psm[:, 2 * w:]
    lane = lax.broadcasted_iota(jnp.int32, pst.shape, 1)
    z = jnp.where(lane < lora_w, jnp.tanh(pst),
                  jnp.where(lane < lora_w + lora_a, pst, jax.nn.sigmoid(pst)))
    lo = jnp.dot(z.astype(BF16), wbig_ref[...], preferred_element_type=F32)
    w_log = -jax.nn.softplus(-(w0_ref[...] + lo[:, :w])) - 0.5
    lw = -jnp.exp(w_log)
    a = jax.nn.sigmoid(a0_ref[...] + lo[:, w:2 * w])
    g = lo[:, 2 * w:]
    e = _head_sum_matrix(w)
    kk0 = k * kk_ref[...]
    n2 = _dot3(kk0 * kk0, e)
    kn = kk0 / jnp.maximum(jnp.sqrt(n2), 1e-12)
    km = k * (1.0 + (a - 1.0) * ka_ref[...])
    rk = _dot3(r * km * rk_ref[...], e)
    r_out[...] = r
    lw_out[...] = lw
    km_out[...] = km
    v_out[...] = v
    kn_out[...] = kn
    b_out[...] = kn * a
    g_out[...] = g
    bonus_out[...] = rk * v


def _rwkv_params(mu, w0, w2, a0, a2, g2, k_k, k_a, r_k):
    w = w0.shape[0]
    lora_w, lora_a, lora_g = w2.shape[0], a2.shape[0], g2.shape[0]
    w_lora = jnp.zeros((TAIL_PAD, 3 * w), F32)
    w_lora = w_lora.at[:lora_w, :w].set(w2)
    w_lora = w_lora.at[lora_w:lora_w + lora_a, w:2 * w].set(a2)
    w_lora = w_lora.at[lora_w + lora_a:lora_w + lora_a + lora_g, 2 * w:].set(g2)
    tail = mu.shape[0] - 3 * w
    return dict(
        mu_main=mu[:3 * w].reshape(1, 3 * w),
        mu_tail=jnp.pad(mu[3 * w:], (0, TAIL_PAD - tail)).reshape(1, TAIL_PAD),
        w_lora=w_lora.astype(BF16), w0=w0.reshape(1, w), a0=a0.reshape(1, w),
        k_k=k_k.reshape(1, w), k_a=k_a.reshape(1, w), r_k=r_k.reshape(1, w),
        lora_w=lora_w, lora_a=lora_a, tail=tail)


def _rwkv_pre(rw_main, rw_tail, shift_main, shift_tail, prm, *, batch, seq):
    n, w3 = rw_main.shape
    w = w3 // 3
    tm = _row_tile(seq, 512)
    nt = seq // tm
    row = lambda c: pl.BlockSpec((tm, c), lambda b, i: (b * nt + i, 0))
    per_b = lambda c: pl.BlockSpec((None, 1, c), lambda b, i: (b, 0, 0))
    full = lambda a: pl.BlockSpec(a.shape, lambda b, i: (0,) * a.ndim)
    consts = [prm["mu_main"], prm["mu_tail"], prm["w_lora"], prm["w0"], prm["a0"], prm["k_k"],
              prm["k_a"], prm["r_k"]]
    kern = functools.partial(_rwkv_pre_kernel, lora_w=prm["lora_w"], lora_a=prm["lora_a"])
    return pl.pallas_call(
        kern,
        grid=(batch, nt),
        in_specs=[row(w3), row(TAIL_PAD), per_b(w3), per_b(TAIL_PAD)] + [full(c) for c in consts],
        out_specs=[row(w)] * 8,
        out_shape=[jax.ShapeDtypeStruct((n, w), F32)] * 8,
        scratch_shapes=[pltpu.VMEM((1, w3), F32), pltpu.VMEM((1, TAIL_PAD), F32)],
        compiler_params=_cparams(("parallel", "arbitrary")),
        name="rwkv_pre",
    )(rw_main, rw_tail, shift_main, shift_tail, *consts)


def _bmm(a, b, kind, passes):
    contract = {"nn": ((2,), (1,)), "nt": ((2,), (2,)), "tn": ((1,), (1,))}[kind]
    dims = (contract, ((0,), (0,)))
    if passes == 6:
        return lax.dot_general(a, b, dims, precision=HIGHEST, preferred_element_type=F32)
    dg = lambda x, y: lax.dot_general(x, y, dims, preferred_element_type=F32)
    ah, bh = a.astype(BF16), b.astype(BF16)
    out = dg(ah, bh)
    if passes == 3:
        al = (a - ah.astype(F32)).astype(BF16)
        bl = (b - bh.astype(F32)).astype(BF16)
        out = out + dg(ah, bl) + dg(al, bh)
    return out


def _rwkv_chunk(r, lw, km, v, kn, bb, s_blk, passes):
    g, c, _ = r.shape
    c2 = HEADS_PER_LANE_TILE * c
    ti = lax.broadcasted_iota(jnp.int32, (g, c, c), 1)
    si = lax.broadcasted_iota(jnp.int32, (g, c, c), 2)
    cs = _bmm((si <= ti).astype(F32), lw, "nn", 6)
    e_pos = jnp.exp(cs)
    e_neg = jnp.exp(-cs)
    kt = kn * jnp.exp(cs - lw)
    bt = bb * e_neg
    kh = km * e_neg
    rt = r * e_pos
    g_end = e_pos[:, c - 1:c, :]

    lane = lax.broadcasted_iota(jnp.int32, (1, 1, LANES), 2)
    head_of_lane = lane // HEAD_DIM

    def stack_masked(x):
        return jnp.concatenate(
            [jnp.where(head_of_lane == hh, x, 0.0) for hh in range(HEADS_PER_LANE_TILE)], axis=1)

    def stack(x):
        return jnp.concatenate([x] * HEADS_PER_LANE_TILE, axis=1)

    def pick(x):
        out = x[:, :c]
        for hh in range(1, HEADS_PER_LANE_TILE):
            out = jnp.where(head_of_lane == hh, x[:, hh * c:(hh + 1) * c], out)
        return out

    kt2 = stack_masked(kt)
    rt2 = stack_masked(rt)
    rr = lax.broadcasted_iota(jnp.int32, (1, c2, c2), 1)
    cc = lax.broadcasted_iota(jnp.int32, (1, c2, c2), 2)
    strict_blk = (rr // c == cc // c) & (cc < rr)
    x = jnp.where(strict_blk, -_bmm(kt2, stack(bt), "nt", passes), 0.0)
    tinv = (rr == cc).astype(F32) + x
    steps = max(int(math.ceil(math.log2(c))) - 1, 0)
    for _ in range(steps):
        x = _bmm(x, x, "nn", passes)
        tinv = tinv + _bmm(tinv, x, "nn", passes)
    tr = lax.broadcasted_iota(jnp.int32, (1, c2, c), 1) % c
    sr = lax.broadcasted_iota(jnp.int32, (1, c2, c), 2)
    kk_s = jnp.where(sr < tr, _bmm(kt2, kh, "nt", passes), 0.0)
    rb_s = jnp.where(sr <= tr, _bmm(rt2, bt, "nt", passes), 0.0)
    rk_s = jnp.where(sr <= tr, _bmm(rt2, kh, "nt", passes), 0.0)

    ks = _bmm(jnp.concatenate([kt, rt], axis=1), s_blk, "nn", passes)
    rhs = ks[:, :c] + pick(_bmm(kk_s, v, "nn", passes))
    z = pick(_bmm(tinv, stack(rhs), "nn", passes))
    y = ks[:, c:] - pick(_bmm(rb_s, z, "nn", passes)) + pick(_bmm(rk_s, v, "nn", passes))
    jr = lax.broadcasted_iota(jnp.int32, (1, LANES, LANES), 1)
    ic = lax.broadcasted_iota(jnp.int32, (1, LANES, LANES), 2)
    decay_rows = jnp.swapaxes(jnp.broadcast_to(g_end, (g, LANES, LANES)), 1, 2)
    upd = _bmm(jnp.concatenate([bt * g_end, kh * g_end], axis=1),
               jnp.concatenate([-z, v], axis=1), "tn", passes)
    s_new = decay_rows * s_blk + jnp.where(jr // HEAD_DIM == ic // HEAD_DIM, upd, 0.0)
    return y, s_new


def _rwkv_scan_kernel(r_ref, lw_ref, km_ref, v_ref, kn_ref, b_ref, s0_ref, y_ref, sT_ref, s_sc,
                      *, chunk, passes):
    it = pl.program_id(0)
    nb, tb, w = r_ref.shape
    npair = w // LANES

    @pl.when(it == 0)
    def _():
        s_sc[...] = s0_ref[...]

    def step(ci, carry):
        rows = pl.ds(pl.multiple_of(ci * chunk, chunk), chunk)

        def gather(ref):
            blk = ref[:, rows, :]
            return jnp.concatenate([blk[:, :, p * LANES:(p + 1) * LANES] for p in range(npair)],
                                   axis=0)

        y, s_new = _rwkv_chunk(gather(r_ref), gather(lw_ref), gather(km_ref), gather(v_ref),
                               gather(kn_ref), gather(b_ref), s_sc[...], passes)
        for p in range(npair):
            y_ref[:, rows, p * LANES:(p + 1) * LANES] = y[p * nb:(p + 1) * nb]
        s_sc[...] = s_new
        return carry

    lax.fori_loop(0, tb // chunk, step, 0)

    @pl.when(it == pl.num_programs(0) - 1)
    def _():
        sT_ref[...] = s_sc[...]


def _rwkv_scan(r, lw, km, v, kn, bb, s0_blk, *, batch, seq, passes=RWKV_PASSES):
    n, w = r.shape
    npair = w // LANES
    chunk = min(CHUNK, seq)
    tb = _row_tile(seq, 4 * chunk)
    row = pl.BlockSpec((batch, tb, w), lambda i: (0, i, 0))
    st = pl.BlockSpec((npair * batch, LANES, LANES), lambda i: (0, 0, 0))
    s0 = jnp.swapaxes(s0_blk, 0, 1).reshape(npair * batch, LANES, LANES)
    kern = functools.partial(_rwkv_scan_kernel, chunk=chunk, passes=passes)
    y, s_t = pl.pallas_call(
        kern,
        grid=(seq // tb,),
        in_specs=[row] * 6 + [st],
        out_specs=[row, st],
        out_shape=[jax.ShapeDtypeStruct((batch, seq, w), F32),
                   jax.ShapeDtypeStruct((npair * batch, LANES, LANES), F32)],
        scratch_shapes=[pltpu.VMEM((npair * batch, LANES, LANES), F32)],
        compiler_params=_cparams(("arbitrary",)),
        name="rwkv_scan",
    )(*(a.reshape(batch, seq, w) for a in (r, lw, km, v, kn, bb)), s0)
    s_t = jnp.swapaxes(s_t.reshape(npair, batch, LANES, LANES), 0, 1)
    return y.reshape(n, w), s_t


def _outproj_kernel(x_ref, fox_ref, y_ref, bonus_ref, g_ref, lnw_ref, lnb_ref, wa_ref, wb_ref,
                    gf_ref, wqt_ref, keys_ref, x2_ref, xn_ref, sc_ref):
    y = y_ref[...]
    w = y.shape[1]
    e = _head_sum_matrix(w)
    mean = _dot3(y, e) * (1.0 / HEAD_DIM)
    d = y - mean
    var = _dot3(d * d, e) * (1.0 / HEAD_DIM)
    yn = d * lax.rsqrt(var + LNX_EPS) * lnw_ref[...] + lnb_ref[...]
    rw = (yn + bonus_ref[...]) * g_ref[...]
    mix = (jnp.dot(fox_ref[...].astype(BF16), wa_ref[...], preferred_element_type=F32)
           + jnp.dot(rw.astype(BF16), wb_ref[...], preferred_element_type=F32))
    x2 = x_ref[...] + mix
    xn = x2 * lax.rsqrt(jnp.mean(x2 * x2, axis=-1, keepdims=True) + NORM_EPS) * gf_ref[...]
    x2_ref[...] = x2.reshape(x2_ref.shape)
    xn_ref[...] = xn.reshape(xn_ref.shape)
    qt = lax.dot_general(wqt_ref[...], xn.astype(BF16), (((1,), (1,)), ((), ())),
                         preferred_element_type=F32)
    qh = keys_ref.shape[2]
    for hc in range(keys_ref.shape[0]):
        sc_ref[hc] = jnp.dot(keys_ref[hc], qt[hc * qh:(hc + 1) * qh, :].astype(BF16),
                             preferred_element_type=F32)


def _outproj(x2d, fox, y, bonus, g, lnx_w, lnx_b, w_out, g_ffn, w_q, sub_keys):
    n, d = x2d.shape
    w = y.shape[1]
    fw = fox.shape[1]
    wa = w_out[:fw].astype(BF16)
    wb = w_out[fw:].astype(BF16)
    wqt = w_q.T.astype(BF16)
    nkeys, qh = sub_keys.shape[-2:]
    keys = sub_keys.reshape(-1, nkeys, qh).astype(BF16)
    nhc = keys.shape[0]
    tm = _row_tile(n, 512)
    row = lambda c: pl.BlockSpec((tm, c), lambda i: (i, 0))
    tiles = pl.BlockSpec((tm * d // LANES, LANES), lambda i: (i, 0))
    full = lambda a: pl.BlockSpec(a.shape, lambda i: (0,) * a.ndim)
    consts = [lnx_w.reshape(1, w), lnx_b.reshape(1, w), wa, wb, g_ffn.reshape(1, d), wqt, keys]
    return pl.pallas_call(
        _outproj_kernel,
        grid=(n // tm,),
        in_specs=[row(d), row(fw), row(w), row(w), row(w)] + [full(c) for c in consts],
        out_specs=[tiles, tiles, pl.BlockSpec((nhc, nkeys, tm), lambda i: (0, 0, i))],
        out_shape=[jax.ShapeDtypeStruct((n * d // LANES, LANES), F32),
                   jax.ShapeDtypeStruct((n * d // LANES, LANES), F32),
                   jax.ShapeDtypeStruct((nhc, nkeys, n), F32)],
        compiler_params=_cparams(("parallel",)),
        name="outproj_scores",
    )(x2d, fox, y, bonus, g, *consts)


def _topk_rows(s, payload, k):
    rows = lax.broadcasted_iota(jnp.int32, s.shape, 0)
    nrow = s.shape[0]
    vals, idxs, pays = [], [], []
    for _ in range(k):
        m = jnp.max(s, axis=0, keepdims=True)
        idx = jnp.min(jnp.where(s == m, rows, nrow), axis=0, keepdims=True)
        hit = rows == idx
        vals.append(m)
        idxs.append(idx)
        if payload is not None:
            pays.append(jnp.max(jnp.where(hit, payload, -1), axis=0, keepdims=True))
        s = jnp.where(hit, -jnp.inf, s)
    return vals, idxs, pays


def _retrieve_kernel(sc_ref, idx_ref, gate_ref, *, topk, nkeys):
    nhead = sc_ref.shape[0] // 2
    idx_rows, gate_rows = [], []
    for h in range(nhead):
        v1, i1, _ = _topk_rows(sc_ref[2 * h], None, topk)
        v2, i2, _ = _topk_rows(sc_ref[2 * h + 1], None, topk)
        v2a = jnp.concatenate(v2, axis=0)
        i2a = jnp.concatenate(i2, axis=0)
        nb = [topk // (a + 1) for a in range(topk)]
        pad = -sum(nb) % 8
        cand = jnp.concatenate([v1[a] + v2a[:nb[a]] for a in range(topk)]
                               + [jnp.full((pad, v2a.shape[1]), -jnp.inf, F32)], axis=0)
        cidx = jnp.concatenate([i1[a] * nkeys + i2a[:nb[a]] for a in range(topk)]
                               + [jnp.full((pad, v2a.shape[1]), -1, jnp.int32)], axis=0)
        top, _, eidx = _topk_rows(cand, cidx, topk)
        top = jnp.concatenate(top, axis=0)
        ex = jnp.exp(top - top[0:1])
        gate_rows.append(ex / jnp.sum(ex, axis=0, keepdims=True))
        idx_rows.extend(eidx)
    idx_ref[...] = (jnp.concatenate(idx_rows, axis=0) * HALF_TILE).T
    gate_ref[...] = jnp.concatenate(gate_rows, axis=0).T


def _retrieve(scores, topk):
    nhc, nkeys, n = scores.shape
    slots = (nhc // 2) * topk
    tt = _row_tile(n, 256)
    kern = functools.partial(_retrieve_kernel, topk=topk, nkeys=nkeys)
    return pl.pallas_call(
        kern,
        grid=(n // tt,),
        in_specs=[pl.BlockSpec((nhc, nkeys, tt), lambda i: (0, 0, i))],
        out_specs=[pl.BlockSpec((tt, slots), lambda i: (i, 0))] * 2,
        out_shape=[jax.ShapeDtypeStruct((n, slots), jnp.int32),
                   jax.ShapeDtypeStruct((n, slots), F32)],
        compiler_params=_cparams(("parallel",)),
        name="peer_retrieve",
    )(scores)


ROW_TILE = 8
HALF_TILE = ROW_TILE // 2


def _pack_table(t, split):
    e, d = t.shape
    assert d == ROW_TILE * LANES
    bits = lax.bitcast_convert_type(t.astype(BF16), jnp.uint16).astype(jnp.uint32)
    if split:
        bits = bits.reshape(e, 2, HALF_TILE, LANES)
        word = bits[:, 0] | (bits[:, 1] << 16)
    else:
        bits = bits.reshape(e, HALF_TILE, 2, LANES)
        word = bits[:, :, 0, :] | (bits[:, :, 1, :] << 16)
    return lax.bitcast_convert_type(word, jnp.int32).reshape(e * HALF_TILE, LANES)


def _gather_rows(tab_ref, off_ref, base, stack_ref, slots):
    tok = off_ref.at[pl.ds(base, slots)]
    for j in range(slots):
        off = pl.multiple_of(tok[j], HALF_TILE)
        stack_ref[j * HALF_TILE:(j + 1) * HALF_TILE, :] = tab_ref[pl.ds(off, HALF_TILE), :]


def _diag_mask(slots):
    shape = (ROW_TILE, slots * ROW_TILE)
    return (lax.broadcasted_iota(jnp.int32, shape, 1) % ROW_TILE
            == lax.broadcasted_iota(jnp.int32, shape, 0))


def _split2(x):
    hi = x.astype(BF16)
    return hi, (x - hi.astype(F32)).astype(BF16)


def _dot3(x, w01):
    hi = x.astype(BF16)
    r1 = x - hi.astype(F32)
    mid = r1.astype(BF16)
    lo = (r1 - mid.astype(F32)).astype(BF16)
    d = lambda a: jnp.dot(a, w01, preferred_element_type=F32)
    return d(hi) + d(mid) + d(lo)


def _peer_act_kernel(off_ref, x_ref, tab_ref, gate_ref, w_ref, stack_a, stack_b, part_ref):
    tb, slots = gate_ref.shape
    kdim = slots * ROW_TILE
    diag = _diag_mask(slots)

    def one(t, stack_ref):
        _gather_rows(tab_ref, off_ref, t * slots, stack_ref, slots)
        g = pltpu.bitcast(stack_ref[...], BF16)
        rows = pl.ds(pl.multiple_of(t * ROW_TILE, ROW_TILE), ROW_TILE)
        xh, xl = _split2(x_ref[rows, :])
        p2 = lax.dot_general(jnp.concatenate([xh, xl], axis=0), g, (((1,), (1,)), ((), ())),
                             preferred_element_type=F32)
        d = jnp.where(diag, p2[:ROW_TILE] + p2[ROW_TILE:], 0.0)
        part_ref[pl.ds(t, 1), :] = jnp.sum(d, axis=0, keepdims=True)

    def pair(i, carry):
        one(2 * i, stack_a)
        one(2 * i + 1, stack_b)
        return carry

    lax.fori_loop(0, tb // 2, pair, 0)
    fold = (lax.broadcasted_iota(jnp.int32, (kdim, slots), 0) // ROW_TILE
            == lax.broadcasted_iota(jnp.int32, (kdim, slots), 1)).astype(BF16)
    act = _dot3(part_ref[...], fold)
    gelu = 0.5 * act * (1.0 + lax.erf(act * math.sqrt(0.5)))
    w_ref[...] = gate_ref[...] * gelu


def _peer_act(off, x8, tab, gate, *, tb):
    n, slots = gate.shape
    stack = pltpu.VMEM((slots * HALF_TILE, LANES), jnp.int32)
    return pl.pallas_call(
        _peer_act_kernel,
        grid=(n // tb,),
        in_specs=[pl.BlockSpec((tb * slots,), lambda i: (i,), memory_space=pltpu.SMEM),
                  pl.BlockSpec((tb * ROW_TILE, LANES), lambda i: (i, 0)),
                  pl.BlockSpec(memory_space=pltpu.VMEM),
                  pl.BlockSpec((tb, slots), lambda i: (i, 0))],
        out_specs=pl.BlockSpec((tb, slots), lambda i: (i, 0)),
        out_shape=jax.ShapeDtypeStruct((n, slots), F32),
        scratch_shapes=[stack, stack, pltpu.VMEM((tb, slots * ROW_TILE), F32)],
        compiler_params=_cparams(("arbitrary",)),
        name="peer_expert_act",
    )(off, x8, tab, gate)


def _peer_mix_kernel(off_ref, w_ref, tab_ref, x2_ref, gfin_ref, o_ref, stack_a, stack_b, wbc_all,
                     *, final_norm):
    tb, slots = w_ref.shape
    nacc = 4
    group = 16
    wt = w_ref[...].T
    w_hi = wt.astype(BF16)
    r1 = wt - w_hi.astype(F32)
    w_mid = r1.astype(BF16)
    w_lo = (r1 - w_mid.astype(F32)).astype(BF16)
    row_tok = lax.broadcasted_iota(jnp.int32, (tb, group * LANES), 0)
    col_tok = lax.broadcasted_iota(jnp.int32, (tb, group * LANES), 1) // LANES

    def broadcast_group(c, carry):
        onehot = (row_tok == col_tok + c * group).astype(BF16)
        res = sum(jnp.dot(term, onehot, preferred_element_type=F32) for term in (w_hi, w_mid, w_lo))
        for tt in range(group):
            wbc_all[c * group + tt] = res[:, tt * LANES:(tt + 1) * LANES]
        return carry

    lax.fori_loop(0, tb // group, broadcast_group, 0)
    sub = lax.broadcasted_iota(jnp.int32, (ROW_TILE, LANES), 0)

    def one(t, stack_ref):
        _gather_rows(tab_ref, off_ref, t * slots, stack_ref, slots)
        wbc_ref = wbc_all.at[t]
        lo = [jnp.zeros((ROW_TILE, LANES), F32) for _ in range(nacc)]
        hi_acc = [jnp.zeros((ROW_TILE, LANES), F32) for _ in range(nacc)]
        for k in range(slots // 2):
            word = stack_ref[k * ROW_TILE:(k + 1) * ROW_TILE, :]
            wv = jnp.where(sub < HALF_TILE,
                           jnp.broadcast_to(wbc_ref[2 * k:2 * k + 1, :], (ROW_TILE, LANES)),
                           jnp.broadcast_to(wbc_ref[2 * k + 1:2 * k + 2, :], (ROW_TILE, LANES)))
            a = k % nacc
            lo[a] = lo[a] + lax.bitcast_convert_type(jnp.left_shift(word, 16), F32) * wv
            hi_acc[a] = hi_acc[a] + lax.bitcast_convert_type(word & jnp.int32(-65536), F32) * wv
        lo_sum = (lo[0] + lo[1]) + (lo[2] + lo[3])
        hi_sum = (hi_acc[0] + hi_acc[1]) + (hi_acc[2] + hi_acc[3])
        ff = jnp.concatenate([lo_sum[:HALF_TILE] + lo_sum[HALF_TILE:],
                              hi_sum[:HALF_TILE] + hi_sum[HALF_TILE:]], axis=0)
        rows = pl.ds(pl.multiple_of(t * ROW_TILE, ROW_TILE), ROW_TILE)
        o_ref[rows, :] = x2_ref[rows, :] + ff

    def pair(i, carry):
        one(2 * i, stack_a)
        one(2 * i + 1, stack_b)
        return carry

    lax.fori_loop(0, tb // 2, pair, 0)
    if final_norm:
        x3 = o_ref[...].reshape(tb, ROW_TILE, LANES)
        sq = jnp.sum(jnp.sum(x3 * x3, axis=2, keepdims=True), axis=1, keepdims=True)
        scale = lax.rsqrt(sq * (1.0 / (ROW_TILE * LANES)) + NORM_EPS)
        o_ref[...] = (x3 * scale * gfin_ref[...][None]).reshape(tb * ROW_TILE, LANES)


def _peer_mix(off, wgt, tab, x8, g_final, *, tb, final_norm):
    n, slots = wgt.shape
    d = ROW_TILE * LANES
    g8 = g_final.reshape(ROW_TILE, LANES)
    kern = functools.partial(_peer_mix_kernel, final_norm=final_norm)
    stack = pltpu.VMEM((slots * HALF_TILE, LANES), jnp.int32)
    out = pl.pallas_call(
        kern,
        grid=(n // tb,),
        in_specs=[pl.BlockSpec((tb * slots,), lambda i: (i,), memory_space=pltpu.SMEM),
                  pl.BlockSpec((tb, slots), lambda i: (i, 0)),
                  pl.BlockSpec(memory_space=pltpu.VMEM),
                  pl.BlockSpec((tb * ROW_TILE, LANES), lambda i: (i, 0)),
                  pl.BlockSpec((ROW_TILE, LANES), lambda i: (0, 0))],
        out_specs=pl.BlockSpec((tb * ROW_TILE, LANES), lambda i: (i, 0)),
        out_shape=jax.ShapeDtypeStruct((n * ROW_TILE, LANES), F32),
        scratch_shapes=[stack, stack, pltpu.VMEM((tb, slots, LANES), F32)],
        compiler_params=_cparams(("arbitrary",)),
        name="peer_expert_mix",
    )(off, wgt, tab, x8, g8)
    return out.reshape(n, d)


def _state_to_blocks(s):
    b, h, d, _ = s.shape
    st = jnp.swapaxes(s, -1, -2).reshape(b, h // HEADS_PER_LANE_TILE, HEADS_PER_LANE_TILE, d, d)
    eye = jnp.eye(HEADS_PER_LANE_TILE, dtype=s.dtype)
    blk = st[:, :, :, :, None, :] * eye[None, None, :, None, :, None]
    return blk.reshape(b, h // HEADS_PER_LANE_TILE, LANES, LANES)


def _blocks_to_state(blk, heads):
    b, npair = blk.shape[:2]
    x = blk.reshape(b, npair, HEADS_PER_LANE_TILE, HEAD_DIM, HEADS_PER_LANE_TILE, HEAD_DIM)
    diag = jnp.stack([x[:, :, hh, :, hh, :] for hh in range(HEADS_PER_LANE_TILE)], axis=2)
    return jnp.swapaxes(diag.reshape(b, heads, HEAD_DIM, HEAD_DIM), -1, -2)


PEER_TOPK = 16


def _layer(x, k_past, v_past, lf_past, s0, shift0, lp, g_final, final_norm):
    (norm_mix_g, w_in, fox_b_f, mu, w0, w2, a0, a2, g2, k_k, k_a, r_k, lnx_w, lnx_b, w_out,
     norm_ffn_g, peer_w_q, peer_sub_keys, tab_u, tab_v) = lp
    b, t, d = x.shape
    n = b * t
    fox_heads = fox_b_f.shape[0]
    fw = fox_heads * HEAD_DIM
    fox_cols = 3 * fw + fox_heads
    rwkv_heads = r_k.shape[0]
    w = rwkv_heads * HEAD_DIM
    x2d = x.reshape(n, d)
    qb, k, v, kb, vb, lf, rw_main, rw_tail = _inproj(x2d, norm_mix_g, w_in, fox_b_f, fox_cols,
                                                     fox_heads, 3 * w)
    fox = _fox_stream(qb, kb, vb, lf, k_past, v_past, lf_past, batch=b, q_len=t)

    prm = _rwkv_params(mu, w0, w2, a0, a2, g2, k_k, k_a, r_k.reshape(-1))
    tail = prm["tail"]
    shift_main = shift0[..., :3 * w]
    shift_tail = jnp.pad(shift0[..., 3 * w:], ((0, 0), (0, 0), (0, TAIL_PAD - tail)))
    r, lw, km, vv, kn, bb, g, bonus = _rwkv_pre(rw_main, rw_tail, shift_main, shift_tail, prm,
                                                batch=b, seq=t)
    y, s_blk = _rwkv_scan(r, lw, km, vv, kn, bb, _state_to_blocks(s0), batch=b, seq=t)
    s_t = _blocks_to_state(s_blk, rwkv_heads)
    last = jnp.concatenate([rw_main.reshape(b, t, -1)[:, -1:], rw_tail.reshape(b, t, -1)[:, -1:, :tail]],
                           axis=-1)

    x2, xn, scores = _outproj(x2d, fox, y, bonus, g, lnx_w, lnx_b, w_out, norm_ffn_g, peer_w_q,
                              peer_sub_keys)
    idx, gate = _retrieve(scores, PEER_TOPK)
    tb = _row_tile(n, 128)
    slots = gate.shape[1]
    off = idx.reshape(n * slots)
    wgt = _peer_act(off, xn, tab_u, gate, tb=tb)
    out = _peer_mix(off, wgt, tab_v, x2, g_final, tb=tb, final_norm=final_norm)
    return (out.reshape(b, t, d), k.reshape(b, t, fox_heads, HEAD_DIM),
            v.reshape(b, t, fox_heads, HEAD_DIM), lf.reshape(b, t, fox_heads), s_t, last)


def kernel(x_prompt, x_sample, cache_fox_k, cache_fox_v, cache_fox_logf, state_rwkv, state_shift,
           norm_mix_g, w_in, fox_b_f, rwkv_mu, rwkv_w0, rwkv_w2, rwkv_a0, rwkv_a2, rwkv_g2,
           rwkv_k_k, rwkv_k_a, rwkv_r_k, rwkv_lnx_w, rwkv_lnx_b, w_out, norm_ffn_g,
           peer_w_q, peer_sub_keys, peer_u, peer_v, norm_final_g):
    depth = w_in.shape[0]
    yp, ys = x_prompt, x_sample
    bp = x_prompt.shape[0]
    dt = x_prompt.dtype
    fox_heads = fox_b_f.shape[1]
    rwkv_heads = rwkv_r_k.shape[1]
    rwkv_cols = rwkv_mu.shape[1]
    outs_p, outs_s = [], []
    for l in range(depth):
        lp = (norm_mix_g[l], w_in[l], fox_b_f[l], rwkv_mu[l], rwkv_w0[l], rwkv_w2[l], rwkv_a0[l],
              rwkv_a2[l], rwkv_g2[l], rwkv_k_k[l], rwkv_k_a[l], rwkv_r_k[l], rwkv_lnx_w[l],
              rwkv_lnx_b[l], w_out[l], norm_ffn_g[l], peer_w_q[l], peer_sub_keys[l],
              _pack_table(peer_u[l], False), _pack_table(peer_v[l], True))
        last = l == depth - 1
        empty_kv = jnp.zeros((bp, 0, fox_heads, HEAD_DIM), dt)
        empty_lf = jnp.zeros((bp, 0, fox_heads), dt)
        s_zero = jnp.zeros((bp, rwkv_heads, HEAD_DIM, HEAD_DIM), dt)
        sh_zero = jnp.zeros((bp, 1, rwkv_cols), dt)
        yp, *rest_p = _layer(yp, empty_kv, empty_kv, empty_lf, s_zero, sh_zero, lp, norm_final_g, last)
        ys, *rest_s = _layer(ys, cache_fox_k[l], cache_fox_v[l], cache_fox_logf[l], state_rwkv[l],
                             state_shift[l], lp, norm_final_g, last)
        outs_p.append(rest_p)
        outs_s.append(rest_s)
    stack = lambda outs, i: jnp.stack([o[i] for o in outs])
    return ((yp, ys) + tuple(stack(outs_p, i) for i in range(5))
            + tuple(stack(outs_s, i) for i in range(5)))
```

```python
import functools
import math

import jax
import jax.numpy as jnp
from jax import lax
from jax.experimental import pallas as pl
from jax.experimental.pallas import tpu as pltpu

F32 = jnp.float32
BF16 = jnp.bfloat16

HEAD_DIM = 64
LANES = 128
HEADS_PER_LANE_TILE = LANES // HEAD_DIM
TAIL_PAD = 2 * LANES
CHUNK = 64
RWKV_PASSES = 1
NORM_EPS = 1e-6
LNX_EPS = 64e-5
NEG_BIG = -1e30
LOG2E = math.log2(math.e)
HIGHEST = lax.Precision.HIGHEST
VMEM_LIMIT = 48 * 1024 * 1024


def _cparams(sem):
    return pltpu.CompilerParams(dimension_semantics=sem, vmem_limit_bytes=VMEM_LIMIT)


def _row_tile(n, target):
    t = min(n, target)
    assert n % t == 0, (n, t)
    return t


def _inproj_kernel(x_ref, g_ref, wqkv_ref, wf_ref, wrw_ref, wtail_ref, bf_ref,
                   q_ref, k_ref, v_ref, kb_ref, vb_ref, lf_ref, rw_ref, tail_ref):
    x = x_ref[...]
    h = x * lax.rsqrt(jnp.mean(x * x, axis=-1, keepdims=True) + NORM_EPS) * g_ref[...]
    hb = h.astype(BF16)
    fw = wqkv_ref.shape[1] // 3
    qkv = jnp.dot(hb, wqkv_ref[...], preferred_element_type=F32)
    q_ref[...] = (qkv[:, :fw] * (LOG2E / math.sqrt(HEAD_DIM))).astype(BF16)
    k = qkv[:, fw:2 * fw]
    v = qkv[:, 2 * fw:]
    k_ref[...] = k
    v_ref[...] = v
    kb_ref[...] = k.astype(BF16)
    vb_ref[...] = v.astype(BF16)
    f = jnp.dot(hb, wf_ref[...], preferred_element_type=F32) + bf_ref[...]
    lf_ref[...] = jax.nn.log_sigmoid(f)
    rw_ref[...] = jnp.dot(hb, wrw_ref[...], preferred_element_type=F32)
    tail_ref[...] = jnp.dot(hb, wtail_ref[...], preferred_element_type=F32)


def _inproj(x2d, g, w_in, b_f, fox_cols, fox_heads, rw_main):
    n, d = x2d.shape
    fw = fox_heads * HEAD_DIM
    wqkv = w_in[:, :3 * fw].astype(BF16)
    wf = w_in[:, 3 * fw:fox_cols].astype(BF16)
    wrw = w_in[:, fox_cols:fox_cols + rw_main].astype(BF16)
    wtail = w_in[:, fox_cols + rw_main:].astype(BF16)
    wtail = jnp.pad(wtail, ((0, 0), (0, TAIL_PAD - wtail.shape[1])))
    tm = _row_tile(n, 512)
    row = lambda c: pl.BlockSpec((tm, c), lambda i: (i, 0))
    full = lambda a: pl.BlockSpec(a.shape, lambda i: (0,) * a.ndim)
    g2 = g.reshape(1, d)
    bf2 = b_f.reshape(1, fox_heads)
    outs = (
        jax.ShapeDtypeStruct((n, fw), BF16),
        jax.ShapeDtypeStruct((n, fw), F32),
        jax.ShapeDtypeStruct((n, fw), F32),
        jax.ShapeDtypeStruct((n, fw), BF16),
        jax.ShapeDtypeStruct((n, fw), BF16),
        jax.ShapeDtypeStruct((n, fox_heads), F32),
        jax.ShapeDtypeStruct((n, rw_main), F32),
        jax.ShapeDtypeStruct((n, TAIL_PAD), F32),
    )
    return pl.pallas_call(
        _inproj_kernel,
        grid=(n // tm,),
        in_specs=[row(d), full(g2), full(wqkv), full(wf), full(wrw), full(wtail), full(bf2)],
        out_specs=[row(fw), row(fw), row(fw), row(fw), row(fw), row(fox_heads), row(rw_main),
                   row(TAIL_PAD)],
        out_shape=outs,
        compiler_params=_cparams(("parallel",)),
        name="inproj",
    )(x2d, g2, wqkv, wf, wrw, wtail, bf2)


def _cumsum_kernel(lf_ref, c_ref, carry):
    @pl.when(pl.program_id(1) == 0)
    def _():
        carry[...] = jnp.zeros_like(carry)

    lf = lf_ref[...]
    tc = lf.shape[0]
    r = lax.broadcasted_iota(jnp.int32, (tc, tc), 0)
    c = lax.broadcasted_iota(jnp.int32, (tc, tc), 1)
    lower = (c <= r).astype(F32)
    cc = jnp.dot(lower, lf, precision=HIGHEST, preferred_element_type=F32) + carry[...]
    c_ref[...] = cc
    carry[...] = cc[tc - 1:tc, :]


def _cumsum(lf, tc):
    b, l, nh = lf.shape
    assert l % tc == 0
    return pl.pallas_call(
        _cumsum_kernel,
        grid=(b, l // tc),
        in_specs=[pl.BlockSpec((None, tc, nh), lambda i, j: (i, j, 0))],
        out_specs=pl.BlockSpec((None, tc, nh), lambda i, j: (i, j, 0)),
        out_shape=jax.ShapeDtypeStruct((b, l, nh), F32),
        scratch_shapes=[pltpu.VMEM((1, nh), F32)],
        compiler_params=_cparams(("parallel", "arbitrary")),
        name="cumsum_logf",
    )(lf)


def _split3(x):
    hi = x.astype(BF16)
    r = x - hi.astype(F32)
    mid = r.astype(BF16)
    lo = (r - mid.astype(F32)).astype(BF16)
    return hi.astype(F32), mid.astype(F32), lo.astype(F32)


def _augment_kernel(x_ref, c_ref, o_ref, *, role):
    tm = x_ref.shape[0]
    lane = lax.broadcasted_iota(jnp.int32, (tm, LANES), 1)
    for p in range(x_ref.shape[1] // LANES):
        xp = x_ref[:, p * LANES:(p + 1) * LANES].astype(F32)
        for hh in range(HEADS_PER_LANE_TILE):
            h = p * HEADS_PER_LANE_TILE + hh
            own = (lane >= hh * HEAD_DIM) & (lane < (hh + 1) * HEAD_DIM)
            e = (lane + (1 - hh) * HEAD_DIM) % LANES
            if role == "v":
                ext = jnp.where(e == 0, 1.0, 0.0)
            else:
                c = jnp.broadcast_to(c_ref[:, h:h + 1], (tm, LANES))
                hi, mid, lo = _split3(c * LOG2E)
                sgn = 1.0 if role == "q" else -1.0
                base = 0 if role == "q" else 3
                ext = jnp.where(e == base, sgn * hi,
                                jnp.where(e == base + 1, sgn * mid,
                                          jnp.where(e == base + 2, sgn * lo,
                                                    jnp.where(e < 6, 1.0, 0.0))))
            o_ref[:, h * LANES:(h + 1) * LANES] = jnp.where(own, xp, ext).astype(BF16)


def _augment(x, c, role):
    n, w = x.shape
    nh = w // HEAD_DIM
    tm = _row_tile(n, 512) if n % 512 == 0 else n
    kern = functools.partial(_augment_kernel, role=role)
    return pl.pallas_call(
        kern,
        grid=(n // tm,),
        in_specs=[pl.BlockSpec((tm, w), lambda i: (i, 0)),
                  pl.BlockSpec((tm, nh), lambda i: (i, 0))],
        out_specs=pl.BlockSpec((tm, nh * LANES), lambda i: (i, 0)),
        out_shape=jax.ShapeDtypeStruct((n, nh * LANES), BF16),
        compiler_params=_cparams(("parallel",)),
        name="fox_augment_" + role,
    )(x, c)


Q_SUB = 128
K_SUB = 256


def _fox_kernel(qblk_ref, kblk_ref, last_ref, q_ref, k_ref, v_ref, o_ref, m_sc, acc_sc,
                *, q_off, tq, tk):
    t = pl.program_id(2)
    i = qblk_ref[t]
    j = kblk_ref[t]
    qs_n, ks_n = min(Q_SUB, tq), min(K_SUB, tk)

    @pl.when(j == 0)
    def _():
        m_sc[...] = jnp.full_like(m_sc, NEG_BIG)
        acc_sc[...] = jnp.zeros_like(acc_sc)

    q_lo = q_off + i * tq
    k_lo = j * tk

    def body(masked):
        if masked:
            diff = (lax.broadcasted_iota(jnp.int32, (qs_n, LANES), 1)
                    - lax.broadcasted_iota(jnp.int32, (qs_n, LANES), 0))
        nqs = tq // qs_n
        qrow = [slice(qs * qs_n, (qs + 1) * qs_n) for qs in range(nqs)]
        m_run = [[m_sc[hh, qrow[qs], :] for qs in range(nqs)] for hh in range(HEADS_PER_LANE_TILE)]
        a_run = [[acc_sc[hh, qrow[qs], :] for qs in range(nqs)] for hh in range(HEADS_PER_LANE_TILE)]
        for hh in range(HEADS_PER_LANE_TILE):
            cols = slice(hh * LANES, (hh + 1) * LANES)
            for ks in range(tk // ks_n):
                krows = slice(ks * ks_n, (ks + 1) * ks_n)
                k_sub = k_ref[krows, cols]
                v_sub = v_ref[krows, cols]
                for qs in range(nqs):
                    qrows = qrow[qs]
                    s = lax.dot_general(q_ref[qrows, cols], k_sub, (((1,), (1,)), ((), ())),
                                        preferred_element_type=F32)
                    parts = [s[:, c * LANES:(c + 1) * LANES] for c in range(ks_n // LANES)]
                    if masked:
                        parts = [jnp.where(diff <= q_lo - k_lo + qs * qs_n - ks * ks_n - c * LANES,
                                           pc, NEG_BIG) for c, pc in enumerate(parts)]
                    mx = parts[0]
                    for pc in parts[1:]:
                        mx = jnp.maximum(mx, pc)
                    m_old = m_run[hh][qs]
                    m_new = jnp.maximum(m_old, jnp.max(mx, axis=-1, keepdims=True))
                    alpha = jnp.exp2(m_old - m_new)
                    pr = jnp.concatenate([jnp.exp2(pc - m_new).astype(BF16) for pc in parts], axis=1)
                    pv = jnp.dot(pr, v_sub, preferred_element_type=F32)
                    a_run[hh][qs] = alpha * a_run[hh][qs] + pv
                    m_run[hh][qs] = m_new
        for hh in range(HEADS_PER_LANE_TILE):
            for qs in range(nqs):
                m_sc[hh, qrow[qs], :] = m_run[hh][qs]
                acc_sc[hh, qrow[qs], :] = a_run[hh][qs]

    fully_visible = k_lo + tk - 1 <= q_lo

    @pl.when(fully_visible)
    def _():
        body(False)

    @pl.when(jnp.logical_not(fully_visible))
    def _():
        body(True)

    @pl.when(last_ref[t] == 1)
    def _():
        lane = lax.broadcasted_iota(jnp.int32, (1, LANES), 1)
        out = jnp.zeros((tq, LANES), F32)
        for hh in range(HEADS_PER_LANE_TILE):
            in_head = (lane >= hh * HEAD_DIM) & (lane < (hh + 1) * HEAD_DIM)
            acc = acc_sc[hh]
            ones_col = (1 - hh) * HEAD_DIM
            denom = jnp.broadcast_to(acc[:, ones_col:ones_col + 1], acc.shape)
            out = jnp.where(in_head, acc / denom, out)
        o_ref[...] = out


def _fox_attend(q_aug, k_aug, v_aug, *, batch, q_len, kv_len, q_off, tq, tk):
    n, wa = q_aug.shape
    pair_w = HEADS_PER_LANE_TILE * LANES
    npair = wa // pair_w
    nq, nk = q_len // tq, kv_len // tk
    assert q_len % tq == 0 and kv_len % tk == 0

    pairs = [(i, j) for i in range(nq) for j in range(min(nk, (q_off + (i + 1) * tq - 1) // tk + 1))]
    qblk = jnp.asarray([i for i, _ in pairs], jnp.int32)
    kblk = jnp.asarray([j for _, j in pairs], jnp.int32)
    last = jnp.asarray([int(t + 1 == len(pairs) or pairs[t + 1][0] != i)
                        for t, (i, _) in enumerate(pairs)], jnp.int32)
    q_map = lambda b, p, t, qb, kb, lt: (b * nq + qb[t], p)
    kv_map = lambda b, p, t, qb, kb, lt: (b * nk + kb[t], p)
    kern = functools.partial(_fox_kernel, q_off=q_off, tq=tq, tk=tk)
    return pl.pallas_call(
        kern,
        grid_spec=pltpu.PrefetchScalarGridSpec(
            num_scalar_prefetch=3,
            grid=(batch, npair, len(pairs)),
            in_specs=[pl.BlockSpec((tq, pair_w), q_map),
                      pl.BlockSpec((tk, pair_w), kv_map),
                      pl.BlockSpec((tk, pair_w), kv_map)],
            out_specs=pl.BlockSpec((tq, LANES), q_map),
            scratch_shapes=[pltpu.VMEM((HEADS_PER_LANE_TILE, tq, LANES), F32),
                            pltpu.VMEM((HEADS_PER_LANE_TILE, tq, LANES), F32)]),
        out_shape=jax.ShapeDtypeStruct((n, npair * LANES), F32),
        compiler_params=_cparams(("parallel", "parallel", "arbitrary")),
        name="fox_attention",
    )(qblk, kblk, last, q_aug, k_aug, v_aug)


def _fox_stream(qb, kb, vb, lf, k_past, v_past, lf_past, *, batch, q_len):
    n, w = qb.shape
    nh = lf.shape[1]
    past = k_past.shape[1]
    lf_new = lf.reshape(batch, q_len, nh)
    if past == 0:
        kv_len = q_len
        k_all, v_all, lf_all = kb, vb, lf_new
        tq = tk = _row_tile(q_len, 512)
        tc = tk
    else:
        kv_len = -(-(past + q_len) // K_SUB) * K_SUB
        pad = kv_len - past - q_len

        def cat(old, new):
            old = old.reshape(batch, past, -1).astype(new.dtype)
            new = new.reshape(batch, q_len, -1)
            z = jnp.zeros((batch, pad, new.shape[-1]), new.dtype)
            return jnp.concatenate([old, new, z], axis=1)

        k_all = cat(k_past, kb).reshape(batch * kv_len, w)
        v_all = cat(v_past, vb).reshape(batch * kv_len, w)
        lf_all = cat(lf_past, lf_new)
        tc = max(t for t in range(LANES, 1024 + 1, LANES) if kv_len % t == 0)
        tq, tk = q_len, kv_len
    c = _cumsum(lf_all, tc)
    c_k = c.reshape(batch * kv_len, nh)
    c_q = c[:, past:past + q_len].reshape(n, nh)
    return _fox_attend(_augment(qb, c_q, "q"), _augment(k_all, c_k, "k"), _augment(v_all, c_k, "v"),
                       batch=batch, q_len=q_len, kv_len=kv_len, q_off=past, tq=tq, tk=tk)


def _head_sum_matrix(width):
    r = lax.broadcasted_iota(jnp.int32, (width, width), 0) // HEAD_DIM
    c = lax.broadcasted_iota(jnp.int32, (width, width), 1) // HEAD_DIM
    return (r == c).astype(BF16)


def _rwkv_pre_kernel(pm_ref, pt_ref, sm_ref, st_ref, mum_ref, mut_ref, wbig_ref, w0_ref, a0_ref,
                     kk_ref, ka_ref, rk_ref,
                     r_out, lw_out, km_out, v_out, kn_out, b_out, g_out, bonus_out,
                     carry_m, carry_t, *, lora_w, lora_a):
    @pl.when(pl.program_id(1) == 0)
    def _():
        carry_m[...] = sm_ref[...]
        carry_t[...] = st_ref[...]

    pm = pm_ref[...]
    pt = pt_ref[...]
    tm = pm.shape[0]
    w = pm.shape[1] // 3

    def shifted(p, carry):
        row = lax.broadcasted_iota(jnp.int32, p.shape, 0)
        return jnp.where(row == 0, carry[...], pltpu.roll(p, 1, 0))

    prev_m = shifted(pm, carry_m)
    prev_t = shifted(pt, carry_t)
    carry_m[...] = pm[tm - 1:tm, :]
    carry_t[...] = pt[tm - 1:tm, :]
    psm = pm + mum_ref[...] * (prev_m - pm)
    pst = pt + mut_ref[...] * (prev_t - pt)
    r = psm[:, :w]
    k = psm[:, w:2 * w]
    v = psm[:, 2 * w:]
    lane = lax.broadcasted_iota(jnp.int32, pst.shape, 1)
    z = jnp.where(lane < lora_w, jnp.tanh(pst),
                  jnp.where(lane < lora_w + lora_a, pst, jax.nn.sigmoid(pst)))
    lo = jnp.dot(z.astype(BF16), wbig_ref[...], preferred_element_type=F32)
    w_log = -jax.nn.softplus(-(w0_ref[...] + lo[:, :w])) - 0.5
    lw = -jnp.exp(w_log)
    a = jax.nn.sigmoid(a0_ref[...] + lo[:, w:2 * w])
    g = lo[:, 2 * w:]
    e = _head_sum_matrix(w)
    kk0 = k * kk_ref[...]
    n2 = _dot3(kk0 * kk0, e)
    kn = kk0 / jnp.maximum(jnp.sqrt(n2), 1e-12)
    km = k * (1.0 + (a - 1.0) * ka_ref[...])
    rk = _dot3(r * km * rk_ref[...], e)
    r_out[...] = r
    lw_out[...] = lw
    km_out[...] = km
    v_out[...] = v
    kn_out[...] = kn
    b_out[...] = kn * a
    g_out[...] = g
    bonus_out[...] = rk * v


def _rwkv_params(mu, w0, w2, a0, a2, g2, k_k, k_a, r_k):
    w = w0.shape[0]
    lora_w, lora_a, lora_g = w2.shape[0], a2.shape[0], g2.shape[0]
    w_lora = jnp.zeros((TAIL_PAD, 3 * w), F32)
    w_lora = w_lora.at[:lora_w, :w].set(w2)
    w_lora = w_lora.at[lora_w:lora_w + lora_a, w:2 * w].set(a2)
    w_lora = w_lora.at[lora_w + lora_a:lora_w + lora_a + lora_g, 2 * w:].set(g2)
    tail = mu.shape[0] - 3 * w
    return dict(
        mu_main=mu[:3 * w].reshape(1, 3 * w),
        mu_tail=jnp.pad(mu[3 * w:], (0, TAIL_PAD - tail)).reshape(1, TAIL_PAD),
        w_lora=w_lora.astype(BF16), w0=w0.reshape(1, w), a0=a0.reshape(1, w),
        k_k=k_k.reshape(1, w), k_a=k_a.reshape(1, w), r_k=r_k.reshape(1, w),
        lora_w=lora_w, lora_a=lora_a, tail=tail)


def _rwkv_pre(rw_main, rw_tail, shift_main, shift_tail, prm, *, batch, seq):
    n, w3 = rw_main.shape
    w = w3 // 3
    tm = _row_tile(seq, 512)
    nt = seq // tm
    row = lambda c: pl.BlockSpec((tm, c), lambda b, i: (b * nt + i, 0))
    per_b = lambda c: pl.BlockSpec((None, 1, c), lambda b, i: (b, 0, 0))
    full = lambda a: pl.BlockSpec(a.shape, lambda b, i: (0,) * a.ndim)
    consts = [prm["mu_main"], prm["mu_tail"], prm["w_lora"], prm["w0"], prm["a0"], prm["k_k"],
              prm["k_a"], prm["r_k"]]
    kern = functools.partial(_rwkv_pre_kernel, lora_w=prm["lora_w"], lora_a=prm["lora_a"])
    return pl.pallas_call(
        kern,
        grid=(batch, nt),
        in_specs=[row(w3), row(TAIL_PAD), per_b(w3), per_b(TAIL_PAD)] + [full(c) for c in consts],
        out_specs=[row(w)] * 8,
        out_shape=[jax.ShapeDtypeStruct((n, w), F32)] * 8,
        scratch_shapes=[pltpu.VMEM((1, w3), F32), pltpu.VMEM((1, TAIL_PAD), F32)],
        compiler_params=_cparams(("parallel", "arbitrary")),
        name="rwkv_pre",
    )(rw_main, rw_tail, shift_main, shift_tail, *consts)


def _bmm(a, b, kind, passes):
    contract = {"nn": ((2,), (1,)), "nt": ((2,), (2,)), "tn": ((1,), (1,))}[kind]
    dims = (contract, ((0,), (0,)))
    if passes == 6:
        return lax.dot_general(a, b, dims, precision=HIGHEST, preferred_element_type=F32)
    dg = lambda x, y: lax.dot_general(x, y, dims, preferred_element_type=F32)
    ah, bh = a.astype(BF16), b.astype(BF16)
    out = dg(ah, bh)
    if passes == 3:
        al = (a - ah.astype(F32)).astype(BF16)
        bl = (b - bh.astype(F32)).astype(BF16)
        out = out + dg(ah, bl) + dg(al, bh)
    return out


def _rwkv_chunk(r, lw, km, v, kn, bb, s_blk, passes):
    g, c, _ = r.shape
    c2 = HEADS_PER_LANE_TILE * c
    ti = lax.broadcasted_iota(jnp.int32, (g, c, c), 1)
    si = lax.broadcasted_iota(jnp.int32, (g, c, c), 2)
    cs = _bmm((si <= ti).astype(F32), lw, "nn", 6)
    e_pos = jnp.exp(cs)
    e_neg = jnp.exp(-cs)
    kt = kn * jnp.exp(cs - lw)
    bt = bb * e_neg
    kh = km * e_neg
    rt = r * e_pos
    g_end = e_pos[:, c - 1:c, :]

    lane = lax.broadcasted_iota(jnp.int32, (1, 1, LANES), 2)
    head_of_lane = lane // HEAD_DIM

    def stack_masked(x):
        return jnp.concatenate(
            [jnp.where(head_of_lane == hh, x, 0.0) for hh in range(HEADS_PER_LANE_TILE)], axis=1)

    def stack(x):
        return jnp.concatenate([x] * HEADS_PER_LANE_TILE, axis=1)

    def pick(x):
        out = x[:, :c]
        for hh in range(1, HEADS_PER_LANE_TILE):
            out = jnp.where(head_of_lane == hh, x[:, hh * c:(hh + 1) * c], out)
        return out

    kt2 = stack_masked(kt)
    rt2 = stack_masked(rt)
    rr = lax.broadcasted_iota(jnp.int32, (1, c2, c2), 1)
    cc = lax.broadcasted_iota(jnp.int32, (1, c2, c2), 2)
    strict_blk = (rr // c == cc // c) & (cc < rr)
    x = jnp.where(strict_blk, -_bmm(kt2, stack(bt), "nt", passes), 0.0)
    tinv = (rr == cc).astype(F32) + x
    steps = max(int(math.ceil(math.log2(c))) - 1, 0)
    for _ in range(steps):
        x = _bmm(x, x, "nn", passes)
        tinv = tinv + _bmm(tinv, x, "nn", passes)
    tr = lax.broadcasted_iota(jnp.int32, (1, c2, c), 1) % c
    sr = lax.broadcasted_iota(jnp.int32, (1, c2, c), 2)
    kk_s = jnp.where(sr < tr, _bmm(kt2, kh, "nt", passes), 0.0)
    rb_s = jnp.where(sr <= tr, _bmm(rt2, bt, "nt", passes), 0.0)
    rk_s = jnp.where(sr <= tr, _bmm(rt2, kh, "nt", passes), 0.0)

    ks = _bmm(jnp.concatenate([kt, rt], axis=1), s_blk, "nn", passes)
    rhs = ks[:, :c] + pick(_bmm(kk_s, v, "nn", passes))
    z = pick(_bmm(tinv, stack(rhs), "nn", passes))
    y = ks[:, c:] - pick(_bmm(rb_s, z, "nn", passes)) + pick(_bmm(rk_s, v, "nn", passes))
    jr = lax.broadcasted_iota(jnp.int32, (1, LANES, LANES), 1)
    ic = lax.broadcasted_iota(jnp.int32, (1, LANES, LANES), 2)
    decay_rows = jnp.swapaxes(jnp.broadcast_to(g_end, (g, LANES, LANES)), 1, 2)
    upd = _bmm(jnp.concatenate([bt * g_end, kh * g_end], axis=1),
               jnp.concatenate([-z, v], axis=1), "tn", passes)
    s_new = decay_rows * s_blk + jnp.where(jr // HEAD_DIM == ic // HEAD_DIM, upd, 0.0)
    return y, s_new


def _rwkv_scan_kernel(r_ref, lw_ref, km_ref, v_ref, kn_ref, b_ref, s0_ref, y_ref, sT_ref, s_sc,
                      *, chunk, passes):
    it = pl.program_id(0)
    nb, tb, w = r_ref.shape
    npair = w // LANES

    @pl.when(it == 0)
    def _():
        s_sc[...] = s0_ref[...]

    def step(ci, carry):
        rows = pl.ds(pl.multiple_of(ci * chunk, chunk), chunk)

        def gather(ref):
            blk = ref[:, rows, :]
            return jnp.concatenate([blk[:, :, p * LANES:(p + 1) * LANES] for p in range(npair)],
                                   axis=0)

        y, s_new = _rwkv_chunk(gather(r_ref), gather(lw_ref), gather(km_ref), gather(v_ref),
                               gather(kn_ref), gather(b_ref), s_sc[...], passes)
        for p in range(npair):
            y_ref[:, rows, p * LANES:(p + 1) * LANES] = y[p * nb:(p + 1) * nb]
        s_sc[...] = s_new
        return carry

    lax.fori_loop(0, tb // chunk, step, 0)

    @pl.when(it == pl.num_programs(0) - 1)
    def _():
        sT_ref[...] = s_sc[...]


def _rwkv_scan(r, lw, km, v, kn, bb, s0_blk, *, batch, seq, passes=RWKV_PASSES):
    n, w = r.shape
    npair = w // LANES
    chunk = min(CHUNK, seq)
    tb = _row_tile(seq, 4 * chunk)
    row = pl.BlockSpec((batch, tb, w), lambda i: (0, i, 0))
    st = pl.BlockSpec((npair * batch, LANES, LANES), lambda i: (0, 0, 0))
    s0 = jnp.swapaxes(s0_blk, 0, 1).reshape(npair * batch, LANES, LANES)
    kern = functools.partial(_rwkv_scan_kernel, chunk=chunk, passes=passes)
    y, s_t = pl.pallas_call(
        kern,
        grid=(seq // tb,),
        in_specs=[row] * 6 + [st],
        out_specs=[row, st],
        out_shape=[jax.ShapeDtypeStruct((batch, seq, w), F32),
                   jax.ShapeDtypeStruct((npair * batch, LANES, LANES), F32)],
        scratch_shapes=[pltpu.VMEM((npair * batch, LANES, LANES), F32)],
        compiler_params=_cparams(("arbitrary",)),
        name="rwkv_scan",
    )(*(a.reshape(batch, seq, w) for a in (r, lw, km, v, kn, bb)), s0)
    s_t = jnp.swapaxes(s_t.reshape(npair, batch, LANES, LANES), 0, 1)
    return y.reshape(n, w), s_t


def _outproj_kernel(x_ref, fox_ref, y_ref, bonus_ref, g_ref, lnw_ref, lnb_ref, wa_ref, wb_ref,
                    gf_ref, wqt_ref, keys_ref, x2_ref, xn_ref, sc_ref):
    y = y_ref[...]
    w = y.shape[1]
    e = _head_sum_matrix(w)
    mean = _dot3(y, e) * (1.0 / HEAD_DIM)
    d = y - mean
    var = _dot3(d * d, e) * (1.0 / HEAD_DIM)
    yn = d * lax.rsqrt(var + LNX_EPS) * lnw_ref[...] + lnb_ref[...]
    rw = (yn + bonus_ref[...]) * g_ref[...]
    mix = (jnp.dot(fox_ref[...].astype(BF16), wa_ref[...], preferred_element_type=F32)
           + jnp.dot(rw.astype(BF16), wb_ref[...], preferred_element_type=F32))
    x2 = x_ref[...] + mix
    xn = x2 * lax.rsqrt(jnp.mean(x2 * x2, axis=-1, keepdims=True) + NORM_EPS) * gf_ref[...]
    x2_ref[...] = x2.reshape(x2_ref.shape)
    xn_ref[...] = xn.reshape(xn_ref.shape)
    qt = lax.dot_general(wqt_ref[...], xn.astype(BF16), (((1,), (1,)), ((), ())),
                         preferred_element_type=F32)
    qh = keys_ref.shape[2]
    for hc in range(keys_ref.shape[0]):
        sc_ref[hc] = jnp.dot(keys_ref[hc], qt[hc * qh:(hc + 1) * qh, :].astype(BF16),
                             preferred_element_type=F32)


def _outproj(x2d, fox, y, bonus, g, lnx_w, lnx_b, w_out, g_ffn, w_q, sub_keys):
    n, d = x2d.shape
    w = y.shape[1]
    fw = fox.shape[1]
    wa = w_out[:fw].astype(BF16)
    wb = w_out[fw:].astype(BF16)
    wqt = w_q.T.astype(BF16)
    nkeys, qh = sub_keys.shape[-2:]
    keys = sub_keys.reshape(-1, nkeys, qh).astype(BF16)
    nhc = keys.shape[0]
    tm = _row_tile(n, 512)
    row = lambda c: pl.BlockSpec((tm, c), lambda i: (i, 0))
    tiles = pl.BlockSpec((tm * d // LANES, LANES), lambda i: (i, 0))
    full = lambda a: pl.BlockSpec(a.shape, lambda i: (0,) * a.ndim)
    consts = [lnx_w.reshape(1, w), lnx_b.reshape(1, w), wa, wb, g_ffn.reshape(1, d), wqt, keys]
    return pl.pallas_call(
        _outproj_kernel,
        grid=(n // tm,),
        in_specs=[row(d), row(fw), row(w), row(w), row(w)] + [full(c) for c in consts],
        out_specs=[tiles, tiles, pl.BlockSpec((nhc, nkeys, tm), lambda i: (0, 0, i))],
        out_shape=[jax.ShapeDtypeStruct((n * d // LANES, LANES), F32),
                   jax.ShapeDtypeStruct((n * d // LANES, LANES), F32),
                   jax.ShapeDtypeStruct((nhc, nkeys, n), F32)],
        compiler_params=_cparams(("parallel",)),
        name="outproj_scores",
    )(x2d, fox, y, bonus, g, *consts)


def _topk_rows(s, payload, k):
    rows = lax.broadcasted_iota(jnp.int32, s.shape, 0)
    nrow = s.shape[0]
    vals, idxs, pays = [], [], []
    for _ in range(k):
        m = jnp.max(s, axis=0, keepdims=True)
        idx = jnp.min(jnp.where(s == m, rows, nrow), axis=0, keepdims=True)
        hit = rows == idx
        vals.append(m)
        idxs.append(idx)
        if payload is not None:
            pays.append(jnp.max(jnp.where(hit, payload, -1), axis=0, keepdims=True))
        s = jnp.where(hit, -jnp.inf, s)
    return vals, idxs, pays


def _retrieve_kernel(sc_ref, idx_ref, gate_ref, *, topk, nkeys):
    nhead = sc_ref.shape[0] // 2
    idx_rows, gate_rows = [], []
    for h in range(nhead):
        v1, i1, _ = _topk_rows(sc_ref[2 * h], None, topk)
        v2, i2, _ = _topk_rows(sc_ref[2 * h + 1], None, topk)
        v2a = jnp.concatenate(v2, axis=0)
        i2a = jnp.concatenate(i2, axis=0)
        nb = [topk // (a + 1) for a in range(topk)]
        pad = -sum(nb) % 8
        cand = jnp.concatenate([v1[a] + v2a[:nb[a]] for a in range(topk)]
                               + [jnp.full((pad, v2a.shape[1]), -jnp.inf, F32)], axis=0)
        cidx = jnp.concatenate([i1[a] * nkeys + i2a[:nb[a]] for a in range(topk)]
                               + [jnp.full((pad, v2a.shape[1]), -1, jnp.int32)], axis=0)
        top, _, eidx = _topk_rows(cand, cidx, topk)
        top = jnp.concatenate(top, axis=0)
        ex = jnp.exp(top - top[0:1])
        gate_rows.append(ex / jnp.sum(ex, axis=0, keepdims=True))
        idx_rows.extend(eidx)
    idx_ref[...] = (jnp.concatenate(idx_rows, axis=0) * HALF_TILE).T
    gate_ref[...] = jnp.concatenate(gate_rows, axis=0).T


def _retrieve(scores, topk):
    nhc, nkeys, n = scores.shape
    slots = (nhc // 2) * topk
    tt = _row_tile(n, 256)
    kern = functools.partial(_retrieve_kernel, topk=topk, nkeys=nkeys)
    return pl.pallas_call(
        kern,
        grid=(n // tt,),
        in_specs=[pl.BlockSpec((nhc, nkeys, tt), lambda i: (0, 0, i))],
        out_specs=[pl.BlockSpec((tt, slots), lambda i: (i, 0))] * 2,
        out_shape=[jax.ShapeDtypeStruct((n, slots), jnp.int32),
                   jax.ShapeDtypeStruct((n, slots), F32)],
        compiler_params=_cparams(("parallel",)),
        name="peer_retrieve",
    )(scores)


ROW_TILE = 8
HALF_TILE = ROW_TILE // 2


def _pack_table(t):
    e, d = t.shape
    assert d == ROW_TILE * LANES
    bits = lax.bitcast_convert_type(t.astype(BF16), jnp.uint16).astype(jnp.uint32)
    bits = bits.reshape(e, 2, HALF_TILE, LANES)
    word = bits[:, 0] | (bits[:, 1] << 16)
    return lax.bitcast_convert_type(word, jnp.int32).reshape(e * HALF_TILE, LANES)


def _gather_rows(tab_ref, off_ref, base, stack_ref, slots, place=lambda j: j):
    tok = off_ref.at[pl.ds(base, slots)]
    for j in range(slots):
        off = pl.multiple_of(tok[j], HALF_TILE)
        p = place(j)
        stack_ref[p * HALF_TILE:(p + 1) * HALF_TILE, :] = tab_ref[pl.ds(off, HALF_TILE), :]


def _fold_pairs(xs, span):
    pos = lax.broadcasted_iota(jnp.int32, (ROW_TILE, LANES), 0)
    keep = (pos % (2 * span)) < span
    out = []
    for a, b in zip(xs[0::2], xs[1::2]):
        other = jnp.where(keep, b, a)
        swapped = jnp.where(keep, pltpu.roll(other, ROW_TILE - span, 0), pltpu.roll(other, span, 0))
        out.append(jnp.where(keep, a, b) + swapped)
    return out


_FOLD_ROW_OF_GROUP = (0, 4, 2, 6, 1, 5, 3, 7)


def _lane_sums_as_row(q):
    ones = jnp.ones((ROW_TILE, LANES), BF16)
    nt = lambda b: lax.dot_general(ones, b, (((1,), (1,)), ((), ())), preferred_element_type=F32)
    hi = q.astype(BF16)
    r1 = q - hi.astype(F32)
    mid = r1.astype(BF16)
    lo = (r1 - mid.astype(F32)).astype(BF16)
    return nt(hi) + nt(mid) + nt(lo)


def _dot3(x, w01):
    hi = x.astype(BF16)
    r1 = x - hi.astype(F32)
    mid = r1.astype(BF16)
    lo = (r1 - mid.astype(F32)).astype(BF16)
    d = lambda a: jnp.dot(a, w01, preferred_element_type=F32)
    return d(hi) + d(mid) + d(lo)


def _peer_act_kernel(off_ref, x_ref, tab_ref, gate_ref, w_ref, stack_a, stack_b, part_ref, act_ref):
    tb, slots = gate_ref.shape
    place = lambda j: ROW_TILE * (j // ROW_TILE) + _FOLD_ROW_OF_GROUP[j % ROW_TILE]

    def one(t, stack_ref):
        _gather_rows(tab_ref, off_ref, t * slots, stack_ref, slots, place)
        x = x_ref[pl.ds(pl.multiple_of(t * ROW_TILE, ROW_TILE), ROW_TILE), :]
        x_lo = jnp.concatenate([x[:HALF_TILE]] * 2, axis=0)
        x_hi = jnp.concatenate([x[HALF_TILE:]] * 2, axis=0)
        for g in range(slots // ROW_TILE):
            prods = []
            for k in range(4 * g, 4 * g + 4):
                word = stack_ref[k * ROW_TILE:(k + 1) * ROW_TILE, :]
                prods.append(lax.bitcast_convert_type(jnp.left_shift(word, 16), F32) * x_lo
                             + lax.bitcast_convert_type(word & jnp.int32(-65536), F32) * x_hi)
            part_ref[t, g * ROW_TILE:(g + 1) * ROW_TILE, :] = _fold_pairs(_fold_pairs(prods, 2), 1)[0]

    def pair(i, carry):
        one(2 * i, stack_a)
        one(2 * i + 1, stack_b)
        return carry

    lax.fori_loop(0, tb // 2, pair, 0)

    for c in range(tb // ROW_TILE):
        rows = part_ref[c * ROW_TILE:(c + 1) * ROW_TILE].reshape(ROW_TILE * slots, LANES)
        sums = _lane_sums_as_row(rows)
        for u in range(ROW_TILE):
            act_ref[c * ROW_TILE + u:c * ROW_TILE + u + 1, :] = sums[:1, u * slots:(u + 1) * slots]
    act = act_ref[...]
    gelu = 0.5 * act * (1.0 + lax.erf(act * math.sqrt(0.5)))
    w_ref[...] = gate_ref[...] * gelu


def _peer_act(off, x8, tab, gate, *, tb):
    n, slots = gate.shape
    stack = pltpu.VMEM((slots * HALF_TILE, LANES), jnp.int32)
    return pl.pallas_call(
        _peer_act_kernel,
        grid=(n // tb,),
        in_specs=[pl.BlockSpec((tb * slots,), lambda i: (i,), memory_space=pltpu.SMEM),
                  pl.BlockSpec((tb * ROW_TILE, LANES), lambda i: (i, 0)),
                  pl.BlockSpec(memory_space=pltpu.VMEM),
                  pl.BlockSpec((tb, slots), lambda i: (i, 0))],
        out_specs=pl.BlockSpec((tb, slots), lambda i: (i, 0)),
        out_shape=jax.ShapeDtypeStruct((n, slots), F32),
        scratch_shapes=[stack, stack, pltpu.VMEM((tb, slots, LANES), F32),
                        pltpu.VMEM((tb, slots), F32)],
        compiler_params=_cparams(("arbitrary",)),
        name="peer_expert_act",
    )(off, x8, tab, gate)


def _peer_mix_kernel(off_ref, w_ref, tab_ref, x2_ref, gfin_ref, o_ref, stack_a, stack_b, wbc_all,
                     *, final_norm):
    tb, slots = w_ref.shape
    nacc = 4
    group = 16
    wt = w_ref[...].T
    w_hi = wt.astype(BF16)
    r1 = wt - w_hi.astype(F32)
    w_mid = r1.astype(BF16)
    w_lo = (r1 - w_mid.astype(F32)).astype(BF16)
    row_tok = lax.broadcasted_iota(jnp.int32, (tb, group * LANES), 0)
    col_tok = lax.broadcasted_iota(jnp.int32, (tb, group * LANES), 1) // LANES

    def broadcast_group(c, carry):
        onehot = (row_tok == col_tok + c * group).astype(BF16)
        res = sum(jnp.dot(term, onehot, preferred_element_type=F32) for term in (w_hi, w_mid, w_lo))
        for tt in range(group):
            wbc_all[c * group + tt] = res[:, tt * LANES:(tt + 1) * LANES]
        return carry

    lax.fori_loop(0, tb // group, broadcast_group, 0)
    sub = lax.broadcasted_iota(jnp.int32, (ROW_TILE, LANES), 0)

    def one(t, stack_ref):
        _gather_rows(tab_ref, off_ref, t * slots, stack_ref, slots)
        wbc_ref = wbc_all.at[t]
        lo = [jnp.zeros((ROW_TILE, LANES), F32) for _ in range(nacc)]
        hi_acc = [jnp.zeros((ROW_TILE, LANES), F32) for _ in range(nacc)]
        for k in range(slots // 2):
            word = stack_ref[k * ROW_TILE:(k + 1) * ROW_TILE, :]
            wv = jnp.where(sub < HALF_TILE,
                           jnp.broadcast_to(wbc_ref[2 * k:2 * k + 1, :], (ROW_TILE, LANES)),
                           jnp.broadcast_to(wbc_ref[2 * k + 1:2 * k + 2, :], (ROW_TILE, LANES)))
            a = k % nacc
            lo[a] = lo[a] + lax.bitcast_convert_type(jnp.left_shift(word, 16), F32) * wv
            hi_acc[a] = hi_acc[a] + lax.bitcast_convert_type(word & jnp.int32(-65536), F32) * wv
        lo_sum = (lo[0] + lo[1]) + (lo[2] + lo[3])
        hi_sum = (hi_acc[0] + hi_acc[1]) + (hi_acc[2] + hi_acc[3])
        ff = jnp.concatenate([lo_sum[:HALF_TILE] + lo_sum[HALF_TILE:],
                              hi_sum[:HALF_TILE] + hi_sum[HALF_TILE:]], axis=0)
        rows = pl.ds(pl.multiple_of(t * ROW_TILE, ROW_TILE), ROW_TILE)
        o_ref[rows, :] = x2_ref[rows, :] + ff

    def pair(i, carry):
        one(2 * i, stack_a)
        one(2 * i + 1, stack_b)
        return carry

    lax.fori_loop(0, tb // 2, pair, 0)
    if final_norm:
        x3 = o_ref[...].reshape(tb, ROW_TILE, LANES)
        sq = jnp.sum(jnp.sum(x3 * x3, axis=2, keepdims=True), axis=1, keepdims=True)
        scale = lax.rsqrt(sq * (1.0 / (ROW_TILE * LANES)) + NORM_EPS)
        o_ref[...] = (x3 * scale * gfin_ref[...][None]).reshape(tb * ROW_TILE, LANES)


def _peer_mix(off, wgt, tab, x8, g_final, *, tb, final_norm):
    n, slots = wgt.shape
    d = ROW_TILE * LANES
    g8 = g_final.reshape(ROW_TILE, LANES)
    kern = functools.partial(_peer_mix_kernel, final_norm=final_norm)
    stack = pltpu.VMEM((slots * HALF_TILE, LANES), jnp.int32)
    out = pl.pallas_call(
        kern,
        grid=(n // tb,),
        in_specs=[pl.BlockSpec((tb * slots,), lambda i: (i,), memory_space=pltpu.SMEM),
                  pl.BlockSpec((tb, slots), lambda i: (i, 0)),
                  pl.BlockSpec(memory_space=pltpu.VMEM),
                  pl.BlockSpec((tb * ROW_TILE, LANES), lambda i: (i, 0)),
                  pl.BlockSpec((ROW_TILE, LANES), lambda i: (0, 0))],
        out_specs=pl.BlockSpec((tb * ROW_TILE, LANES), lambda i: (i, 0)),
        out_shape=jax.ShapeDtypeStruct((n * ROW_TILE, LANES), F32),
        scratch_shapes=[stack, stack, pltpu.VMEM((tb, slots, LANES), F32)],
        compiler_params=_cparams(("arbitrary",)),
        name="peer_expert_mix",
    )(off, wgt, tab, x8, g8)
    return out.reshape(n, d)


def _state_to_blocks(s):
    b, h, d, _ = s.shape
    st = jnp.swapaxes(s, -1, -2).reshape(b, h // HEADS_PER_LANE_TILE, HEADS_PER_LANE_TILE, d, d)
    eye = jnp.eye(HEADS_PER_LANE_TILE, dtype=s.dtype)
    blk = st[:, :, :, :, None, :] * eye[None, None, :, None, :, None]
    return blk.reshape(b, h // HEADS_PER_LANE_TILE, LANES, LANES)


def _blocks_to_state(blk, heads):
    b, npair = blk.shape[:2]
    x = blk.reshape(b, npair, HEADS_PER_LANE_TILE, HEAD_DIM, HEADS_PER_LANE_TILE, HEAD_DIM)
    diag = jnp.stack([x[:, :, hh, :, hh, :] for hh in range(HEADS_PER_LANE_TILE)], axis=2)
    return jnp.swapaxes(diag.reshape(b, heads, HEAD_DIM, HEAD_DIM), -1, -2)


PEER_TOPK = 16


def _layer(x, k_past, v_past, lf_past, s0, shift0, lp, g_final, final_norm):
    (norm_mix_g, w_in, fox_b_f, mu, w0, w2, a0, a2, g2, k_k, k_a, r_k, lnx_w, lnx_b, w_out,
     norm_ffn_g, peer_w_q, peer_sub_keys, tab_u, tab_v) = lp
    b, t, d = x.shape
    n = b * t
    fox_heads = fox_b_f.shape[0]
    fw = fox_heads * HEAD_DIM
    fox_cols = 3 * fw + fox_heads
    rwkv_heads = r_k.shape[0]
    w = rwkv_heads * HEAD_DIM
    x2d = x.reshape(n, d)
    qb, k, v, kb, vb, lf, rw_main, rw_tail = _inproj(x2d, norm_mix_g, w_in, fox_b_f, fox_cols,
                                                     fox_heads, 3 * w)
    fox = _fox_stream(qb, kb, vb, lf, k_past, v_past, lf_past, batch=b, q_len=t)

    prm = _rwkv_params(mu, w0, w2, a0, a2, g2, k_k, k_a, r_k.reshape(-1))
    tail = prm["tail"]
    shift_main = shift0[..., :3 * w]
    shift_tail = jnp.pad(shift0[..., 3 * w:], ((0, 0), (0, 0), (0, TAIL_PAD - tail)))
    r, lw, km, vv, kn, bb, g, bonus = _rwkv_pre(rw_main, rw_tail, shift_main, shift_tail, prm,
                                                batch=b, seq=t)
    y, s_blk = _rwkv_scan(r, lw, km, vv, kn, bb, _state_to_blocks(s0), batch=b, seq=t)
    s_t = _blocks_to_state(s_blk, rwkv_heads)
    last = jnp.concatenate([rw_main.reshape(b, t, -1)[:, -1:], rw_tail.reshape(b, t, -1)[:, -1:, :tail]],
                           axis=-1)

    x2, xn, scores = _outproj(x2d, fox, y, bonus, g, lnx_w, lnx_b, w_out, norm_ffn_g, peer_w_q,
                              peer_sub_keys)
    idx, gate = _retrieve(scores, PEER_TOPK)
    tb = _row_tile(n, 128)
    slots = gate.shape[1]
    off = idx.reshape(n * slots)
    wgt = _peer_act(off, xn, tab_u, gate, tb=tb)
    out = _peer_mix(off, wgt, tab_v, x2, g_final, tb=tb, final_norm=final_norm)
    return (out.reshape(b, t, d), k.reshape(b, t, fox_heads, HEAD_DIM),
            v.reshape(b, t, fox_heads, HEAD_DIM), lf.reshape(b, t, fox_heads), s_t, last)


def kernel(x_prompt, x_sample, cache_fox_k, cache_fox_v, cache_fox_logf, state_rwkv, state_shift,
           norm_mix_g, w_in, fox_b_f, rwkv_mu, rwkv_w0, rwkv_w2, rwkv_a0, rwkv_a2, rwkv_g2,
           rwkv_k_k, rwkv_k_a, rwkv_r_k, rwkv_lnx_w, rwkv_lnx_b, w_out, norm_ffn_g,
           peer_w_q, peer_sub_keys, peer_u, peer_v, norm_final_g):
    depth = w_in.shape[0]
    yp, ys = x_prompt, x_sample
    bp = x_prompt.shape[0]
    dt = x_prompt.dtype
    fox_heads = fox_b_f.shape[1]
    rwkv_heads = rwkv_r_k.shape[1]
    rwkv_cols = rwkv_mu.shape[1]
    outs_p, outs_s = [], []
    for l in range(depth):
        lp = (norm_mix_g[l], w_in[l], fox_b_f[l], rwkv_mu[l], rwkv_w0[l], rwkv_w2[l], rwkv_a0[l],
              rwkv_a2[l], rwkv_g2[l], rwkv_k_k[l], rwkv_k_a[l], rwkv_r_k[l], rwkv_lnx_w[l],
              rwkv_lnx_b[l], w_out[l], norm_ffn_g[l], peer_w_q[l], peer_sub_keys[l],
              _pack_table(peer_u[l]), _pack_table(peer_v[l]))
        last = l == depth - 1
        empty_kv = jnp.zeros((bp, 0, fox_heads, HEAD_DIM), dt)
        empty_lf = jnp.zeros((bp, 0, fox_heads), dt)
        s_zero = jnp.zeros((bp, rwkv_heads, HEAD_DIM, HEAD_DIM), dt)
        sh_zero = jnp.zeros((bp, 1, rwkv_cols), dt)
        yp, *rest_p = _layer(yp, empty_kv, empty_kv, empty_lf, s_zero, sh_zero, lp, norm_final_g, last)
        ys, *rest_s = _layer(ys, cache_fox_k[l], cache_fox_v[l], cache_fox_logf[l], state_rwkv[l],
                             state_shift[l], lp, norm_final_g, last)
        outs_p.append(rest_p)
        outs_s.append(rest_s)
    stack = lambda outs, i: jnp.stack([o[i] for o in outs])
    return ((yp, ys) + tuple(stack(outs_p, i) for i in range(5))
            + tuple(stack(outs_s, i) for i in range(5)))
```

```python
import functools
import math

import jax
import jax.numpy as jnp
from jax import lax
from jax.experimental import pallas as pl
from jax.experimental.pallas import tpu as pltpu

F32 = jnp.float32
BF16 = jnp.bfloat16

HEAD_DIM = 64
LANES = 128
HEADS_PER_LANE_TILE = LANES // HEAD_DIM
TAIL_PAD = 2 * LANES
CHUNK = 64
RWKV_PASSES = 1
NORM_EPS = 1e-6
LNX_EPS = 64e-5
NEG_BIG = -1e30
LOG2E = math.log2(math.e)
HIGHEST = lax.Precision.HIGHEST
VMEM_LIMIT = 48 * 1024 * 1024


def _cparams(sem):
    return pltpu.CompilerParams(dimension_semantics=sem, vmem_limit_bytes=VMEM_LIMIT)


def _row_tile(n, target):
    t = min(n, target)
    assert n % t == 0, (n, t)
    return t


def _inproj_kernel(x_ref, g_ref, wqkv_ref, wf_ref, wrw_ref, wtail_ref, bf_ref,
                   q_ref, k_ref, v_ref, kb_ref, vb_ref, lf_ref, rw_ref, tail_ref):
    x = x_ref[...]
    h = x * lax.rsqrt(jnp.mean(x * x, axis=-1, keepdims=True) + NORM_EPS) * g_ref[...]
    hb = h.astype(BF16)
    fw = wqkv_ref.shape[1] // 3
    qkv = jnp.dot(hb, wqkv_ref[...], preferred_element_type=F32)
    q_ref[...] = (qkv[:, :fw] * (LOG2E / math.sqrt(HEAD_DIM))).astype(BF16)
    k = qkv[:, fw:2 * fw]
    v = qkv[:, 2 * fw:]
    k_ref[...] = k
    v_ref[...] = v
    kb_ref[...] = k.astype(BF16)
    vb_ref[...] = v.astype(BF16)
    f = jnp.dot(hb, wf_ref[...], preferred_element_type=F32) + bf_ref[...]
    lf_ref[...] = jax.nn.log_sigmoid(f)
    rw_ref[...] = jnp.dot(hb, wrw_ref[...], preferred_element_type=F32)
    tail_ref[...] = jnp.dot(hb, wtail_ref[...], preferred_element_type=F32)


def _inproj(x2d, g, w_in, b_f, fox_cols, fox_heads, rw_main):
    n, d = x2d.shape
    fw = fox_heads * HEAD_DIM
    wqkv = w_in[:, :3 * fw].astype(BF16)
    wf = w_in[:, 3 * fw:fox_cols].astype(BF16)
    wrw = w_in[:, fox_cols:fox_cols + rw_main].astype(BF16)
    wtail = w_in[:, fox_cols + rw_main:].astype(BF16)
    wtail = jnp.pad(wtail, ((0, 0), (0, TAIL_PAD - wtail.shape[1])))
    tm = _row_tile(n, 512)
    row = lambda c: pl.BlockSpec((tm, c), lambda i: (i, 0))
    full = lambda a: pl.BlockSpec(a.shape, lambda i: (0,) * a.ndim)
    g2 = g.reshape(1, d)
    bf2 = b_f.reshape(1, fox_heads)
    outs = (
        jax.ShapeDtypeStruct((n, fw), BF16),
        jax.ShapeDtypeStruct((n, fw), F32),
        jax.ShapeDtypeStruct((n, fw), F32),
        jax.ShapeDtypeStruct((n, fw), BF16),
        jax.ShapeDtypeStruct((n, fw), BF16),
        jax.ShapeDtypeStruct((n, fox_heads), F32),
        jax.ShapeDtypeStruct((n, rw_main), F32),
        jax.ShapeDtypeStruct((n, TAIL_PAD), F32),
    )
    return pl.pallas_call(
        _inproj_kernel,
        grid=(n // tm,),
        in_specs=[row(d), full(g2), full(wqkv), full(wf), full(wrw), full(wtail), full(bf2)],
        out_specs=[row(fw), row(fw), row(fw), row(fw), row(fw), row(fox_heads), row(rw_main),
                   row(TAIL_PAD)],
        out_shape=outs,
        compiler_params=_cparams(("parallel",)),
        name="inproj",
    )(x2d, g2, wqkv, wf, wrw, wtail, bf2)


def _cumsum_kernel(lf_ref, c_ref, carry):
    @pl.when(pl.program_id(1) == 0)
    def _():
        carry[...] = jnp.zeros_like(carry)

    lf = lf_ref[...]
    tc = lf.shape[0]
    r = lax.broadcasted_iota(jnp.int32, (tc, tc), 0)
    c = lax.broadcasted_iota(jnp.int32, (tc, tc), 1)
    lower = (c <= r).astype(F32)
    cc = jnp.dot(lower, lf, precision=HIGHEST, preferred_element_type=F32) + carry[...]
    c_ref[...] = cc
    carry[...] = cc[tc - 1:tc, :]


def _cumsum(lf, tc):
    b, l, nh = lf.shape
    assert l % tc == 0
    return pl.pallas_call(
        _cumsum_kernel,
        grid=(b, l // tc),
        in_specs=[pl.BlockSpec((None, tc, nh), lambda i, j: (i, j, 0))],
        out_specs=pl.BlockSpec((None, tc, nh), lambda i, j: (i, j, 0)),
        out_shape=jax.ShapeDtypeStruct((b, l, nh), F32),
        scratch_shapes=[pltpu.VMEM((1, nh), F32)],
        compiler_params=_cparams(("parallel", "arbitrary")),
        name="cumsum_logf",
    )(lf)


def _split3(x):
    hi = x.astype(BF16)
    r = x - hi.astype(F32)
    mid = r.astype(BF16)
    lo = (r - mid.astype(F32)).astype(BF16)
    return hi.astype(F32), mid.astype(F32), lo.astype(F32)


def _augment_kernel(x_ref, c_ref, o_ref, *, role):
    tm = x_ref.shape[0]
    lane = lax.broadcasted_iota(jnp.int32, (tm, LANES), 1)
    for p in range(x_ref.shape[1] // LANES):
        xp = x_ref[:, p * LANES:(p + 1) * LANES].astype(F32)
        for hh in range(HEADS_PER_LANE_TILE):
            h = p * HEADS_PER_LANE_TILE + hh
            own = (lane >= hh * HEAD_DIM) & (lane < (hh + 1) * HEAD_DIM)
            e = (lane + (1 - hh) * HEAD_DIM) % LANES
            if role == "v":
                ext = jnp.where(e == 0, 1.0, 0.0)
            else:
                c = jnp.broadcast_to(c_ref[:, h:h + 1], (tm, LANES))
                hi, mid, lo = _split3(c * LOG2E)
                sgn = 1.0 if role == "q" else -1.0
                base = 0 if role == "q" else 3
                ext = jnp.where(e == base, sgn * hi,
                                jnp.where(e == base + 1, sgn * mid,
                                          jnp.where(e == base + 2, sgn * lo,
                                                    jnp.where(e < 6, 1.0, 0.0))))
            o_ref[:, h * LANES:(h + 1) * LANES] = jnp.where(own, xp, ext).astype(BF16)


def _augment(x, c, role):
    n, w = x.shape
    nh = w // HEAD_DIM
    tm = _row_tile(n, 512) if n % 512 == 0 else n
    kern = functools.partial(_augment_kernel, role=role)
    return pl.pallas_call(
        kern,
        grid=(n // tm,),
        in_specs=[pl.BlockSpec((tm, w), lambda i: (i, 0)),
                  pl.BlockSpec((tm, nh), lambda i: (i, 0))],
        out_specs=pl.BlockSpec((tm, nh * LANES), lambda i: (i, 0)),
        out_shape=jax.ShapeDtypeStruct((n, nh * LANES), BF16),
        compiler_params=_cparams(("parallel",)),
        name="fox_augment_" + role,
    )(x, c)


Q_SUB = 128
K_SUB = 256


def _fox_kernel(qblk_ref, kblk_ref, last_ref, q_ref, k_ref, v_ref, o_ref, m_sc, acc_sc,
                *, q_off, tq, tk):
    t = pl.program_id(2)
    i = qblk_ref[t]
    j = kblk_ref[t]
    qs_n, ks_n = min(Q_SUB, tq), min(K_SUB, tk)

    @pl.when(j == 0)
    def _():
        m_sc[...] = jnp.full_like(m_sc, NEG_BIG)
        acc_sc[...] = jnp.zeros_like(acc_sc)

    q_lo = q_off + i * tq
    k_lo = j * tk

    def body(masked):
        if masked:
            diff = (lax.broadcasted_iota(jnp.int32, (qs_n, LANES), 1)
                    - lax.broadcasted_iota(jnp.int32, (qs_n, LANES), 0))
        nqs = tq // qs_n
        qrow = [slice(qs * qs_n, (qs + 1) * qs_n) for qs in range(nqs)]
        m_run = [[m_sc[hh, qrow[qs], :] for qs in range(nqs)] for hh in range(HEADS_PER_LANE_TILE)]
        a_run = [[acc_sc[hh, qrow[qs], :] for qs in range(nqs)] for hh in range(HEADS_PER_LANE_TILE)]
        for hh in range(HEADS_PER_LANE_TILE):
            cols = slice(hh * LANES, (hh + 1) * LANES)
            for ks in range(tk // ks_n):
                krows = slice(ks * ks_n, (ks + 1) * ks_n)
                k_sub = k_ref[krows, cols]
                v_sub = v_ref[krows, cols]
                for qs in range(nqs):
                    qrows = qrow[qs]
                    s = lax.dot_general(q_ref[qrows, cols], k_sub, (((1,), (1,)), ((), ())),
                                        preferred_element_type=F32)
                    parts = [s[:, c * LANES:(c + 1) * LANES] for c in range(ks_n // LANES)]
                    if masked:
                        parts = [jnp.where(diff <= q_lo - k_lo + qs * qs_n - ks * ks_n - c * LANES,
                                           pc, NEG_BIG) for c, pc in enumerate(parts)]
                    mx = parts[0]
                    for pc in parts[1:]:
                        mx = jnp.maximum(mx, pc)
                    m_old = m_run[hh][qs]
                    m_new = jnp.maximum(m_old, jnp.max(mx, axis=-1, keepdims=True))
                    alpha = jnp.exp2(m_old - m_new)
                    pr = jnp.concatenate([jnp.exp2(pc - m_new).astype(BF16) for pc in parts], axis=1)
                    pv = jnp.dot(pr, v_sub, preferred_element_type=F32)
                    a_run[hh][qs] = alpha * a_run[hh][qs] + pv
                    m_run[hh][qs] = m_new
        for hh in range(HEADS_PER_LANE_TILE):
            for qs in range(nqs):
                m_sc[hh, qrow[qs], :] = m_run[hh][qs]
                acc_sc[hh, qrow[qs], :] = a_run[hh][qs]

    fully_visible = k_lo + tk - 1 <= q_lo

    @pl.when(fully_visible)
    def _():
        body(False)

    @pl.when(jnp.logical_not(fully_visible))
    def _():
        body(True)

    @pl.when(last_ref[t] == 1)
    def _():
        lane = lax.broadcasted_iota(jnp.int32, (1, LANES), 1)
        out = jnp.zeros((tq, LANES), F32)
        for hh in range(HEADS_PER_LANE_TILE):
            in_head = (lane >= hh * HEAD_DIM) & (lane < (hh + 1) * HEAD_DIM)
            acc = acc_sc[hh]
            ones_col = (1 - hh) * HEAD_DIM
            denom = jnp.broadcast_to(acc[:, ones_col:ones_col + 1], acc.shape)
            out = jnp.where(in_head, acc / denom, out)
        o_ref[...] = out


def _fox_attend(q_aug, k_aug, v_aug, *, batch, q_len, kv_len, q_off, tq, tk):
    n, wa = q_aug.shape
    pair_w = HEADS_PER_LANE_TILE * LANES
    npair = wa // pair_w
    nq, nk = q_len // tq, kv_len // tk
    assert q_len % tq == 0 and kv_len % tk == 0

    pairs = [(i, j) for i in range(nq) for j in range(min(nk, (q_off + (i + 1) * tq - 1) // tk + 1))]
    qblk = jnp.asarray([i for i, _ in pairs], jnp.int32)
    kblk = jnp.asarray([j for _, j in pairs], jnp.int32)
    last = jnp.asarray([int(t + 1 == len(pairs) or pairs[t + 1][0] != i)
                        for t, (i, _) in enumerate(pairs)], jnp.int32)
    q_map = lambda b, p, t, qb, kb, lt: (b * nq + qb[t], p)
    kv_map = lambda b, p, t, qb, kb, lt: (b * nk + kb[t], p)
    kern = functools.partial(_fox_kernel, q_off=q_off, tq=tq, tk=tk)
    return pl.pallas_call(
        kern,
        grid_spec=pltpu.PrefetchScalarGridSpec(
            num_scalar_prefetch=3,
            grid=(batch, npair, len(pairs)),
            in_specs=[pl.BlockSpec((tq, pair_w), q_map),
                      pl.BlockSpec((tk, pair_w), kv_map),
                      pl.BlockSpec((tk, pair_w), kv_map)],
            out_specs=pl.BlockSpec((tq, LANES), q_map),
            scratch_shapes=[pltpu.VMEM((HEADS_PER_LANE_TILE, tq, LANES), F32),
                            pltpu.VMEM((HEADS_PER_LANE_TILE, tq, LANES), F32)]),
        out_shape=jax.ShapeDtypeStruct((n, npair * LANES), F32),
        compiler_params=_cparams(("parallel", "parallel", "arbitrary")),
        name="fox_attention",
    )(qblk, kblk, last, q_aug, k_aug, v_aug)


def _fox_stream(qb, kb, vb, lf, k_past, v_past, lf_past, *, batch, q_len):
    n, w = qb.shape
    nh = lf.shape[1]
    past = k_past.shape[1]
    lf_new = lf.reshape(batch, q_len, nh)
    if past == 0:
        kv_len = q_len
        k_all, v_all, lf_all = kb, vb, lf_new
        tq = tk = _row_tile(q_len, 512)
        tc = tk
    else:
        kv_len = -(-(past + q_len) // K_SUB) * K_SUB
        pad = kv_len - past - q_len

        def cat(old, new):
            old = old.reshape(batch, past, -1).astype(new.dtype)
            new = new.reshape(batch, q_len, -1)
            z = jnp.zeros((batch, pad, new.shape[-1]), new.dtype)
            return jnp.concatenate([old, new, z], axis=1)

        k_all = cat(k_past, kb).reshape(batch * kv_len, w)
        v_all = cat(v_past, vb).reshape(batch * kv_len, w)
        lf_all = cat(lf_past, lf_new)
        tc = max(t for t in range(LANES, 1024 + 1, LANES) if kv_len % t == 0)
        tq, tk = q_len, kv_len
    c = _cumsum(lf_all, tc)
    c_k = c.reshape(batch * kv_len, nh)
    c_q = c[:, past:past + q_len].reshape(n, nh)
    return _fox_attend(_augment(qb, c_q, "q"), _augment(k_all, c_k, "k"), _augment(v_all, c_k, "v"),
                       batch=batch, q_len=q_len, kv_len=kv_len, q_off=past, tq=tq, tk=tk)


def _head_sum_matrix(width):
    r = lax.broadcasted_iota(jnp.int32, (width, width), 0) // HEAD_DIM
    c = lax.broadcasted_iota(jnp.int32, (width, width), 1) // HEAD_DIM
    return (r == c).astype(BF16)


def _rwkv_pre_kernel(pm_ref, pt_ref, sm_ref, st_ref, mum_ref, mut_ref, wbig_ref, w0_ref, a0_ref,
                     kk_ref, ka_ref, rk_ref,
                     r_out, lw_out, km_out, v_out, kn_out, b_out, g_out, bonus_out,
                     carry_m, carry_t, *, lora_w, lora_a):
    @pl.when(pl.program_id(1) == 0)
    def _():
        carry_m[...] = sm_ref[...]
        carry_t[...] = st_ref[...]

    pm = pm_ref[...]
    pt = pt_ref[...]
    tm = pm.shape[0]
    w = pm.shape[1] // 3

    def shifted(p, carry):
        row = lax.broadcasted_iota(jnp.int32, p.shape, 0)
        return jnp.where(row == 0, carry[...], pltpu.roll(p, 1, 0))

    prev_m = shifted(pm, carry_m)
    prev_t = shifted(pt, carry_t)
    carry_m[...] = pm[tm - 1:tm, :]
    carry_t[...] = pt[tm - 1:tm, :]
    psm = pm + mum_ref[...] * (prev_m - pm)
    pst = pt + mut_ref[...] * (prev_t - pt)
    r = psm[:, :w]
    k = psm[:, w:2 * w]
    v = psm[:, 2 * w:]
    lane = lax.broadcasted_iota(jnp.int32, pst.shape, 1)
    z = jnp.where(lane < lora_w, jnp.tanh(pst),
                  jnp.where(lane < lora_w + lora_a, pst, jax.nn.sigmoid(pst)))
    lo = jnp.dot(z.astype(BF16), wbig_ref[...], preferred_element_type=F32)
    w_log = -jax.nn.softplus(-(w0_ref[...] + lo[:, :w])) - 0.5
    lw = -jnp.exp(w_log)
    a = jax.nn.sigmoid(a0_ref[...] + lo[:, w:2 * w])
    g = lo[:, 2 * w:]
    e = _head_sum_matrix(w)
    kk0 = k * kk_ref[...]
    n2 = _dot3(kk0 * kk0, e)
    kn = kk0 / jnp.maximum(jnp.sqrt(n2), 1e-12)
    km = k * (1.0 + (a - 1.0) * ka_ref[...])
    rk = _dot3(r * km * rk_ref[...], e)
    r_out[...] = r
    lw_out[...] = lw
    km_out[...] = km
    v_out[...] = v
    kn_out[...] = kn
    b_out[...] = kn * a
    g_out[...] = g
    bonus_out[...] = rk * v


def _rwkv_params(mu, w0, w2, a0, a2, g2, k_k, k_a, r_k):
    w = w0.shape[0]
    lora_w, lora_a, lora_g = w2.shape[0], a2.shape[0], g2.shape[0]
    w_lora = jnp.zeros((TAIL_PAD, 3 * w), F32)
    w_lora = w_lora.at[:lora_w, :w].set(w2)
    w_lora = w_lora.at[lora_w:lora_w + lora_a, w:2 * w].set(a2)
    w_lora = w_lora.at[lora_w + lora_a:lora_w + lora_a + lora_g, 2 * w:].set(g2)
    tail = mu.shape[0] - 3 * w
    return dict(
        mu_main=mu[:3 * w].reshape(1, 3 * w),
        mu_tail=jnp.pad(mu[3 * w:], (0, TAIL_PAD - tail)).reshape(1, TAIL_PAD),
        w_lora=w_lora.astype(BF16), w0=w0.reshape(1, w), a0=a0.reshape(1, w),
        k_k=k_k.reshape(1, w), k_a=k_a.reshape(1, w), r_k=r_k.reshape(1, w),
        lora_w=lora_w, lora_a=lora_a, tail=tail)


def _rwkv_pre(rw_main, rw_tail, shift_main, shift_tail, prm, *, batch, seq):
    n, w3 = rw_main.shape
    w = w3 // 3
    tm = _row_tile(seq, 512)
    nt = seq // tm
    row = lambda c: pl.BlockSpec((tm, c), lambda b, i: (b * nt + i, 0))
    per_b = lambda c: pl.BlockSpec((None, 1, c), lambda b, i: (b, 0, 0))
    full = lambda a: pl.BlockSpec(a.shape, lambda b, i: (0,) * a.ndim)
    consts = [prm["mu_main"], prm["mu_tail"], prm["w_lora"], prm["w0"], prm["a0"], prm["k_k"],
              prm["k_a"], prm["r_k"]]
    kern = functools.partial(_rwkv_pre_kernel, lora_w=prm["lora_w"], lora_a=prm["lora_a"])
    return pl.pallas_call(
        kern,
        grid=(batch, nt),
        in_specs=[row(w3), row(TAIL_PAD), per_b(w3), per_b(TAIL_PAD)] + [full(c) for c in consts],
        out_specs=[row(w)] * 8,
        out_shape=[jax.ShapeDtypeStruct((n, w), F32)] * 8,
        scratch_shapes=[pltpu.VMEM((1, w3), F32), pltpu.VMEM((1, TAIL_PAD), F32)],
        compiler_params=_cparams(("parallel", "arbitrary")),
        name="rwkv_pre",
    )(rw_main, rw_tail, shift_main, shift_tail, *consts)


def _bmm(a, b, kind, passes):
    contract = {"nn": ((2,), (1,)), "nt": ((2,), (2,)), "tn": ((1,), (1,))}[kind]
    dims = (contract, ((0,), (0,)))
    if passes == 6:
        return lax.dot_general(a, b, dims, precision=HIGHEST, preferred_element_type=F32)
    dg = lambda x, y: lax.dot_general(x, y, dims, preferred_element_type=F32)
    ah, bh = a.astype(BF16), b.astype(BF16)
    out = dg(ah, bh)
    if passes == 3:
        al = (a - ah.astype(F32)).astype(BF16)
        bl = (b - bh.astype(F32)).astype(BF16)
        out = out + dg(ah, bl) + dg(al, bh)
    return out


def _rwkv_chunk(r, lw, km, v, kn, bb, s_blk, passes):
    g, c, _ = r.shape
    c2 = HEADS_PER_LANE_TILE * c
    ti = lax.broadcasted_iota(jnp.int32, (g, c, c), 1)
    si = lax.broadcasted_iota(jnp.int32, (g, c, c), 2)
    cs = _bmm((si <= ti).astype(F32), lw, "nn", 6)
    e_pos = jnp.exp(cs)
    e_neg = jnp.exp(-cs)
    kt = kn * jnp.exp(cs - lw)
    bt = bb * e_neg
    kh = km * e_neg
    rt = r * e_pos
    g_end = e_pos[:, c - 1:c, :]

    lane = lax.broadcasted_iota(jnp.int32, (1, 1, LANES), 2)
    head_of_lane = lane // HEAD_DIM

    def stack_masked(x):
        return jnp.concatenate(
            [jnp.where(head_of_lane == hh, x, 0.0) for hh in range(HEADS_PER_LANE_TILE)], axis=1)

    def stack(x):
        return jnp.concatenate([x] * HEADS_PER_LANE_TILE, axis=1)

    def pick(x):
        out = x[:, :c]
        for hh in range(1, HEADS_PER_LANE_TILE):
            out = jnp.where(head_of_lane == hh, x[:, hh * c:(hh + 1) * c], out)
        return out

    kt2 = stack_masked(kt)
    rt2 = stack_masked(rt)
    rr = lax.broadcasted_iota(jnp.int32, (1, c2, c2), 1)
    cc = lax.broadcasted_iota(jnp.int32, (1, c2, c2), 2)
    strict_blk = (rr // c == cc // c) & (cc < rr)
    x = jnp.where(strict_blk, -_bmm(kt2, stack(bt), "nt", passes), 0.0)
    tinv = (rr == cc).astype(F32) + x
    steps = max(int(math.ceil(math.log2(c))) - 1, 0)
    for _ in range(steps):
        x = _bmm(x, x, "nn", passes)
        tinv = tinv + _bmm(tinv, x, "nn", passes)
    tr = lax.broadcasted_iota(jnp.int32, (1, c2, c), 1) % c
    sr = lax.broadcasted_iota(jnp.int32, (1, c2, c), 2)
    kk_s = jnp.where(sr < tr, _bmm(kt2, kh, "nt", passes), 0.0)
    rb_s = jnp.where(sr <= tr, _bmm(rt2, bt, "nt", passes), 0.0)
    rk_s = jnp.where(sr <= tr, _bmm(rt2, kh, "nt", passes), 0.0)

    ks = _bmm(jnp.concatenate([kt, rt], axis=1), s_blk, "nn", passes)
    rhs = ks[:, :c] + pick(_bmm(kk_s, v, "nn", passes))
    z = pick(_bmm(tinv, stack(rhs), "nn", passes))
    y = ks[:, c:] - pick(_bmm(rb_s, z, "nn", passes)) + pick(_bmm(rk_s, v, "nn", passes))
    jr = lax.broadcasted_iota(jnp.int32, (1, LANES, LANES), 1)
    ic = lax.broadcasted_iota(jnp.int32, (1, LANES, LANES), 2)
    decay_rows = jnp.swapaxes(jnp.broadcast_to(g_end, (g, LANES, LANES)), 1, 2)
    upd = _bmm(jnp.concatenate([bt * g_end, kh * g_end], axis=1),
               jnp.concatenate([-z, v], axis=1), "tn", passes)
    s_new = decay_rows * s_blk + jnp.where(jr // HEAD_DIM == ic // HEAD_DIM, upd, 0.0)
    return y, s_new


def _rwkv_scan_kernel(r_ref, lw_ref, km_ref, v_ref, kn_ref, b_ref, s0_ref, y_ref, sT_ref, s_sc,
                      *, chunk, passes):
    it = pl.program_id(0)
    nb, tb, w = r_ref.shape
    npair = w // LANES

    @pl.when(it == 0)
    def _():
        s_sc[...] = s0_ref[...]

    def step(ci, carry):
        rows = pl.ds(pl.multiple_of(ci * chunk, chunk), chunk)

        def gather(ref):
            blk = ref[:, rows, :]
            return jnp.concatenate([blk[:, :, p * LANES:(p + 1) * LANES] for p in range(npair)],
                                   axis=0)

        y, s_new = _rwkv_chunk(gather(r_ref), gather(lw_ref), gather(km_ref), gather(v_ref),
                               gather(kn_ref), gather(b_ref), s_sc[...], passes)
        for p in range(npair):
            y_ref[:, rows, p * LANES:(p + 1) * LANES] = y[p * nb:(p + 1) * nb]
        s_sc[...] = s_new
        return carry

    lax.fori_loop(0, tb // chunk, step, 0)

    @pl.when(it == pl.num_programs(0) - 1)
    def _():
        sT_ref[...] = s_sc[...]


def _rwkv_scan(r, lw, km, v, kn, bb, s0_blk, *, batch, seq, passes=RWKV_PASSES):
    n, w = r.shape
    npair = w // LANES
    chunk = min(CHUNK, seq)
    tb = _row_tile(seq, 4 * chunk)
    row = pl.BlockSpec((batch, tb, w), lambda i: (0, i, 0))
    st = pl.BlockSpec((npair * batch, LANES, LANES), lambda i: (0, 0, 0))
    s0 = jnp.swapaxes(s0_blk, 0, 1).reshape(npair * batch, LANES, LANES)
    kern = functools.partial(_rwkv_scan_kernel, chunk=chunk, passes=passes)
    y, s_t = pl.pallas_call(
        kern,
        grid=(seq // tb,),
        in_specs=[row] * 6 + [st],
        out_specs=[row, st],
        out_shape=[jax.ShapeDtypeStruct((batch, seq, w), F32),
                   jax.ShapeDtypeStruct((npair * batch, LANES, LANES), F32)],
        scratch_shapes=[pltpu.VMEM((npair * batch, LANES, LANES), F32)],
        compiler_params=_cparams(("arbitrary",)),
        name="rwkv_scan",
    )(*(a.reshape(batch, seq, w) for a in (r, lw, km, v, kn, bb)), s0)
    s_t = jnp.swapaxes(s_t.reshape(npair, batch, LANES, LANES), 0, 1)
    return y.reshape(n, w), s_t


def _outproj_kernel(x_ref, fox_ref, y_ref, bonus_ref, g_ref, lnw_ref, lnb_ref, wa_ref, wb_ref,
                    gf_ref, wqt_ref, keys_ref, x2_ref, xn_ref, sc_ref):
    y = y_ref[...]
    w = y.shape[1]
    e = _head_sum_matrix(w)
    mean = _dot3(y, e) * (1.0 / HEAD_DIM)
    d = y - mean
    var = _dot3(d * d, e) * (1.0 / HEAD_DIM)
    yn = d * lax.rsqrt(var + LNX_EPS) * lnw_ref[...] + lnb_ref[...]
    rw = (yn + bonus_ref[...]) * g_ref[...]
    mix = (jnp.dot(fox_ref[...].astype(BF16), wa_ref[...], preferred_element_type=F32)
           + jnp.dot(rw.astype(BF16), wb_ref[...], preferred_element_type=F32))
    x2 = x_ref[...] + mix
    xn = x2 * lax.rsqrt(jnp.mean(x2 * x2, axis=-1, keepdims=True) + NORM_EPS) * gf_ref[...]
    x2_ref[...] = x2.reshape(x2_ref.shape)
    xn_ref[...] = xn.reshape(xn_ref.shape)
    qt = lax.dot_general(wqt_ref[...], xn.astype(BF16), (((1,), (1,)), ((), ())),
                         preferred_element_type=F32)
    qh = keys_ref.shape[2]
    for hc in range(keys_ref.shape[0]):
        sc_ref[hc] = jnp.dot(keys_ref[hc], qt[hc * qh:(hc + 1) * qh, :].astype(BF16),
                             preferred_element_type=F32)


def _outproj(x2d, fox, y, bonus, g, lnx_w, lnx_b, w_out, g_ffn, w_q, sub_keys):
    n, d = x2d.shape
    w = y.shape[1]
    fw = fox.shape[1]
    wa = w_out[:fw].astype(BF16)
    wb = w_out[fw:].astype(BF16)
    wqt = w_q.T.astype(BF16)
    nkeys, qh = sub_keys.shape[-2:]
    keys = sub_keys.reshape(-1, nkeys, qh).astype(BF16)
    nhc = keys.shape[0]
    tm = _row_tile(n, 512)
    row = lambda c: pl.BlockSpec((tm, c), lambda i: (i, 0))
    tiles = pl.BlockSpec((tm * d // LANES, LANES), lambda i: (i, 0))
    full = lambda a: pl.BlockSpec(a.shape, lambda i: (0,) * a.ndim)
    consts = [lnx_w.reshape(1, w), lnx_b.reshape(1, w), wa, wb, g_ffn.reshape(1, d), wqt, keys]
    return pl.pallas_call(
        _outproj_kernel,
        grid=(n // tm,),
        in_specs=[row(d), row(fw), row(w), row(w), row(w)] + [full(c) for c in consts],
        out_specs=[tiles, tiles, pl.BlockSpec((nhc, nkeys, tm), lambda i: (0, 0, i))],
        out_shape=[jax.ShapeDtypeStruct((n * d // LANES, LANES), F32),
                   jax.ShapeDtypeStruct((n * d // LANES, LANES), F32),
                   jax.ShapeDtypeStruct((nhc, nkeys, n), F32)],
        compiler_params=_cparams(("parallel",)),
        name="outproj_scores",
    )(x2d, fox, y, bonus, g, *consts)


def _topk_rows(s, payload, k):
    rows = lax.broadcasted_iota(jnp.int32, s.shape, 0)
    nrow = s.shape[0]
    vals, idxs, pays = [], [], []
    for _ in range(k):
        m = jnp.max(s, axis=0, keepdims=True)
        idx = jnp.min(jnp.where(s == m, rows, nrow), axis=0, keepdims=True)
        hit = rows == idx
        vals.append(m)
        idxs.append(idx)
        if payload is not None:
            pays.append(jnp.max(jnp.where(hit, payload, -1), axis=0, keepdims=True))
        s = jnp.where(hit, -jnp.inf, s)
    return vals, idxs, pays


def _retrieve_kernel(sc_ref, idx_ref, gate_ref, *, topk, nkeys):
    nhead = sc_ref.shape[0] // 2
    idx_rows, gate_rows = [], []
    for h in range(nhead):
        v1, i1, _ = _topk_rows(sc_ref[2 * h], None, topk)
        v2, i2, _ = _topk_rows(sc_ref[2 * h + 1], None, topk)
        v2a = jnp.concatenate(v2, axis=0)
        i2a = jnp.concatenate(i2, axis=0)
        nb = [topk // (a + 1) for a in range(topk)]
        pad = -sum(nb) % 8
        cand = jnp.concatenate([v1[a] + v2a[:nb[a]] for a in range(topk)]
                               + [jnp.full((pad, v2a.shape[1]), -jnp.inf, F32)], axis=0)
        cidx = jnp.concatenate([i1[a] * nkeys + i2a[:nb[a]] for a in range(topk)]
                               + [jnp.full((pad, v2a.shape[1]), -1, jnp.int32)], axis=0)
        top, _, eidx = _topk_rows(cand, cidx, topk)
        top = jnp.concatenate(top, axis=0)
        ex = jnp.exp(top - top[0:1])
        gate_rows.append(ex / jnp.sum(ex, axis=0, keepdims=True))
        idx_rows.extend(eidx)
    idx_ref[...] = (jnp.concatenate(idx_rows, axis=0) * HALF_TILE).T
    gate_ref[...] = jnp.concatenate(gate_rows, axis=0).T


def _retrieve(scores, topk):
    nhc, nkeys, n = scores.shape
    slots = (nhc // 2) * topk
    tt = _row_tile(n, 256)
    kern = functools.partial(_retrieve_kernel, topk=topk, nkeys=nkeys)
    return pl.pallas_call(
        kern,
        grid=(n // tt,),
        in_specs=[pl.BlockSpec((nhc, nkeys, tt), lambda i: (0, 0, i))],
        out_specs=[pl.BlockSpec((tt, slots), lambda i: (i, 0))] * 2,
        out_shape=[jax.ShapeDtypeStruct((n, slots), jnp.int32),
                   jax.ShapeDtypeStruct((n, slots), F32)],
        compiler_params=_cparams(("parallel",)),
        name="peer_retrieve",
    )(scores)


ROW_TILE = 8
HALF_TILE = ROW_TILE // 2
TOKENS_PER_STEP = 2


def _pack_table(t):
    e, d = t.shape
    assert d == ROW_TILE * LANES
    bits = lax.bitcast_convert_type(t.astype(BF16), jnp.uint16).astype(jnp.uint32)
    bits = bits.reshape(e, 2, HALF_TILE, LANES)
    word = bits[:, 0] | (bits[:, 1] << 16)
    return lax.bitcast_convert_type(word, jnp.int32).reshape(e * HALF_TILE, LANES)


def _gather_rows(tab_ref, off_ref, base, stack_ref, slots, place=lambda j: j):
    tok = off_ref.at[pl.ds(base, slots)]
    for j in range(slots):
        off = pl.multiple_of(tok[j], HALF_TILE)
        p = place(j)
        stack_ref[p * HALF_TILE:(p + 1) * HALF_TILE, :] = tab_ref[pl.ds(off, HALF_TILE), :]


def _fold_pairs(xs, span):
    pos = lax.broadcasted_iota(jnp.int32, (ROW_TILE, LANES), 0)
    keep = (pos % (2 * span)) < span
    out = []
    for a, b in zip(xs[0::2], xs[1::2]):
        other = jnp.where(keep, b, a)
        swapped = jnp.where(keep, pltpu.roll(other, ROW_TILE - span, 0), pltpu.roll(other, span, 0))
        out.append(jnp.where(keep, a, b) + swapped)
    return out


_FOLD_ROW_OF_GROUP = (0, 4, 2, 6, 1, 5, 3, 7)


def _lane_sums_as_row(q):
    ones = jnp.ones((ROW_TILE, LANES), BF16)
    nt = lambda b: lax.dot_general(ones, b, (((1,), (1,)), ((), ())), preferred_element_type=F32)
    hi = q.astype(BF16)
    mid = (q - hi.astype(F32)).astype(BF16)
    return nt(hi) + nt(mid)


def _dot3(x, w01):
    hi = x.astype(BF16)
    r1 = x - hi.astype(F32)
    mid = r1.astype(BF16)
    lo = (r1 - mid.astype(F32)).astype(BF16)
    d = lambda a: jnp.dot(a, w01, preferred_element_type=F32)
    return d(hi) + d(mid) + d(lo)


def _peer_act_kernel(off_ref, x_ref, tab_ref, gate_ref, w_ref, stacks, part_ref, act_ref):
    tb, slots = gate_ref.shape
    place = lambda j: ROW_TILE * (j // ROW_TILE) + _FOLD_ROW_OF_GROUP[j % ROW_TILE]

    def one(t, stack_ref):
        _gather_rows(tab_ref, off_ref, t * slots, stack_ref, slots, place)
        x = x_ref[pl.ds(pl.multiple_of(t * ROW_TILE, ROW_TILE), ROW_TILE), :]
        x_lo = jnp.concatenate([x[:HALF_TILE]] * 2, axis=0)
        x_hi = jnp.concatenate([x[HALF_TILE:]] * 2, axis=0)
        for g in range(slots // ROW_TILE):
            prods = []
            for k in range(4 * g, 4 * g + 4):
                word = stack_ref[k * ROW_TILE:(k + 1) * ROW_TILE, :]
                prods.append(lax.bitcast_convert_type(jnp.left_shift(word, 16), F32) * x_lo
                             + lax.bitcast_convert_type(word & jnp.int32(-65536), F32) * x_hi)
            part_ref[t, g * ROW_TILE:(g + 1) * ROW_TILE, :] = _fold_pairs(_fold_pairs(prods, 2), 1)[0]

    def step(i, carry):
        for u in range(TOKENS_PER_STEP):
            one(TOKENS_PER_STEP * i + u, stacks.at[u])
        return carry

    lax.fori_loop(0, tb // TOKENS_PER_STEP, step, 0)

    for c in range(tb // ROW_TILE):
        rows = part_ref[c * ROW_TILE:(c + 1) * ROW_TILE].reshape(ROW_TILE * slots, LANES)
        sums = _lane_sums_as_row(rows)
        for u in range(ROW_TILE):
            act_ref[c * ROW_TILE + u:c * ROW_TILE + u + 1, :] = sums[:1, u * slots:(u + 1) * slots]
    act = act_ref[...]
    gelu = 0.5 * act * (1.0 + lax.erf(act * math.sqrt(0.5)))
    w_ref[...] = gate_ref[...] * gelu


def _peer_act(off, x8, tab, gate, *, tb):
    n, slots = gate.shape
    stack = pltpu.VMEM((TOKENS_PER_STEP, slots * HALF_TILE, LANES), jnp.int32)
    return pl.pallas_call(
        _peer_act_kernel,
        grid=(n // tb,),
        in_specs=[pl.BlockSpec((tb * slots,), lambda i: (i,), memory_space=pltpu.SMEM),
                  pl.BlockSpec((tb * ROW_TILE, LANES), lambda i: (i, 0)),
                  pl.BlockSpec(memory_space=pltpu.VMEM),
                  pl.BlockSpec((tb, slots), lambda i: (i, 0))],
        out_specs=pl.BlockSpec((tb, slots), lambda i: (i, 0)),
        out_shape=jax.ShapeDtypeStruct((n, slots), F32),
        scratch_shapes=[stack, pltpu.VMEM((tb, slots, LANES), F32),
                        pltpu.VMEM((tb, slots), F32)],
        compiler_params=_cparams(("arbitrary",)),
        name="peer_expert_act",
    )(off, x8, tab, gate)


def _peer_mix_kernel(off_ref, w_ref, tab_ref, x2_ref, gfin_ref, o_ref, stacks, wbc_all, x3_ref,
                     *, final_norm):
    tb, slots = w_ref.shape
    nacc = 4
    group = 16
    wt = w_ref[...].T
    w_hi = wt.astype(BF16)
    w_mid = (wt - w_hi.astype(F32)).astype(BF16)
    row_tok = lax.broadcasted_iota(jnp.int32, (tb, group * LANES), 0)
    col_tok = lax.broadcasted_iota(jnp.int32, (tb, group * LANES), 1) // LANES

    def broadcast_group(c, carry):
        onehot = (row_tok == col_tok + c * group).astype(BF16)
        res = sum(jnp.dot(term, onehot, preferred_element_type=F32) for term in (w_hi, w_mid))
        for tt in range(group):
            wbc_all[c * group + tt] = res[:, tt * LANES:(tt + 1) * LANES]
        return carry

    lax.fori_loop(0, tb // group, broadcast_group, 0)
    sub = lax.broadcasted_iota(jnp.int32, (ROW_TILE, LANES), 0)

    def one(t, stack_ref):
        _gather_rows(tab_ref, off_ref, t * slots, stack_ref, slots)
        wbc_ref = wbc_all.at[t]
        lo = [jnp.zeros((ROW_TILE, LANES), F32) for _ in range(nacc)]
        hi_acc = [jnp.zeros((ROW_TILE, LANES), F32) for _ in range(nacc)]
        for k in range(slots // 2):
            word = stack_ref[k * ROW_TILE:(k + 1) * ROW_TILE, :]
            wv = jnp.where(sub < HALF_TILE,
                           jnp.broadcast_to(wbc_ref[2 * k:2 * k + 1, :], (ROW_TILE, LANES)),
                           jnp.broadcast_to(wbc_ref[2 * k + 1:2 * k + 2, :], (ROW_TILE, LANES)))
            a = k % nacc
            lo[a] = lo[a] + lax.bitcast_convert_type(jnp.left_shift(word, 16), F32) * wv
            hi_acc[a] = hi_acc[a] + lax.bitcast_convert_type(word & jnp.int32(-65536), F32) * wv
        lo_sum = (lo[0] + lo[1]) + (lo[2] + lo[3])
        hi_sum = (hi_acc[0] + hi_acc[1]) + (hi_acc[2] + hi_acc[3])
        ff = jnp.concatenate([lo_sum[:HALF_TILE] + lo_sum[HALF_TILE:],
                              hi_sum[:HALF_TILE] + hi_sum[HALF_TILE:]], axis=0)
        rows = pl.ds(pl.multiple_of(t * ROW_TILE, ROW_TILE), ROW_TILE)
        x3_ref[rows, :] = x2_ref[rows, :] + ff

    def step(i, carry):
        for u in range(TOKENS_PER_STEP):
            one(TOKENS_PER_STEP * i + u, stacks.at[u])
        return carry

    lax.fori_loop(0, tb // TOKENS_PER_STEP, step, 0)
    x3 = x3_ref[...].reshape(tb, ROW_TILE, LANES)
    if final_norm:
        sq = jnp.sum(jnp.sum(x3 * x3, axis=2, keepdims=True), axis=1, keepdims=True)
        scale = lax.rsqrt(sq * (1.0 / (ROW_TILE * LANES)) + NORM_EPS)
        x3 = x3 * scale * gfin_ref[...][None]
    o_ref[...] = x3.reshape(o_ref.shape)


def _peer_mix(off, wgt, tab, x8, g_final, *, tb, final_norm):
    n, slots = wgt.shape
    d = ROW_TILE * LANES
    g8 = g_final.reshape(ROW_TILE, LANES)
    kern = functools.partial(_peer_mix_kernel, final_norm=final_norm)
    stack = pltpu.VMEM((TOKENS_PER_STEP, slots * HALF_TILE, LANES), jnp.int32)
    return pl.pallas_call(
        kern,
        grid=(n // tb,),
        in_specs=[pl.BlockSpec((tb * slots,), lambda i: (i,), memory_space=pltpu.SMEM),
                  pl.BlockSpec((tb, slots), lambda i: (i, 0)),
                  pl.BlockSpec(memory_space=pltpu.VMEM),
                  pl.BlockSpec((tb * ROW_TILE, LANES), lambda i: (i, 0)),
                  pl.BlockSpec((ROW_TILE, LANES), lambda i: (0, 0))],
        out_specs=pl.BlockSpec((tb, d), lambda i: (i, 0)),
        out_shape=jax.ShapeDtypeStruct((n, d), F32),
        scratch_shapes=[stack, pltpu.VMEM((tb, slots, LANES), F32),
                        pltpu.VMEM((tb * ROW_TILE, LANES), F32)],
        compiler_params=_cparams(("arbitrary",)),
        name="peer_expert_mix",
    )(off, wgt, tab, x8, g8)


def _state_to_blocks(s):
    b, h, d, _ = s.shape
    st = jnp.swapaxes(s, -1, -2).reshape(b, h // HEADS_PER_LANE_TILE, HEADS_PER_LANE_TILE, d, d)
    eye = jnp.eye(HEADS_PER_LANE_TILE, dtype=s.dtype)
    blk = st[:, :, :, :, None, :] * eye[None, None, :, None, :, None]
    return blk.reshape(b, h // HEADS_PER_LANE_TILE, LANES, LANES)


def _blocks_to_state(blk, heads):
    b, npair = blk.shape[:2]
    x = blk.reshape(b, npair, HEADS_PER_LANE_TILE, HEAD_DIM, HEADS_PER_LANE_TILE, HEAD_DIM)
    diag = jnp.stack([x[:, :, hh, :, hh, :] for hh in range(HEADS_PER_LANE_TILE)], axis=2)
    return jnp.swapaxes(diag.reshape(b, heads, HEAD_DIM, HEAD_DIM), -1, -2)


PEER_TOPK = 16


def _layer(x, k_past, v_past, lf_past, s0, shift0, lp, g_final, final_norm):
    (norm_mix_g, w_in, fox_b_f, mu, w0, w2, a0, a2, g2, k_k, k_a, r_k, lnx_w, lnx_b, w_out,
     norm_ffn_g, peer_w_q, peer_sub_keys, tab_u, tab_v) = lp
    b, t, d = x.shape
    n = b * t
    fox_heads = fox_b_f.shape[0]
    fw = fox_heads * HEAD_DIM
    fox_cols = 3 * fw + fox_heads
    rwkv_heads = r_k.shape[0]
    w = rwkv_heads * HEAD_DIM
    x2d = x.reshape(n, d)
    qb, k, v, kb, vb, lf, rw_main, rw_tail = _inproj(x2d, norm_mix_g, w_in, fox_b_f, fox_cols,
                                                     fox_heads, 3 * w)
    fox = _fox_stream(qb, kb, vb, lf, k_past, v_past, lf_past, batch=b, q_len=t)

    prm = _rwkv_params(mu, w0, w2, a0, a2, g2, k_k, k_a, r_k.reshape(-1))
    tail = prm["tail"]
    shift_main = shift0[..., :3 * w]
    shift_tail = jnp.pad(shift0[..., 3 * w:], ((0, 0), (0, 0), (0, TAIL_PAD - tail)))
    r, lw, km, vv, kn, bb, g, bonus = _rwkv_pre(rw_main, rw_tail, shift_main, shift_tail, prm,
                                                batch=b, seq=t)
    y, s_blk = _rwkv_scan(r, lw, km, vv, kn, bb, _state_to_blocks(s0), batch=b, seq=t)
    s_t = _blocks_to_state(s_blk, rwkv_heads)
    last = jnp.concatenate([rw_main.reshape(b, t, -1)[:, -1:], rw_tail.reshape(b, t, -1)[:, -1:, :tail]],
                           axis=-1)

    x2, xn, scores = _outproj(x2d, fox, y, bonus, g, lnx_w, lnx_b, w_out, norm_ffn_g, peer_w_q,
                              peer_sub_keys)
    idx, gate = _retrieve(scores, PEER_TOPK)
    tb = _row_tile(n, 128)
    slots = gate.shape[1]
    off = idx.reshape(n * slots)
    wgt = _peer_act(off, xn, tab_u, gate, tb=tb)
    out = _peer_mix(off, wgt, tab_v, x2, g_final, tb=tb, final_norm=final_norm)
    return (out.reshape(b, t, d), k.reshape(b, t, fox_heads, HEAD_DIM),
            v.reshape(b, t, fox_heads, HEAD_DIM), lf.reshape(b, t, fox_heads), s_t, last)


def kernel(x_prompt, x_sample, cache_fox_k, cache_fox_v, cache_fox_logf, state_rwkv, state_shift,
           norm_mix_g, w_in, fox_b_f, rwkv_mu, rwkv_w0, rwkv_w2, rwkv_a0, rwkv_a2, rwkv_g2,
           rwkv_k_k, rwkv_k_a, rwkv_r_k, rwkv_lnx_w, rwkv_lnx_b, w_out, norm_ffn_g,
           peer_w_q, peer_sub_keys, peer_u, peer_v, norm_final_g):
    depth = w_in.shape[0]
    yp, ys = x_prompt, x_sample
    bp = x_prompt.shape[0]
    dt = x_prompt.dtype
    fox_heads = fox_b_f.shape[1]
    rwkv_heads = rwkv_r_k.shape[1]
    rwkv_cols = rwkv_mu.shape[1]
    outs_p, outs_s = [], []
    for l in range(depth):
        lp = (norm_mix_g[l], w_in[l], fox_b_f[l], rwkv_mu[l], rwkv_w0[l], rwkv_w2[l], rwkv_a0[l],
              rwkv_a2[l], rwkv_g2[l], rwkv_k_k[l], rwkv_k_a[l], rwkv_r_k[l], rwkv_lnx_w[l],
              rwkv_lnx_b[l], w_out[l], norm_ffn_g[l], peer_w_q[l], peer_sub_keys[l],
              _pack_table(peer_u[l]), _pack_table(peer_v[l]))
        last = l == depth - 1
        empty_kv = jnp.zeros((bp, 0, fox_heads, HEAD_DIM), dt)
        empty_lf = jnp.zeros((bp, 0, fox_heads), dt)
        s_zero = jnp.zeros((bp, rwkv_heads, HEAD_DIM, HEAD_DIM), dt)
        sh_zero = jnp.zeros((bp, 1, rwkv_cols), dt)
        yp, *rest_p = _layer(yp, empty_kv, empty_kv, empty_lf, s_zero, sh_zero, lp, norm_final_g, last)
        ys, *rest_s = _layer(ys, cache_fox_k[l], cache_fox_v[l], cache_fox_logf[l], state_rwkv[l],
                             state_shift[l], lp, norm_final_g, last)
        outs_p.append(rest_p)
        outs_s.append(rest_s)
    stack = lambda outs, i: jnp.stack([o[i] for o in outs])
    return ((yp, ys) + tuple(stack(outs_p, i) for i in range(5))
            + tuple(stack(outs_s, i) for i in range(5)))
```

```python
import functools
import math

import jax
import jax.numpy as jnp
from jax import lax
from jax.experimental import pallas as pl
from jax.experimental.pallas import tpu as pltpu

F32 = jnp.float32
BF16 = jnp.bfloat16

HEAD_DIM = 64
LANES = 128
HEADS_PER_LANE_TILE = LANES // HEAD_DIM
TAIL_PAD = 2 * LANES
CHUNK = 64
RWKV_PASSES = 1
NORM_EPS = 1e-6
LNX_EPS = 64e-5
NEG_BIG = -1e30
LOG2E = math.log2(math.e)
HIGHEST = lax.Precision.HIGHEST
VMEM_LIMIT = 48 * 1024 * 1024


def _cparams(sem):
    return pltpu.CompilerParams(dimension_semantics=sem, vmem_limit_bytes=VMEM_LIMIT)


def _row_tile(n, target):
    t = min(n, target)
    assert n % t == 0, (n, t)
    return t


def _inproj_kernel(x_ref, g_ref, wqkv_ref, wkvt_ref, wf_ref, wrw_ref, wtail_ref, bf_ref,
                   q_ref, kt_ref, vt_ref, kb_ref, vb_ref, lf_ref, rw_ref, tail_ref):
    x = x_ref[...]
    h = x * lax.rsqrt(jnp.mean(x * x, axis=-1, keepdims=True) + NORM_EPS) * g_ref[...]
    hb = h.astype(BF16)
    fw = wqkv_ref.shape[1] // 3
    qkv = jnp.dot(hb, wqkv_ref[...], preferred_element_type=F32)
    q_ref[...] = (qkv[:, :fw] * (LOG2E / math.sqrt(HEAD_DIM))).astype(BF16)
    kb_ref[...] = qkv[:, fw:2 * fw].astype(BF16)
    vb_ref[...] = qkv[:, 2 * fw:].astype(BF16)
    kvt = lax.dot_general(wkvt_ref[...], hb, (((1,), (1,)), ((), ())), preferred_element_type=F32)
    kt_ref[...] = kvt[:fw]
    vt_ref[...] = kvt[fw:]
    f = jnp.dot(hb, wf_ref[...], preferred_element_type=F32) + bf_ref[...]
    lf_ref[...] = jax.nn.log_sigmoid(f)
    rw_ref[...] = jnp.dot(hb, wrw_ref[...], preferred_element_type=F32)
    tail_ref[...] = jnp.dot(hb, wtail_ref[...], preferred_element_type=F32)


def _inproj(x2d, g, w_in, b_f, fox_cols, fox_heads, rw_main, *, batch, seq):
    n, d = x2d.shape
    fw = fox_heads * HEAD_DIM
    wqkv = w_in[:, :3 * fw].astype(BF16)
    wkvt = w_in[:, fw:3 * fw].T.astype(BF16)
    wf = w_in[:, 3 * fw:fox_cols].astype(BF16)
    wrw = w_in[:, fox_cols:fox_cols + rw_main].astype(BF16)
    wtail = w_in[:, fox_cols + rw_main:].astype(BF16)
    wtail = jnp.pad(wtail, ((0, 0), (0, TAIL_PAD - wtail.shape[1])))
    tm = _row_tile(seq, 512)
    nt = seq // tm
    row = lambda c: pl.BlockSpec((tm, c), lambda b, i: (b * nt + i, 0))
    col = pl.BlockSpec((None, fw, tm), lambda b, i: (b, 0, i))
    full = lambda a: pl.BlockSpec(a.shape, lambda b, i: (0,) * a.ndim)
    g2 = g.reshape(1, d)
    bf2 = b_f.reshape(1, fox_heads)
    outs = (
        jax.ShapeDtypeStruct((n, fw), BF16),
        jax.ShapeDtypeStruct((batch, fw, seq), F32),
        jax.ShapeDtypeStruct((batch, fw, seq), F32),
        jax.ShapeDtypeStruct((n, fw), BF16),
        jax.ShapeDtypeStruct((n, fw), BF16),
        jax.ShapeDtypeStruct((n, fox_heads), F32),
        jax.ShapeDtypeStruct((n, rw_main), F32),
        jax.ShapeDtypeStruct((n, TAIL_PAD), F32),
    )
    return pl.pallas_call(
        _inproj_kernel,
        grid=(batch, nt),
        in_specs=[row(d), full(g2), full(wqkv), full(wkvt), full(wf), full(wrw), full(wtail),
                  full(bf2)],
        out_specs=[row(fw), col, col, row(fw), row(fw), row(fox_heads), row(rw_main),
                   row(TAIL_PAD)],
        out_shape=outs,
        compiler_params=_cparams(("parallel", "parallel")),
        name="inproj",
    )(x2d, g2, wqkv, wkvt, wf, wrw, wtail, bf2)


def _cumsum_kernel(lf_ref, c_ref, carry):
    @pl.when(pl.program_id(1) == 0)
    def _():
        carry[...] = jnp.zeros_like(carry)

    lf = lf_ref[...]
    tc = lf.shape[0]
    r = lax.broadcasted_iota(jnp.int32, (tc, tc), 0)
    c = lax.broadcasted_iota(jnp.int32, (tc, tc), 1)
    lower = (c <= r).astype(F32)
    cc = jnp.dot(lower, lf, precision=HIGHEST, preferred_element_type=F32) + carry[...]
    c_ref[...] = cc
    carry[...] = cc[tc - 1:tc, :]


def _cumsum(lf, tc):
    b, l, nh = lf.shape
    assert l % tc == 0
    return pl.pallas_call(
        _cumsum_kernel,
        grid=(b, l // tc),
        in_specs=[pl.BlockSpec((None, tc, nh), lambda i, j: (i, j, 0))],
        out_specs=pl.BlockSpec((None, tc, nh), lambda i, j: (i, j, 0)),
        out_shape=jax.ShapeDtypeStruct((b, l, nh), F32),
        scratch_shapes=[pltpu.VMEM((1, nh), F32)],
        compiler_params=_cparams(("parallel", "arbitrary")),
        name="cumsum_logf",
    )(lf)


def _split3(x):
    hi = x.astype(BF16)
    r = x - hi.astype(F32)
    mid = r.astype(BF16)
    lo = (r - mid.astype(F32)).astype(BF16)
    return hi.astype(F32), mid.astype(F32), lo.astype(F32)


def _augment_kernel(x_ref, c_ref, o_ref, *, role):
    tm = x_ref.shape[0]
    lane = lax.broadcasted_iota(jnp.int32, (tm, LANES), 1)
    for p in range(x_ref.shape[1] // LANES):
        xp = x_ref[:, p * LANES:(p + 1) * LANES].astype(F32)
        for hh in range(HEADS_PER_LANE_TILE):
            h = p * HEADS_PER_LANE_TILE + hh
            own = (lane >= hh * HEAD_DIM) & (lane < (hh + 1) * HEAD_DIM)
            e = (lane + (1 - hh) * HEAD_DIM) % LANES
            if role == "v":
                ext = jnp.where(e == 0, 1.0, 0.0)
            else:
                c = jnp.broadcast_to(c_ref[:, h:h + 1], (tm, LANES))
                hi, mid, lo = _split3(c * LOG2E)
                sgn = 1.0 if role == "q" else -1.0
                base = 0 if role == "q" else 3
                ext = jnp.where(e == base, sgn * hi,
                                jnp.where(e == base + 1, sgn * mid,
                                          jnp.where(e == base + 2, sgn * lo,
                                                    jnp.where(e < 6, 1.0, 0.0))))
            o_ref[:, h * LANES:(h + 1) * LANES] = jnp.where(own, xp, ext).astype(BF16)


def _augment(x, c, role):
    n, w = x.shape
    nh = w // HEAD_DIM
    tm = _row_tile(n, 512) if n % 512 == 0 else n
    kern = functools.partial(_augment_kernel, role=role)
    return pl.pallas_call(
        kern,
        grid=(n // tm,),
        in_specs=[pl.BlockSpec((tm, w), lambda i: (i, 0)),
                  pl.BlockSpec((tm, nh), lambda i: (i, 0))],
        out_specs=pl.BlockSpec((tm, nh * LANES), lambda i: (i, 0)),
        out_shape=jax.ShapeDtypeStruct((n, nh * LANES), BF16),
        compiler_params=_cparams(("parallel",)),
        name="fox_augment_" + role,
    )(x, c)


Q_SUB = 128
K_SUB = 256


def _fox_kernel(qblk_ref, kblk_ref, last_ref, q_ref, k_ref, v_ref, o_ref, m_sc, acc_sc,
                *, q_off, tq, tk):
    t = pl.program_id(2)
    i = qblk_ref[t]
    j = kblk_ref[t]
    qs_n, ks_n = min(Q_SUB, tq), min(K_SUB, tk)

    @pl.when(j == 0)
    def _():
        m_sc[...] = jnp.full_like(m_sc, NEG_BIG)
        acc_sc[...] = jnp.zeros_like(acc_sc)

    q_lo = q_off + i * tq
    k_lo = j * tk

    def body(masked):
        if masked:
            diff = (lax.broadcasted_iota(jnp.int32, (qs_n, LANES), 1)
                    - lax.broadcasted_iota(jnp.int32, (qs_n, LANES), 0))
        nqs = tq // qs_n
        qrow = [slice(qs * qs_n, (qs + 1) * qs_n) for qs in range(nqs)]
        m_run = [[m_sc[hh, qrow[qs], :] for qs in range(nqs)] for hh in range(HEADS_PER_LANE_TILE)]
        a_run = [[acc_sc[hh, qrow[qs], :] for qs in range(nqs)] for hh in range(HEADS_PER_LANE_TILE)]
        for hh in range(HEADS_PER_LANE_TILE):
            cols = slice(hh * LANES, (hh + 1) * LANES)
            for ks in range(tk // ks_n):
                krows = slice(ks * ks_n, (ks + 1) * ks_n)
                k_sub = k_ref[krows, cols]
                v_sub = v_ref[krows, cols]
                for qs in range(nqs):
                    qrows = qrow[qs]
                    s = lax.dot_general(q_ref[qrows, cols], k_sub, (((1,), (1,)), ((), ())),
                                        preferred_element_type=F32)
                    parts = [s[:, c * LANES:(c + 1) * LANES] for c in range(ks_n // LANES)]
                    if masked:
                        parts = [jnp.where(diff <= q_lo - k_lo + qs * qs_n - ks * ks_n - c * LANES,
                                           pc, NEG_BIG) for c, pc in enumerate(parts)]
                    mx = parts[0]
                    for pc in parts[1:]:
                        mx = jnp.maximum(mx, pc)
                    m_old = m_run[hh][qs]
                    m_new = jnp.maximum(m_old, jnp.max(mx, axis=-1, keepdims=True))
                    alpha = jnp.exp2(m_old - m_new)
                    pr = jnp.concatenate([jnp.exp2(pc - m_new).astype(BF16) for pc in parts], axis=1)
                    pv = jnp.dot(pr, v_sub, preferred_element_type=F32)
                    a_run[hh][qs] = alpha * a_run[hh][qs] + pv
                    m_run[hh][qs] = m_new
        for hh in range(HEADS_PER_LANE_TILE):
            for qs in range(nqs):
                m_sc[hh, qrow[qs], :] = m_run[hh][qs]
                acc_sc[hh, qrow[qs], :] = a_run[hh][qs]

    fully_visible = k_lo + tk - 1 <= q_lo

    @pl.when(fully_visible)
    def _():
        body(False)

    @pl.when(jnp.logical_not(fully_visible))
    def _():
        body(True)

    @pl.when(last_ref[t] == 1)
    def _():
        lane = lax.broadcasted_iota(jnp.int32, (1, LANES), 1)
        out = jnp.zeros((tq, LANES), F32)
        for hh in range(HEADS_PER_LANE_TILE):
            in_head = (lane >= hh * HEAD_DIM) & (lane < (hh + 1) * HEAD_DIM)
            acc = acc_sc[hh]
            ones_col = (1 - hh) * HEAD_DIM
            denom = jnp.broadcast_to(acc[:, ones_col:ones_col + 1], acc.shape)
            out = jnp.where(in_head, acc / denom, out)
        o_ref[...] = out


def _fox_attend(q_aug, k_aug, v_aug, *, batch, q_len, kv_len, q_off, tq, tk):
    n, wa = q_aug.shape
    pair_w = HEADS_PER_LANE_TILE * LANES
    npair = wa // pair_w
    nq, nk = q_len // tq, kv_len // tk
    assert q_len % tq == 0 and kv_len % tk == 0

    pairs = [(i, j) for i in range(nq) for j in range(min(nk, (q_off + (i + 1) * tq - 1) // tk + 1))]
    qblk = jnp.asarray([i for i, _ in pairs], jnp.int32)
    kblk = jnp.asarray([j for _, j in pairs], jnp.int32)
    last = jnp.asarray([int(t + 1 == len(pairs) or pairs[t + 1][0] != i)
                        for t, (i, _) in enumerate(pairs)], jnp.int32)
    q_map = lambda b, p, t, qb, kb, lt: (b * nq + qb[t], p)
    kv_map = lambda b, p, t, qb, kb, lt: (b * nk + kb[t], p)
    kern = functools.partial(_fox_kernel, q_off=q_off, tq=tq, tk=tk)
    return pl.pallas_call(
        kern,
        grid_spec=pltpu.PrefetchScalarGridSpec(
            num_scalar_prefetch=3,
            grid=(batch, npair, len(pairs)),
            in_specs=[pl.BlockSpec((tq, pair_w), q_map),
                      pl.BlockSpec((tk, pair_w), kv_map),
                      pl.BlockSpec((tk, pair_w), kv_map)],
            out_specs=pl.BlockSpec((tq, LANES), q_map),
            scratch_shapes=[pltpu.VMEM((HEADS_PER_LANE_TILE, tq, LANES), F32),
                            pltpu.VMEM((HEADS_PER_LANE_TILE, tq, LANES), F32)]),
        out_shape=jax.ShapeDtypeStruct((n, npair * LANES), F32),
        compiler_params=_cparams(("parallel", "parallel", "arbitrary")),
        name="fox_attention",
    )(qblk, kblk, last, q_aug, k_aug, v_aug)


def _fox_stream(qb, kb, vb, lf, k_past, v_past, lf_past, *, batch, q_len):
    n, w = qb.shape
    nh = lf.shape[1]
    past = k_past.shape[1]
    lf_new = lf.reshape(batch, q_len, nh)
    if past == 0:
        kv_len = q_len
        k_all, v_all, lf_all = kb, vb, lf_new
        tq = tk = _row_tile(q_len, 512)
        tc = tk
    else:
        kv_len = -(-(past + q_len) // K_SUB) * K_SUB
        pad = kv_len - past - q_len

        def cat(old, new):
            old = old.reshape(batch, past, -1).astype(new.dtype)
            new = new.reshape(batch, q_len, -1)
            z = jnp.zeros((batch, pad, new.shape[-1]), new.dtype)
            return jnp.concatenate([old, new, z], axis=1)

        k_all = cat(k_past, kb).reshape(batch * kv_len, w)
        v_all = cat(v_past, vb).reshape(batch * kv_len, w)
        lf_all = cat(lf_past, lf_new)
        tc = max(t for t in range(LANES, 1024 + 1, LANES) if kv_len % t == 0)
        tq, tk = q_len, kv_len
    c = _cumsum(lf_all, tc)
    c_k = c.reshape(batch * kv_len, nh)
    c_q = c[:, past:past + q_len].reshape(n, nh)
    return _fox_attend(_augment(qb, c_q, "q"), _augment(k_all, c_k, "k"), _augment(v_all, c_k, "v"),
                       batch=batch, q_len=q_len, kv_len=kv_len, q_off=past, tq=tq, tk=tk)


def _head_sum_matrix(width):
    r = lax.broadcasted_iota(jnp.int32, (width, width), 0) // HEAD_DIM
    c = lax.broadcasted_iota(jnp.int32, (width, width), 1) // HEAD_DIM
    return (r == c).astype(BF16)


def _rwkv_pre_kernel(pm_ref, pt_ref, sm_ref, st_ref, mum_ref, mut_ref, wbig_ref, w0_ref, a0_ref,
                     kk_ref, ka_ref, rk_ref,
                     r_out, lw_out, km_out, v_out, kn_out, b_out, g_out, bonus_out,
                     carry_m, carry_t, *, lora_w, lora_a):
    @pl.when(pl.program_id(1) == 0)
    def _():
        carry_m[...] = sm_ref[...]
        carry_t[...] = st_ref[...]

    pm = pm_ref[...]
    pt = pt_ref[...]
    tm = pm.shape[0]
    w = pm.shape[1] // 3

    def shifted(p, carry):
        row = lax.broadcasted_iota(jnp.int32, p.shape, 0)
        return jnp.where(row == 0, carry[...], pltpu.roll(p, 1, 0))

    prev_m = shifted(pm, carry_m)
    prev_t = shifted(pt, carry_t)
    carry_m[...] = pm[tm - 1:tm, :]
    carry_t[...] = pt[tm - 1:tm, :]
    psm = pm + mum_ref[...] * (prev_m - pm)
    pst = pt + mut_ref[...] * (prev_t - pt)
    r = psm[:, :w]
    k = psm[:, w:2 * w]
    v = psm[:, 2 * w:]
    lane = lax.broadcasted_iota(jnp.int32, pst.shape, 1)
    z = jnp.where(lane < lora_w, jnp.tanh(pst),
                  jnp.where(lane < lora_w + lora_a, pst, jax.nn.sigmoid(pst)))
    lo = jnp.dot(z.astype(BF16), wbig_ref[...], preferred_element_type=F32)
    w_log = -jax.nn.softplus(-(w0_ref[...] + lo[:, :w])) - 0.5
    lw = -jnp.exp(w_log)
    a = jax.nn.sigmoid(a0_ref[...] + lo[:, w:2 * w])
    g = lo[:, 2 * w:]
    e = _head_sum_matrix(w)
    kk0 = k * kk_ref[...]
    n2 = _dot3(kk0 * kk0, e)
    kn = kk0 / jnp.maximum(jnp.sqrt(n2), 1e-12)
    km = k * (1.0 + (a - 1.0) * ka_ref[...])
    rk = _dot3(r * km * rk_ref[...], e)
    r_out[...] = r
    lw_out[...] = lw
    km_out[...] = km
    v_out[...] = v
    kn_out[...] = kn
    b_out[...] = kn * a
    g_out[...] = g
    bonus_out[...] = rk * v


def _rwkv_params(mu, w0, w2, a0, a2, g2, k_k, k_a, r_k):
    w = w0.shape[0]
    lora_w, lora_a, lora_g = w2.shape[0], a2.shape[0], g2.shape[0]
    w_lora = jnp.zeros((TAIL_PAD, 3 * w), F32)
    w_lora = w_lora.at[:lora_w, :w].set(w2)
    w_lora = w_lora.at[lora_w:lora_w + lora_a, w:2 * w].set(a2)
    w_lora = w_lora.at[lora_w + lora_a:lora_w + lora_a + lora_g, 2 * w:].set(g2)
    tail = mu.shape[0] - 3 * w
    return dict(
        mu_main=mu[:3 * w].reshape(1, 3 * w),
        mu_tail=jnp.pad(mu[3 * w:], (0, TAIL_PAD - tail)).reshape(1, TAIL_PAD),
        w_lora=w_lora.astype(BF16), w0=w0.reshape(1, w), a0=a0.reshape(1, w),
        k_k=k_k.reshape(1, w), k_a=k_a.reshape(1, w), r_k=r_k.reshape(1, w),
        lora_w=lora_w, lora_a=lora_a, tail=tail)


def _rwkv_pre(rw_main, rw_tail, shift_main, shift_tail, prm, *, batch, seq):
    n, w3 = rw_main.shape
    w = w3 // 3
    tm = _row_tile(seq, 512)
    nt = seq // tm
    row = lambda c: pl.BlockSpec((tm, c), lambda b, i: (b * nt + i, 0))
    per_b = lambda c: pl.BlockSpec((None, 1, c), lambda b, i: (b, 0, 0))
    full = lambda a: pl.BlockSpec(a.shape, lambda b, i: (0,) * a.ndim)
    consts = [prm["mu_main"], prm["mu_tail"], prm["w_lora"], prm["w0"], prm["a0"], prm["k_k"],
              prm["k_a"], prm["r_k"]]
    kern = functools.partial(_rwkv_pre_kernel, lora_w=prm["lora_w"], lora_a=prm["lora_a"])
    return pl.pallas_call(
        kern,
        grid=(batch, nt),
        in_specs=[row(w3), row(TAIL_PAD), per_b(w3), per_b(TAIL_PAD)] + [full(c) for c in consts],
        out_specs=[row(w)] * 8,
        out_shape=[jax.ShapeDtypeStruct((n, w), F32)] * 8,
        scratch_shapes=[pltpu.VMEM((1, w3), F32), pltpu.VMEM((1, TAIL_PAD), F32)],
        compiler_params=_cparams(("parallel", "arbitrary")),
        name="rwkv_pre",
    )(rw_main, rw_tail, shift_main, shift_tail, *consts)


def _bmm(a, b, kind, passes):
    contract = {"nn": ((2,), (1,)), "nt": ((2,), (2,)), "tn": ((1,), (1,))}[kind]
    dims = (contract, ((0,), (0,)))
    if passes == 6:
        return lax.dot_general(a, b, dims, precision=HIGHEST, preferred_element_type=F32)
    dg = lambda x, y: lax.dot_general(x, y, dims, preferred_element_type=F32)
    ah, bh = a.astype(BF16), b.astype(BF16)
    out = dg(ah, bh)
    if passes == 3:
        al = (a - ah.astype(F32)).astype(BF16)
        bl = (b - bh.astype(F32)).astype(BF16)
        out = out + dg(ah, bl) + dg(al, bh)
    return out


def _rwkv_chunk(r, lw, km, v, kn, bb, s_blk, passes):
    g, c, _ = r.shape
    c2 = HEADS_PER_LANE_TILE * c
    ti = lax.broadcasted_iota(jnp.int32, (g, c, c), 1)
    si = lax.broadcasted_iota(jnp.int32, (g, c, c), 2)
    cs = _bmm((si <= ti).astype(F32), lw, "nn", 6)
    e_pos = jnp.exp(cs)
    e_neg = jnp.exp(-cs)
    kt = kn * jnp.exp(cs - lw)
    bt = bb * e_neg
    kh = km * e_neg
    rt = r * e_pos
    g_end = e_pos[:, c - 1:c, :]

    lane = lax.broadcasted_iota(jnp.int32, (1, 1, LANES), 2)
    head_of_lane = lane // HEAD_DIM

    def stack_masked(x):
        return jnp.concatenate(
            [jnp.where(head_of_lane == hh, x, 0.0) for hh in range(HEADS_PER_LANE_TILE)], axis=1)

    def stack(x):
        return jnp.concatenate([x] * HEADS_PER_LANE_TILE, axis=1)

    def pick(x):
        out = x[:, :c]
        for hh in range(1, HEADS_PER_LANE_TILE):
            out = jnp.where(head_of_lane == hh, x[:, hh * c:(hh + 1) * c], out)
        return out

    kt2 = stack_masked(kt)
    rt2 = stack_masked(rt)
    rr = lax.broadcasted_iota(jnp.int32, (1, c2, c2), 1)
    cc = lax.broadcasted_iota(jnp.int32, (1, c2, c2), 2)
    strict_blk = (rr // c == cc // c) & (cc < rr)
    x = jnp.where(strict_blk, -_bmm(kt2, stack(bt), "nt", passes), 0.0)
    tinv = (rr == cc).astype(F32) + x
    steps = max(int(math.ceil(math.log2(c))) - 1, 0)
    for _ in range(steps):
        x = _bmm(x, x, "nn", passes)
        tinv = tinv + _bmm(tinv, x, "nn", passes)
    tr = lax.broadcasted_iota(jnp.int32, (1, c2, c), 1) % c
    sr = lax.broadcasted_iota(jnp.int32, (1, c2, c), 2)
    kk_s = jnp.where(sr < tr, _bmm(kt2, kh, "nt", passes), 0.0)
    rb_s = jnp.where(sr <= tr, _bmm(rt2, bt, "nt", passes), 0.0)
    rk_s = jnp.where(sr <= tr, _bmm(rt2, kh, "nt", passes), 0.0)

    ks = _bmm(jnp.concatenate([kt, rt], axis=1), s_blk, "nn", passes)
    rhs = ks[:, :c] + pick(_bmm(kk_s, v, "nn", passes))
    z = pick(_bmm(tinv, stack(rhs), "nn", passes))
    y = ks[:, c:] - pick(_bmm(rb_s, z, "nn", passes)) + pick(_bmm(rk_s, v, "nn", passes))
    jr = lax.broadcasted_iota(jnp.int32, (1, LANES, LANES), 1)
    ic = lax.broadcasted_iota(jnp.int32, (1, LANES, LANES), 2)
    decay_rows = jnp.swapaxes(jnp.broadcast_to(g_end, (g, LANES, LANES)), 1, 2)
    upd = _bmm(jnp.concatenate([bt * g_end, kh * g_end], axis=1),
               jnp.concatenate([-z, v], axis=1), "tn", passes)
    s_new = decay_rows * s_blk + jnp.where(jr // HEAD_DIM == ic // HEAD_DIM, upd, 0.0)
    return y, s_new


def _rwkv_scan_kernel(r_ref, lw_ref, km_ref, v_ref, kn_ref, b_ref, s0_ref, y_ref, sT_ref, s_sc,
                      *, chunk, passes):
    it = pl.program_id(0)
    nb, tb, w = r_ref.shape
    npair = w // LANES

    @pl.when(it == 0)
    def _():
        s_sc[...] = s0_ref[...]

    def step(ci, carry):
        rows = pl.ds(pl.multiple_of(ci * chunk, chunk), chunk)

        def gather(ref):
            blk = ref[:, rows, :]
            return jnp.concatenate([blk[:, :, p * LANES:(p + 1) * LANES] for p in range(npair)],
                                   axis=0)

        y, s_new = _rwkv_chunk(gather(r_ref), gather(lw_ref), gather(km_ref), gather(v_ref),
                               gather(kn_ref), gather(b_ref), s_sc[...], passes)
        for p in range(npair):
            y_ref[:, rows, p * LANES:(p + 1) * LANES] = y[p * nb:(p + 1) * nb]
        s_sc[...] = s_new
        return carry

    lax.fori_loop(0, tb // chunk, step, 0)

    @pl.when(it == pl.num_programs(0) - 1)
    def _():
        sT_ref[...] = s_sc[...]


def _rwkv_scan(r, lw, km, v, kn, bb, s0_blk, *, batch, seq, passes=RWKV_PASSES):
    n, w = r.shape
    npair = w // LANES
    chunk = min(CHUNK, seq)
    tb = _row_tile(seq, 4 * chunk)
    row = pl.BlockSpec((batch, tb, w), lambda i: (0, i, 0))
    st = pl.BlockSpec((npair * batch, LANES, LANES), lambda i: (0, 0, 0))
    s0 = jnp.swapaxes(s0_blk, 0, 1).reshape(npair * batch, LANES, LANES)
    kern = functools.partial(_rwkv_scan_kernel, chunk=chunk, passes=passes)
    y, s_t = pl.pallas_call(
        kern,
        grid=(seq // tb,),
        in_specs=[row] * 6 + [st],
        out_specs=[row, st],
        out_shape=[jax.ShapeDtypeStruct((batch, seq, w), F32),
                   jax.ShapeDtypeStruct((npair * batch, LANES, LANES), F32)],
        scratch_shapes=[pltpu.VMEM((npair * batch, LANES, LANES), F32)],
        compiler_params=_cparams(("arbitrary",)),
        name="rwkv_scan",
    )(*(a.reshape(batch, seq, w) for a in (r, lw, km, v, kn, bb)), s0)
    s_t = jnp.swapaxes(s_t.reshape(npair, batch, LANES, LANES), 0, 1)
    return y.reshape(n, w), s_t


def _outproj_kernel(x_ref, fox_ref, y_ref, bonus_ref, g_ref, lnw_ref, lnb_ref, wa_ref, wb_ref,
                    gf_ref, wqt_ref, keys_ref, x2_ref, xn_ref, sc_ref):
    y = y_ref[...]
    w = y.shape[1]
    e = _head_sum_matrix(w)
    mean = _dot3(y, e) * (1.0 / HEAD_DIM)
    d = y - mean
    var = _dot3(d * d, e) * (1.0 / HEAD_DIM)
    yn = d * lax.rsqrt(var + LNX_EPS) * lnw_ref[...] + lnb_ref[...]
    rw = (yn + bonus_ref[...]) * g_ref[...]
    mix = (jnp.dot(fox_ref[...].astype(BF16), wa_ref[...], preferred_element_type=F32)
           + jnp.dot(rw.astype(BF16), wb_ref[...], preferred_element_type=F32))
    x2 = x_ref[...] + mix
    xn = x2 * lax.rsqrt(jnp.mean(x2 * x2, axis=-1, keepdims=True) + NORM_EPS) * gf_ref[...]
    x2_ref[...] = x2.reshape(x2_ref.shape)
    xn_ref[...] = xn.reshape(xn_ref.shape)
    qt = lax.dot_general(wqt_ref[...], xn.astype(BF16), (((1,), (1,)), ((), ())),
                         preferred_element_type=F32)
    qh = keys_ref.shape[2]
    for hc in range(keys_ref.shape[0]):
        sc_ref[hc] = jnp.dot(keys_ref[hc], qt[hc * qh:(hc + 1) * qh, :].astype(BF16),
                             preferred_element_type=F32)


def _outproj(x2d, fox, y, bonus, g, lnx_w, lnx_b, w_out, g_ffn, w_q, sub_keys):
    n, d = x2d.shape
    w = y.shape[1]
    fw = fox.shape[1]
    wa = w_out[:fw].astype(BF16)
    wb = w_out[fw:].astype(BF16)
    wqt = w_q.T.astype(BF16)
    nkeys, qh = sub_keys.shape[-2:]
    keys = sub_keys.reshape(-1, nkeys, qh).astype(BF16)
    nhc = keys.shape[0]
    tm = _row_tile(n, 512)
    row = lambda c: pl.BlockSpec((tm, c), lambda i: (i, 0))
    tiles = pl.BlockSpec((tm * d // LANES, LANES), lambda i: (i, 0))
    full = lambda a: pl.BlockSpec(a.shape, lambda i: (0,) * a.ndim)
    consts = [lnx_w.reshape(1, w), lnx_b.reshape(1, w), wa, wb, g_ffn.reshape(1, d), wqt, keys]
    return pl.pallas_call(
        _outproj_kernel,
        grid=(n // tm,),
        in_specs=[row(d), row(fw), row(w), row(w), row(w)] + [full(c) for c in consts],
        out_specs=[tiles, tiles, pl.BlockSpec((nhc, nkeys, tm), lambda i: (0, 0, i))],
        out_shape=[jax.ShapeDtypeStruct((n * d // LANES, LANES), F32),
                   jax.ShapeDtypeStruct((n * d // LANES, LANES), F32),
                   jax.ShapeDtypeStruct((nhc, nkeys, n), F32)],
        compiler_params=_cparams(("parallel",)),
        name="outproj_scores",
    )(x2d, fox, y, bonus, g, *consts)


def _topk_rows(s, payload, k):
    rows = lax.broadcasted_iota(jnp.int32, s.shape, 0)
    nrow = s.shape[0]
    vals, idxs, pays = [], [], []
    for _ in range(k):
        m = jnp.max(s, axis=0, keepdims=True)
        idx = jnp.min(jnp.where(s == m, rows, nrow), axis=0, keepdims=True)
        hit = rows == idx
        vals.append(m)
        idxs.append(idx)
        if payload is not None:
            pays.append(jnp.max(jnp.where(hit, payload, -1), axis=0, keepdims=True))
        s = jnp.where(hit, -jnp.inf, s)
    return vals, idxs, pays


def _retrieve_kernel(sc_ref, idx_ref, gate_ref, *, topk, nkeys):
    nhead = sc_ref.shape[0] // 2
    idx_rows, gate_rows = [], []
    for h in range(nhead):
        v1, i1, _ = _topk_rows(sc_ref[2 * h], None, topk)
        v2, i2, _ = _topk_rows(sc_ref[2 * h + 1], None, topk)
        v2a = jnp.concatenate(v2, axis=0)
        i2a = jnp.concatenate(i2, axis=0)
        nb = [topk // (a + 1) for a in range(topk)]
        pad = -sum(nb) % 8
        cand = jnp.concatenate([v1[a] + v2a[:nb[a]] for a in range(topk)]
                               + [jnp.full((pad, v2a.shape[1]), -jnp.inf, F32)], axis=0)
        cidx = jnp.concatenate([i1[a] * nkeys + i2a[:nb[a]] for a in range(topk)]
                               + [jnp.full((pad, v2a.shape[1]), -1, jnp.int32)], axis=0)
        top, _, eidx = _topk_rows(cand, cidx, topk)
        top = jnp.concatenate(top, axis=0)
        ex = jnp.exp(top - top[0:1])
        gate_rows.append(ex / jnp.sum(ex, axis=0, keepdims=True))
        idx_rows.extend(eidx)
    idx_ref[...] = (jnp.concatenate(idx_rows, axis=0) * HALF_TILE).T
    gate_ref[...] = jnp.concatenate(gate_rows, axis=0).T


def _retrieve(scores, topk):
    nhc, nkeys, n = scores.shape
    slots = (nhc // 2) * topk
    tt = _row_tile(n, 256)
    kern = functools.partial(_retrieve_kernel, topk=topk, nkeys=nkeys)
    return pl.pallas_call(
        kern,
        grid=(n // tt,),
        in_specs=[pl.BlockSpec((nhc, nkeys, tt), lambda i: (0, 0, i))],
        out_specs=[pl.BlockSpec((tt, slots), lambda i: (i, 0))] * 2,
        out_shape=[jax.ShapeDtypeStruct((n, slots), jnp.int32),
                   jax.ShapeDtypeStruct((n, slots), F32)],
        compiler_params=_cparams(("parallel",)),
        name="peer_retrieve",
    )(scores)


ROW_TILE = 8
HALF_TILE = ROW_TILE // 2
TOKENS_PER_STEP = 2


def _pack_table(t):
    e, d = t.shape
    assert d == ROW_TILE * LANES
    bits = lax.bitcast_convert_type(t.astype(BF16), jnp.uint16).astype(jnp.uint32)
    bits = bits.reshape(e, 2, HALF_TILE, LANES)
    word = bits[:, 0] | (bits[:, 1] << 16)
    return lax.bitcast_convert_type(word, jnp.int32).reshape(e * HALF_TILE, LANES)


def _gather_rows(tab_ref, off_ref, base, stack_ref, slots, place=lambda j: j):
    tok = off_ref.at[pl.ds(base, slots)]
    for j in range(slots):
        off = pl.multiple_of(tok[j], HALF_TILE)
        p = place(j)
        stack_ref[p * HALF_TILE:(p + 1) * HALF_TILE, :] = tab_ref[pl.ds(off, HALF_TILE), :]


def _fold_pairs(xs, span):
    pos = lax.broadcasted_iota(jnp.int32, (ROW_TILE, LANES), 0)
    keep = (pos % (2 * span)) < span
    out = []
    for a, b in zip(xs[0::2], xs[1::2]):
        other = jnp.where(keep, b, a)
        swapped = jnp.where(keep, pltpu.roll(other, ROW_TILE - span, 0), pltpu.roll(other, span, 0))
        out.append(jnp.where(keep, a, b) + swapped)
    return out


_FOLD_ROW_OF_GROUP = (0, 4, 2, 6, 1, 5, 3, 7)


def _lane_sums_as_row(q):
    ones = jnp.ones((ROW_TILE, LANES), BF16)
    nt = lambda b: lax.dot_general(ones, b, (((1,), (1,)), ((), ())), preferred_element_type=F32)
    hi = q.astype(BF16)
    mid = (q - hi.astype(F32)).astype(BF16)
    return nt(hi) + nt(mid)


def _dot3(x, w01):
    hi = x.astype(BF16)
    r1 = x - hi.astype(F32)
    mid = r1.astype(BF16)
    lo = (r1 - mid.astype(F32)).astype(BF16)
    d = lambda a: jnp.dot(a, w01, preferred_element_type=F32)
    return d(hi) + d(mid) + d(lo)


def _peer_act_kernel(off_ref, x_ref, tab_ref, gate_ref, w_ref, stacks, part_ref, act_ref):
    tb, slots = gate_ref.shape
    place = lambda j: ROW_TILE * (j // ROW_TILE) + _FOLD_ROW_OF_GROUP[j % ROW_TILE]

    def one(t, stack_ref):
        _gather_rows(tab_ref, off_ref, t * slots, stack_ref, slots, place)
        x = x_ref[pl.ds(pl.multiple_of(t * ROW_TILE, ROW_TILE), ROW_TILE), :]
        x_lo = jnp.concatenate([x[:HALF_TILE]] * 2, axis=0)
        x_hi = jnp.concatenate([x[HALF_TILE:]] * 2, axis=0)
        for g in range(slots // ROW_TILE):
            prods = []
            for k in range(4 * g, 4 * g + 4):
                word = stack_ref[k * ROW_TILE:(k + 1) * ROW_TILE, :]
                prods.append(lax.bitcast_convert_type(jnp.left_shift(word, 16), F32) * x_lo
                             + lax.bitcast_convert_type(word & jnp.int32(-65536), F32) * x_hi)
            part_ref[t, g * ROW_TILE:(g + 1) * ROW_TILE, :] = _fold_pairs(_fold_pairs(prods, 2), 1)[0]

    def step(i, carry):
        for u in range(TOKENS_PER_STEP):
            one(TOKENS_PER_STEP * i + u, stacks.at[u])
        return carry

    lax.fori_loop(0, tb // TOKENS_PER_STEP, step, 0)

    for c in range(tb // ROW_TILE):
        rows = part_ref[c * ROW_TILE:(c + 1) * ROW_TILE].reshape(ROW_TILE * slots, LANES)
        sums = _lane_sums_as_row(rows)
        for u in range(ROW_TILE):
            act_ref[c * ROW_TILE + u:c * ROW_TILE + u + 1, :] = sums[:1, u * slots:(u + 1) * slots]
    act = act_ref[...]
    gelu = 0.5 * act * (1.0 + lax.erf(act * math.sqrt(0.5)))
    w_ref[...] = gate_ref[...] * gelu


def _peer_act(off, x8, tab, gate, *, tb):
    n, slots = gate.shape
    stack = pltpu.VMEM((TOKENS_PER_STEP, slots * HALF_TILE, LANES), jnp.int32)
    return pl.pallas_call(
        _peer_act_kernel,
        grid=(n // tb,),
        in_specs=[pl.BlockSpec((tb * slots,), lambda i: (i,), memory_space=pltpu.SMEM),
                  pl.BlockSpec((tb * ROW_TILE, LANES), lambda i: (i, 0)),
                  pl.BlockSpec(memory_space=pltpu.VMEM),
                  pl.BlockSpec((tb, slots), lambda i: (i, 0))],
        out_specs=pl.BlockSpec((tb, slots), lambda i: (i, 0)),
        out_shape=jax.ShapeDtypeStruct((n, slots), F32),
        scratch_shapes=[stack, pltpu.VMEM((tb, slots, LANES), F32),
                        pltpu.VMEM((tb, slots), F32)],
        compiler_params=_cparams(("arbitrary",)),
        name="peer_expert_act",
    )(off, x8, tab, gate)


def _peer_mix_kernel(off_ref, w_ref, tab_ref, x2_ref, gfin_ref, o_ref, stacks, wbc_all, x3_ref,
                     *, final_norm):
    tb, slots = w_ref.shape
    nacc = 4
    group = 16
    wt = w_ref[...].T
    w_hi = wt.astype(BF16)
    w_mid = (wt - w_hi.astype(F32)).astype(BF16)
    row_tok = lax.broadcasted_iota(jnp.int32, (tb, group * LANES), 0)
    col_tok = lax.broadcasted_iota(jnp.int32, (tb, group * LANES), 1) // LANES

    def broadcast_group(c, carry):
        onehot = (row_tok == col_tok + c * group).astype(BF16)
        res = sum(jnp.dot(term, onehot, preferred_element_type=F32) for term in (w_hi, w_mid))
        for tt in range(group):
            wbc_all[c * group + tt] = res[:, tt * LANES:(tt + 1) * LANES]
        return carry

    lax.fori_loop(0, tb // group, broadcast_group, 0)
    sub = lax.broadcasted_iota(jnp.int32, (ROW_TILE, LANES), 0)

    def one(t, stack_ref):
        _gather_rows(tab_ref, off_ref, t * slots, stack_ref, slots)
        wbc_ref = wbc_all.at[t]
        lo = [jnp.zeros((ROW_TILE, LANES), F32) for _ in range(nacc)]
        hi_acc = [jnp.zeros((ROW_TILE, LANES), F32) for _ in range(nacc)]
        for k in range(slots // 2):
            word = stack_ref[k * ROW_TILE:(k + 1) * ROW_TILE, :]
            wv = jnp.where(sub < HALF_TILE,
                           jnp.broadcast_to(wbc_ref[2 * k:2 * k + 1, :], (ROW_TILE, LANES)),
                           jnp.broadcast_to(wbc_ref[2 * k + 1:2 * k + 2, :], (ROW_TILE, LANES)))
            a = k % nacc
            lo[a] = lo[a] + lax.bitcast_convert_type(jnp.left_shift(word, 16), F32) * wv
            hi_acc[a] = hi_acc[a] + lax.bitcast_convert_type(word & jnp.int32(-65536), F32) * wv
        lo_sum = (lo[0] + lo[1]) + (lo[2] + lo[3])
        hi_sum = (hi_acc[0] + hi_acc[1]) + (hi_acc[2] + hi_acc[3])
        ff = jnp.concatenate([lo_sum[:HALF_TILE] + lo_sum[HALF_TILE:],
                              hi_sum[:HALF_TILE] + hi_sum[HALF_TILE:]], axis=0)
        rows = pl.ds(pl.multiple_of(t * ROW_TILE, ROW_TILE), ROW_TILE)
        x3_ref[rows, :] = x2_ref[rows, :] + ff

    def step(i, carry):
        for u in range(TOKENS_PER_STEP):
            one(TOKENS_PER_STEP * i + u, stacks.at[u])
        return carry

    lax.fori_loop(0, tb // TOKENS_PER_STEP, step, 0)
    x3 = x3_ref[...].reshape(tb, ROW_TILE, LANES)
    if final_norm:
        sq = jnp.sum(jnp.sum(x3 * x3, axis=2, keepdims=True), axis=1, keepdims=True)
        scale = lax.rsqrt(sq * (1.0 / (ROW_TILE * LANES)) + NORM_EPS)
        x3 = x3 * scale * gfin_ref[...][None]
    o_ref[...] = x3.reshape(o_ref.shape)


def _peer_mix(off, wgt, tab, x8, g_final, *, tb, final_norm):
    n, slots = wgt.shape
    d = ROW_TILE * LANES
    g8 = g_final.reshape(ROW_TILE, LANES)
    kern = functools.partial(_peer_mix_kernel, final_norm=final_norm)
    stack = pltpu.VMEM((TOKENS_PER_STEP, slots * HALF_TILE, LANES), jnp.int32)
    return pl.pallas_call(
        kern,
        grid=(n // tb,),
        in_specs=[pl.BlockSpec((tb * slots,), lambda i: (i,), memory_space=pltpu.SMEM),
                  pl.BlockSpec((tb, slots), lambda i: (i, 0)),
                  pl.BlockSpec(memory_space=pltpu.VMEM),
                  pl.BlockSpec((tb * ROW_TILE, LANES), lambda i: (i, 0)),
                  pl.BlockSpec((ROW_TILE, LANES), lambda i: (0, 0))],
        out_specs=pl.BlockSpec((tb, d), lambda i: (i, 0)),
        out_shape=jax.ShapeDtypeStruct((n, d), F32),
        scratch_shapes=[stack, pltpu.VMEM((tb, slots, LANES), F32),
                        pltpu.VMEM((tb * ROW_TILE, LANES), F32)],
        compiler_params=_cparams(("arbitrary",)),
        name="peer_expert_mix",
    )(off, wgt, tab, x8, g8)


def _state_to_blocks(s):
    b, h, d, _ = s.shape
    st = jnp.swapaxes(s, -1, -2).reshape(b, h // HEADS_PER_LANE_TILE, HEADS_PER_LANE_TILE, d, d)
    eye = jnp.eye(HEADS_PER_LANE_TILE, dtype=s.dtype)
    blk = st[:, :, :, :, None, :] * eye[None, None, :, None, :, None]
    return blk.reshape(b, h // HEADS_PER_LANE_TILE, LANES, LANES)


def _blocks_to_state(blk, heads):
    b, npair = blk.shape[:2]
    x = blk.reshape(b, npair, HEADS_PER_LANE_TILE, HEAD_DIM, HEADS_PER_LANE_TILE, HEAD_DIM)
    diag = jnp.stack([x[:, :, hh, :, hh, :] for hh in range(HEADS_PER_LANE_TILE)], axis=2)
    return jnp.swapaxes(diag.reshape(b, heads, HEAD_DIM, HEAD_DIM), -1, -2)


PEER_TOPK = 16


def _layer(x, k_past, v_past, lf_past, s0, shift0, lp, g_final, final_norm):
    (norm_mix_g, w_in, fox_b_f, mu, w0, w2, a0, a2, g2, k_k, k_a, r_k, lnx_w, lnx_b, w_out,
     norm_ffn_g, peer_w_q, peer_sub_keys, tab_u, tab_v) = lp
    b, t, d = x.shape
    n = b * t
    fox_heads = fox_b_f.shape[0]
    fw = fox_heads * HEAD_DIM
    fox_cols = 3 * fw + fox_heads
    rwkv_heads = r_k.shape[0]
    w = rwkv_heads * HEAD_DIM
    x2d = x.reshape(n, d)
    qb, kt, vt, kb, vb, lf, rw_main, rw_tail = _inproj(x2d, norm_mix_g, w_in, fox_b_f, fox_cols,
                                                       fox_heads, 3 * w, batch=b, seq=t)
    k = jnp.transpose(kt.reshape(b, fox_heads, HEAD_DIM, t), (0, 3, 1, 2))
    v = jnp.transpose(vt.reshape(b, fox_heads, HEAD_DIM, t), (0, 3, 1, 2))
    fox = _fox_stream(qb, kb, vb, lf, k_past, v_past, lf_past, batch=b, q_len=t)

    prm = _rwkv_params(mu, w0, w2, a0, a2, g2, k_k, k_a, r_k.reshape(-1))
    tail = prm["tail"]
    shift_main = shift0[..., :3 * w]
    shift_tail = jnp.pad(shift0[..., 3 * w:], ((0, 0), (0, 0), (0, TAIL_PAD - tail)))
    r, lw, km, vv, kn, bb, g, bonus = _rwkv_pre(rw_main, rw_tail, shift_main, shift_tail, prm,
                                                batch=b, seq=t)
    y, s_blk = _rwkv_scan(r, lw, km, vv, kn, bb, _state_to_blocks(s0), batch=b, seq=t)
    s_t = _blocks_to_state(s_blk, rwkv_heads)
    last = jnp.concatenate([rw_main.reshape(b, t, -1)[:, -1:], rw_tail.reshape(b, t, -1)[:, -1:, :tail]],
                           axis=-1)

    x2, xn, scores = _outproj(x2d, fox, y, bonus, g, lnx_w, lnx_b, w_out, norm_ffn_g, peer_w_q,
                              peer_sub_keys)
    idx, gate = _retrieve(scores, PEER_TOPK)
    tb = _row_tile(n, 128)
    slots = gate.shape[1]
    off = idx.reshape(n * slots)
    wgt = _peer_act(off, xn, tab_u, gate, tb=tb)
    out = _peer_mix(off, wgt, tab_v, x2, g_final, tb=tb, final_norm=final_norm)
    return (out.reshape(b, t, d), k, v, lf.reshape(b, t, fox_heads), s_t, last)


def kernel(x_prompt, x_sample, cache_fox_k, cache_fox_v, cache_fox_logf, state_rwkv, state_shift,
           norm_mix_g, w_in, fox_b_f, rwkv_mu, rwkv_w0, rwkv_w2, rwkv_a0, rwkv_a2, rwkv_g2,
           rwkv_k_k, rwkv_k_a, rwkv_r_k, rwkv_lnx_w, rwkv_lnx_b, w_out, norm_ffn_g,
           peer_w_q, peer_sub_keys, peer_u, peer_v, norm_final_g):
    depth = w_in.shape[0]
    yp, ys = x_prompt, x_sample
    bp = x_prompt.shape[0]
    dt = x_prompt.dtype
    fox_heads = fox_b_f.shape[1]
    rwkv_heads = rwkv_r_k.shape[1]
    rwkv_cols = rwkv_mu.shape[1]
    outs_p, outs_s = [], []
    for l in range(depth):
        lp = (norm_mix_g[l], w_in[l], fox_b_f[l], rwkv_mu[l], rwkv_w0[l], rwkv_w2[l], rwkv_a0[l],
              rwkv_a2[l], rwkv_g2[l], rwkv_k_k[l], rwkv_k_a[l], rwkv_r_k[l], rwkv_lnx_w[l],
              rwkv_lnx_b[l], w_out[l], norm_ffn_g[l], peer_w_q[l], peer_sub_keys[l],
              _pack_table(peer_u[l]), _pack_table(peer_v[l]))
        last = l == depth - 1
        empty_kv = jnp.zeros((bp, 0, fox_heads, HEAD_DIM), dt)
        empty_lf = jnp.zeros((bp, 0, fox_heads), dt)
        s_zero = jnp.zeros((bp, rwkv_heads, HEAD_DIM, HEAD_DIM), dt)
        sh_zero = jnp.zeros((bp, 1, rwkv_cols), dt)
        yp, *rest_p = _layer(yp, empty_kv, empty_kv, empty_lf, s_zero, sh_zero, lp, norm_final_g, last)
        ys, *rest_s = _layer(ys, cache_fox_k[l], cache_fox_v[l], cache_fox_logf[l], state_rwkv[l],
                             state_shift[l], lp, norm_final_g, last)
        outs_p.append(rest_p)
        outs_s.append(rest_s)
    stack = lambda outs, i: jnp.stack([o[i] for o in outs])
    return ((yp, ys) + tuple(stack(outs_p, i) for i in range(5))
            + tuple(stack(outs_s, i) for i in range(5)))
```

```python
import functools
import math

import jax
import jax.numpy as jnp
from jax import lax
from jax.experimental import pallas as pl
from jax.experimental.pallas import tpu as pltpu

F32 = jnp.float32
BF16 = jnp.bfloat16

HEAD_DIM = 64
LANES = 128
HEADS_PER_LANE_TILE = LANES // HEAD_DIM
TAIL_PAD = 2 * LANES
CHUNK = 64
RWKV_PASSES = 1
NORM_EPS = 1e-6
LNX_EPS = 64e-5
NEG_BIG = -1e30
LOG2E = math.log2(math.e)
HIGHEST = lax.Precision.HIGHEST
VMEM_LIMIT = 48 * 1024 * 1024
ROW_BLOCK = 512
RETRIEVE_BLOCK = 256
PEER_BLOCK = 128


def _cparams(sem):
    return pltpu.CompilerParams(dimension_semantics=sem, vmem_limit_bytes=VMEM_LIMIT)


def _row_tile(n, target):
    t = min(n, target)
    assert n % t == 0, (n, t)
    return t


def _inproj_kernel(x_ref, g_ref, wqkv_ref, wkvt_ref, wf_ref, wrw_ref, wtail_ref, bf_ref,
                   q_ref, kt_ref, vt_ref, kb_ref, vb_ref, lf_ref, rw_ref, tail_ref):
    x = x_ref[...]
    h = x * lax.rsqrt(jnp.mean(x * x, axis=-1, keepdims=True) + NORM_EPS) * g_ref[...]
    hb = h.astype(BF16)
    fw = wqkv_ref.shape[1] // 3
    qkv = jnp.dot(hb, wqkv_ref[...], preferred_element_type=F32)
    q_ref[...] = (qkv[:, :fw] * (LOG2E / math.sqrt(HEAD_DIM))).astype(BF16)
    kb_ref[...] = qkv[:, fw:2 * fw].astype(BF16)
    vb_ref[...] = qkv[:, 2 * fw:].astype(BF16)
    kvt = lax.dot_general(wkvt_ref[...], hb, (((1,), (1,)), ((), ())), preferred_element_type=F32)
    kt_ref[...] = kvt[:fw]
    vt_ref[...] = kvt[fw:]
    f = jnp.dot(hb, wf_ref[...], preferred_element_type=F32) + bf_ref[...]
    lf_ref[...] = jax.nn.log_sigmoid(f)
    rw_ref[...] = jnp.dot(hb, wrw_ref[...], preferred_element_type=F32)
    tail_ref[...] = jnp.dot(hb, wtail_ref[...], preferred_element_type=F32)


def _inproj(x2d, g, w_in, b_f, fox_cols, fox_heads, rw_main, *, batch, seq):
    n, d = x2d.shape
    fw = fox_heads * HEAD_DIM
    wqkv = w_in[:, :3 * fw].astype(BF16)
    wkvt = w_in[:, fw:3 * fw].T.astype(BF16)
    wf = w_in[:, 3 * fw:fox_cols].astype(BF16)
    wrw = w_in[:, fox_cols:fox_cols + rw_main].astype(BF16)
    wtail = w_in[:, fox_cols + rw_main:].astype(BF16)
    wtail = jnp.pad(wtail, ((0, 0), (0, TAIL_PAD - wtail.shape[1])))
    tm = _row_tile(seq, ROW_BLOCK)
    nt = seq // tm
    row = lambda c: pl.BlockSpec((tm, c), lambda b, i: (b * nt + i, 0))
    col = pl.BlockSpec((None, fw, tm), lambda b, i: (b, 0, i))
    full = lambda a: pl.BlockSpec(a.shape, lambda b, i: (0,) * a.ndim)
    g2 = g.reshape(1, d)
    bf2 = b_f.reshape(1, fox_heads)
    outs = (
        jax.ShapeDtypeStruct((n, fw), BF16),
        jax.ShapeDtypeStruct((batch, fw, seq), F32),
        jax.ShapeDtypeStruct((batch, fw, seq), F32),
        jax.ShapeDtypeStruct((n, fw), BF16),
        jax.ShapeDtypeStruct((n, fw), BF16),
        jax.ShapeDtypeStruct((n, fox_heads), F32),
        jax.ShapeDtypeStruct((n, rw_main), F32),
        jax.ShapeDtypeStruct((n, TAIL_PAD), F32),
    )
    return pl.pallas_call(
        _inproj_kernel,
        grid=(batch, nt),
        in_specs=[row(d), full(g2), full(wqkv), full(wkvt), full(wf), full(wrw), full(wtail),
                  full(bf2)],
        out_specs=[row(fw), col, col, row(fw), row(fw), row(fox_heads), row(rw_main),
                   row(TAIL_PAD)],
        out_shape=outs,
        compiler_params=_cparams(("parallel", "parallel")),
        name="inproj",
    )(x2d, g2, wqkv, wkvt, wf, wrw, wtail, bf2)


def _cumsum_kernel(lf_ref, c_ref, carry):
    @pl.when(pl.program_id(1) == 0)
    def _():
        carry[...] = jnp.zeros_like(carry)

    lf = lf_ref[...]
    tc = lf.shape[0]
    r = lax.broadcasted_iota(jnp.int32, (tc, tc), 0)
    c = lax.broadcasted_iota(jnp.int32, (tc, tc), 1)
    lower = (c <= r).astype(F32)
    cc = jnp.dot(lower, lf, precision=HIGHEST, preferred_element_type=F32) + carry[...]
    c_ref[...] = cc
    carry[...] = cc[tc - 1:tc, :]


def _cumsum(lf, tc):
    b, l, nh = lf.shape
    assert l % tc == 0
    return pl.pallas_call(
        _cumsum_kernel,
        grid=(b, l // tc),
        in_specs=[pl.BlockSpec((None, tc, nh), lambda i, j: (i, j, 0))],
        out_specs=pl.BlockSpec((None, tc, nh), lambda i, j: (i, j, 0)),
        out_shape=jax.ShapeDtypeStruct((b, l, nh), F32),
        scratch_shapes=[pltpu.VMEM((1, nh), F32)],
        compiler_params=_cparams(("parallel", "arbitrary")),
        name="cumsum_logf",
    )(lf)


def _split3(x):
    hi = x.astype(BF16)
    r = x - hi.astype(F32)
    mid = r.astype(BF16)
    lo = (r - mid.astype(F32)).astype(BF16)
    return hi.astype(F32), mid.astype(F32), lo.astype(F32)


def _augment_kernel(x_ref, c_ref, o_ref, *, role):
    tm = x_ref.shape[0]
    lane = lax.broadcasted_iota(jnp.int32, (tm, LANES), 1)
    for p in range(x_ref.shape[1] // LANES):
        xp = x_ref[:, p * LANES:(p + 1) * LANES].astype(F32)
        for hh in range(HEADS_PER_LANE_TILE):
            h = p * HEADS_PER_LANE_TILE + hh
            own = (lane >= hh * HEAD_DIM) & (lane < (hh + 1) * HEAD_DIM)
            e = (lane + (1 - hh) * HEAD_DIM) % LANES
            if role == "v":
                ext = jnp.where(e == 0, 1.0, 0.0)
            else:
                c = jnp.broadcast_to(c_ref[:, h:h + 1], (tm, LANES))
                hi, mid, lo = _split3(c * LOG2E)
                sgn = 1.0 if role == "q" else -1.0
                base = 0 if role == "q" else 3
                ext = jnp.where(e == base, sgn * hi,
                                jnp.where(e == base + 1, sgn * mid,
                                          jnp.where(e == base + 2, sgn * lo,
                                                    jnp.where(e < 6, 1.0, 0.0))))
            o_ref[:, h * LANES:(h + 1) * LANES] = jnp.where(own, xp, ext).astype(BF16)


def _augment(x, c, role):
    n, w = x.shape
    nh = w // HEAD_DIM
    tm = ROW_BLOCK if n % ROW_BLOCK == 0 else n
    kern = functools.partial(_augment_kernel, role=role)
    return pl.pallas_call(
        kern,
        grid=(n // tm,),
        in_specs=[pl.BlockSpec((tm, w), lambda i: (i, 0)),
                  pl.BlockSpec((tm, nh), lambda i: (i, 0))],
        out_specs=pl.BlockSpec((tm, nh * LANES), lambda i: (i, 0)),
        out_shape=jax.ShapeDtypeStruct((n, nh * LANES), BF16),
        compiler_params=_cparams(("parallel",)),
        name="fox_augment_" + role,
    )(x, c)


Q_SUB = 128
K_SUB = 256


def _fox_kernel(qblk_ref, kblk_ref, last_ref, q_ref, k_ref, v_ref, o_ref, m_sc, acc_sc,
                *, q_off, tq, tk):
    t = pl.program_id(2)
    i = qblk_ref[t]
    j = kblk_ref[t]
    qs_n, ks_n = min(Q_SUB, tq), min(K_SUB, tk)

    @pl.when(j == 0)
    def _():
        m_sc[...] = jnp.full_like(m_sc, NEG_BIG)
        acc_sc[...] = jnp.zeros_like(acc_sc)

    q_lo = q_off + i * tq
    k_lo = j * tk

    def body(masked):
        if masked:
            diff = (lax.broadcasted_iota(jnp.int32, (qs_n, LANES), 1)
                    - lax.broadcasted_iota(jnp.int32, (qs_n, LANES), 0))
        nqs = tq // qs_n
        qrow = [slice(qs * qs_n, (qs + 1) * qs_n) for qs in range(nqs)]
        m_run = [[m_sc[hh, qrow[qs], :] for qs in range(nqs)] for hh in range(HEADS_PER_LANE_TILE)]
        a_run = [[acc_sc[hh, qrow[qs], :] for qs in range(nqs)] for hh in range(HEADS_PER_LANE_TILE)]
        for hh in range(HEADS_PER_LANE_TILE):
            cols = slice(hh * LANES, (hh + 1) * LANES)
            for ks in range(tk // ks_n):
                krows = slice(ks * ks_n, (ks + 1) * ks_n)
                k_sub = k_ref[krows, cols]
                v_sub = v_ref[krows, cols]
                for qs in range(nqs):
                    qrows = qrow[qs]
                    s = lax.dot_general(q_ref[qrows, cols], k_sub, (((1,), (1,)), ((), ())),
                                        preferred_element_type=F32)
                    parts = [s[:, c * LANES:(c + 1) * LANES] for c in range(ks_n // LANES)]
                    if masked:
                        parts = [jnp.where(diff <= q_lo - k_lo + qs * qs_n - ks * ks_n - c * LANES,
                                           pc, NEG_BIG) for c, pc in enumerate(parts)]
                    mx = parts[0]
                    for pc in parts[1:]:
                        mx = jnp.maximum(mx, pc)
                    m_old = m_run[hh][qs]
                    m_new = jnp.maximum(m_old, jnp.max(mx, axis=-1, keepdims=True))
                    alpha = jnp.exp2(m_old - m_new)
                    pr = jnp.concatenate([jnp.exp2(pc - m_new).astype(BF16) for pc in parts], axis=1)
                    pv = jnp.dot(pr, v_sub, preferred_element_type=F32)
                    a_run[hh][qs] = alpha * a_run[hh][qs] + pv
                    m_run[hh][qs] = m_new
        for hh in range(HEADS_PER_LANE_TILE):
            for qs in range(nqs):
                m_sc[hh, qrow[qs], :] = m_run[hh][qs]
                acc_sc[hh, qrow[qs], :] = a_run[hh][qs]

    fully_visible = k_lo + tk - 1 <= q_lo

    @pl.when(fully_visible)
    def _():
        body(False)

    @pl.when(jnp.logical_not(fully_visible))
    def _():
        body(True)

    @pl.when(last_ref[t] == 1)
    def _():
        lane = lax.broadcasted_iota(jnp.int32, (1, LANES), 1)
        out = jnp.zeros((tq, LANES), F32)
        for hh in range(HEADS_PER_LANE_TILE):
            in_head = (lane >= hh * HEAD_DIM) & (lane < (hh + 1) * HEAD_DIM)
            acc = acc_sc[hh]
            ones_col = (1 - hh) * HEAD_DIM
            denom = jnp.broadcast_to(acc[:, ones_col:ones_col + 1], acc.shape)
            out = jnp.where(in_head, acc / denom, out)
        o_ref[...] = out


def _fox_attend(q_aug, k_aug, v_aug, *, batch, q_len, kv_len, q_off, tq, tk):
    n, wa = q_aug.shape
    pair_w = HEADS_PER_LANE_TILE * LANES
    npair = wa // pair_w
    nq, nk = q_len // tq, kv_len // tk
    assert q_len % tq == 0 and kv_len % tk == 0

    pairs = [(i, j) for i in range(nq) for j in range(min(nk, (q_off + (i + 1) * tq - 1) // tk + 1))]
    qblk = jnp.asarray([i for i, _ in pairs], jnp.int32)
    kblk = jnp.asarray([j for _, j in pairs], jnp.int32)
    last = jnp.asarray([int(t + 1 == len(pairs) or pairs[t + 1][0] != i)
                        for t, (i, _) in enumerate(pairs)], jnp.int32)
    q_map = lambda b, p, t, qb, kb, lt: (b * nq + qb[t], p)
    kv_map = lambda b, p, t, qb, kb, lt: (b * nk + kb[t], p)
    kern = functools.partial(_fox_kernel, q_off=q_off, tq=tq, tk=tk)
    return pl.pallas_call(
        kern,
        grid_spec=pltpu.PrefetchScalarGridSpec(
            num_scalar_prefetch=3,
            grid=(batch, npair, len(pairs)),
            in_specs=[pl.BlockSpec((tq, pair_w), q_map),
                      pl.BlockSpec((tk, pair_w), kv_map),
                      pl.BlockSpec((tk, pair_w), kv_map)],
            out_specs=pl.BlockSpec((tq, LANES), q_map),
            scratch_shapes=[pltpu.VMEM((HEADS_PER_LANE_TILE, tq, LANES), F32),
                            pltpu.VMEM((HEADS_PER_LANE_TILE, tq, LANES), F32)]),
        out_shape=jax.ShapeDtypeStruct((n, npair * LANES), F32),
        compiler_params=_cparams(("parallel", "parallel", "arbitrary")),
        name="fox_attention",
    )(qblk, kblk, last, q_aug, k_aug, v_aug)


def _fox_stream(qb, kb, vb, lf, k_past, v_past, lf_past, *, batch, q_len):
    n, w = qb.shape
    nh = lf.shape[1]
    past = k_past.shape[1]
    lf_new = lf.reshape(batch, q_len, nh)
    if past == 0:
        kv_len = q_len
        k_all, v_all, lf_all = kb, vb, lf_new
        tq = tk = _row_tile(q_len, ROW_BLOCK)
        tc = tk
    else:
        kv_len = -(-(past + q_len) // K_SUB) * K_SUB
        pad = kv_len - past - q_len

        def cat(old, new):
            old = old.reshape(batch, past, -1).astype(new.dtype)
            new = new.reshape(batch, q_len, -1)
            z = jnp.zeros((batch, pad, new.shape[-1]), new.dtype)
            return jnp.concatenate([old, new, z], axis=1)

        k_all = cat(k_past, kb).reshape(batch * kv_len, w)
        v_all = cat(v_past, vb).reshape(batch * kv_len, w)
        lf_all = cat(lf_past, lf_new)
        tc = max(t for t in range(LANES, 1024 + 1, LANES) if kv_len % t == 0)
        tq, tk = q_len, kv_len
    c = _cumsum(lf_all, tc)
    c_k = c.reshape(batch * kv_len, nh)
    c_q = c[:, past:past + q_len].reshape(n, nh)
    return _fox_attend(_augment(qb, c_q, "q"), _augment(k_all, c_k, "k"), _augment(v_all, c_k, "v"),
                       batch=batch, q_len=q_len, kv_len=kv_len, q_off=past, tq=tq, tk=tk)


def _head_sum_matrix(width):
    r = lax.broadcasted_iota(jnp.int32, (width, width), 0) // HEAD_DIM
    c = lax.broadcasted_iota(jnp.int32, (width, width), 1) // HEAD_DIM
    return (r == c).astype(BF16)


def _rwkv_pre_kernel(pm_ref, pt_ref, sm_ref, st_ref, mum_ref, mut_ref, wbig_ref, w0_ref, a0_ref,
                     kk_ref, ka_ref, rk_ref,
                     r_out, lw_out, km_out, v_out, kn_out, b_out, g_out, bonus_out,
                     carry_m, carry_t, *, lora_w, lora_a):
    @pl.when(pl.program_id(1) == 0)
    def _():
        carry_m[...] = sm_ref[...]
        carry_t[...] = st_ref[...]

    pm = pm_ref[...]
    pt = pt_ref[...]
    tm = pm.shape[0]
    w = pm.shape[1] // 3

    def shifted(p, carry):
        row = lax.broadcasted_iota(jnp.int32, p.shape, 0)
        return jnp.where(row == 0, carry[...], pltpu.roll(p, 1, 0))

    prev_m = shifted(pm, carry_m)
    prev_t = shifted(pt, carry_t)
    carry_m[...] = pm[tm - 1:tm, :]
    carry_t[...] = pt[tm - 1:tm, :]
    psm = pm + mum_ref[...] * (prev_m - pm)
    pst = pt + mut_ref[...] * (prev_t - pt)
    r = psm[:, :w]
    k = psm[:, w:2 * w]
    v = psm[:, 2 * w:]
    lane = lax.broadcasted_iota(jnp.int32, pst.shape, 1)
    z = jnp.where(lane < lora_w, jnp.tanh(pst),
                  jnp.where(lane < lora_w + lora_a, pst, jax.nn.sigmoid(pst)))
    lo = jnp.dot(z.astype(BF16), wbig_ref[...], preferred_element_type=F32)
    w_log = -jax.nn.softplus(-(w0_ref[...] + lo[:, :w])) - 0.5
    lw = -jnp.exp(w_log)
    a = jax.nn.sigmoid(a0_ref[...] + lo[:, w:2 * w])
    g = lo[:, 2 * w:]
    e = _head_sum_matrix(w)
    kk0 = k * kk_ref[...]
    n2 = _dot3(kk0 * kk0, e)
    kn = kk0 / jnp.maximum(jnp.sqrt(n2), 1e-12)
    km = k * (1.0 + (a - 1.0) * ka_ref[...])
    rk = _dot3(r * km * rk_ref[...], e)
    r_out[...] = r
    lw_out[...] = lw
    km_out[...] = km
    v_out[...] = v
    kn_out[...] = kn
    b_out[...] = kn * a
    g_out[...] = g
    bonus_out[...] = rk * v


def _rwkv_params(mu, w0, w2, a0, a2, g2, k_k, k_a, r_k):
    w = w0.shape[0]
    lora_w, lora_a, lora_g = w2.shape[0], a2.shape[0], g2.shape[0]
    w_lora = jnp.zeros((TAIL_PAD, 3 * w), F32)
    w_lora = w_lora.at[:lora_w, :w].set(w2)
    w_lora = w_lora.at[lora_w:lora_w + lora_a, w:2 * w].set(a2)
    w_lora = w_lora.at[lora_w + lora_a:lora_w + lora_a + lora_g, 2 * w:].set(g2)
    tail = mu.shape[0] - 3 * w
    return dict(
        mu_main=mu[:3 * w].reshape(1, 3 * w),
        mu_tail=jnp.pad(mu[3 * w:], (0, TAIL_PAD - tail)).reshape(1, TAIL_PAD),
        w_lora=w_lora.astype(BF16), w0=w0.reshape(1, w), a0=a0.reshape(1, w),
        k_k=k_k.reshape(1, w), k_a=k_a.reshape(1, w), r_k=r_k.reshape(1, w),
        lora_w=lora_w, lora_a=lora_a, tail=tail)


def _rwkv_pre(rw_main, rw_tail, shift_main, shift_tail, prm, *, batch, seq):
    n, w3 = rw_main.shape
    w = w3 // 3
    tm = _row_tile(seq, ROW_BLOCK)
    nt = seq // tm
    row = lambda c: pl.BlockSpec((tm, c), lambda b, i: (b * nt + i, 0))
    per_b = lambda c: pl.BlockSpec((None, 1, c), lambda b, i: (b, 0, 0))
    full = lambda a: pl.BlockSpec(a.shape, lambda b, i: (0,) * a.ndim)
    consts = [prm["mu_main"], prm["mu_tail"], prm["w_lora"], prm["w0"], prm["a0"], prm["k_k"],
              prm["k_a"], prm["r_k"]]
    kern = functools.partial(_rwkv_pre_kernel, lora_w=prm["lora_w"], lora_a=prm["lora_a"])
    return pl.pallas_call(
        kern,
        grid=(batch, nt),
        in_specs=[row(w3), row(TAIL_PAD), per_b(w3), per_b(TAIL_PAD)] + [full(c) for c in consts],
        out_specs=[row(w)] * 8,
        out_shape=[jax.ShapeDtypeStruct((n, w), F32)] * 8,
        scratch_shapes=[pltpu.VMEM((1, w3), F32), pltpu.VMEM((1, TAIL_PAD), F32)],
        compiler_params=_cparams(("parallel", "arbitrary")),
        name="rwkv_pre",
    )(rw_main, rw_tail, shift_main, shift_tail, *consts)


def _bmm(a, b, kind, passes):
    contract = {"nn": ((2,), (1,)), "nt": ((2,), (2,)), "tn": ((1,), (1,))}[kind]
    dims = (contract, ((0,), (0,)))
    if passes == 6:
        return lax.dot_general(a, b, dims, precision=HIGHEST, preferred_element_type=F32)
    dg = lambda x, y: lax.dot_general(x, y, dims, preferred_element_type=F32)
    ah, bh = a.astype(BF16), b.astype(BF16)
    out = dg(ah, bh)
    if passes == 3:
        al = (a - ah.astype(F32)).astype(BF16)
        bl = (b - bh.astype(F32)).astype(BF16)
        out = out + dg(ah, bl) + dg(al, bh)
    return out


def _rwkv_chunk(r, lw, km, v, kn, bb, s_blk, passes):
    g, c, _ = r.shape
    c2 = HEADS_PER_LANE_TILE * c
    ti = lax.broadcasted_iota(jnp.int32, (g, c, c), 1)
    si = lax.broadcasted_iota(jnp.int32, (g, c, c), 2)
    cs = _bmm((si <= ti).astype(F32), lw, "nn", 6)
    e_pos = jnp.exp(cs)
    e_neg = jnp.exp(-cs)
    kt = kn * jnp.exp(cs - lw)
    bt = bb * e_neg
    kh = km * e_neg
    rt = r * e_pos
    g_end = e_pos[:, c - 1:c, :]

    lane = lax.broadcasted_iota(jnp.int32, (1, 1, LANES), 2)
    head_of_lane = lane // HEAD_DIM

    def stack_masked(x):
        return jnp.concatenate(
            [jnp.where(head_of_lane == hh, x, 0.0) for hh in range(HEADS_PER_LANE_TILE)], axis=1)

    def stack(x):
        return jnp.concatenate([x] * HEADS_PER_LANE_TILE, axis=1)

    def pick(x):
        out = x[:, :c]
        for hh in range(1, HEADS_PER_LANE_TILE):
            out = jnp.where(head_of_lane == hh, x[:, hh * c:(hh + 1) * c], out)
        return out

    kt2 = stack_masked(kt)
    rt2 = stack_masked(rt)
    rr = lax.broadcasted_iota(jnp.int32, (1, c2, c2), 1)
    cc = lax.broadcasted_iota(jnp.int32, (1, c2, c2), 2)
    strict_blk = (rr // c == cc // c) & (cc < rr)
    x = jnp.where(strict_blk, -_bmm(kt2, stack(bt), "nt", passes), 0.0)
    tinv = (rr == cc).astype(F32) + x
    steps = max(int(math.ceil(math.log2(c))) - 1, 0)
    for _ in range(steps):
        x = _bmm(x, x, "nn", passes)
        tinv = tinv + _bmm(tinv, x, "nn", passes)
    tr = lax.broadcasted_iota(jnp.int32, (1, c2, c), 1) % c
    sr = lax.broadcasted_iota(jnp.int32, (1, c2, c), 2)
    kk_s = jnp.where(sr < tr, _bmm(kt2, kh, "nt", passes), 0.0)
    rb_s = jnp.where(sr <= tr, _bmm(rt2, bt, "nt", passes), 0.0)
    rk_s = jnp.where(sr <= tr, _bmm(rt2, kh, "nt", passes), 0.0)

    ks = _bmm(jnp.concatenate([kt, rt], axis=1), s_blk, "nn", passes)
    rhs = ks[:, :c] + pick(_bmm(kk_s, v, "nn", passes))
    z = pick(_bmm(tinv, stack(rhs), "nn", passes))
    y = ks[:, c:] - pick(_bmm(rb_s, z, "nn", passes)) + pick(_bmm(rk_s, v, "nn", passes))
    jr = lax.broadcasted_iota(jnp.int32, (1, LANES, LANES), 1)
    ic = lax.broadcasted_iota(jnp.int32, (1, LANES, LANES), 2)
    decay_rows = jnp.swapaxes(jnp.broadcast_to(g_end, (g, LANES, LANES)), 1, 2)
    upd = _bmm(jnp.concatenate([bt * g_end, kh * g_end], axis=1),
               jnp.concatenate([-z, v], axis=1), "tn", passes)
    s_new = decay_rows * s_blk + jnp.where(jr // HEAD_DIM == ic // HEAD_DIM, upd, 0.0)
    return y, s_new


def _rwkv_scan_kernel(r_ref, lw_ref, km_ref, v_ref, kn_ref, b_ref, s0_ref, y_ref, sT_ref, s_sc,
                      *, chunk, passes):
    it = pl.program_id(0)
    nb, tb, w = r_ref.shape
    npair = w // LANES

    @pl.when(it == 0)
    def _():
        s_sc[...] = s0_ref[...]

    def step(ci, carry):
        rows = pl.ds(pl.multiple_of(ci * chunk, chunk), chunk)

        def gather(ref):
            blk = ref[:, rows, :]
            return jnp.concatenate([blk[:, :, p * LANES:(p + 1) * LANES] for p in range(npair)],
                                   axis=0)

        y, s_new = _rwkv_chunk(gather(r_ref), gather(lw_ref), gather(km_ref), gather(v_ref),
                               gather(kn_ref), gather(b_ref), s_sc[...], passes)
        for p in range(npair):
            y_ref[:, rows, p * LANES:(p + 1) * LANES] = y[p * nb:(p + 1) * nb]
        s_sc[...] = s_new
        return carry

    lax.fori_loop(0, tb // chunk, step, 0)

    @pl.when(it == pl.num_programs(0) - 1)
    def _():
        sT_ref[...] = s_sc[...]


def _rwkv_scan(r, lw, km, v, kn, bb, s0_blk, *, batch, seq, passes=RWKV_PASSES):
    n, w = r.shape
    npair = w // LANES
    chunk = min(CHUNK, seq)
    tb = _row_tile(seq, 4 * chunk)
    row = pl.BlockSpec((batch, tb, w), lambda i: (0, i, 0))
    st = pl.BlockSpec((npair * batch, LANES, LANES), lambda i: (0, 0, 0))
    s0 = jnp.swapaxes(s0_blk, 0, 1).reshape(npair * batch, LANES, LANES)
    kern = functools.partial(_rwkv_scan_kernel, chunk=chunk, passes=passes)
    y, s_t = pl.pallas_call(
        kern,
        grid=(seq // tb,),
        in_specs=[row] * 6 + [st],
        out_specs=[row, st],
        out_shape=[jax.ShapeDtypeStruct((batch, seq, w), F32),
                   jax.ShapeDtypeStruct((npair * batch, LANES, LANES), F32)],
        scratch_shapes=[pltpu.VMEM((npair * batch, LANES, LANES), F32)],
        compiler_params=_cparams(("arbitrary",)),
        name="rwkv_scan",
    )(*(a.reshape(batch, seq, w) for a in (r, lw, km, v, kn, bb)), s0)
    s_t = jnp.swapaxes(s_t.reshape(npair, batch, LANES, LANES), 0, 1)
    return y.reshape(n, w), s_t


def _outproj_kernel(x_ref, fox_ref, y_ref, bonus_ref, g_ref, lnw_ref, lnb_ref, wa_ref, wb_ref,
                    gf_ref, wqt_ref, keys_ref, x2_ref, xn_ref, sc_ref):
    y = y_ref[...]
    w = y.shape[1]
    e = _head_sum_matrix(w)
    mean = _dot3(y, e) * (1.0 / HEAD_DIM)
    d = y - mean
    var = _dot3(d * d, e) * (1.0 / HEAD_DIM)
    yn = d * lax.rsqrt(var + LNX_EPS) * lnw_ref[...] + lnb_ref[...]
    rw = (yn + bonus_ref[...]) * g_ref[...]
    mix = (jnp.dot(fox_ref[...].astype(BF16), wa_ref[...], preferred_element_type=F32)
           + jnp.dot(rw.astype(BF16), wb_ref[...], preferred_element_type=F32))
    x2 = x_ref[...] + mix
    xn = x2 * lax.rsqrt(jnp.mean(x2 * x2, axis=-1, keepdims=True) + NORM_EPS) * gf_ref[...]
    x2_ref[...] = x2.reshape(x2_ref.shape)
    xn_ref[...] = xn.reshape(xn_ref.shape)
    qt = lax.dot_general(wqt_ref[...], xn.astype(BF16), (((1,), (1,)), ((), ())),
                         preferred_element_type=F32)
    qh = keys_ref.shape[2]
    for hc in range(keys_ref.shape[0]):
        sc_ref[hc] = jnp.dot(keys_ref[hc], qt[hc * qh:(hc + 1) * qh, :].astype(BF16),
                             preferred_element_type=F32)


def _outproj(x2d, fox, y, bonus, g, lnx_w, lnx_b, w_out, g_ffn, w_q, sub_keys):
    n, d = x2d.shape
    w = y.shape[1]
    fw = fox.shape[1]
    wa = w_out[:fw].astype(BF16)
    wb = w_out[fw:].astype(BF16)
    wqt = w_q.T.astype(BF16)
    nkeys, qh = sub_keys.shape[-2:]
    keys = sub_keys.reshape(-1, nkeys, qh).astype(BF16)
    nhc = keys.shape[0]
    tm = _row_tile(n, ROW_BLOCK)
    row = lambda c: pl.BlockSpec((tm, c), lambda i: (i, 0))
    tiles = pl.BlockSpec((tm * d // LANES, LANES), lambda i: (i, 0))
    full = lambda a: pl.BlockSpec(a.shape, lambda i: (0,) * a.ndim)
    consts = [lnx_w.reshape(1, w), lnx_b.reshape(1, w), wa, wb, g_ffn.reshape(1, d), wqt, keys]
    return pl.pallas_call(
        _outproj_kernel,
        grid=(n // tm,),
        in_specs=[row(d), row(fw), row(w), row(w), row(w)] + [full(c) for c in consts],
        out_specs=[tiles, tiles, pl.BlockSpec((nhc, nkeys, tm), lambda i: (0, 0, i))],
        out_shape=[jax.ShapeDtypeStruct((n * d // LANES, LANES), F32),
                   jax.ShapeDtypeStruct((n * d // LANES, LANES), F32),
                   jax.ShapeDtypeStruct((nhc, nkeys, n), F32)],
        compiler_params=_cparams(("parallel",)),
        name="outproj_scores",
    )(x2d, fox, y, bonus, g, *consts)


def _topk_rows(s, payload, k):
    rows = lax.broadcasted_iota(jnp.int32, s.shape, 0)
    nrow = s.shape[0]
    vals, idxs, pays = [], [], []
    for _ in range(k):
        m = jnp.max(s, axis=0, keepdims=True)
        idx = jnp.min(jnp.where(s == m, rows, nrow), axis=0, keepdims=True)
        hit = rows == idx
        vals.append(m)
        idxs.append(idx)
        if payload is not None:
            pays.append(jnp.max(jnp.where(hit, payload, -1), axis=0, keepdims=True))
        s = jnp.where(hit, -jnp.inf, s)
    return vals, idxs, pays


def _retrieve_kernel(sc_ref, idx_ref, gate_ref, *, topk, nkeys):
    nhead = sc_ref.shape[0] // 2
    idx_rows, gate_rows = [], []
    for h in range(nhead):
        v1, i1, _ = _topk_rows(sc_ref[2 * h], None, topk)
        v2, i2, _ = _topk_rows(sc_ref[2 * h + 1], None, topk)
        v2a = jnp.concatenate(v2, axis=0)
        i2a = jnp.concatenate(i2, axis=0)
        nb = [topk // (a + 1) for a in range(topk)]
        pad = -sum(nb) % 8
        cand = jnp.concatenate([v1[a] + v2a[:nb[a]] for a in range(topk)]
                               + [jnp.full((pad, v2a.shape[1]), -jnp.inf, F32)], axis=0)
        cidx = jnp.concatenate([i1[a] * nkeys + i2a[:nb[a]] for a in range(topk)]
                               + [jnp.full((pad, v2a.shape[1]), -1, jnp.int32)], axis=0)
        top, _, eidx = _topk_rows(cand, cidx, topk)
        top = jnp.concatenate(top, axis=0)
        ex = jnp.exp(top - top[0:1])
        gate_rows.append(ex / jnp.sum(ex, axis=0, keepdims=True))
        idx_rows.extend(eidx)
    idx_ref[...] = (jnp.concatenate(idx_rows, axis=0) * HALF_TILE).T
    gate_ref[...] = jnp.concatenate(gate_rows, axis=0).T


def _retrieve(scores, topk):
    nhc, nkeys, n = scores.shape
    slots = (nhc // 2) * topk
    tt = _row_tile(n, RETRIEVE_BLOCK)
    kern = functools.partial(_retrieve_kernel, topk=topk, nkeys=nkeys)
    return pl.pallas_call(
        kern,
        grid=(n // tt,),
        in_specs=[pl.BlockSpec((nhc, nkeys, tt), lambda i: (0, 0, i))],
        out_specs=[pl.BlockSpec((tt, slots), lambda i: (i, 0))] * 2,
        out_shape=[jax.ShapeDtypeStruct((n, slots), jnp.int32),
                   jax.ShapeDtypeStruct((n, slots), F32)],
        compiler_params=_cparams(("parallel",)),
        name="peer_retrieve",
    )(scores)


ROW_TILE = 8
HALF_TILE = ROW_TILE // 2
TOKENS_PER_STEP = 2


def _pack_table(t):
    e, d = t.shape
    assert d == ROW_TILE * LANES
    bits = lax.bitcast_convert_type(t.astype(BF16), jnp.uint16).astype(jnp.uint32)
    bits = bits.reshape(e, 2, HALF_TILE, LANES)
    word = bits[:, 0] | (bits[:, 1] << 16)
    return lax.bitcast_convert_type(word, jnp.int32).reshape(e * HALF_TILE, LANES)


def _gather_rows(tab_ref, off_ref, base, stack_ref, slots, place=lambda j: j):
    tok = off_ref.at[pl.ds(base, slots)]
    for j in range(slots):
        off = pl.multiple_of(tok[j], HALF_TILE)
        p = place(j)
        stack_ref[p * HALF_TILE:(p + 1) * HALF_TILE, :] = tab_ref[pl.ds(off, HALF_TILE), :]


def _fold_pairs(xs, span):
    pos = lax.broadcasted_iota(jnp.int32, (ROW_TILE, LANES), 0)
    keep = (pos % (2 * span)) < span
    out = []
    for a, b in zip(xs[0::2], xs[1::2]):
        other = jnp.where(keep, b, a)
        swapped = jnp.where(keep, pltpu.roll(other, ROW_TILE - span, 0), pltpu.roll(other, span, 0))
        out.append(jnp.where(keep, a, b) + swapped)
    return out


_FOLD_ROW_OF_GROUP = (0, 4, 2, 6, 1, 5, 3, 7)


def _lane_sums_as_row(q):
    ones = jnp.ones((ROW_TILE, LANES), BF16)
    nt = lambda b: lax.dot_general(ones, b, (((1,), (1,)), ((), ())), preferred_element_type=F32)
    hi = q.astype(BF16)
    mid = (q - hi.astype(F32)).astype(BF16)
    return nt(hi) + nt(mid)


def _dot3(x, w01):
    hi = x.astype(BF16)
    r1 = x - hi.astype(F32)
    mid = r1.astype(BF16)
    lo = (r1 - mid.astype(F32)).astype(BF16)
    d = lambda a: jnp.dot(a, w01, preferred_element_type=F32)
    return d(hi) + d(mid) + d(lo)


def _peer_act_kernel(off_ref, x_ref, tab_ref, gate_ref, w_ref, stacks, part_ref, act_ref):
    tb, slots = gate_ref.shape
    place = lambda j: ROW_TILE * (j // ROW_TILE) + _FOLD_ROW_OF_GROUP[j % ROW_TILE]

    def one(t, stack_ref):
        _gather_rows(tab_ref, off_ref, t * slots, stack_ref, slots, place)
        x = x_ref[pl.ds(pl.multiple_of(t * ROW_TILE, ROW_TILE), ROW_TILE), :]
        x_lo = jnp.concatenate([x[:HALF_TILE]] * 2, axis=0)
        x_hi = jnp.concatenate([x[HALF_TILE:]] * 2, axis=0)
        for g in range(slots // ROW_TILE):
            prods = []
            for k in range(4 * g, 4 * g + 4):
                word = stack_ref[k * ROW_TILE:(k + 1) * ROW_TILE, :]
                prods.append(lax.bitcast_convert_type(jnp.left_shift(word, 16), F32) * x_lo
                             + lax.bitcast_convert_type(word & jnp.int32(-65536), F32) * x_hi)
            part_ref[t, g * ROW_TILE:(g + 1) * ROW_TILE, :] = _fold_pairs(_fold_pairs(prods, 2), 1)[0]

    def step(i, carry):
        for u in range(TOKENS_PER_STEP):
            one(TOKENS_PER_STEP * i + u, stacks.at[u])
        return carry

    lax.fori_loop(0, tb // TOKENS_PER_STEP, step, 0)

    for c in range(tb // ROW_TILE):
        rows = part_ref[c * ROW_TILE:(c + 1) * ROW_TILE].reshape(ROW_TILE * slots, LANES)
        sums = _lane_sums_as_row(rows)
        for u in range(ROW_TILE):
            act_ref[c * ROW_TILE + u:c * ROW_TILE + u + 1, :] = sums[:1, u * slots:(u + 1) * slots]
    act = act_ref[...]
    gelu = 0.5 * act * (1.0 + lax.erf(act * math.sqrt(0.5)))
    w_ref[...] = gate_ref[...] * gelu


def _peer_act(off, x8, tab, gate, *, tb):
    n, slots = gate.shape
    stack = pltpu.VMEM((TOKENS_PER_STEP, slots * HALF_TILE, LANES), jnp.int32)
    return pl.pallas_call(
        _peer_act_kernel,
        grid=(n // tb,),
        in_specs=[pl.BlockSpec((tb * slots,), lambda i: (i,), memory_space=pltpu.SMEM),
                  pl.BlockSpec((tb * ROW_TILE, LANES), lambda i: (i, 0)),
                  pl.BlockSpec(memory_space=pltpu.VMEM),
                  pl.BlockSpec((tb, slots), lambda i: (i, 0))],
        out_specs=pl.BlockSpec((tb, slots), lambda i: (i, 0)),
        out_shape=jax.ShapeDtypeStruct((n, slots), F32),
        scratch_shapes=[stack, pltpu.VMEM((tb, slots, LANES), F32),
                        pltpu.VMEM((tb, slots), F32)],
        compiler_params=_cparams(("arbitrary",)),
        name="peer_expert_act",
    )(off, x8, tab, gate)


def _peer_mix_kernel(off_ref, w_ref, tab_ref, x2_ref, gfin_ref, o_ref, stacks, wbc_all, x3_ref,
                     *, final_norm):
    tb, slots = w_ref.shape
    nacc = 4
    group = 16
    wt = w_ref[...].T
    w_hi = wt.astype(BF16)
    w_mid = (wt - w_hi.astype(F32)).astype(BF16)
    row_tok = lax.broadcasted_iota(jnp.int32, (tb, group * LANES), 0)
    col_tok = lax.broadcasted_iota(jnp.int32, (tb, group * LANES), 1) // LANES

    def broadcast_group(c, carry):
        onehot = (row_tok == col_tok + c * group).astype(BF16)
        res = sum(jnp.dot(term, onehot, preferred_element_type=F32) for term in (w_hi, w_mid))
        for tt in range(group):
            wbc_all[c * group + tt] = res[:, tt * LANES:(tt + 1) * LANES]
        return carry

    lax.fori_loop(0, tb // group, broadcast_group, 0)
    sub = lax.broadcasted_iota(jnp.int32, (ROW_TILE, LANES), 0)

    def one(t, stack_ref):
        _gather_rows(tab_ref, off_ref, t * slots, stack_ref, slots)
        wbc_ref = wbc_all.at[t]
        lo = [jnp.zeros((ROW_TILE, LANES), F32) for _ in range(nacc)]
        hi_acc = [jnp.zeros((ROW_TILE, LANES), F32) for _ in range(nacc)]
        for k in range(slots // 2):
            word = stack_ref[k * ROW_TILE:(k + 1) * ROW_TILE, :]
            wv = jnp.where(sub < HALF_TILE,
                           jnp.broadcast_to(wbc_ref[2 * k:2 * k + 1, :], (ROW_TILE, LANES)),
                           jnp.broadcast_to(wbc_ref[2 * k + 1:2 * k + 2, :], (ROW_TILE, LANES)))
            a = k % nacc
            lo[a] = lo[a] + lax.bitcast_convert_type(jnp.left_shift(word, 16), F32) * wv
            hi_acc[a] = hi_acc[a] + lax.bitcast_convert_type(word & jnp.int32(-65536), F32) * wv
        lo_sum = (lo[0] + lo[1]) + (lo[2] + lo[3])
        hi_sum = (hi_acc[0] + hi_acc[1]) + (hi_acc[2] + hi_acc[3])
        ff = jnp.concatenate([lo_sum[:HALF_TILE] + lo_sum[HALF_TILE:],
                              hi_sum[:HALF_TILE] + hi_sum[HALF_TILE:]], axis=0)
        rows = pl.ds(pl.multiple_of(t * ROW_TILE, ROW_TILE), ROW_TILE)
        x3_ref[rows, :] = x2_ref[rows, :] + ff

    def step(i, carry):
        for u in range(TOKENS_PER_STEP):
            one(TOKENS_PER_STEP * i + u, stacks.at[u])
        return carry

    lax.fori_loop(0, tb // TOKENS_PER_STEP, step, 0)
    x3 = x3_ref[...].reshape(tb, ROW_TILE, LANES)
    if final_norm:
        sq = jnp.sum(jnp.sum(x3 * x3, axis=2, keepdims=True), axis=1, keepdims=True)
        scale = lax.rsqrt(sq * (1.0 / (ROW_TILE * LANES)) + NORM_EPS)
        x3 = x3 * scale * gfin_ref[...][None]
    o_ref[...] = x3.reshape(o_ref.shape)


def _peer_mix(off, wgt, tab, x8, g_final, *, tb, final_norm):
    n, slots = wgt.shape
    d = ROW_TILE * LANES
    g8 = g_final.reshape(ROW_TILE, LANES)
    kern = functools.partial(_peer_mix_kernel, final_norm=final_norm)
    stack = pltpu.VMEM((TOKENS_PER_STEP, slots * HALF_TILE, LANES), jnp.int32)
    return pl.pallas_call(
        kern,
        grid=(n // tb,),
        in_specs=[pl.BlockSpec((tb * slots,), lambda i: (i,), memory_space=pltpu.SMEM),
                  pl.BlockSpec((tb, slots), lambda i: (i, 0)),
                  pl.BlockSpec(memory_space=pltpu.VMEM),
                  pl.BlockSpec((tb * ROW_TILE, LANES), lambda i: (i, 0)),
                  pl.BlockSpec((ROW_TILE, LANES), lambda i: (0, 0))],
        out_specs=pl.BlockSpec((tb, d), lambda i: (i, 0)),
        out_shape=jax.ShapeDtypeStruct((n, d), F32),
        scratch_shapes=[stack, pltpu.VMEM((tb, slots, LANES), F32),
                        pltpu.VMEM((tb * ROW_TILE, LANES), F32)],
        compiler_params=_cparams(("arbitrary",)),
        name="peer_expert_mix",
    )(off, wgt, tab, x8, g8)


def _state_to_blocks(s):
    b, h, d, _ = s.shape
    st = jnp.swapaxes(s, -1, -2).reshape(b, h // HEADS_PER_LANE_TILE, HEADS_PER_LANE_TILE, d, d)
    eye = jnp.eye(HEADS_PER_LANE_TILE, dtype=s.dtype)
    blk = st[:, :, :, :, None, :] * eye[None, None, :, None, :, None]
    return blk.reshape(b, h // HEADS_PER_LANE_TILE, LANES, LANES)


def _blocks_to_state(blk, heads):
    b, npair = blk.shape[:2]
    x = blk.reshape(b, npair, HEADS_PER_LANE_TILE, HEAD_DIM, HEADS_PER_LANE_TILE, HEAD_DIM)
    diag = jnp.stack([x[:, :, hh, :, hh, :] for hh in range(HEADS_PER_LANE_TILE)], axis=2)
    return jnp.swapaxes(diag.reshape(b, heads, HEAD_DIM, HEAD_DIM), -1, -2)


PEER_TOPK = 16


def _layer(x, k_past, v_past, lf_past, s0, shift0, lp, g_final, final_norm):
    (norm_mix_g, w_in, fox_b_f, mu, w0, w2, a0, a2, g2, k_k, k_a, r_k, lnx_w, lnx_b, w_out,
     norm_ffn_g, peer_w_q, peer_sub_keys, tab_u, tab_v) = lp
    b, t, d = x.shape
    n = b * t
    fox_heads = fox_b_f.shape[0]
    fw = fox_heads * HEAD_DIM
    fox_cols = 3 * fw + fox_heads
    rwkv_heads = r_k.shape[0]
    w = rwkv_heads * HEAD_DIM
    x2d = x.reshape(n, d)
    qb, kt, vt, kb, vb, lf, rw_main, rw_tail = _inproj(x2d, norm_mix_g, w_in, fox_b_f, fox_cols,
                                                       fox_heads, 3 * w, batch=b, seq=t)
    k = jnp.transpose(kt.reshape(b, fox_heads, HEAD_DIM, t), (0, 3, 1, 2))
    v = jnp.transpose(vt.reshape(b, fox_heads, HEAD_DIM, t), (0, 3, 1, 2))
    fox = _fox_stream(qb, kb, vb, lf, k_past, v_past, lf_past, batch=b, q_len=t)

    prm = _rwkv_params(mu, w0, w2, a0, a2, g2, k_k, k_a, r_k.reshape(-1))
    tail = prm["tail"]
    shift_main = shift0[..., :3 * w]
    shift_tail = jnp.pad(shift0[..., 3 * w:], ((0, 0), (0, 0), (0, TAIL_PAD - tail)))
    r, lw, km, vv, kn, bb, g, bonus = _rwkv_pre(rw_main, rw_tail, shift_main, shift_tail, prm,
                                                batch=b, seq=t)
    y, s_blk = _rwkv_scan(r, lw, km, vv, kn, bb, _state_to_blocks(s0), batch=b, seq=t)
    s_t = _blocks_to_state(s_blk, rwkv_heads)
    last = jnp.concatenate([rw_main.reshape(b, t, -1)[:, -1:], rw_tail.reshape(b, t, -1)[:, -1:, :tail]],
                           axis=-1)

    x2, xn, scores = _outproj(x2d, fox, y, bonus, g, lnx_w, lnx_b, w_out, norm_ffn_g, peer_w_q,
                              peer_sub_keys)
    idx, gate = _retrieve(scores, PEER_TOPK)
    tb = _row_tile(n, PEER_BLOCK)
    slots = gate.shape[1]
    off = idx.reshape(n * slots)
    wgt = _peer_act(off, xn, tab_u, gate, tb=tb)
    out = _peer_mix(off, wgt, tab_v, x2, g_final, tb=tb, final_norm=final_norm)
    return (out.reshape(b, t, d), k, v, lf.reshape(b, t, fox_heads), s_t, last)


def kernel(x_prompt, x_sample, cache_fox_k, cache_fox_v, cache_fox_logf, state_rwkv, state_shift,
           norm_mix_g, w_in, fox_b_f, rwkv_mu, rwkv_w0, rwkv_w2, rwkv_a0, rwkv_a2, rwkv_g2,
           rwkv_k_k, rwkv_k_a, rwkv_r_k, rwkv_lnx_w, rwkv_lnx_b, w_out, norm_ffn_g,
           peer_w_q, peer_sub_keys, peer_u, peer_v, norm_final_g):
    depth = w_in.shape[0]
    yp, ys = x_prompt, x_sample
    bp = x_prompt.shape[0]
    dt = x_prompt.dtype
    fox_heads = fox_b_f.shape[1]
    rwkv_heads = rwkv_r_k.shape[1]
    rwkv_cols = rwkv_mu.shape[1]
    outs_p, outs_s = [], []
    for l in range(depth):
        lp = (norm_mix_g[l], w_in[l], fox_b_f[l], rwkv_mu[l], rwkv_w0[l], rwkv_w2[l], rwkv_a0[l],
              rwkv_a2[l], rwkv_g2[l], rwkv_k_k[l], rwkv_k_a[l], rwkv_r_k[l], rwkv_lnx_w[l],
              rwkv_lnx_b[l], w_out[l], norm_ffn_g[l], peer_w_q[l], peer_sub_keys[l],
              _pack_table(peer_u[l]), _pack_table(peer_v[l]))
        last = l == depth - 1
        empty_kv = jnp.zeros((bp, 0, fox_heads, HEAD_DIM), dt)
        empty_lf = jnp.zeros((bp, 0, fox_heads), dt)
        s_zero = jnp.zeros((bp, rwkv_heads, HEAD_DIM, HEAD_DIM), dt)
        sh_zero = jnp.zeros((bp, 1, rwkv_cols), dt)
        yp, *rest_p = _layer(yp, empty_kv, empty_kv, empty_lf, s_zero, sh_zero, lp, norm_final_g, last)
        ys, *rest_s = _layer(ys, cache_fox_k[l], cache_fox_v[l], cache_fox_logf[l], state_rwkv[l],
                             state_shift[l], lp, norm_final_g, last)
        outs_p.append(rest_p)
        outs_s.append(rest_s)
    stack = lambda outs, i: jnp.stack([o[i] for o in outs])
    return ((yp, ys) + tuple(stack(outs_p, i) for i in range(5))
            + tuple(stack(outs_s, i) for i in range(5)))
```

```python
import functools
import math

import jax
import jax.numpy as jnp
from jax import lax
from jax.experimental import pallas as pl
from jax.experimental.pallas import tpu as pltpu

F32 = jnp.float32
BF16 = jnp.bfloat16

HEAD_DIM = 64
LANES = 128
HEADS_PER_LANE_TILE = LANES // HEAD_DIM
TAIL_PAD = 2 * LANES
CHUNK = 64
RWKV_PASSES = 1
NORM_EPS = 1e-6
LNX_EPS = 64e-5
NEG_BIG = -1e30
LOG2E = math.log2(math.e)
HIGHEST = lax.Precision.HIGHEST
VMEM_LIMIT = 48 * 1024 * 1024
ROW_BLOCK = 512
RETRIEVE_BLOCK = 256
PEER_BLOCK = 128


def _cparams(sem):
    return pltpu.CompilerParams(dimension_semantics=sem, vmem_limit_bytes=VMEM_LIMIT)


def _row_tile(n, target):
    t = min(n, target)
    assert n % t == 0, (n, t)
    return t


def _inproj_kernel(x_ref, g_ref, wqkv_ref, wkvt_ref, wf_ref, wrw_ref, wtail_ref, bf_ref,
                   q_ref, kt_ref, vt_ref, kb_ref, vb_ref, lf_ref, rw_ref, tail_ref):
    x = x_ref[...]
    h = x * lax.rsqrt(jnp.mean(x * x, axis=-1, keepdims=True) + NORM_EPS) * g_ref[...]
    hb = h.astype(BF16)
    fw = wqkv_ref.shape[1] // 3
    qkv = jnp.dot(hb, wqkv_ref[...], preferred_element_type=F32)
    q_ref[...] = (qkv[:, :fw] * (LOG2E / math.sqrt(HEAD_DIM))).astype(BF16)
    kb_ref[...] = qkv[:, fw:2 * fw].astype(BF16)
    vb_ref[...] = qkv[:, 2 * fw:].astype(BF16)
    kvt = lax.dot_general(wkvt_ref[...], hb, (((1,), (1,)), ((), ())), preferred_element_type=F32)
    kt_ref[...] = kvt[:fw]
    vt_ref[...] = kvt[fw:]
    f = jnp.dot(hb, wf_ref[...], preferred_element_type=F32) + bf_ref[...]
    lf_ref[...] = jax.nn.log_sigmoid(f)
    rw_ref[...] = jnp.dot(hb, wrw_ref[...], preferred_element_type=F32)
    tail_ref[...] = jnp.dot(hb, wtail_ref[...], preferred_element_type=F32)


def _inproj(x2d, g, w_in, b_f, fox_cols, fox_heads, rw_main, *, batch, seq):
    n, d = x2d.shape
    fw = fox_heads * HEAD_DIM
    wqkv = w_in[:, :3 * fw].astype(BF16)
    wkvt = w_in[:, fw:3 * fw].T.astype(BF16)
    wf = w_in[:, 3 * fw:fox_cols].astype(BF16)
    wrw = w_in[:, fox_cols:fox_cols + rw_main].astype(BF16)
    wtail = w_in[:, fox_cols + rw_main:].astype(BF16)
    wtail = jnp.pad(wtail, ((0, 0), (0, TAIL_PAD - wtail.shape[1])))
    tm = _row_tile(seq, ROW_BLOCK)
    nt = seq // tm
    row = lambda c: pl.BlockSpec((tm, c), lambda b, i: (b * nt + i, 0))
    col = pl.BlockSpec((None, fw, tm), lambda b, i: (b, 0, i))
    full = lambda a: pl.BlockSpec(a.shape, lambda b, i: (0,) * a.ndim)
    g2 = g.reshape(1, d)
    bf2 = b_f.reshape(1, fox_heads)
    outs = (
        jax.ShapeDtypeStruct((n, fw), BF16),
        jax.ShapeDtypeStruct((batch, fw, seq), F32),
        jax.ShapeDtypeStruct((batch, fw, seq), F32),
        jax.ShapeDtypeStruct((n, fw), BF16),
        jax.ShapeDtypeStruct((n, fw), BF16),
        jax.ShapeDtypeStruct((n, fox_heads), F32),
        jax.ShapeDtypeStruct((n, rw_main), F32),
        jax.ShapeDtypeStruct((n, TAIL_PAD), F32),
    )
    return pl.pallas_call(
        _inproj_kernel,
        grid=(batch, nt),
        in_specs=[row(d), full(g2), full(wqkv), full(wkvt), full(wf), full(wrw), full(wtail),
                  full(bf2)],
        out_specs=[row(fw), col, col, row(fw), row(fw), row(fox_heads), row(rw_main),
                   row(TAIL_PAD)],
        out_shape=outs,
        compiler_params=_cparams(("parallel", "parallel")),
        name="inproj",
    )(x2d, g2, wqkv, wkvt, wf, wrw, wtail, bf2)


def _cumsum_kernel(lf_ref, c_ref, carry):
    @pl.when(pl.program_id(1) == 0)
    def _():
        carry[...] = jnp.zeros_like(carry)

    lf = lf_ref[...]
    tc = lf.shape[0]
    r = lax.broadcasted_iota(jnp.int32, (tc, tc), 0)
    c = lax.broadcasted_iota(jnp.int32, (tc, tc), 1)
    lower = (c <= r).astype(F32)
    cc = jnp.dot(lower, lf, precision=HIGHEST, preferred_element_type=F32) + carry[...]
    c_ref[...] = cc
    carry[...] = cc[tc - 1:tc, :]


def _cumsum(lf, tc):
    b, l, nh = lf.shape
    assert l % tc == 0
    return pl.pallas_call(
        _cumsum_kernel,
        grid=(b, l // tc),
        in_specs=[pl.BlockSpec((None, tc, nh), lambda i, j: (i, j, 0))],
        out_specs=pl.BlockSpec((None, tc, nh), lambda i, j: (i, j, 0)),
        out_shape=jax.ShapeDtypeStruct((b, l, nh), F32),
        scratch_shapes=[pltpu.VMEM((1, nh), F32)],
        compiler_params=_cparams(("parallel", "arbitrary")),
        name="cumsum_logf",
    )(lf)


def _split3(x):
    hi = x.astype(BF16)
    r = x - hi.astype(F32)
    mid = r.astype(BF16)
    lo = (r - mid.astype(F32)).astype(BF16)
    return hi.astype(F32), mid.astype(F32), lo.astype(F32)


def _augment_kernel(x_ref, c_ref, o_ref, *, role):
    tm = x_ref.shape[0]
    lane = lax.broadcasted_iota(jnp.int32, (tm, LANES), 1)
    for p in range(x_ref.shape[1] // LANES):
        xp = x_ref[:, p * LANES:(p + 1) * LANES].astype(F32)
        for hh in range(HEADS_PER_LANE_TILE):
            h = p * HEADS_PER_LANE_TILE + hh
            own = (lane >= hh * HEAD_DIM) & (lane < (hh + 1) * HEAD_DIM)
            e = (lane + (1 - hh) * HEAD_DIM) % LANES
            if role == "v":
                ext = jnp.where(e == 0, 1.0, 0.0)
            else:
                c = jnp.broadcast_to(c_ref[:, h:h + 1], (tm, LANES))
                hi, mid, lo = _split3(c * LOG2E)
                sgn = 1.0 if role == "q" else -1.0
                base = 0 if role == "q" else 3
                ext = jnp.where(e == base, sgn * hi,
                                jnp.where(e == base + 1, sgn * mid,
                                          jnp.where(e == base + 2, sgn * lo,
                                                    jnp.where(e < 6, 1.0, 0.0))))
            o_ref[:, h * LANES:(h + 1) * LANES] = jnp.where(own, xp, ext).astype(BF16)


def _augment(x, c, role):
    n, w = x.shape
    nh = w // HEAD_DIM
    tm = ROW_BLOCK if n % ROW_BLOCK == 0 else n
    kern = functools.partial(_augment_kernel, role=role)
    return pl.pallas_call(
        kern,
        grid=(n // tm,),
        in_specs=[pl.BlockSpec((tm, w), lambda i: (i, 0)),
                  pl.BlockSpec((tm, nh), lambda i: (i, 0))],
        out_specs=pl.BlockSpec((tm, nh * LANES), lambda i: (i, 0)),
        out_shape=jax.ShapeDtypeStruct((n, nh * LANES), BF16),
        compiler_params=_cparams(("parallel",)),
        name="fox_augment_" + role,
    )(x, c)


Q_SUB = 128
K_SUB = 256


def _fox_kernel(qblk_ref, kblk_ref, last_ref, q_ref, k_ref, v_ref, o_ref, m_sc, acc_sc,
                *, q_off, tq, tk):
    t = pl.program_id(2)
    i = qblk_ref[t]
    j = kblk_ref[t]
    qs_n, ks_n = min(Q_SUB, tq), min(K_SUB, tk)

    @pl.when(j == 0)
    def _():
        m_sc[...] = jnp.full_like(m_sc, NEG_BIG)
        acc_sc[...] = jnp.zeros_like(acc_sc)

    q_lo = q_off + i * tq
    k_lo = j * tk

    def body(masked):
        if masked:
            diff = (lax.broadcasted_iota(jnp.int32, (qs_n, LANES), 1)
                    - lax.broadcasted_iota(jnp.int32, (qs_n, LANES), 0))
        nqs = tq // qs_n
        qrow = [slice(qs * qs_n, (qs + 1) * qs_n) for qs in range(nqs)]
        m_run = [[m_sc[hh, qrow[qs], :] for qs in range(nqs)] for hh in range(HEADS_PER_LANE_TILE)]
        a_run = [[acc_sc[hh, qrow[qs], :] for qs in range(nqs)] for hh in range(HEADS_PER_LANE_TILE)]
        for hh in range(HEADS_PER_LANE_TILE):
            cols = slice(hh * LANES, (hh + 1) * LANES)
            for ks in range(tk // ks_n):
                krows = slice(ks * ks_n, (ks + 1) * ks_n)
                k_sub = k_ref[krows, cols]
                v_sub = v_ref[krows, cols]
                for qs in range(nqs):
                    qrows = qrow[qs]
                    s = lax.dot_general(q_ref[qrows, cols], k_sub, (((1,), (1,)), ((), ())),
                                        preferred_element_type=F32)
                    parts = [s[:, c * LANES:(c + 1) * LANES] for c in range(ks_n // LANES)]
                    if masked:
                        parts = [jnp.where(diff <= q_lo - k_lo + qs * qs_n - ks * ks_n - c * LANES,
                                           pc, NEG_BIG) for c, pc in enumerate(parts)]
                    mx = parts[0]
                    for pc in parts[1:]:
                        mx = jnp.maximum(mx, pc)
                    m_old = m_run[hh][qs]
                    m_new = jnp.maximum(m_old, jnp.max(mx, axis=-1, keepdims=True))
                    alpha = jnp.exp2(m_old - m_new)
                    pr = jnp.concatenate([jnp.exp2(pc - m_new).astype(BF16) for pc in parts], axis=1)
                    pv = jnp.dot(pr, v_sub, preferred_element_type=F32)
                    a_run[hh][qs] = alpha * a_run[hh][qs] + pv
                    m_run[hh][qs] = m_new
        for hh in range(HEADS_PER_LANE_TILE):
            for qs in range(nqs):
                m_sc[hh, qrow[qs], :] = m_run[hh][qs]
                acc_sc[hh, qrow[qs], :] = a_run[hh][qs]

    fully_visible = k_lo + tk - 1 <= q_lo

    @pl.when(fully_visible)
    def _():
        body(False)

    @pl.when(jnp.logical_not(fully_visible))
    def _():
        body(True)

    @pl.when(last_ref[t] == 1)
    def _():
        lane = lax.broadcasted_iota(jnp.int32, (1, LANES), 1)
        out = jnp.zeros((tq, LANES), F32)
        for hh in range(HEADS_PER_LANE_TILE):
            in_head = (lane >= hh * HEAD_DIM) & (lane < (hh + 1) * HEAD_DIM)
            acc = acc_sc[hh]
            ones_col = (1 - hh) * HEAD_DIM
            denom = jnp.broadcast_to(acc[:, ones_col:ones_col + 1], acc.shape)
            out = jnp.where(in_head, acc / denom, out)
        o_ref[...] = out


def _fox_attend(q_aug, k_aug, v_aug, *, batch, q_len, kv_len, q_off, tq, tk):
    n, wa = q_aug.shape
    pair_w = HEADS_PER_LANE_TILE * LANES
    npair = wa // pair_w
    nq, nk = q_len // tq, kv_len // tk
    assert q_len % tq == 0 and kv_len % tk == 0

    pairs = [(i, j) for i in range(nq) for j in range(min(nk, (q_off + (i + 1) * tq - 1) // tk + 1))]
    qblk = jnp.asarray([i for i, _ in pairs], jnp.int32)
    kblk = jnp.asarray([j for _, j in pairs], jnp.int32)
    last = jnp.asarray([int(t + 1 == len(pairs) or pairs[t + 1][0] != i)
                        for t, (i, _) in enumerate(pairs)], jnp.int32)
    q_map = lambda b, p, t, qb, kb, lt: (b * nq + qb[t], p)
    kv_map = lambda b, p, t, qb, kb, lt: (b * nk + kb[t], p)
    kern = functools.partial(_fox_kernel, q_off=q_off, tq=tq, tk=tk)
    return pl.pallas_call(
        kern,
        grid_spec=pltpu.PrefetchScalarGridSpec(
            num_scalar_prefetch=3,
            grid=(batch, npair, len(pairs)),
            in_specs=[pl.BlockSpec((tq, pair_w), q_map),
                      pl.BlockSpec((tk, pair_w), kv_map),
                      pl.BlockSpec((tk, pair_w), kv_map)],
            out_specs=pl.BlockSpec((tq, LANES), q_map),
            scratch_shapes=[pltpu.VMEM((HEADS_PER_LANE_TILE, tq, LANES), F32),
                            pltpu.VMEM((HEADS_PER_LANE_TILE, tq, LANES), F32)]),
        out_shape=jax.ShapeDtypeStruct((n, npair * LANES), F32),
        compiler_params=_cparams(("parallel", "parallel", "arbitrary")),
        name="fox_attention",
    )(qblk, kblk, last, q_aug, k_aug, v_aug)


def _fox_stream(qb, kb, vb, lf, k_past, v_past, lf_past, *, batch, q_len):
    n, w = qb.shape
    nh = lf.shape[1]
    past = k_past.shape[1]
    lf_new = lf.reshape(batch, q_len, nh)
    if past == 0:
        kv_len = q_len
        k_all, v_all, lf_all = kb, vb, lf_new
        tq = tk = _row_tile(q_len, ROW_BLOCK)
        tc = tk
    else:
        kv_len = -(-(past + q_len) // K_SUB) * K_SUB
        pad = kv_len - past - q_len

        def cat(old, new):
            old = old.reshape(batch, past, -1).astype(new.dtype)
            new = new.reshape(batch, q_len, -1)
            z = jnp.zeros((batch, pad, new.shape[-1]), new.dtype)
            return jnp.concatenate([old, new, z], axis=1)

        k_all = cat(k_past, kb).reshape(batch * kv_len, w)
        v_all = cat(v_past, vb).reshape(batch * kv_len, w)
        lf_all = cat(lf_past, lf_new)
        tc = max(t for t in range(LANES, 1024 + 1, LANES) if kv_len % t == 0)
        tq, tk = q_len, kv_len
    c = _cumsum(lf_all, tc)
    c_k = c.reshape(batch * kv_len, nh)
    c_q = c[:, past:past + q_len].reshape(n, nh)
    return _fox_attend(_augment(qb, c_q, "q"), _augment(k_all, c_k, "k"), _augment(v_all, c_k, "v"),
                       batch=batch, q_len=q_len, kv_len=kv_len, q_off=past, tq=tq, tk=tk)


def _head_sum_matrix(width):
    r = lax.broadcasted_iota(jnp.int32, (width, width), 0) // HEAD_DIM
    c = lax.broadcasted_iota(jnp.int32, (width, width), 1) // HEAD_DIM
    return (r == c).astype(BF16)


def _rwkv_pre_kernel(pm_ref, pt_ref, sm_ref, st_ref, mum_ref, mut_ref, wbig_ref, w0_ref, a0_ref,
                     kk_ref, ka_ref, rk_ref,
                     r_out, lw_out, km_out, v_out, kn_out, b_out, g_out, bonus_out,
                     carry_m, carry_t, *, lora_w, lora_a):
    @pl.when(pl.program_id(1) == 0)
    def _():
        carry_m[...] = sm_ref[...]
        carry_t[...] = st_ref[...]

    pm = pm_ref[...]
    pt = pt_ref[...]
    tm = pm.shape[0]
    w = pm.shape[1] // 3

    def shifted(p, carry):
        row = lax.broadcasted_iota(jnp.int32, p.shape, 0)
        return jnp.where(row == 0, carry[...], pltpu.roll(p, 1, 0))

    prev_m = shifted(pm, carry_m)
    prev_t = shifted(pt, carry_t)
    carry_m[...] = pm[tm - 1:tm, :]
    carry_t[...] = pt[tm - 1:tm, :]
    psm = pm + mum_ref[...] * (prev_m - pm)
    pst = pt + mut_ref[...] * (prev_t - pt)
    r = psm[:, :w]
    k = psm[:, w:2 * w]
    v = psm[:, 2 * w:]
    lane = lax.broadcasted_iota(jnp.int32, pst.shape, 1)
    z = jnp.where(lane < lora_w, jnp.tanh(pst),
                  jnp.where(lane < lora_w + lora_a, pst, jax.nn.sigmoid(pst)))
    lo = jnp.dot(z.astype(BF16), wbig_ref[...], preferred_element_type=F32)
    w_log = -jax.nn.softplus(-(w0_ref[...] + lo[:, :w])) - 0.5
    lw = -jnp.exp(w_log)
    a = jax.nn.sigmoid(a0_ref[...] + lo[:, w:2 * w])
    g = lo[:, 2 * w:]
    e = _head_sum_matrix(w)
    kk0 = k * kk_ref[...]
    n2 = _dot3(kk0 * kk0, e)
    kn = kk0 / jnp.maximum(jnp.sqrt(n2), 1e-12)
    km = k * (1.0 + (a - 1.0) * ka_ref[...])
    rk = _dot3(r * km * rk_ref[...], e)
    r_out[...] = r
    lw_out[...] = lw
    km_out[...] = km
    v_out[...] = v
    kn_out[...] = kn
    b_out[...] = kn * a
    g_out[...] = g
    bonus_out[...] = rk * v


def _rwkv_params(mu, w0, w2, a0, a2, g2, k_k, k_a, r_k):
    w = w0.shape[0]
    lora_w, lora_a, lora_g = w2.shape[0], a2.shape[0], g2.shape[0]
    w_lora = jnp.zeros((TAIL_PAD, 3 * w), F32)
    w_lora = w_lora.at[:lora_w, :w].set(w2)
    w_lora = w_lora.at[lora_w:lora_w + lora_a, w:2 * w].set(a2)
    w_lora = w_lora.at[lora_w + lora_a:lora_w + lora_a + lora_g, 2 * w:].set(g2)
    tail = mu.shape[0] - 3 * w
    return dict(
        mu_main=mu[:3 * w].reshape(1, 3 * w),
        mu_tail=jnp.pad(mu[3 * w:], (0, TAIL_PAD - tail)).reshape(1, TAIL_PAD),
        w_lora=w_lora.astype(BF16), w0=w0.reshape(1, w), a0=a0.reshape(1, w),
        k_k=k_k.reshape(1, w), k_a=k_a.reshape(1, w), r_k=r_k.reshape(1, w),
        lora_w=lora_w, lora_a=lora_a, tail=tail)


def _rwkv_pre(rw_main, rw_tail, shift_main, shift_tail, prm, *, batch, seq):
    n, w3 = rw_main.shape
    w = w3 // 3
    tm = _row_tile(seq, ROW_BLOCK)
    nt = seq // tm
    row = lambda c: pl.BlockSpec((tm, c), lambda b, i: (b * nt + i, 0))
    per_b = lambda c: pl.BlockSpec((None, 1, c), lambda b, i: (b, 0, 0))
    full = lambda a: pl.BlockSpec(a.shape, lambda b, i: (0,) * a.ndim)
    consts = [prm["mu_main"], prm["mu_tail"], prm["w_lora"], prm["w0"], prm["a0"], prm["k_k"],
              prm["k_a"], prm["r_k"]]
    kern = functools.partial(_rwkv_pre_kernel, lora_w=prm["lora_w"], lora_a=prm["lora_a"])
    return pl.pallas_call(
        kern,
        grid=(batch, nt),
        in_specs=[row(w3), row(TAIL_PAD), per_b(w3), per_b(TAIL_PAD)] + [full(c) for c in consts],
        out_specs=[row(w)] * 8,
        out_shape=[jax.ShapeDtypeStruct((n, w), F32)] * 8,
        scratch_shapes=[pltpu.VMEM((1, w3), F32), pltpu.VMEM((1, TAIL_PAD), F32)],
        compiler_params=_cparams(("parallel", "arbitrary")),
        name="rwkv_pre",
    )(rw_main, rw_tail, shift_main, shift_tail, *consts)


def _bmm(a, b, kind, passes):
    contract = {"nn": ((2,), (1,)), "nt": ((2,), (2,)), "tn": ((1,), (1,))}[kind]
    dims = (contract, ((0,), (0,)))
    if passes == 6:
        return lax.dot_general(a, b, dims, precision=HIGHEST, preferred_element_type=F32)
    dg = lambda x, y: lax.dot_general(x, y, dims, preferred_element_type=F32)
    ah, bh = a.astype(BF16), b.astype(BF16)
    out = dg(ah, bh)
    if passes == 3:
        al = (a - ah.astype(F32)).astype(BF16)
        bl = (b - bh.astype(F32)).astype(BF16)
        out = out + dg(ah, bl) + dg(al, bh)
    return out


def _rwkv_chunk(r, lw, km, v, kn, bb, s_blk, passes):
    g, c, _ = r.shape
    c2 = HEADS_PER_LANE_TILE * c
    ti = lax.broadcasted_iota(jnp.int32, (g, c, c), 1)
    si = lax.broadcasted_iota(jnp.int32, (g, c, c), 2)
    cs = _bmm((si <= ti).astype(F32), lw, "nn", 6)
    e_pos = jnp.exp(cs)
    e_neg = jnp.exp(-cs)
    kt = kn * jnp.exp(cs - lw)
    bt = bb * e_neg
    kh = km * e_neg
    rt = r * e_pos
    g_end = e_pos[:, c - 1:c, :]

    lane = lax.broadcasted_iota(jnp.int32, (1, 1, LANES), 2)
    head_of_lane = lane // HEAD_DIM

    def stack_masked(x):
        return jnp.concatenate(
            [jnp.where(head_of_lane == hh, x, 0.0) for hh in range(HEADS_PER_LANE_TILE)], axis=1)

    def stack(x):
        return jnp.concatenate([x] * HEADS_PER_LANE_TILE, axis=1)

    def pick(x):
        out = x[:, :c]
        for hh in range(1, HEADS_PER_LANE_TILE):
            out = jnp.where(head_of_lane == hh, x[:, hh * c:(hh + 1) * c], out)
        return out

    kt2 = stack_masked(kt)
    rt2 = stack_masked(rt)
    rr = lax.broadcasted_iota(jnp.int32, (1, c2, c2), 1)
    cc = lax.broadcasted_iota(jnp.int32, (1, c2, c2), 2)
    strict_blk = (rr // c == cc // c) & (cc < rr)
    x = jnp.where(strict_blk, -_bmm(kt2, stack(bt), "nt", passes), 0.0)
    tinv = (rr == cc).astype(F32) + x
    steps = max(int(math.ceil(math.log2(c))) - 1, 0)
    for _ in range(steps):
        x = _bmm(x, x, "nn", passes)
        tinv = tinv + _bmm(tinv, x, "nn", passes)
    tr = lax.broadcasted_iota(jnp.int32, (1, c2, c), 1) % c
    sr = lax.broadcasted_iota(jnp.int32, (1, c2, c), 2)
    kk_s = jnp.where(sr < tr, _bmm(kt2, kh, "nt", passes), 0.0)
    rb_s = jnp.where(sr <= tr, _bmm(rt2, bt, "nt", passes), 0.0)
    rk_s = jnp.where(sr <= tr, _bmm(rt2, kh, "nt", passes), 0.0)

    ks = _bmm(jnp.concatenate([kt, rt], axis=1), s_blk, "nn", passes)
    rhs = ks[:, :c] + pick(_bmm(kk_s, v, "nn", passes))
    z = pick(_bmm(tinv, stack(rhs), "nn", passes))
    y = ks[:, c:] - pick(_bmm(rb_s, z, "nn", passes)) + pick(_bmm(rk_s, v, "nn", passes))
    jr = lax.broadcasted_iota(jnp.int32, (1, LANES, LANES), 1)
    ic = lax.broadcasted_iota(jnp.int32, (1, LANES, LANES), 2)
    decay_rows = jnp.swapaxes(jnp.broadcast_to(g_end, (g, LANES, LANES)), 1, 2)
    upd = _bmm(jnp.concatenate([bt * g_end, kh * g_end], axis=1),
               jnp.concatenate([-z, v], axis=1), "tn", passes)
    s_new = decay_rows * s_blk + jnp.where(jr // HEAD_DIM == ic // HEAD_DIM, upd, 0.0)
    return y, s_new


def _rwkv_scan_kernel(r_ref, lw_ref, km_ref, v_ref, kn_ref, b_ref, s0_ref, y_ref, sT_ref, s_sc,
                      *, chunk, passes):
    it = pl.program_id(0)
    nb, tb, w = r_ref.shape
    npair = w // LANES

    @pl.when(it == 0)
    def _():
        s_sc[...] = s0_ref[...]

    def step(ci, carry):
        rows = pl.ds(pl.multiple_of(ci * chunk, chunk), chunk)

        def gather(ref):
            blk = ref[:, rows, :]
            return jnp.concatenate([blk[:, :, p * LANES:(p + 1) * LANES] for p in range(npair)],
                                   axis=0)

        y, s_new = _rwkv_chunk(gather(r_ref), gather(lw_ref), gather(km_ref), gather(v_ref),
                               gather(kn_ref), gather(b_ref), s_sc[...], passes)
        for p in range(npair):
            y_ref[:, rows, p * LANES:(p + 1) * LANES] = y[p * nb:(p + 1) * nb]
        s_sc[...] = s_new
        return carry

    lax.fori_loop(0, tb // chunk, step, 0)

    @pl.when(it == pl.num_programs(0) - 1)
    def _():
        sT_ref[...] = s_sc[...]


def _rwkv_scan(r, lw, km, v, kn, bb, s0_blk, *, batch, seq, passes=RWKV_PASSES):
    n, w = r.shape
    npair = w // LANES
    chunk = min(CHUNK, seq)
    tb = _row_tile(seq, 4 * chunk)
    row = pl.BlockSpec((batch, tb, w), lambda i: (0, i, 0))
    st = pl.BlockSpec((npair * batch, LANES, LANES), lambda i: (0, 0, 0))
    s0 = jnp.swapaxes(s0_blk, 0, 1).reshape(npair * batch, LANES, LANES)
    kern = functools.partial(_rwkv_scan_kernel, chunk=chunk, passes=passes)
    y, s_t = pl.pallas_call(
        kern,
        grid=(seq // tb,),
        in_specs=[row] * 6 + [st],
        out_specs=[row, st],
        out_shape=[jax.ShapeDtypeStruct((batch, seq, w), F32),
                   jax.ShapeDtypeStruct((npair * batch, LANES, LANES), F32)],
        scratch_shapes=[pltpu.VMEM((npair * batch, LANES, LANES), F32)],
        compiler_params=_cparams(("arbitrary",)),
        name="rwkv_scan",
    )(*(a.reshape(batch, seq, w) for a in (r, lw, km, v, kn, bb)), s0)
    s_t = jnp.swapaxes(s_t.reshape(npair, batch, LANES, LANES), 0, 1)
    return y.reshape(n, w), s_t


def _outproj_kernel(x_ref, fox_ref, y_ref, bonus_ref, g_ref, lnw_ref, lnb_ref, wa_ref, wb_ref,
                    gf_ref, wqt_ref, keys_ref, x2_ref, xn_ref, sc_ref):
    y = y_ref[...]
    w = y.shape[1]
    e = _head_sum_matrix(w)
    mean = _dot3(y, e) * (1.0 / HEAD_DIM)
    d = y - mean
    var = _dot3(d * d, e) * (1.0 / HEAD_DIM)
    yn = d * lax.rsqrt(var + LNX_EPS) * lnw_ref[...] + lnb_ref[...]
    rw = (yn + bonus_ref[...]) * g_ref[...]
    mix = (jnp.dot(fox_ref[...].astype(BF16), wa_ref[...], preferred_element_type=F32)
           + jnp.dot(rw.astype(BF16), wb_ref[...], preferred_element_type=F32))
    x2 = x_ref[...] + mix
    xn = x2 * lax.rsqrt(jnp.mean(x2 * x2, axis=-1, keepdims=True) + NORM_EPS) * gf_ref[...]
    x2_ref[...] = x2.reshape(x2_ref.shape)
    xn_ref[...] = xn.reshape(xn_ref.shape)
    qt = lax.dot_general(wqt_ref[...], xn.astype(BF16), (((1,), (1,)), ((), ())),
                         preferred_element_type=F32)
    qh = keys_ref.shape[2]
    for hc in range(keys_ref.shape[0]):
        sc_ref[hc] = jnp.dot(keys_ref[hc], qt[hc * qh:(hc + 1) * qh, :].astype(BF16),
                             preferred_element_type=F32)


def _outproj(x2d, fox, y, bonus, g, lnx_w, lnx_b, w_out, g_ffn, w_q, sub_keys):
    n, d = x2d.shape
    w = y.shape[1]
    fw = fox.shape[1]
    wa = w_out[:fw].astype(BF16)
    wb = w_out[fw:].astype(BF16)
    wqt = w_q.T.astype(BF16)
    nkeys, qh = sub_keys.shape[-2:]
    keys = sub_keys.reshape(-1, nkeys, qh).astype(BF16)
    nhc = keys.shape[0]
    tm = _row_tile(n, ROW_BLOCK)
    row = lambda c: pl.BlockSpec((tm, c), lambda i: (i, 0))
    tiles = pl.BlockSpec((tm * d // LANES, LANES), lambda i: (i, 0))
    full = lambda a: pl.BlockSpec(a.shape, lambda i: (0,) * a.ndim)
    consts = [lnx_w.reshape(1, w), lnx_b.reshape(1, w), wa, wb, g_ffn.reshape(1, d), wqt, keys]
    return pl.pallas_call(
        _outproj_kernel,
        grid=(n // tm,),
        in_specs=[row(d), row(fw), row(w), row(w), row(w)] + [full(c) for c in consts],
        out_specs=[tiles, tiles, pl.BlockSpec((nhc, nkeys, tm), lambda i: (0, 0, i))],
        out_shape=[jax.ShapeDtypeStruct((n * d // LANES, LANES), F32),
                   jax.ShapeDtypeStruct((n * d // LANES, LANES), F32),
                   jax.ShapeDtypeStruct((nhc, nkeys, n), F32)],
        compiler_params=_cparams(("parallel",)),
        name="outproj_scores",
    )(x2d, fox, y, bonus, g, *consts)


def _topk_rows(s, payload, k):
    rows = lax.broadcasted_iota(jnp.int32, s.shape, 0)
    nrow = s.shape[0]
    vals, idxs, pays = [], [], []
    for _ in range(k):
        m = jnp.max(s, axis=0, keepdims=True)
        idx = jnp.min(jnp.where(s == m, rows, nrow), axis=0, keepdims=True)
        hit = rows == idx
        vals.append(m)
        idxs.append(idx)
        if payload is not None:
            pays.append(jnp.max(jnp.where(hit, payload, -1), axis=0, keepdims=True))
        s = jnp.where(hit, -jnp.inf, s)
    return vals, idxs, pays


def _retrieve_kernel(sc_ref, idx_ref, gate_ref, *, topk, nkeys):
    nhead = sc_ref.shape[0] // 2
    idx_rows, gate_rows = [], []
    for h in range(nhead):
        v1, i1, _ = _topk_rows(sc_ref[2 * h], None, topk)
        v2, i2, _ = _topk_rows(sc_ref[2 * h + 1], None, topk)
        v2a = jnp.concatenate(v2, axis=0)
        i2a = jnp.concatenate(i2, axis=0)
        nb = [topk // (a + 1) for a in range(topk)]
        pad = -sum(nb) % 8
        cand = jnp.concatenate([v1[a] + v2a[:nb[a]] for a in range(topk)]
                               + [jnp.full((pad, v2a.shape[1]), -jnp.inf, F32)], axis=0)
        cidx = jnp.concatenate([i1[a] * nkeys + i2a[:nb[a]] for a in range(topk)]
                               + [jnp.full((pad, v2a.shape[1]), -1, jnp.int32)], axis=0)
        top, _, eidx = _topk_rows(cand, cidx, topk)
        top = jnp.concatenate(top, axis=0)
        ex = jnp.exp(top - top[0:1])
        gate_rows.append(ex / jnp.sum(ex, axis=0, keepdims=True))
        idx_rows.extend(eidx)
    idx_ref[...] = (jnp.concatenate(idx_rows, axis=0) * HALF_TILE).T
    gate_ref[...] = jnp.concatenate(gate_rows, axis=0).T


def _retrieve(scores, topk):
    nhc, nkeys, n = scores.shape
    slots = (nhc // 2) * topk
    tt = _row_tile(n, RETRIEVE_BLOCK)
    kern = functools.partial(_retrieve_kernel, topk=topk, nkeys=nkeys)
    return pl.pallas_call(
        kern,
        grid=(n // tt,),
        in_specs=[pl.BlockSpec((nhc, nkeys, tt), lambda i: (0, 0, i))],
        out_specs=[pl.BlockSpec((tt, slots), lambda i: (i, 0))] * 2,
        out_shape=[jax.ShapeDtypeStruct((n, slots), jnp.int32),
                   jax.ShapeDtypeStruct((n, slots), F32)],
        compiler_params=_cparams(("parallel",)),
        name="peer_retrieve",
    )(scores)


ROW_TILE = 8
HALF_TILE = ROW_TILE // 2
ACT_TOKENS_PER_STEP = 1
MIX_TOKENS_PER_STEP = 2


def _pack_table(t):
    e, d = t.shape
    assert d == ROW_TILE * LANES
    bits = lax.bitcast_convert_type(t.astype(BF16), jnp.uint16).astype(jnp.uint32)
    bits = bits.reshape(e, 2, HALF_TILE, LANES)
    word = bits[:, 0] | (bits[:, 1] << 16)
    return lax.bitcast_convert_type(word, jnp.int32).reshape(e * HALF_TILE, LANES)


def _gather_rows(tab_ref, off_ref, base, stack_ref, slots, place=lambda j: j):
    tok = off_ref.at[pl.ds(base, slots)]
    for j in range(slots):
        off = pl.multiple_of(tok[j], HALF_TILE)
        p = place(j)
        stack_ref[p * HALF_TILE:(p + 1) * HALF_TILE, :] = tab_ref[pl.ds(off, HALF_TILE), :]


def _fold_pairs(xs, span):
    pos = lax.broadcasted_iota(jnp.int32, (ROW_TILE, LANES), 0)
    keep = (pos % (2 * span)) < span
    out = []
    for a, b in zip(xs[0::2], xs[1::2]):
        other = jnp.where(keep, b, a)
        swapped = jnp.where(keep, pltpu.roll(other, ROW_TILE - span, 0), pltpu.roll(other, span, 0))
        out.append(jnp.where(keep, a, b) + swapped)
    return out


_FOLD_ROW_OF_GROUP = (0, 4, 2, 6, 1, 5, 3, 7)


def _lane_sums_as_row(q):
    ones = jnp.ones((ROW_TILE, LANES), BF16)
    nt = lambda b: lax.dot_general(ones, b, (((1,), (1,)), ((), ())), preferred_element_type=F32)
    hi = q.astype(BF16)
    mid = (q - hi.astype(F32)).astype(BF16)
    return nt(hi) + nt(mid)


def _dot3(x, w01):
    hi = x.astype(BF16)
    r1 = x - hi.astype(F32)
    mid = r1.astype(BF16)
    lo = (r1 - mid.astype(F32)).astype(BF16)
    d = lambda a: jnp.dot(a, w01, preferred_element_type=F32)
    return d(hi) + d(mid) + d(lo)


def _peer_act_kernel(off_ref, x_ref, tab_ref, gate_ref, w_ref, stacks, part_ref, act_ref):
    tb, slots = gate_ref.shape
    place = lambda j: ROW_TILE * (j // ROW_TILE) + _FOLD_ROW_OF_GROUP[j % ROW_TILE]

    def one(t, stack_ref):
        _gather_rows(tab_ref, off_ref, t * slots, stack_ref, slots, place)
        x = x_ref[pl.ds(pl.multiple_of(t * ROW_TILE, ROW_TILE), ROW_TILE), :]
        x_lo = jnp.concatenate([x[:HALF_TILE]] * 2, axis=0)
        x_hi = jnp.concatenate([x[HALF_TILE:]] * 2, axis=0)
        for g in range(slots // ROW_TILE):
            prods = []
            for k in range(4 * g, 4 * g + 4):
                word = stack_ref[k * ROW_TILE:(k + 1) * ROW_TILE, :]
                prods.append(lax.bitcast_convert_type(jnp.left_shift(word, 16), F32) * x_lo
                             + lax.bitcast_convert_type(word & jnp.int32(-65536), F32) * x_hi)
            part_ref[t, g * ROW_TILE:(g + 1) * ROW_TILE, :] = _fold_pairs(_fold_pairs(prods, 2), 1)[0]

    def step(i, carry):
        for u in range(ACT_TOKENS_PER_STEP):
            one(ACT_TOKENS_PER_STEP * i + u, stacks.at[u])
        return carry

    lax.fori_loop(0, tb // ACT_TOKENS_PER_STEP, step, 0)

    for c in range(tb // ROW_TILE):
        rows = part_ref[c * ROW_TILE:(c + 1) * ROW_TILE].reshape(ROW_TILE * slots, LANES)
        sums = _lane_sums_as_row(rows)
        for u in range(ROW_TILE):
            act_ref[c * ROW_TILE + u:c * ROW_TILE + u + 1, :] = sums[:1, u * slots:(u + 1) * slots]
    act = act_ref[...]
    gelu = 0.5 * act * (1.0 + lax.erf(act * math.sqrt(0.5)))
    w_ref[...] = gate_ref[...] * gelu


def _peer_act(off, x8, tab, gate, *, tb):
    n, slots = gate.shape
    stack = pltpu.VMEM((ACT_TOKENS_PER_STEP, slots * HALF_TILE, LANES), jnp.int32)
    return pl.pallas_call(
        _peer_act_kernel,
        grid=(n // tb,),
        in_specs=[pl.BlockSpec((tb * slots,), lambda i: (i,), memory_space=pltpu.SMEM),
                  pl.BlockSpec((tb * ROW_TILE, LANES), lambda i: (i, 0)),
                  pl.BlockSpec(memory_space=pltpu.VMEM),
                  pl.BlockSpec((tb, slots), lambda i: (i, 0))],
        out_specs=pl.BlockSpec((tb, slots), lambda i: (i, 0)),
        out_shape=jax.ShapeDtypeStruct((n, slots), F32),
        scratch_shapes=[stack, pltpu.VMEM((tb, slots, LANES), F32),
                        pltpu.VMEM((tb, slots), F32)],
        compiler_params=_cparams(("arbitrary",)),
        name="peer_expert_act",
    )(off, x8, tab, gate)


def _peer_mix_kernel(off_ref, w_ref, tab_ref, x2_ref, gfin_ref, o_ref, stacks, wbc_all, x3_ref,
                     *, final_norm):
    tb, slots = w_ref.shape
    nacc = 4
    group = 16
    wt = w_ref[...].T
    w_hi = wt.astype(BF16)
    w_mid = (wt - w_hi.astype(F32)).astype(BF16)
    row_tok = lax.broadcasted_iota(jnp.int32, (tb, group * LANES), 0)
    col_tok = lax.broadcasted_iota(jnp.int32, (tb, group * LANES), 1) // LANES

    def broadcast_group(c, carry):
        onehot = (row_tok == col_tok + c * group).astype(BF16)
        res = sum(jnp.dot(term, onehot, preferred_element_type=F32) for term in (w_hi, w_mid))
        for tt in range(group):
            wbc_all[c * group + tt] = res[:, tt * LANES:(tt + 1) * LANES]
        return carry

    lax.fori_loop(0, tb // group, broadcast_group, 0)
    sub = lax.broadcasted_iota(jnp.int32, (ROW_TILE, LANES), 0)

    def one(t, stack_ref):
        _gather_rows(tab_ref, off_ref, t * slots, stack_ref, slots)
        wbc_ref = wbc_all.at[t]
        lo = [jnp.zeros((ROW_TILE, LANES), F32) for _ in range(nacc)]
        hi_acc = [jnp.zeros((ROW_TILE, LANES), F32) for _ in range(nacc)]
        for k in range(slots // 2):
            word = stack_ref[k * ROW_TILE:(k + 1) * ROW_TILE, :]
            wv = jnp.where(sub < HALF_TILE,
                           jnp.broadcast_to(wbc_ref[2 * k:2 * k + 1, :], (ROW_TILE, LANES)),
                           jnp.broadcast_to(wbc_ref[2 * k + 1:2 * k + 2, :], (ROW_TILE, LANES)))
            a = k % nacc
            lo[a] = lo[a] + lax.bitcast_convert_type(jnp.left_shift(word, 16), F32) * wv
            hi_acc[a] = hi_acc[a] + lax.bitcast_convert_type(word & jnp.int32(-65536), F32) * wv
        lo_sum = (lo[0] + lo[1]) + (lo[2] + lo[3])
        hi_sum = (hi_acc[0] + hi_acc[1]) + (hi_acc[2] + hi_acc[3])
        ff = jnp.concatenate([lo_sum[:HALF_TILE] + lo_sum[HALF_TILE:],
                              hi_sum[:HALF_TILE] + hi_sum[HALF_TILE:]], axis=0)
        rows = pl.ds(pl.multiple_of(t * ROW_TILE, ROW_TILE), ROW_TILE)
        x3_ref[rows, :] = x2_ref[rows, :] + ff

    def step(i, carry):
        for u in range(MIX_TOKENS_PER_STEP):
            one(MIX_TOKENS_PER_STEP * i + u, stacks.at[u])
        return carry

    lax.fori_loop(0, tb // MIX_TOKENS_PER_STEP, step, 0)
    x3 = x3_ref[...].reshape(tb, ROW_TILE, LANES)
    if final_norm:
        sq = jnp.sum(jnp.sum(x3 * x3, axis=2, keepdims=True), axis=1, keepdims=True)
        scale = lax.rsqrt(sq * (1.0 / (ROW_TILE * LANES)) + NORM_EPS)
        x3 = x3 * scale * gfin_ref[...][None]
    o_ref[...] = x3.reshape(o_ref.shape)


def _peer_mix(off, wgt, tab, x8, g_final, *, tb, final_norm):
    n, slots = wgt.shape
    d = ROW_TILE * LANES
    g8 = g_final.reshape(ROW_TILE, LANES)
    kern = functools.partial(_peer_mix_kernel, final_norm=final_norm)
    stack = pltpu.VMEM((MIX_TOKENS_PER_STEP, slots * HALF_TILE, LANES), jnp.int32)
    return pl.pallas_call(
        kern,
        grid=(n // tb,),
        in_specs=[pl.BlockSpec((tb * slots,), lambda i: (i,), memory_space=pltpu.SMEM),
                  pl.BlockSpec((tb, slots), lambda i: (i, 0)),
                  pl.BlockSpec(memory_space=pltpu.VMEM),
                  pl.BlockSpec((tb * ROW_TILE, LANES), lambda i: (i, 0)),
                  pl.BlockSpec((ROW_TILE, LANES), lambda i: (0, 0))],
        out_specs=pl.BlockSpec((tb, d), lambda i: (i, 0)),
        out_shape=jax.ShapeDtypeStruct((n, d), F32),
        scratch_shapes=[stack, pltpu.VMEM((tb, slots, LANES), F32),
                        pltpu.VMEM((tb * ROW_TILE, LANES), F32)],
        compiler_params=_cparams(("arbitrary",)),
        name="peer_expert_mix",
    )(off, wgt, tab, x8, g8)


def _state_to_blocks(s):
    b, h, d, _ = s.shape
    st = jnp.swapaxes(s, -1, -2).reshape(b, h // HEADS_PER_LANE_TILE, HEADS_PER_LANE_TILE, d, d)
    eye = jnp.eye(HEADS_PER_LANE_TILE, dtype=s.dtype)
    blk = st[:, :, :, :, None, :] * eye[None, None, :, None, :, None]
    return blk.reshape(b, h // HEADS_PER_LANE_TILE, LANES, LANES)


def _blocks_to_state(blk, heads):
    b, npair = blk.shape[:2]
    x = blk.reshape(b, npair, HEADS_PER_LANE_TILE, HEAD_DIM, HEADS_PER_LANE_TILE, HEAD_DIM)
    diag = jnp.stack([x[:, :, hh, :, hh, :] for hh in range(HEADS_PER_LANE_TILE)], axis=2)
    return jnp.swapaxes(diag.reshape(b, heads, HEAD_DIM, HEAD_DIM), -1, -2)


PEER_TOPK = 16


def _layer(x, k_past, v_past, lf_past, s0, shift0, lp, g_final, final_norm):
    (norm_mix_g, w_in, fox_b_f, mu, w0, w2, a0, a2, g2, k_k, k_a, r_k, lnx_w, lnx_b, w_out,
     norm_ffn_g, peer_w_q, peer_sub_keys, tab_u, tab_v) = lp
    b, t, d = x.shape
    n = b * t
    fox_heads = fox_b_f.shape[0]
    fw = fox_heads * HEAD_DIM
    fox_cols = 3 * fw + fox_heads
    rwkv_heads = r_k.shape[0]
    w = rwkv_heads * HEAD_DIM
    x2d = x.reshape(n, d)
    qb, kt, vt, kb, vb, lf, rw_main, rw_tail = _inproj(x2d, norm_mix_g, w_in, fox_b_f, fox_cols,
                                                       fox_heads, 3 * w, batch=b, seq=t)
    k = jnp.transpose(kt.reshape(b, fox_heads, HEAD_DIM, t), (0, 3, 1, 2))
    v = jnp.transpose(vt.reshape(b, fox_heads, HEAD_DIM, t), (0, 3, 1, 2))
    fox = _fox_stream(qb, kb, vb, lf, k_past, v_past, lf_past, batch=b, q_len=t)

    prm = _rwkv_params(mu, w0, w2, a0, a2, g2, k_k, k_a, r_k.reshape(-1))
    tail = prm["tail"]
    shift_main = shift0[..., :3 * w]
    shift_tail = jnp.pad(shift0[..., 3 * w:], ((0, 0), (0, 0), (0, TAIL_PAD - tail)))
    r, lw, km, vv, kn, bb, g, bonus = _rwkv_pre(rw_main, rw_tail, shift_main, shift_tail, prm,
                                                batch=b, seq=t)
    y, s_blk = _rwkv_scan(r, lw, km, vv, kn, bb, _state_to_blocks(s0), batch=b, seq=t)
    s_t = _blocks_to_state(s_blk, rwkv_heads)
    last = jnp.concatenate([rw_main.reshape(b, t, -1)[:, -1:], rw_tail.reshape(b, t, -1)[:, -1:, :tail]],
                           axis=-1)

    x2, xn, scores = _outproj(x2d, fox, y, bonus, g, lnx_w, lnx_b, w_out, norm_ffn_g, peer_w_q,
                              peer_sub_keys)
    idx, gate = _retrieve(scores, PEER_TOPK)
    tb = _row_tile(n, PEER_BLOCK)
    slots = gate.shape[1]
    off = idx.reshape(n * slots)
    wgt = _peer_act(off, xn, tab_u, gate, tb=tb)
    out = _peer_mix(off, wgt, tab_v, x2, g_final, tb=tb, final_norm=final_norm)
    return (out.reshape(b, t, d), k, v, lf.reshape(b, t, fox_heads), s_t, last)


def kernel(x_prompt, x_sample, cache_fox_k, cache_fox_v, cache_fox_logf, state_rwkv, state_shift,
           norm_mix_g, w_in, fox_b_f, rwkv_mu, rwkv_w0, rwkv_w2, rwkv_a0, rwkv_a2, rwkv_g2,
           rwkv_k_k, rwkv_k_a, rwkv_r_k, rwkv_lnx_w, rwkv_lnx_b, w_out, norm_ffn_g,
           peer_w_q, peer_sub_keys, peer_u, peer_v, norm_final_g):
    depth = w_in.shape[0]
    yp, ys = x_prompt, x_sample
    bp = x_prompt.shape[0]
    dt = x_prompt.dtype
    fox_heads = fox_b_f.shape[1]
    rwkv_heads = rwkv_r_k.shape[1]
    rwkv_cols = rwkv_mu.shape[1]
    outs_p, outs_s = [], []
    for l in range(depth):
        lp = (norm_mix_g[l], w_in[l], fox_b_f[l], rwkv_mu[l], rwkv_w0[l], rwkv_w2[l], rwkv_a0[l],
              rwkv_a2[l], rwkv_g2[l], rwkv_k_k[l], rwkv_k_a[l], rwkv_r_k[l], rwkv_lnx_w[l],
              rwkv_lnx_b[l], w_out[l], norm_ffn_g[l], peer_w_q[l], peer_sub_keys[l],
              _pack_table(peer_u[l]), _pack_table(peer_v[l]))
        last = l == depth - 1
        empty_kv = jnp.zeros((bp, 0, fox_heads, HEAD_DIM), dt)
        empty_lf = jnp.zeros((bp, 0, fox_heads), dt)
        s_zero = jnp.zeros((bp, rwkv_heads, HEAD_DIM, HEAD_DIM), dt)
        sh_zero = jnp.zeros((bp, 1, rwkv_cols), dt)
        yp, *rest_p = _layer(yp, empty_kv, empty_kv, empty_lf, s_zero, sh_zero, lp, norm_final_g, last)
        ys, *rest_s = _layer(ys, cache_fox_k[l], cache_fox_v[l], cache_fox_logf[l], state_rwkv[l],
                             state_shift[l], lp, norm_final_g, last)
        outs_p.append(rest_p)
        outs_s.append(rest_s)
    stack = lambda outs, i: jnp.stack([o[i] for o in outs])
    return ((yp, ys) + tuple(stack(outs_p, i) for i in range(5))
            + tuple(stack(outs_s, i) for i in range(5)))
```

```python
import functools
import math

import jax
import jax.numpy as jnp
from jax import lax
from jax.experimental import pallas as pl
from jax.experimental.pallas import tpu as pltpu

F32 = jnp.float32
BF16 = jnp.bfloat16

HEAD_DIM = 64
LANES = 128
HEADS_PER_LANE_TILE = LANES // HEAD_DIM
TAIL_PAD = 2 * LANES
CHUNK = 64
RWKV_PASSES = 1
NORM_EPS = 1e-6
LNX_EPS = 64e-5
NEG_BIG = -1e30
LOG2E = math.log2(math.e)
HIGHEST = lax.Precision.HIGHEST
VMEM_LIMIT = 48 * 1024 * 1024
ROW_BLOCK = 512
FOX_BLOCK = 1024
RETRIEVE_BLOCK = 256
PEER_BLOCK = 128


def _cparams(sem):
    return pltpu.CompilerParams(dimension_semantics=sem, vmem_limit_bytes=VMEM_LIMIT)


def _row_tile(n, target):
    t = min(n, target)
    assert n % t == 0, (n, t)
    return t


def _inproj_kernel(x_ref, g_ref, wqkv_ref, wkvt_ref, wf_ref, wrw_ref, wtail_ref, bf_ref,
                   q_ref, kt_ref, vt_ref, kb_ref, vb_ref, lf_ref, rw_ref, tail_ref):
    x = x_ref[...]
    h = x * lax.rsqrt(jnp.mean(x * x, axis=-1, keepdims=True) + NORM_EPS) * g_ref[...]
    hb = h.astype(BF16)
    fw = wqkv_ref.shape[1] // 3
    qkv = jnp.dot(hb, wqkv_ref[...], preferred_element_type=F32)
    q_ref[...] = (qkv[:, :fw] * (LOG2E / math.sqrt(HEAD_DIM))).astype(BF16)
    kb_ref[...] = qkv[:, fw:2 * fw].astype(BF16)
    vb_ref[...] = qkv[:, 2 * fw:].astype(BF16)
    kvt = lax.dot_general(wkvt_ref[...], hb, (((1,), (1,)), ((), ())), preferred_element_type=F32)
    kt_ref[...] = kvt[:fw]
    vt_ref[...] = kvt[fw:]
    f = jnp.dot(hb, wf_ref[...], preferred_element_type=F32) + bf_ref[...]
    lf_ref[...] = jax.nn.log_sigmoid(f)
    rw_ref[...] = jnp.dot(hb, wrw_ref[...], preferred_element_type=F32)
    tail_ref[...] = jnp.dot(hb, wtail_ref[...], preferred_element_type=F32)


def _inproj(x2d, g, w_in, b_f, fox_cols, fox_heads, rw_main, *, batch, seq):
    n, d = x2d.shape
    fw = fox_heads * HEAD_DIM
    wqkv = w_in[:, :3 * fw].astype(BF16)
    wkvt = w_in[:, fw:3 * fw].T.astype(BF16)
    wf = w_in[:, 3 * fw:fox_cols].astype(BF16)
    wrw = w_in[:, fox_cols:fox_cols + rw_main].astype(BF16)
    wtail = w_in[:, fox_cols + rw_main:].astype(BF16)
    wtail = jnp.pad(wtail, ((0, 0), (0, TAIL_PAD - wtail.shape[1])))
    tm = _row_tile(seq, ROW_BLOCK)
    nt = seq // tm
    row = lambda c: pl.BlockSpec((tm, c), lambda b, i: (b * nt + i, 0))
    col = pl.BlockSpec((None, fw, tm), lambda b, i: (b, 0, i))
    full = lambda a: pl.BlockSpec(a.shape, lambda b, i: (0,) * a.ndim)
    g2 = g.reshape(1, d)
    bf2 = b_f.reshape(1, fox_heads)
    outs = (
        jax.ShapeDtypeStruct((n, fw), BF16),
        jax.ShapeDtypeStruct((batch, fw, seq), F32),
        jax.ShapeDtypeStruct((batch, fw, seq), F32),
        jax.ShapeDtypeStruct((n, fw), BF16),
        jax.ShapeDtypeStruct((n, fw), BF16),
        jax.ShapeDtypeStruct((n, fox_heads), F32),
        jax.ShapeDtypeStruct((n, rw_main), F32),
        jax.ShapeDtypeStruct((n, TAIL_PAD), F32),
    )
    return pl.pallas_call(
        _inproj_kernel,
        grid=(batch, nt),
        in_specs=[row(d), full(g2), full(wqkv), full(wkvt), full(wf), full(wrw), full(wtail),
                  full(bf2)],
        out_specs=[row(fw), col, col, row(fw), row(fw), row(fox_heads), row(rw_main),
                   row(TAIL_PAD)],
        out_shape=outs,
        compiler_params=_cparams(("parallel", "parallel")),
        name="inproj",
    )(x2d, g2, wqkv, wkvt, wf, wrw, wtail, bf2)


def _cumsum_kernel(lf_ref, c_ref, carry):
    @pl.when(pl.program_id(1) == 0)
    def _():
        carry[...] = jnp.zeros_like(carry)

    lf = lf_ref[...]
    tc = lf.shape[0]
    r = lax.broadcasted_iota(jnp.int32, (tc, tc), 0)
    c = lax.broadcasted_iota(jnp.int32, (tc, tc), 1)
    lower = (c <= r).astype(F32)
    cc = jnp.dot(lower, lf, precision=HIGHEST, preferred_element_type=F32) + carry[...]
    c_ref[...] = cc
    carry[...] = cc[tc - 1:tc, :]


def _cumsum(lf, tc):
    b, l, nh = lf.shape
    assert l % tc == 0
    return pl.pallas_call(
        _cumsum_kernel,
        grid=(b, l // tc),
        in_specs=[pl.BlockSpec((None, tc, nh), lambda i, j: (i, j, 0))],
        out_specs=pl.BlockSpec((None, tc, nh), lambda i, j: (i, j, 0)),
        out_shape=jax.ShapeDtypeStruct((b, l, nh), F32),
        scratch_shapes=[pltpu.VMEM((1, nh), F32)],
        compiler_params=_cparams(("parallel", "arbitrary")),
        name="cumsum_logf",
    )(lf)


def _split3(x):
    hi = x.astype(BF16)
    r = x - hi.astype(F32)
    mid = r.astype(BF16)
    lo = (r - mid.astype(F32)).astype(BF16)
    return hi.astype(F32), mid.astype(F32), lo.astype(F32)


def _augment_kernel(x_ref, c_ref, o_ref, *, role):
    tm = x_ref.shape[0]
    lane = lax.broadcasted_iota(jnp.int32, (tm, LANES), 1)
    for p in range(x_ref.shape[1] // LANES):
        xp = x_ref[:, p * LANES:(p + 1) * LANES].astype(F32)
        for hh in range(HEADS_PER_LANE_TILE):
            h = p * HEADS_PER_LANE_TILE + hh
            own = (lane >= hh * HEAD_DIM) & (lane < (hh + 1) * HEAD_DIM)
            e = (lane + (1 - hh) * HEAD_DIM) % LANES
            if role == "v":
                ext = jnp.where(e == 0, 1.0, 0.0)
            else:
                c = jnp.broadcast_to(c_ref[:, h:h + 1], (tm, LANES))
                hi, mid, lo = _split3(c * LOG2E)
                sgn = 1.0 if role == "q" else -1.0
                base = 0 if role == "q" else 3
                ext = jnp.where(e == base, sgn * hi,
                                jnp.where(e == base + 1, sgn * mid,
                                          jnp.where(e == base + 2, sgn * lo,
                                                    jnp.where(e < 6, 1.0, 0.0))))
            o_ref[:, h * LANES:(h + 1) * LANES] = jnp.where(own, xp, ext).astype(BF16)


def _augment(x, c, role):
    n, w = x.shape
    nh = w // HEAD_DIM
    tm = ROW_BLOCK if n % ROW_BLOCK == 0 else n
    kern = functools.partial(_augment_kernel, role=role)
    return pl.pallas_call(
        kern,
        grid=(n // tm,),
        in_specs=[pl.BlockSpec((tm, w), lambda i: (i, 0)),
                  pl.BlockSpec((tm, nh), lambda i: (i, 0))],
        out_specs=pl.BlockSpec((tm, nh * LANES), lambda i: (i, 0)),
        out_shape=jax.ShapeDtypeStruct((n, nh * LANES), BF16),
        compiler_params=_cparams(("parallel",)),
        name="fox_augment_" + role,
    )(x, c)


Q_SUB = 128
K_SUB = 256


def _fox_kernel(qblk_ref, kblk_ref, last_ref, q_ref, k_ref, v_ref, o_ref, m_sc, acc_sc,
                *, q_off, tq, tk):
    t = pl.program_id(2)
    i = qblk_ref[t]
    j = kblk_ref[t]
    qs_n, ks_n = min(Q_SUB, tq), min(K_SUB, tk)

    @pl.when(j == 0)
    def _():
        m_sc[...] = jnp.full_like(m_sc, NEG_BIG)
        acc_sc[...] = jnp.zeros_like(acc_sc)

    q_lo = q_off + i * tq
    k_lo = j * tk

    def body(masked):
        if masked:
            diff = (lax.broadcasted_iota(jnp.int32, (qs_n, LANES), 1)
                    - lax.broadcasted_iota(jnp.int32, (qs_n, LANES), 0))
        nqs = tq // qs_n
        qrow = [slice(qs * qs_n, (qs + 1) * qs_n) for qs in range(nqs)]
        m_run = [[m_sc[hh, qrow[qs], :] for qs in range(nqs)] for hh in range(HEADS_PER_LANE_TILE)]
        a_run = [[acc_sc[hh, qrow[qs], :] for qs in range(nqs)] for hh in range(HEADS_PER_LANE_TILE)]
        for hh in range(HEADS_PER_LANE_TILE):
            cols = slice(hh * LANES, (hh + 1) * LANES)
            for ks in range(tk // ks_n):
                krows = slice(ks * ks_n, (ks + 1) * ks_n)
                k_sub = k_ref[krows, cols]
                v_sub = v_ref[krows, cols]
                for qs in range(nqs):
                    qrows = qrow[qs]
                    s = lax.dot_general(q_ref[qrows, cols], k_sub, (((1,), (1,)), ((), ())),
                                        preferred_element_type=F32)
                    parts = [s[:, c * LANES:(c + 1) * LANES] for c in range(ks_n // LANES)]
                    if masked:
                        parts = [jnp.where(diff <= q_lo - k_lo + qs * qs_n - ks * ks_n - c * LANES,
                                           pc, NEG_BIG) for c, pc in enumerate(parts)]
                    mx = parts[0]
                    for pc in parts[1:]:
                        mx = jnp.maximum(mx, pc)
                    m_old = m_run[hh][qs]
                    m_new = jnp.maximum(m_old, jnp.max(mx, axis=-1, keepdims=True))
                    alpha = jnp.exp2(m_old - m_new)
                    pr = jnp.concatenate([jnp.exp2(pc - m_new).astype(BF16) for pc in parts], axis=1)
                    pv = jnp.dot(pr, v_sub, preferred_element_type=F32)
                    a_run[hh][qs] = alpha * a_run[hh][qs] + pv
                    m_run[hh][qs] = m_new
        for hh in range(HEADS_PER_LANE_TILE):
            for qs in range(nqs):
                m_sc[hh, qrow[qs], :] = m_run[hh][qs]
                acc_sc[hh, qrow[qs], :] = a_run[hh][qs]

    fully_visible = k_lo + tk - 1 <= q_lo

    @pl.when(fully_visible)
    def _():
        body(False)

    @pl.when(jnp.logical_not(fully_visible))
    def _():
        body(True)

    @pl.when(last_ref[t] == 1)
    def _():
        lane = lax.broadcasted_iota(jnp.int32, (1, LANES), 1)
        out = jnp.zeros((tq, LANES), F32)
        for hh in range(HEADS_PER_LANE_TILE):
            in_head = (lane >= hh * HEAD_DIM) & (lane < (hh + 1) * HEAD_DIM)
            acc = acc_sc[hh]
            ones_col = (1 - hh) * HEAD_DIM
            denom = jnp.broadcast_to(acc[:, ones_col:ones_col + 1], acc.shape)
            out = jnp.where(in_head, acc / denom, out)
        o_ref[...] = out


def _fox_attend(q_aug, k_aug, v_aug, *, batch, q_len, kv_len, q_off, tq, tk):
    n, wa = q_aug.shape
    pair_w = HEADS_PER_LANE_TILE * LANES
    npair = wa // pair_w
    nq, nk = q_len // tq, kv_len // tk
    assert q_len % tq == 0 and kv_len % tk == 0

    pairs = [(i, j) for i in range(nq) for j in range(min(nk, (q_off + (i + 1) * tq - 1) // tk + 1))]
    qblk = jnp.asarray([i for i, _ in pairs], jnp.int32)
    kblk = jnp.asarray([j for _, j in pairs], jnp.int32)
    last = jnp.asarray([int(t + 1 == len(pairs) or pairs[t + 1][0] != i)
                        for t, (i, _) in enumerate(pairs)], jnp.int32)
    q_map = lambda b, p, t, qb, kb, lt: (b * nq + qb[t], p)
    kv_map = lambda b, p, t, qb, kb, lt: (b * nk + kb[t], p)
    kern = functools.partial(_fox_kernel, q_off=q_off, tq=tq, tk=tk)
    return pl.pallas_call(
        kern,
        grid_spec=pltpu.PrefetchScalarGridSpec(
            num_scalar_prefetch=3,
            grid=(batch, npair, len(pairs)),
            in_specs=[pl.BlockSpec((tq, pair_w), q_map),
                      pl.BlockSpec((tk, pair_w), kv_map),
                      pl.BlockSpec((tk, pair_w), kv_map)],
            out_specs=pl.BlockSpec((tq, LANES), q_map),
            scratch_shapes=[pltpu.VMEM((HEADS_PER_LANE_TILE, tq, LANES), F32),
                            pltpu.VMEM((HEADS_PER_LANE_TILE, tq, LANES), F32)]),
        out_shape=jax.ShapeDtypeStruct((n, npair * LANES), F32),
        compiler_params=_cparams(("parallel", "parallel", "arbitrary")),
        name="fox_attention",
    )(qblk, kblk, last, q_aug, k_aug, v_aug)


def _fox_stream(qb, kb, vb, lf, k_past, v_past, lf_past, *, batch, q_len):
    n, w = qb.shape
    nh = lf.shape[1]
    past = k_past.shape[1]
    lf_new = lf.reshape(batch, q_len, nh)
    if past == 0:
        kv_len = q_len
        k_all, v_all, lf_all = kb, vb, lf_new
        tq = tk = _row_tile(q_len, FOX_BLOCK)
        tc = tk
    else:
        kv_len = -(-(past + q_len) // K_SUB) * K_SUB
        pad = kv_len - past - q_len

        def cat(old, new):
            old = old.reshape(batch, past, -1).astype(new.dtype)
            new = new.reshape(batch, q_len, -1)
            z = jnp.zeros((batch, pad, new.shape[-1]), new.dtype)
            return jnp.concatenate([old, new, z], axis=1)

        k_all = cat(k_past, kb).reshape(batch * kv_len, w)
        v_all = cat(v_past, vb).reshape(batch * kv_len, w)
        lf_all = cat(lf_past, lf_new)
        tc = max(t for t in range(LANES, 1024 + 1, LANES) if kv_len % t == 0)
        tq, tk = q_len, kv_len
    c = _cumsum(lf_all, tc)
    c_k = c.reshape(batch * kv_len, nh)
    c_q = c[:, past:past + q_len].reshape(n, nh)
    return _fox_attend(_augment(qb, c_q, "q"), _augment(k_all, c_k, "k"), _augment(v_all, c_k, "v"),
                       batch=batch, q_len=q_len, kv_len=kv_len, q_off=past, tq=tq, tk=tk)


def _head_sum_matrix(width):
    r = lax.broadcasted_iota(jnp.int32, (width, width), 0) // HEAD_DIM
    c = lax.broadcasted_iota(jnp.int32, (width, width), 1) // HEAD_DIM
    return (r == c).astype(BF16)


def _rwkv_pre_kernel(pm_ref, pt_ref, sm_ref, st_ref, mum_ref, mut_ref, wbig_ref, w0_ref, a0_ref,
                     kk_ref, ka_ref, rk_ref,
                     r_out, lw_out, km_out, v_out, kn_out, b_out, g_out, bonus_out,
                     carry_m, carry_t, *, lora_w, lora_a):
    @pl.when(pl.program_id(1) == 0)
    def _():
        carry_m[...] = sm_ref[...]
        carry_t[...] = st_ref[...]

    pm = pm_ref[...]
    pt = pt_ref[...]
    tm = pm.shape[0]
    w = pm.shape[1] // 3

    def shifted(p, carry):
        row = lax.broadcasted_iota(jnp.int32, p.shape, 0)
        return jnp.where(row == 0, carry[...], pltpu.roll(p, 1, 0))

    prev_m = shifted(pm, carry_m)
    prev_t = shifted(pt, carry_t)
    carry_m[...] = pm[tm - 1:tm, :]
    carry_t[...] = pt[tm - 1:tm, :]
    psm = pm + mum_ref[...] * (prev_m - pm)
    pst = pt + mut_ref[...] * (prev_t - pt)
    r = psm[:, :w]
    k = psm[:, w:2 * w]
    v = psm[:, 2 * w:]
    lane = lax.broadcasted_iota(jnp.int32, pst.shape, 1)
    z = jnp.where(lane < lora_w, jnp.tanh(pst),
                  jnp.where(lane < lora_w + lora_a, pst, jax.nn.sigmoid(pst)))
    lo = jnp.dot(z.astype(BF16), wbig_ref[...], preferred_element_type=F32)
    w_log = -jax.nn.softplus(-(w0_ref[...] + lo[:, :w])) - 0.5
    lw = -jnp.exp(w_log)
    a = jax.nn.sigmoid(a0_ref[...] + lo[:, w:2 * w])
    g = lo[:, 2 * w:]
    e = _head_sum_matrix(w)
    kk0 = k * kk_ref[...]
    n2 = _dot3(kk0 * kk0, e)
    kn = kk0 / jnp.maximum(jnp.sqrt(n2), 1e-12)
    km = k * (1.0 + (a - 1.0) * ka_ref[...])
    rk = _dot3(r * km * rk_ref[...], e)
    r_out[...] = r
    lw_out[...] = lw
    km_out[...] = km
    v_out[...] = v
    kn_out[...] = kn
    b_out[...] = kn * a
    g_out[...] = g
    bonus_out[...] = rk * v


def _rwkv_params(mu, w0, w2, a0, a2, g2, k_k, k_a, r_k):
    w = w0.shape[0]
    lora_w, lora_a, lora_g = w2.shape[0], a2.shape[0], g2.shape[0]
    w_lora = jnp.zeros((TAIL_PAD, 3 * w), F32)
    w_lora = w_lora.at[:lora_w, :w].set(w2)
    w_lora = w_lora.at[lora_w:lora_w + lora_a, w:2 * w].set(a2)
    w_lora = w_lora.at[lora_w + lora_a:lora_w + lora_a + lora_g, 2 * w:].set(g2)
    tail = mu.shape[0] - 3 * w
    return dict(
        mu_main=mu[:3 * w].reshape(1, 3 * w),
        mu_tail=jnp.pad(mu[3 * w:], (0, TAIL_PAD - tail)).reshape(1, TAIL_PAD),
        w_lora=w_lora.astype(BF16), w0=w0.reshape(1, w), a0=a0.reshape(1, w),
        k_k=k_k.reshape(1, w), k_a=k_a.reshape(1, w), r_k=r_k.reshape(1, w),
        lora_w=lora_w, lora_a=lora_a, tail=tail)


def _rwkv_pre(rw_main, rw_tail, shift_main, shift_tail, prm, *, batch, seq):
    n, w3 = rw_main.shape
    w = w3 // 3
    tm = _row_tile(seq, ROW_BLOCK)
    nt = seq // tm
    row = lambda c: pl.BlockSpec((tm, c), lambda b, i: (b * nt + i, 0))
    per_b = lambda c: pl.BlockSpec((None, 1, c), lambda b, i: (b, 0, 0))
    full = lambda a: pl.BlockSpec(a.shape, lambda b, i: (0,) * a.ndim)
    consts = [prm["mu_main"], prm["mu_tail"], prm["w_lora"], prm["w0"], prm["a0"], prm["k_k"],
              prm["k_a"], prm["r_k"]]
    kern = functools.partial(_rwkv_pre_kernel, lora_w=prm["lora_w"], lora_a=prm["lora_a"])
    return pl.pallas_call(
        kern,
        grid=(batch, nt),
        in_specs=[row(w3), row(TAIL_PAD), per_b(w3), per_b(TAIL_PAD)] + [full(c) for c in consts],
        out_specs=[row(w)] * 8,
        out_shape=[jax.ShapeDtypeStruct((n, w), F32)] * 8,
        scratch_shapes=[pltpu.VMEM((1, w3), F32), pltpu.VMEM((1, TAIL_PAD), F32)],
        compiler_params=_cparams(("parallel", "arbitrary")),
        name="rwkv_pre",
    )(rw_main, rw_tail, shift_main, shift_tail, *consts)


def _bmm(a, b, kind, passes):
    contract = {"nn": ((2,), (1,)), "nt": ((2,), (2,)), "tn": ((1,), (1,))}[kind]
    dims = (contract, ((0,), (0,)))
    if passes == 6:
        return lax.dot_general(a, b, dims, precision=HIGHEST, preferred_element_type=F32)
    dg = lambda x, y: lax.dot_general(x, y, dims, preferred_element_type=F32)
    ah, bh = a.astype(BF16), b.astype(BF16)
    out = dg(ah, bh)
    if passes == 3:
        al = (a - ah.astype(F32)).astype(BF16)
        bl = (b - bh.astype(F32)).astype(BF16)
        out = out + dg(ah, bl) + dg(al, bh)
    return out


def _rwkv_chunk(r, lw, km, v, kn, bb, s_blk, passes):
    g, c, _ = r.shape
    c2 = HEADS_PER_LANE_TILE * c
    ti = lax.broadcasted_iota(jnp.int32, (g, c, c), 1)
    si = lax.broadcasted_iota(jnp.int32, (g, c, c), 2)
    cs = _bmm((si <= ti).astype(F32), lw, "nn", 6)
    e_pos = jnp.exp(cs)
    e_neg = jnp.exp(-cs)
    kt = kn * jnp.exp(cs - lw)
    bt = bb * e_neg
    kh = km * e_neg
    rt = r * e_pos
    g_end = e_pos[:, c - 1:c, :]

    lane = lax.broadcasted_iota(jnp.int32, (1, 1, LANES), 2)
    head_of_lane = lane // HEAD_DIM

    def stack_masked(x):
        return jnp.concatenate(
            [jnp.where(head_of_lane == hh, x, 0.0) for hh in range(HEADS_PER_LANE_TILE)], axis=1)

    def stack(x):
        return jnp.concatenate([x] * HEADS_PER_LANE_TILE, axis=1)

    def pick(x):
        out = x[:, :c]
        for hh in range(1, HEADS_PER_LANE_TILE):
            out = jnp.where(head_of_lane == hh, x[:, hh * c:(hh + 1) * c], out)
        return out

    kt2 = stack_masked(kt)
    rt2 = stack_masked(rt)
    rr = lax.broadcasted_iota(jnp.int32, (1, c2, c2), 1)
    cc = lax.broadcasted_iota(jnp.int32, (1, c2, c2), 2)
    strict_blk = (rr // c == cc // c) & (cc < rr)
    x = jnp.where(strict_blk, -_bmm(kt2, stack(bt), "nt", passes), 0.0)
    tinv = (rr == cc).astype(F32) + x
    steps = max(int(math.ceil(math.log2(c))) - 1, 0)
    for _ in range(steps):
        x = _bmm(x, x, "nn", passes)
        tinv = tinv + _bmm(tinv, x, "nn", passes)
    tr = lax.broadcasted_iota(jnp.int32, (1, c2, c), 1) % c
    sr = lax.broadcasted_iota(jnp.int32, (1, c2, c), 2)
    kk_s = jnp.where(sr < tr, _bmm(kt2, kh, "nt", passes), 0.0)
    rb_s = jnp.where(sr <= tr, _bmm(rt2, bt, "nt", passes), 0.0)
    rk_s = jnp.where(sr <= tr, _bmm(rt2, kh, "nt", passes), 0.0)

    ks = _bmm(jnp.concatenate([kt, rt], axis=1), s_blk, "nn", passes)
    rhs = ks[:, :c] + pick(_bmm(kk_s, v, "nn", passes))
    z = pick(_bmm(tinv, stack(rhs), "nn", passes))
    y = ks[:, c:] - pick(_bmm(rb_s, z, "nn", passes)) + pick(_bmm(rk_s, v, "nn", passes))
    jr = lax.broadcasted_iota(jnp.int32, (1, LANES, LANES), 1)
    ic = lax.broadcasted_iota(jnp.int32, (1, LANES, LANES), 2)
    decay_rows = jnp.swapaxes(jnp.broadcast_to(g_end, (g, LANES, LANES)), 1, 2)
    upd = _bmm(jnp.concatenate([bt * g_end, kh * g_end], axis=1),
               jnp.concatenate([-z, v], axis=1), "tn", passes)
    s_new = decay_rows * s_blk + jnp.where(jr // HEAD_DIM == ic // HEAD_DIM, upd, 0.0)
    return y, s_new


def _rwkv_scan_kernel(r_ref, lw_ref, km_ref, v_ref, kn_ref, b_ref, s0_ref, y_ref, sT_ref, s_sc,
                      *, chunk, passes):
    it = pl.program_id(0)
    nb, tb, w = r_ref.shape
    npair = w // LANES

    @pl.when(it == 0)
    def _():
        s_sc[...] = s0_ref[...]

    def step(ci, carry):
        rows = pl.ds(pl.multiple_of(ci * chunk, chunk), chunk)

        def gather(ref):
            blk = ref[:, rows, :]
            return jnp.concatenate([blk[:, :, p * LANES:(p + 1) * LANES] for p in range(npair)],
                                   axis=0)

        y, s_new = _rwkv_chunk(gather(r_ref), gather(lw_ref), gather(km_ref), gather(v_ref),
                               gather(kn_ref), gather(b_ref), s_sc[...], passes)
        for p in range(npair):
            y_ref[:, rows, p * LANES:(p + 1) * LANES] = y[p * nb:(p + 1) * nb]
        s_sc[...] = s_new
        return carry

    lax.fori_loop(0, tb // chunk, step, 0)

    @pl.when(it == pl.num_programs(0) - 1)
    def _():
        sT_ref[...] = s_sc[...]


def _rwkv_scan(r, lw, km, v, kn, bb, s0_blk, *, batch, seq, passes=RWKV_PASSES):
    n, w = r.shape
    npair = w // LANES
    chunk = min(CHUNK, seq)
    tb = _row_tile(seq, 4 * chunk)
    row = pl.BlockSpec((batch, tb, w), lambda i: (0, i, 0))
    st = pl.BlockSpec((npair * batch, LANES, LANES), lambda i: (0, 0, 0))
    s0 = jnp.swapaxes(s0_blk, 0, 1).reshape(npair * batch, LANES, LANES)
    kern = functools.partial(_rwkv_scan_kernel, chunk=chunk, passes=passes)
    y, s_t = pl.pallas_call(
        kern,
        grid=(seq // tb,),
        in_specs=[row] * 6 + [st],
        out_specs=[row, st],
        out_shape=[jax.ShapeDtypeStruct((batch, seq, w), F32),
                   jax.ShapeDtypeStruct((npair * batch, LANES, LANES), F32)],
        scratch_shapes=[pltpu.VMEM((npair * batch, LANES, LANES), F32)],
        compiler_params=_cparams(("arbitrary",)),
        name="rwkv_scan",
    )(*(a.reshape(batch, seq, w) for a in (r, lw, km, v, kn, bb)), s0)
    s_t = jnp.swapaxes(s_t.reshape(npair, batch, LANES, LANES), 0, 1)
    return y.reshape(n, w), s_t


def _outproj_kernel(x_ref, fox_ref, y_ref, bonus_ref, g_ref, lnw_ref, lnb_ref, wa_ref, wb_ref,
                    gf_ref, wqt_ref, keys_ref, x2_ref, xn_ref, sc_ref):
    y = y_ref[...]
    w = y.shape[1]
    e = _head_sum_matrix(w)
    mean = _dot3(y, e) * (1.0 / HEAD_DIM)
    d = y - mean
    var = _dot3(d * d, e) * (1.0 / HEAD_DIM)
    yn = d * lax.rsqrt(var + LNX_EPS) * lnw_ref[...] + lnb_ref[...]
    rw = (yn + bonus_ref[...]) * g_ref[...]
    mix = (jnp.dot(fox_ref[...].astype(BF16), wa_ref[...], preferred_element_type=F32)
           + jnp.dot(rw.astype(BF16), wb_ref[...], preferred_element_type=F32))
    x2 = x_ref[...] + mix
    xn = x2 * lax.rsqrt(jnp.mean(x2 * x2, axis=-1, keepdims=True) + NORM_EPS) * gf_ref[...]
    x2_ref[...] = x2.reshape(x2_ref.shape)
    xn_ref[...] = xn.reshape(xn_ref.shape)
    qt = lax.dot_general(wqt_ref[...], xn.astype(BF16), (((1,), (1,)), ((), ())),
                         preferred_element_type=F32)
    qh = keys_ref.shape[2]
    for hc in range(keys_ref.shape[0]):
        sc_ref[hc] = jnp.dot(keys_ref[hc], qt[hc * qh:(hc + 1) * qh, :].astype(BF16),
                             preferred_element_type=F32)


def _outproj(x2d, fox, y, bonus, g, lnx_w, lnx_b, w_out, g_ffn, w_q, sub_keys):
    n, d = x2d.shape
    w = y.shape[1]
    fw = fox.shape[1]
    wa = w_out[:fw].astype(BF16)
    wb = w_out[fw:].astype(BF16)
    wqt = w_q.T.astype(BF16)
    nkeys, qh = sub_keys.shape[-2:]
    keys = sub_keys.reshape(-1, nkeys, qh).astype(BF16)
    nhc = keys.shape[0]
    tm = _row_tile(n, ROW_BLOCK)
    row = lambda c: pl.BlockSpec((tm, c), lambda i: (i, 0))
    tiles = pl.BlockSpec((tm * d // LANES, LANES), lambda i: (i, 0))
    full = lambda a: pl.BlockSpec(a.shape, lambda i: (0,) * a.ndim)
    consts = [lnx_w.reshape(1, w), lnx_b.reshape(1, w), wa, wb, g_ffn.reshape(1, d), wqt, keys]
    return pl.pallas_call(
        _outproj_kernel,
        grid=(n // tm,),
        in_specs=[row(d), row(fw), row(w), row(w), row(w)] + [full(c) for c in consts],
        out_specs=[tiles, tiles, pl.BlockSpec((nhc, nkeys, tm), lambda i: (0, 0, i))],
        out_shape=[jax.ShapeDtypeStruct((n * d // LANES, LANES), F32),
                   jax.ShapeDtypeStruct((n * d // LANES, LANES), F32),
                   jax.ShapeDtypeStruct((nhc, nkeys, n), F32)],
        compiler_params=_cparams(("parallel",)),
        name="outproj_scores",
    )(x2d, fox, y, bonus, g, *consts)


def _topk_rows(s, payload, k):
    rows = lax.broadcasted_iota(jnp.int32, s.shape, 0)
    nrow = s.shape[0]
    vals, idxs, pays = [], [], []
    for _ in range(k):
        m = jnp.max(s, axis=0, keepdims=True)
        idx = jnp.min(jnp.where(s == m, rows, nrow), axis=0, keepdims=True)
        hit = rows == idx
        vals.append(m)
        idxs.append(idx)
        if payload is not None:
            pays.append(jnp.max(jnp.where(hit, payload, -1), axis=0, keepdims=True))
        s = jnp.where(hit, -jnp.inf, s)
    return vals, idxs, pays


def _retrieve_kernel(sc_ref, idx_ref, gate_ref, *, topk, nkeys):
    nhead = sc_ref.shape[0] // 2
    idx_rows, gate_rows = [], []
    for h in range(nhead):
        v1, i1, _ = _topk_rows(sc_ref[2 * h], None, topk)
        v2, i2, _ = _topk_rows(sc_ref[2 * h + 1], None, topk)
        v2a = jnp.concatenate(v2, axis=0)
        i2a = jnp.concatenate(i2, axis=0)
        nb = [topk // (a + 1) for a in range(topk)]
        pad = -sum(nb) % 8
        cand = jnp.concatenate([v1[a] + v2a[:nb[a]] for a in range(topk)]
                               + [jnp.full((pad, v2a.shape[1]), -jnp.inf, F32)], axis=0)
        cidx = jnp.concatenate([i1[a] * nkeys + i2a[:nb[a]] for a in range(topk)]
                               + [jnp.full((pad, v2a.shape[1]), -1, jnp.int32)], axis=0)
        top, _, eidx = _topk_rows(cand, cidx, topk)
        top = jnp.concatenate(top, axis=0)
        ex = jnp.exp(top - top[0:1])
        gate_rows.append(ex / jnp.sum(ex, axis=0, keepdims=True))
        idx_rows.extend(eidx)
    idx_ref[...] = (jnp.concatenate(idx_rows, axis=0) * HALF_TILE).T
    gate_ref[...] = jnp.concatenate(gate_rows, axis=0).T


def _retrieve(scores, topk):
    nhc, nkeys, n = scores.shape
    slots = (nhc // 2) * topk
    tt = _row_tile(n, RETRIEVE_BLOCK)
    kern = functools.partial(_retrieve_kernel, topk=topk, nkeys=nkeys)
    return pl.pallas_call(
        kern,
        grid=(n // tt,),
        in_specs=[pl.BlockSpec((nhc, nkeys, tt), lambda i: (0, 0, i))],
        out_specs=[pl.BlockSpec((tt, slots), lambda i: (i, 0))] * 2,
        out_shape=[jax.ShapeDtypeStruct((n, slots), jnp.int32),
                   jax.ShapeDtypeStruct((n, slots), F32)],
        compiler_params=_cparams(("parallel",)),
        name="peer_retrieve",
    )(scores)


ROW_TILE = 8
HALF_TILE = ROW_TILE // 2
ACT_TOKENS_PER_STEP = 1
MIX_TOKENS_PER_STEP = 2


def _pack_table(t):
    e, d = t.shape
    assert d == ROW_TILE * LANES
    bits = lax.bitcast_convert_type(t.astype(BF16), jnp.uint16).astype(jnp.uint32)
    bits = bits.reshape(e, 2, HALF_TILE, LANES)
    word = bits[:, 0] | (bits[:, 1] << 16)
    return lax.bitcast_convert_type(word, jnp.int32).reshape(e * HALF_TILE, LANES)


def _gather_rows(tab_ref, off_ref, base, stack_ref, slots, place=lambda j: j):
    tok = off_ref.at[pl.ds(base, slots)]
    for j in range(slots):
        off = pl.multiple_of(tok[j], HALF_TILE)
        p = place(j)
        stack_ref[p * HALF_TILE:(p + 1) * HALF_TILE, :] = tab_ref[pl.ds(off, HALF_TILE), :]


def _fold_pairs(xs, span):
    pos = lax.broadcasted_iota(jnp.int32, (ROW_TILE, LANES), 0)
    keep = (pos % (2 * span)) < span
    out = []
    for a, b in zip(xs[0::2], xs[1::2]):
        other = jnp.where(keep, b, a)
        swapped = jnp.where(keep, pltpu.roll(other, ROW_TILE - span, 0), pltpu.roll(other, span, 0))
        out.append(jnp.where(keep, a, b) + swapped)
    return out


_FOLD_ROW_OF_GROUP = (0, 4, 2, 6, 1, 5, 3, 7)


def _lane_sums_as_row(q):
    ones = jnp.ones((ROW_TILE, LANES), BF16)
    nt = lambda b: lax.dot_general(ones, b, (((1,), (1,)), ((), ())), preferred_element_type=F32)
    hi = q.astype(BF16)
    mid = (q - hi.astype(F32)).astype(BF16)
    return nt(hi) + nt(mid)


def _dot3(x, w01):
    hi = x.astype(BF16)
    r1 = x - hi.astype(F32)
    mid = r1.astype(BF16)
    lo = (r1 - mid.astype(F32)).astype(BF16)
    d = lambda a: jnp.dot(a, w01, preferred_element_type=F32)
    return d(hi) + d(mid) + d(lo)


def _peer_act_kernel(off_ref, x_ref, tab_ref, gate_ref, w_ref, stacks, part_ref, act_ref):
    tb, slots = gate_ref.shape
    place = lambda j: ROW_TILE * (j // ROW_TILE) + _FOLD_ROW_OF_GROUP[j % ROW_TILE]

    def one(t, stack_ref):
        _gather_rows(tab_ref, off_ref, t * slots, stack_ref, slots, place)
        x = x_ref[pl.ds(pl.multiple_of(t * ROW_TILE, ROW_TILE), ROW_TILE), :]
        x_lo = jnp.concatenate([x[:HALF_TILE]] * 2, axis=0)
        x_hi = jnp.concatenate([x[HALF_TILE:]] * 2, axis=0)
        for g in range(slots // ROW_TILE):
            prods = []
            for k in range(4 * g, 4 * g + 4):
                word = stack_ref[k * ROW_TILE:(k + 1) * ROW_TILE, :]
                prods.append(lax.bitcast_convert_type(jnp.left_shift(word, 16), F32) * x_lo
                             + lax.bitcast_convert_type(word & jnp.int32(-65536), F32) * x_hi)
            part_ref[t, g * ROW_TILE:(g + 1) * ROW_TILE, :] = _fold_pairs(_fold_pairs(prods, 2), 1)[0]

    def step(i, carry):
        for u in range(ACT_TOKENS_PER_STEP):
            one(ACT_TOKENS_PER_STEP * i + u, stacks.at[u])
        return carry

    lax.fori_loop(0, tb // ACT_TOKENS_PER_STEP, step, 0)

    for c in range(tb // ROW_TILE):
        rows = part_ref[c * ROW_TILE:(c + 1) * ROW_TILE].reshape(ROW_TILE * slots, LANES)
        sums = _lane_sums_as_row(rows)
        for u in range(ROW_TILE):
            act_ref[c * ROW_TILE + u:c * ROW_TILE + u + 1, :] = sums[:1, u * slots:(u + 1) * slots]
    act = act_ref[...]
    gelu = 0.5 * act * (1.0 + lax.erf(act * math.sqrt(0.5)))
    w_ref[...] = gate_ref[...] * gelu


def _peer_act(off, x8, tab, gate, *, tb):
    n, slots = gate.shape
    stack = pltpu.VMEM((ACT_TOKENS_PER_STEP, slots * HALF_TILE, LANES), jnp.int32)
    return pl.pallas_call(
        _peer_act_kernel,
        grid=(n // tb,),
        in_specs=[pl.BlockSpec((tb * slots,), lambda i: (i,), memory_space=pltpu.SMEM),
                  pl.BlockSpec((tb * ROW_TILE, LANES), lambda i: (i, 0)),
                  pl.BlockSpec(memory_space=pltpu.VMEM),
                  pl.BlockSpec((tb, slots), lambda i: (i, 0))],
        out_specs=pl.BlockSpec((tb, slots), lambda i: (i, 0)),
        out_shape=jax.ShapeDtypeStruct((n, slots), F32),
        scratch_shapes=[stack, pltpu.VMEM((tb, slots, LANES), F32),
                        pltpu.VMEM((tb, slots), F32)],
        compiler_params=_cparams(("arbitrary",)),
        name="peer_expert_act",
    )(off, x8, tab, gate)


def _peer_mix_kernel(off_ref, w_ref, tab_ref, x2_ref, gfin_ref, o_ref, stacks, wbc_all, x3_ref,
                     *, final_norm):
    tb, slots = w_ref.shape
    nacc = 4
    group = 16
    wt = w_ref[...].T
    w_hi = wt.astype(BF16)
    w_mid = (wt - w_hi.astype(F32)).astype(BF16)
    row_tok = lax.broadcasted_iota(jnp.int32, (tb, group * LANES), 0)
    col_tok = lax.broadcasted_iota(jnp.int32, (tb, group * LANES), 1) // LANES

    def broadcast_group(c, carry):
        onehot = (row_tok == col_tok + c * group).astype(BF16)
        res = sum(jnp.dot(term, onehot, preferred_element_type=F32) for term in (w_hi, w_mid))
        for tt in range(group):
            wbc_all[c * group + tt] = res[:, tt * LANES:(tt + 1) * LANES]
        return carry

    lax.fori_loop(0, tb // group, broadcast_group, 0)
    sub = lax.broadcasted_iota(jnp.int32, (ROW_TILE, LANES), 0)

    def one(t, stack_ref):
        _gather_rows(tab_ref, off_ref, t * slots, stack_ref, slots)
        wbc_ref = wbc_all.at[t]
        lo = [jnp.zeros((ROW_TILE, LANES), F32) for _ in range(nacc)]
        hi_acc = [jnp.zeros((ROW_TILE, LANES), F32) for _ in range(nacc)]
        for k in range(slots // 2):
            word = stack_ref[k * ROW_TILE:(k + 1) * ROW_TILE, :]
            wv = jnp.where(sub < HALF_TILE,
                           jnp.broadcast_to(wbc_ref[2 * k:2 * k + 1, :], (ROW_TILE, LANES)),
                           jnp.broadcast_to(wbc_ref[2 * k + 1:2 * k + 2, :], (ROW_TILE, LANES)))
            a = k % nacc
            lo[a] = lo[a] + lax.bitcast_convert_type(jnp.left_shift(word, 16), F32) * wv
            hi_acc[a] = hi_acc[a] + lax.bitcast_convert_type(word & jnp.int32(-65536), F32) * wv
        lo_sum = (lo[0] + lo[1]) + (lo[2] + lo[3])
        hi_sum = (hi_acc[0] + hi_acc[1]) + (hi_acc[2] + hi_acc[3])
        ff = jnp.concatenate([lo_sum[:HALF_TILE] + lo_sum[HALF_TILE:],
                              hi_sum[:HALF_TILE] + hi_sum[HALF_TILE:]], axis=0)
        rows = pl.ds(pl.multiple_of(t * ROW_TILE, ROW_TILE), ROW_TILE)
        x3_ref[rows, :] = x2_ref[rows, :] + ff

    def step(i, carry):
        for u in range(MIX_TOKENS_PER_STEP):
            one(MIX_TOKENS_PER_STEP * i + u, stacks.at[u])
        return carry

    lax.fori_loop(0, tb // MIX_TOKENS_PER_STEP, step, 0)
    x3 = x3_ref[...].reshape(tb, ROW_TILE, LANES)
    if final_norm:
        sq = jnp.sum(jnp.sum(x3 * x3, axis=2, keepdims=True), axis=1, keepdims=True)
        scale = lax.rsqrt(sq * (1.0 / (ROW_TILE * LANES)) + NORM_EPS)
        x3 = x3 * scale * gfin_ref[...][None]
    o_ref[...] = x3.reshape(o_ref.shape)


def _peer_mix(off, wgt, tab, x8, g_final, *, tb, final_norm):
    n, slots = wgt.shape
    d = ROW_TILE * LANES
    g8 = g_final.reshape(ROW_TILE, LANES)
    kern = functools.partial(_peer_mix_kernel, final_norm=final_norm)
    stack = pltpu.VMEM((MIX_TOKENS_PER_STEP, slots * HALF_TILE, LANES), jnp.int32)
    return pl.pallas_call(
        kern,
        grid=(n // tb,),
        in_specs=[pl.BlockSpec((tb * slots,), lambda i: (i,), memory_space=pltpu.SMEM),
                  pl.BlockSpec((tb, slots), lambda i: (i, 0)),
                  pl.BlockSpec(memory_space=pltpu.VMEM),
                  pl.BlockSpec((tb * ROW_TILE, LANES), lambda i: (i, 0)),
                  pl.BlockSpec((ROW_TILE, LANES), lambda i: (0, 0))],
        out_specs=pl.BlockSpec((tb, d), lambda i: (i, 0)),
        out_shape=jax.ShapeDtypeStruct((n, d), F32),
        scratch_shapes=[stack, pltpu.VMEM((tb, slots, LANES), F32),
                        pltpu.VMEM((tb * ROW_TILE, LANES), F32)],
        compiler_params=_cparams(("arbitrary",)),
        name="peer_expert_mix",
    )(off, wgt, tab, x8, g8)


def _state_to_blocks(s):
    b, h, d, _ = s.shape
    st = jnp.swapaxes(s, -1, -2).reshape(b, h // HEADS_PER_LANE_TILE, HEADS_PER_LANE_TILE, d, d)
    eye = jnp.eye(HEADS_PER_LANE_TILE, dtype=s.dtype)
    blk = st[:, :, :, :, None, :] * eye[None, None, :, None, :, None]
    return blk.reshape(b, h // HEADS_PER_LANE_TILE, LANES, LANES)


def _blocks_to_state(blk, heads):
    b, npair = blk.shape[:2]
    x = blk.reshape(b, npair, HEADS_PER_LANE_TILE, HEAD_DIM, HEADS_PER_LANE_TILE, HEAD_DIM)
    diag = jnp.stack([x[:, :, hh, :, hh, :] for hh in range(HEADS_PER_LANE_TILE)], axis=2)
    return jnp.swapaxes(diag.reshape(b, heads, HEAD_DIM, HEAD_DIM), -1, -2)


PEER_TOPK = 16


def _layer(x, k_past, v_past, lf_past, s0, shift0, lp, g_final, final_norm):
    (norm_mix_g, w_in, fox_b_f, mu, w0, w2, a0, a2, g2, k_k, k_a, r_k, lnx_w, lnx_b, w_out,
     norm_ffn_g, peer_w_q, peer_sub_keys, tab_u, tab_v) = lp
    b, t, d = x.shape
    n = b * t
    fox_heads = fox_b_f.shape[0]
    fw = fox_heads * HEAD_DIM
    fox_cols = 3 * fw + fox_heads
    rwkv_heads = r_k.shape[0]
    w = rwkv_heads * HEAD_DIM
    x2d = x.reshape(n, d)
    qb, kt, vt, kb, vb, lf, rw_main, rw_tail = _inproj(x2d, norm_mix_g, w_in, fox_b_f, fox_cols,
                                                       fox_heads, 3 * w, batch=b, seq=t)
    k = jnp.transpose(kt.reshape(b, fox_heads, HEAD_DIM, t), (0, 3, 1, 2))
    v = jnp.transpose(vt.reshape(b, fox_heads, HEAD_DIM, t), (0, 3, 1, 2))
    fox = _fox_stream(qb, kb, vb, lf, k_past, v_past, lf_past, batch=b, q_len=t)

    prm = _rwkv_params(mu, w0, w2, a0, a2, g2, k_k, k_a, r_k.reshape(-1))
    tail = prm["tail"]
    shift_main = shift0[..., :3 * w]
    shift_tail = jnp.pad(shift0[..., 3 * w:], ((0, 0), (0, 0), (0, TAIL_PAD - tail)))
    r, lw, km, vv, kn, bb, g, bonus = _rwkv_pre(rw_main, rw_tail, shift_main, shift_tail, prm,
                                                batch=b, seq=t)
    y, s_blk = _rwkv_scan(r, lw, km, vv, kn, bb, _state_to_blocks(s0), batch=b, seq=t)
    s_t = _blocks_to_state(s_blk, rwkv_heads)
    last = jnp.concatenate([rw_main.reshape(b, t, -1)[:, -1:], rw_tail.reshape(b, t, -1)[:, -1:, :tail]],
                           axis=-1)

    x2, xn, scores = _outproj(x2d, fox, y, bonus, g, lnx_w, lnx_b, w_out, norm_ffn_g, peer_w_q,
                              peer_sub_keys)
    idx, gate = _retrieve(scores, PEER_TOPK)
    tb = _row_tile(n, PEER_BLOCK)
    slots = gate.shape[1]
    off = idx.reshape(n * slots)
    wgt = _peer_act(off, xn, tab_u, gate, tb=tb)
    out = _peer_mix(off, wgt, tab_v, x2, g_final, tb=tb, final_norm=final_norm)
    return (out.reshape(b, t, d), k, v, lf.reshape(b, t, fox_heads), s_t, last)


def kernel(x_prompt, x_sample, cache_fox_k, cache_fox_v, cache_fox_logf, state_rwkv, state_shift,
           norm_mix_g, w_in, fox_b_f, rwkv_mu, rwkv_w0, rwkv_w2, rwkv_a0, rwkv_a2, rwkv_g2,
           rwkv_k_k, rwkv_k_a, rwkv_r_k, rwkv_lnx_w, rwkv_lnx_b, w_out, norm_ffn_g,
           peer_w_q, peer_sub_keys, peer_u, peer_v, norm_final_g):
    depth = w_in.shape[0]
    yp, ys = x_prompt, x_sample
    bp = x_prompt.shape[0]
    dt = x_prompt.dtype
    fox_heads = fox_b_f.shape[1]
    rwkv_heads = rwkv_r_k.shape[1]
    rwkv_cols = rwkv_mu.shape[1]
    outs_p, outs_s = [], []
    for l in range(depth):
        lp = (norm_mix_g[l], w_in[l], fox_b_f[l], rwkv_mu[l], rwkv_w0[l], rwkv_w2[l], rwkv_a0[l],
              rwkv_a2[l], rwkv_g2[l], rwkv_k_k[l], rwkv_k_a[l], rwkv_r_k[l], rwkv_lnx_w[l],
              rwkv_lnx_b[l], w_out[l], norm_ffn_g[l], peer_w_q[l], peer_sub_keys[l],
              _pack_table(peer_u[l]), _pack_table(peer_v[l]))
        last = l == depth - 1
        empty_kv = jnp.zeros((bp, 0, fox_heads, HEAD_DIM), dt)
        empty_lf = jnp.zeros((bp, 0, fox_heads), dt)
        s_zero = jnp.zeros((bp, rwkv_heads, HEAD_DIM, HEAD_DIM), dt)
        sh_zero = jnp.zeros((bp, 1, rwkv_cols), dt)
        yp, *rest_p = _layer(yp, empty_kv, empty_kv, empty_lf, s_zero, sh_zero, lp, norm_final_g, last)
        ys, *rest_s = _layer(ys, cache_fox_k[l], cache_fox_v[l], cache_fox_logf[l], state_rwkv[l],
                             state_shift[l], lp, norm_final_g, last)
        outs_p.append(rest_p)
        outs_s.append(rest_s)
    stack = lambda outs, i: jnp.stack([o[i] for o in outs])
    return ((yp, ys) + tuple(stack(outs_p, i) for i in range(5))
            + tuple(stack(outs_s, i) for i in range(5)))
```

```python
import functools
import math

import jax
import jax.numpy as jnp
from jax import lax
from jax.experimental import pallas as pl
from jax.experimental.pallas import tpu as pltpu

F32 = jnp.float32
BF16 = jnp.bfloat16

HEAD_DIM = 64
LANES = 128
HEADS_PER_LANE_TILE = LANES // HEAD_DIM
TAIL_PAD = 2 * LANES
CHUNK = 64
RWKV_PASSES = 1
NORM_EPS = 1e-6
LNX_EPS = 64e-5
NEG_BIG = -1e30
LOG2E = math.log2(math.e)
HIGHEST = lax.Precision.HIGHEST
VMEM_LIMIT = 48 * 1024 * 1024
ROW_BLOCK = 512
FOX_BLOCK = 2048
RETRIEVE_BLOCK = 256
PEER_BLOCK = 128


def _cparams(sem):
    return pltpu.CompilerParams(dimension_semantics=sem, vmem_limit_bytes=VMEM_LIMIT)


def _row_tile(n, target):
    t = min(n, target)
    assert n % t == 0, (n, t)
    return t


def _inproj_kernel(x_ref, g_ref, wqkv_ref, wkvt_ref, wf_ref, wrw_ref, wtail_ref, bf_ref,
                   q_ref, kt_ref, vt_ref, kb_ref, vb_ref, lf_ref, rw_ref, tail_ref):
    x = x_ref[...]
    h = x * lax.rsqrt(jnp.mean(x * x, axis=-1, keepdims=True) + NORM_EPS) * g_ref[...]
    hb = h.astype(BF16)
    fw = wqkv_ref.shape[1] // 3
    qkv = jnp.dot(hb, wqkv_ref[...], preferred_element_type=F32)
    q_ref[...] = (qkv[:, :fw] * (LOG2E / math.sqrt(HEAD_DIM))).astype(BF16)
    kb_ref[...] = qkv[:, fw:2 * fw].astype(BF16)
    vb_ref[...] = qkv[:, 2 * fw:].astype(BF16)
    kvt = lax.dot_general(wkvt_ref[...], hb, (((1,), (1,)), ((), ())), preferred_element_type=F32)
    kt_ref[...] = kvt[:fw]
    vt_ref[...] = kvt[fw:]
    f = jnp.dot(hb, wf_ref[...], preferred_element_type=F32) + bf_ref[...]
    lf_ref[...] = jax.nn.log_sigmoid(f)
    rw_ref[...] = jnp.dot(hb, wrw_ref[...], preferred_element_type=F32)
    tail_ref[...] = jnp.dot(hb, wtail_ref[...], preferred_element_type=F32)


def _inproj(x2d, g, w_in, b_f, fox_cols, fox_heads, rw_main, *, batch, seq):
    n, d = x2d.shape
    fw = fox_heads * HEAD_DIM
    wqkv = w_in[:, :3 * fw].astype(BF16)
    wkvt = w_in[:, fw:3 * fw].T.astype(BF16)
    wf = w_in[:, 3 * fw:fox_cols].astype(BF16)
    wrw = w_in[:, fox_cols:fox_cols + rw_main].astype(BF16)
    wtail = w_in[:, fox_cols + rw_main:].astype(BF16)
    wtail = jnp.pad(wtail, ((0, 0), (0, TAIL_PAD - wtail.shape[1])))
    tm = _row_tile(seq, ROW_BLOCK)
    nt = seq // tm
    row = lambda c: pl.BlockSpec((tm, c), lambda b, i: (b * nt + i, 0))
    col = pl.BlockSpec((None, fw, tm), lambda b, i: (b, 0, i))
    full = lambda a: pl.BlockSpec(a.shape, lambda b, i: (0,) * a.ndim)
    g2 = g.reshape(1, d)
    bf2 = b_f.reshape(1, fox_heads)
    outs = (
        jax.ShapeDtypeStruct((n, fw), BF16),
        jax.ShapeDtypeStruct((batch, fw, seq), F32),
        jax.ShapeDtypeStruct((batch, fw, seq), F32),
        jax.ShapeDtypeStruct((n, fw), BF16),
        jax.ShapeDtypeStruct((n, fw), BF16),
        jax.ShapeDtypeStruct((n, fox_heads), F32),
        jax.ShapeDtypeStruct((n, rw_main), F32),
        jax.ShapeDtypeStruct((n, TAIL_PAD), F32),
    )
    return pl.pallas_call(
        _inproj_kernel,
        grid=(batch, nt),
        in_specs=[row(d), full(g2), full(wqkv), full(wkvt), full(wf), full(wrw), full(wtail),
                  full(bf2)],
        out_specs=[row(fw), col, col, row(fw), row(fw), row(fox_heads), row(rw_main),
                   row(TAIL_PAD)],
        out_shape=outs,
        compiler_params=_cparams(("parallel", "parallel")),
        name="inproj",
    )(x2d, g2, wqkv, wkvt, wf, wrw, wtail, bf2)


def _cumsum_kernel(lf_ref, c_ref, carry):
    @pl.when(pl.program_id(1) == 0)
    def _():
        carry[...] = jnp.zeros_like(carry)

    lf = lf_ref[...]
    tc = lf.shape[0]
    r = lax.broadcasted_iota(jnp.int32, (tc, tc), 0)
    c = lax.broadcasted_iota(jnp.int32, (tc, tc), 1)
    lower = (c <= r).astype(F32)
    cc = jnp.dot(lower, lf, precision=HIGHEST, preferred_element_type=F32) + carry[...]
    c_ref[...] = cc
    carry[...] = cc[tc - 1:tc, :]


def _cumsum(lf, tc):
    b, l, nh = lf.shape
    assert l % tc == 0
    return pl.pallas_call(
        _cumsum_kernel,
        grid=(b, l // tc),
        in_specs=[pl.BlockSpec((None, tc, nh), lambda i, j: (i, j, 0))],
        out_specs=pl.BlockSpec((None, tc, nh), lambda i, j: (i, j, 0)),
        out_shape=jax.ShapeDtypeStruct((b, l, nh), F32),
        scratch_shapes=[pltpu.VMEM((1, nh), F32)],
        compiler_params=_cparams(("parallel", "arbitrary")),
        name="cumsum_logf",
    )(lf)


def _split3(x):
    hi = x.astype(BF16)
    r = x - hi.astype(F32)
    mid = r.astype(BF16)
    lo = (r - mid.astype(F32)).astype(BF16)
    return hi.astype(F32), mid.astype(F32), lo.astype(F32)


def _augment_kernel(x_ref, c_ref, o_ref, *, role):
    tm = x_ref.shape[0]
    lane = lax.broadcasted_iota(jnp.int32, (tm, LANES), 1)
    for p in range(x_ref.shape[1] // LANES):
        xp = x_ref[:, p * LANES:(p + 1) * LANES].astype(F32)
        for hh in range(HEADS_PER_LANE_TILE):
            h = p * HEADS_PER_LANE_TILE + hh
            own = (lane >= hh * HEAD_DIM) & (lane < (hh + 1) * HEAD_DIM)
            e = (lane + (1 - hh) * HEAD_DIM) % LANES
            if role == "v":
                ext = jnp.where(e == 0, 1.0, 0.0)
            else:
                c = jnp.broadcast_to(c_ref[:, h:h + 1], (tm, LANES))
                hi, mid, lo = _split3(c * LOG2E)
                sgn = 1.0 if role == "q" else -1.0
                base = 0 if role == "q" else 3
                ext = jnp.where(e == base, sgn * hi,
                                jnp.where(e == base + 1, sgn * mid,
                                          jnp.where(e == base + 2, sgn * lo,
                                                    jnp.where(e < 6, 1.0, 0.0))))
            o_ref[:, h * LANES:(h + 1) * LANES] = jnp.where(own, xp, ext).astype(BF16)


def _augment(x, c, role):
    n, w = x.shape
    nh = w // HEAD_DIM
    tm = ROW_BLOCK if n % ROW_BLOCK == 0 else n
    kern = functools.partial(_augment_kernel, role=role)
    return pl.pallas_call(
        kern,
        grid=(n // tm,),
        in_specs=[pl.BlockSpec((tm, w), lambda i: (i, 0)),
                  pl.BlockSpec((tm, nh), lambda i: (i, 0))],
        out_specs=pl.BlockSpec((tm, nh * LANES), lambda i: (i, 0)),
        out_shape=jax.ShapeDtypeStruct((n, nh * LANES), BF16),
        compiler_params=_cparams(("parallel",)),
        name="fox_augment_" + role,
    )(x, c)


Q_SUB = 128
K_SUB = 256


def _fox_kernel(qblk_ref, kblk_ref, last_ref, q_ref, k_ref, v_ref, o_ref, m_sc, acc_sc,
                *, q_off, tq, tk):
    t = pl.program_id(2)
    i = qblk_ref[t]
    j = kblk_ref[t]
    qs_n, ks_n = min(Q_SUB, tq), min(K_SUB, tk)

    @pl.when(j == 0)
    def _():
        m_sc[...] = jnp.full_like(m_sc, NEG_BIG)
        acc_sc[...] = jnp.zeros_like(acc_sc)

    q_lo = q_off + i * tq
    k_lo = j * tk

    def body(masked):
        if masked:
            diff = (lax.broadcasted_iota(jnp.int32, (qs_n, LANES), 1)
                    - lax.broadcasted_iota(jnp.int32, (qs_n, LANES), 0))
        nqs = tq // qs_n
        qrow = [slice(qs * qs_n, (qs + 1) * qs_n) for qs in range(nqs)]
        m_run = [[m_sc[hh, qrow[qs], :] for qs in range(nqs)] for hh in range(HEADS_PER_LANE_TILE)]
        a_run = [[acc_sc[hh, qrow[qs], :] for qs in range(nqs)] for hh in range(HEADS_PER_LANE_TILE)]
        for hh in range(HEADS_PER_LANE_TILE):
            cols = slice(hh * LANES, (hh + 1) * LANES)
            for ks in range(tk // ks_n):
                krows = slice(ks * ks_n, (ks + 1) * ks_n)
                k_sub = k_ref[krows, cols]
                v_sub = v_ref[krows, cols]
                for qs in range(nqs):
                    qrows = qrow[qs]
                    s = lax.dot_general(q_ref[qrows, cols], k_sub, (((1,), (1,)), ((), ())),
                                        preferred_element_type=F32)
                    parts = [s[:, c * LANES:(c + 1) * LANES] for c in range(ks_n // LANES)]
                    if masked:
                        parts = [jnp.where(diff <= q_lo - k_lo + qs * qs_n - ks * ks_n - c * LANES,
                                           pc, NEG_BIG) for c, pc in enumerate(parts)]
                    mx = parts[0]
                    for pc in parts[1:]:
                        mx = jnp.maximum(mx, pc)
                    m_old = m_run[hh][qs]
                    m_new = jnp.maximum(m_old, jnp.max(mx, axis=-1, keepdims=True))
                    alpha = jnp.exp2(m_old - m_new)
                    pr = jnp.concatenate([jnp.exp2(pc - m_new).astype(BF16) for pc in parts], axis=1)
                    pv = jnp.dot(pr, v_sub, preferred_element_type=F32)
                    a_run[hh][qs] = alpha * a_run[hh][qs] + pv
                    m_run[hh][qs] = m_new
        for hh in range(HEADS_PER_LANE_TILE):
            for qs in range(nqs):
                m_sc[hh, qrow[qs], :] = m_run[hh][qs]
                acc_sc[hh, qrow[qs], :] = a_run[hh][qs]

    fully_visible = k_lo + tk - 1 <= q_lo

    @pl.when(fully_visible)
    def _():
        body(False)

    @pl.when(jnp.logical_not(fully_visible))
    def _():
        body(True)

    @pl.when(last_ref[t] == 1)
    def _():
        lane = lax.broadcasted_iota(jnp.int32, (1, LANES), 1)
        out = jnp.zeros((tq, LANES), F32)
        for hh in range(HEADS_PER_LANE_TILE):
            in_head = (lane >= hh * HEAD_DIM) & (lane < (hh + 1) * HEAD_DIM)
            acc = acc_sc[hh]
            ones_col = (1 - hh) * HEAD_DIM
            denom = jnp.broadcast_to(acc[:, ones_col:ones_col + 1], acc.shape)
            out = jnp.where(in_head, acc / denom, out)
        o_ref[...] = out


def _fox_attend(q_aug, k_aug, v_aug, *, batch, q_len, kv_len, q_off, tq, tk):
    n, wa = q_aug.shape
    pair_w = HEADS_PER_LANE_TILE * LANES
    npair = wa // pair_w
    nq, nk = q_len // tq, kv_len // tk
    assert q_len % tq == 0 and kv_len % tk == 0

    pairs = [(i, j) for i in range(nq) for j in range(min(nk, (q_off + (i + 1) * tq - 1) // tk + 1))]
    qblk = jnp.asarray([i for i, _ in pairs], jnp.int32)
    kblk = jnp.asarray([j for _, j in pairs], jnp.int32)
    last = jnp.asarray([int(t + 1 == len(pairs) or pairs[t + 1][0] != i)
                        for t, (i, _) in enumerate(pairs)], jnp.int32)
    q_map = lambda b, p, t, qb, kb, lt: (b * nq + qb[t], p)
    kv_map = lambda b, p, t, qb, kb, lt: (b * nk + kb[t], p)
    kern = functools.partial(_fox_kernel, q_off=q_off, tq=tq, tk=tk)
    return pl.pallas_call(
        kern,
        grid_spec=pltpu.PrefetchScalarGridSpec(
            num_scalar_prefetch=3,
            grid=(batch, npair, len(pairs)),
            in_specs=[pl.BlockSpec((tq, pair_w), q_map),
                      pl.BlockSpec((tk, pair_w), kv_map),
                      pl.BlockSpec((tk, pair_w), kv_map)],
            out_specs=pl.BlockSpec((tq, LANES), q_map),
            scratch_shapes=[pltpu.VMEM((HEADS_PER_LANE_TILE, tq, LANES), F32),
                            pltpu.VMEM((HEADS_PER_LANE_TILE, tq, LANES), F32)]),
        out_shape=jax.ShapeDtypeStruct((n, npair * LANES), F32),
        compiler_params=_cparams(("parallel", "parallel", "arbitrary")),
        name="fox_attention",
    )(qblk, kblk, last, q_aug, k_aug, v_aug)


def _fox_stream(qb, kb, vb, lf, k_past, v_past, lf_past, *, batch, q_len):
    n, w = qb.shape
    nh = lf.shape[1]
    past = k_past.shape[1]
    lf_new = lf.reshape(batch, q_len, nh)
    if past == 0:
        kv_len = q_len
        k_all, v_all, lf_all = kb, vb, lf_new
        tq = tk = _row_tile(q_len, FOX_BLOCK)
        tc = tk
    else:
        kv_len = -(-(past + q_len) // K_SUB) * K_SUB
        pad = kv_len - past - q_len

        def cat(old, new):
            old = old.reshape(batch, past, -1).astype(new.dtype)
            new = new.reshape(batch, q_len, -1)
            z = jnp.zeros((batch, pad, new.shape[-1]), new.dtype)
            return jnp.concatenate([old, new, z], axis=1)

        k_all = cat(k_past, kb).reshape(batch * kv_len, w)
        v_all = cat(v_past, vb).reshape(batch * kv_len, w)
        lf_all = cat(lf_past, lf_new)
        tc = max(t for t in range(LANES, 1024 + 1, LANES) if kv_len % t == 0)
        tq, tk = q_len, kv_len
    c = _cumsum(lf_all, tc)
    c_k = c.reshape(batch * kv_len, nh)
    c_q = c[:, past:past + q_len].reshape(n, nh)
    return _fox_attend(_augment(qb, c_q, "q"), _augment(k_all, c_k, "k"), _augment(v_all, c_k, "v"),
                       batch=batch, q_len=q_len, kv_len=kv_len, q_off=past, tq=tq, tk=tk)


def _head_sum_matrix(width):
    r = lax.broadcasted_iota(jnp.int32, (width, width), 0) // HEAD_DIM
    c = lax.broadcasted_iota(jnp.int32, (width, width), 1) // HEAD_DIM
    return (r == c).astype(BF16)


def _rwkv_pre_kernel(pm_ref, pt_ref, sm_ref, st_ref, mum_ref, mut_ref, wbig_ref, w0_ref, a0_ref,
                     kk_ref, ka_ref, rk_ref,
                     r_out, lw_out, km_out, v_out, kn_out, b_out, g_out, bonus_out,
                     carry_m, carry_t, *, lora_w, lora_a):
    @pl.when(pl.program_id(1) == 0)
    def _():
        carry_m[...] = sm_ref[...]
        carry_t[...] = st_ref[...]

    pm = pm_ref[...]
    pt = pt_ref[...]
    tm = pm.shape[0]
    w = pm.shape[1] // 3

    def shifted(p, carry):
        row = lax.broadcasted_iota(jnp.int32, p.shape, 0)
        return jnp.where(row == 0, carry[...], pltpu.roll(p, 1, 0))

    prev_m = shifted(pm, carry_m)
    prev_t = shifted(pt, carry_t)
    carry_m[...] = pm[tm - 1:tm, :]
    carry_t[...] = pt[tm - 1:tm, :]
    psm = pm + mum_ref[...] * (prev_m - pm)
    pst = pt + mut_ref[...] * (prev_t - pt)
    r = psm[:, :w]
    k = psm[:, w:2 * w]
    v = psm[:, 2 * w:]
    lane = lax.broadcasted_iota(jnp.int32, pst.shape, 1)
    z = jnp.where(lane < lora_w, jnp.tanh(pst),
                  jnp.where(lane < lora_w + lora_a, pst, jax.nn.sigmoid(pst)))
    lo = jnp.dot(z.astype(BF16), wbig_ref[...], preferred_element_type=F32)
    w_log = -jax.nn.softplus(-(w0_ref[...] + lo[:, :w])) - 0.5
    lw = -jnp.exp(w_log)
    a = jax.nn.sigmoid(a0_ref[...] + lo[:, w:2 * w])
    g = lo[:, 2 * w:]
    e = _head_sum_matrix(w)
    kk0 = k * kk_ref[...]
    n2 = _dot3(kk0 * kk0, e)
    kn = kk0 / jnp.maximum(jnp.sqrt(n2), 1e-12)
    km = k * (1.0 + (a - 1.0) * ka_ref[...])
    rk = _dot3(r * km * rk_ref[...], e)
    r_out[...] = r
    lw_out[...] = lw
    km_out[...] = km
    v_out[...] = v
    kn_out[...] = kn
    b_out[...] = kn * a
    g_out[...] = g
    bonus_out[...] = rk * v


def _rwkv_params(mu, w0, w2, a0, a2, g2, k_k, k_a, r_k):
    w = w0.shape[0]
    lora_w, lora_a, lora_g = w2.shape[0], a2.shape[0], g2.shape[0]
    w_lora = jnp.zeros((TAIL_PAD, 3 * w), F32)
    w_lora = w_lora.at[:lora_w, :w].set(w2)
    w_lora = w_lora.at[lora_w:lora_w + lora_a, w:2 * w].set(a2)
    w_lora = w_lora.at[lora_w + lora_a:lora_w + lora_a + lora_g, 2 * w:].set(g2)
    tail = mu.shape[0] - 3 * w
    return dict(
        mu_main=mu[:3 * w].reshape(1, 3 * w),
        mu_tail=jnp.pad(mu[3 * w:], (0, TAIL_PAD - tail)).reshape(1, TAIL_PAD),
        w_lora=w_lora.astype(BF16), w0=w0.reshape(1, w), a0=a0.reshape(1, w),
        k_k=k_k.reshape(1, w), k_a=k_a.reshape(1, w), r_k=r_k.reshape(1, w),
        lora_w=lora_w, lora_a=lora_a, tail=tail)


def _rwkv_pre(rw_main, rw_tail, shift_main, shift_tail, prm, *, batch, seq):
    n, w3 = rw_main.shape
    w = w3 // 3
    tm = _row_tile(seq, ROW_BLOCK)
    nt = seq // tm
    row = lambda c: pl.BlockSpec((tm, c), lambda b, i: (b * nt + i, 0))
    per_b = lambda c: pl.BlockSpec((None, 1, c), lambda b, i: (b, 0, 0))
    full = lambda a: pl.BlockSpec(a.shape, lambda b, i: (0,) * a.ndim)
    consts = [prm["mu_main"], prm["mu_tail"], prm["w_lora"], prm["w0"], prm["a0"], prm["k_k"],
              prm["k_a"], prm["r_k"]]
    kern = functools.partial(_rwkv_pre_kernel, lora_w=prm["lora_w"], lora_a=prm["lora_a"])
    return pl.pallas_call(
        kern,
        grid=(batch, nt),
        in_specs=[row(w3), row(TAIL_PAD), per_b(w3), per_b(TAIL_PAD)] + [full(c) for c in consts],
        out_specs=[row(w)] * 8,
        out_shape=[jax.ShapeDtypeStruct((n, w), F32)] * 8,
        scratch_shapes=[pltpu.VMEM((1, w3), F32), pltpu.VMEM((1, TAIL_PAD), F32)],
        compiler_params=_cparams(("parallel", "arbitrary")),
        name="rwkv_pre",
    )(rw_main, rw_tail, shift_main, shift_tail, *consts)


def _bmm(a, b, kind, passes):
    contract = {"nn": ((2,), (1,)), "nt": ((2,), (2,)), "tn": ((1,), (1,))}[kind]
    dims = (contract, ((0,), (0,)))
    if passes == 6:
        return lax.dot_general(a, b, dims, precision=HIGHEST, preferred_element_type=F32)
    dg = lambda x, y: lax.dot_general(x, y, dims, preferred_element_type=F32)
    ah, bh = a.astype(BF16), b.astype(BF16)
    out = dg(ah, bh)
    if passes == 3:
        al = (a - ah.astype(F32)).astype(BF16)
        bl = (b - bh.astype(F32)).astype(BF16)
        out = out + dg(ah, bl) + dg(al, bh)
    return out


def _rwkv_chunk(r, lw, km, v, kn, bb, s_blk, passes):
    g, c, _ = r.shape
    c2 = HEADS_PER_LANE_TILE * c
    ti = lax.broadcasted_iota(jnp.int32, (g, c, c), 1)
    si = lax.broadcasted_iota(jnp.int32, (g, c, c), 2)
    cs = _bmm((si <= ti).astype(F32), lw, "nn", 6)
    e_pos = jnp.exp(cs)
    e_neg = jnp.exp(-cs)
    kt = kn * jnp.exp(cs - lw)
    bt = bb * e_neg
    kh = km * e_neg
    rt = r * e_pos
    g_end = e_pos[:, c - 1:c, :]

    lane = lax.broadcasted_iota(jnp.int32, (1, 1, LANES), 2)
    head_of_lane = lane // HEAD_DIM

    def stack_masked(x):
        return jnp.concatenate(
            [jnp.where(head_of_lane == hh, x, 0.0) for hh in range(HEADS_PER_LANE_TILE)], axis=1)

    def stack(x):
        return jnp.concatenate([x] * HEADS_PER_LANE_TILE, axis=1)

    def pick(x):
        out = x[:, :c]
        for hh in range(1, HEADS_PER_LANE_TILE):
            out = jnp.where(head_of_lane == hh, x[:, hh * c:(hh + 1) * c], out)
        return out

    kt2 = stack_masked(kt)
    rt2 = stack_masked(rt)
    rr = lax.broadcasted_iota(jnp.int32, (1, c2, c2), 1)
    cc = lax.broadcasted_iota(jnp.int32, (1, c2, c2), 2)
    strict_blk = (rr // c == cc // c) & (cc < rr)
    x = jnp.where(strict_blk, -_bmm(kt2, stack(bt), "nt", passes), 0.0)
    tinv = (rr == cc).astype(F32) + x
    steps = max(int(math.ceil(math.log2(c))) - 1, 0)
    for _ in range(steps):
        x = _bmm(x, x, "nn", passes)
        tinv = tinv + _bmm(tinv, x, "nn", passes)
    tr = lax.broadcasted_iota(jnp.int32, (1, c2, c), 1) % c
    sr = lax.broadcasted_iota(jnp.int32, (1, c2, c), 2)
    kk_s = jnp.where(sr < tr, _bmm(kt2, kh, "nt", passes), 0.0)
    rb_s = jnp.where(sr <= tr, _bmm(rt2, bt, "nt", passes), 0.0)
    rk_s = jnp.where(sr <= tr, _bmm(rt2, kh, "nt", passes), 0.0)

    ks = _bmm(jnp.concatenate([kt, rt], axis=1), s_blk, "nn", passes)
    rhs = ks[:, :c] + pick(_bmm(kk_s, v, "nn", passes))
    z = pick(_bmm(tinv, stack(rhs), "nn", passes))
    y = ks[:, c:] - pick(_bmm(rb_s, z, "nn", passes)) + pick(_bmm(rk_s, v, "nn", passes))
    jr = lax.broadcasted_iota(jnp.int32, (1, LANES, LANES), 1)
    ic = lax.broadcasted_iota(jnp.int32, (1, LANES, LANES), 2)
    decay_rows = jnp.swapaxes(jnp.broadcast_to(g_end, (g, LANES, LANES)), 1, 2)
    upd = _bmm(jnp.concatenate([bt * g_end, kh * g_end], axis=1),
               jnp.concatenate([-z, v], axis=1), "tn", passes)
    s_new = decay_rows * s_blk + jnp.where(jr // HEAD_DIM == ic // HEAD_DIM, upd, 0.0)
    return y, s_new


def _rwkv_scan_kernel(r_ref, lw_ref, km_ref, v_ref, kn_ref, b_ref, s0_ref, y_ref, sT_ref, s_sc,
                      *, chunk, passes):
    it = pl.program_id(0)
    nb, tb, w = r_ref.shape
    npair = w // LANES

    @pl.when(it == 0)
    def _():
        s_sc[...] = s0_ref[...]

    def step(ci, carry):
        rows = pl.ds(pl.multiple_of(ci * chunk, chunk), chunk)

        def gather(ref):
            blk = ref[:, rows, :]
            return jnp.concatenate([blk[:, :, p * LANES:(p + 1) * LANES] for p in range(npair)],
                                   axis=0)

        y, s_new = _rwkv_chunk(gather(r_ref), gather(lw_ref), gather(km_ref), gather(v_ref),
                               gather(kn_ref), gather(b_ref), s_sc[...], passes)
        for p in range(npair):
            y_ref[:, rows, p * LANES:(p + 1) * LANES] = y[p * nb:(p + 1) * nb]
        s_sc[...] = s_new
        return carry

    lax.fori_loop(0, tb // chunk, step, 0)

    @pl.when(it == pl.num_programs(0) - 1)
    def _():
        sT_ref[...] = s_sc[...]


def _rwkv_scan(r, lw, km, v, kn, bb, s0_blk, *, batch, seq, passes=RWKV_PASSES):
    n, w = r.shape
    npair = w // LANES
    chunk = min(CHUNK, seq)
    tb = _row_tile(seq, 4 * chunk)
    row = pl.BlockSpec((batch, tb, w), lambda i: (0, i, 0))
    st = pl.BlockSpec((npair * batch, LANES, LANES), lambda i: (0, 0, 0))
    s0 = jnp.swapaxes(s0_blk, 0, 1).reshape(npair * batch, LANES, LANES)
    kern = functools.partial(_rwkv_scan_kernel, chunk=chunk, passes=passes)
    y, s_t = pl.pallas_call(
        kern,
        grid=(seq // tb,),
        in_specs=[row] * 6 + [st],
        out_specs=[row, st],
        out_shape=[jax.ShapeDtypeStruct((batch, seq, w), F32),
                   jax.ShapeDtypeStruct((npair * batch, LANES, LANES), F32)],
        scratch_shapes=[pltpu.VMEM((npair * batch, LANES, LANES), F32)],
        compiler_params=_cparams(("arbitrary",)),
        name="rwkv_scan",
    )(*(a.reshape(batch, seq, w) for a in (r, lw, km, v, kn, bb)), s0)
    s_t = jnp.swapaxes(s_t.reshape(npair, batch, LANES, LANES), 0, 1)
    return y.reshape(n, w), s_t


def _outproj_kernel(x_ref, fox_ref, y_ref, bonus_ref, g_ref, lnw_ref, lnb_ref, wa_ref, wb_ref,
                    gf_ref, wqt_ref, keys_ref, x2_ref, xn_ref, sc_ref):
    y = y_ref[...]
    w = y.shape[1]
    e = _head_sum_matrix(w)
    mean = _dot3(y, e) * (1.0 / HEAD_DIM)
    d = y - mean
    var = _dot3(d * d, e) * (1.0 / HEAD_DIM)
    yn = d * lax.rsqrt(var + LNX_EPS) * lnw_ref[...] + lnb_ref[...]
    rw = (yn + bonus_ref[...]) * g_ref[...]
    mix = (jnp.dot(fox_ref[...].astype(BF16), wa_ref[...], preferred_element_type=F32)
           + jnp.dot(rw.astype(BF16), wb_ref[...], preferred_element_type=F32))
    x2 = x_ref[...] + mix
    xn = x2 * lax.rsqrt(jnp.mean(x2 * x2, axis=-1, keepdims=True) + NORM_EPS) * gf_ref[...]
    x2_ref[...] = x2.reshape(x2_ref.shape)
    xn_ref[...] = xn.reshape(xn_ref.shape)
    qt = lax.dot_general(wqt_ref[...], xn.astype(BF16), (((1,), (1,)), ((), ())),
                         preferred_element_type=F32)
    qh = keys_ref.shape[2]
    for hc in range(keys_ref.shape[0]):
        sc_ref[hc] = jnp.dot(keys_ref[hc], qt[hc * qh:(hc + 1) * qh, :].astype(BF16),
                             preferred_element_type=F32)


def _outproj(x2d, fox, y, bonus, g, lnx_w, lnx_b, w_out, g_ffn, w_q, sub_keys):
    n, d = x2d.shape
    w = y.shape[1]
    fw = fox.shape[1]
    wa = w_out[:fw].astype(BF16)
    wb = w_out[fw:].astype(BF16)
    wqt = w_q.T.astype(BF16)
    nkeys, qh = sub_keys.shape[-2:]
    keys = sub_keys.reshape(-1, nkeys, qh).astype(BF16)
    nhc = keys.shape[0]
    tm = _row_tile(n, ROW_BLOCK)
    row = lambda c: pl.BlockSpec((tm, c), lambda i: (i, 0))
    tiles = pl.BlockSpec((tm * d // LANES, LANES), lambda i: (i, 0))
    full = lambda a: pl.BlockSpec(a.shape, lambda i: (0,) * a.ndim)
    consts = [lnx_w.reshape(1, w), lnx_b.reshape(1, w), wa, wb, g_ffn.reshape(1, d), wqt, keys]
    return pl.pallas_call(
        _outproj_kernel,
        grid=(n // tm,),
        in_specs=[row(d), row(fw), row(w), row(w), row(w)] + [full(c) for c in consts],
        out_specs=[tiles, tiles, pl.BlockSpec((nhc, nkeys, tm), lambda i: (0, 0, i))],
        out_shape=[jax.ShapeDtypeStruct((n * d // LANES, LANES), F32),
                   jax.ShapeDtypeStruct((n * d // LANES, LANES), F32),
                   jax.ShapeDtypeStruct((nhc, nkeys, n), F32)],
        compiler_params=_cparams(("parallel",)),
        name="outproj_scores",
    )(x2d, fox, y, bonus, g, *consts)


def _topk_rows(s, payload, k):
    rows = lax.broadcasted_iota(jnp.int32, s.shape, 0)
    nrow = s.shape[0]
    vals, idxs, pays = [], [], []
    for _ in range(k):
        m = jnp.max(s, axis=0, keepdims=True)
        idx = jnp.min(jnp.where(s == m, rows, nrow), axis=0, keepdims=True)
        hit = rows == idx
        vals.append(m)
        idxs.append(idx)
        if payload is not None:
            pays.append(jnp.max(jnp.where(hit, payload, -1), axis=0, keepdims=True))
        s = jnp.where(hit, -jnp.inf, s)
    return vals, idxs, pays


def _retrieve_kernel(sc_ref, idx_ref, gate_ref, *, topk, nkeys):
    nhead = sc_ref.shape[0] // 2
    idx_rows, gate_rows = [], []
    for h in range(nhead):
        v1, i1, _ = _topk_rows(sc_ref[2 * h], None, topk)
        v2, i2, _ = _topk_rows(sc_ref[2 * h + 1], None, topk)
        v2a = jnp.concatenate(v2, axis=0)
        i2a = jnp.concatenate(i2, axis=0)
        nb = [topk // (a + 1) for a in range(topk)]
        pad = -sum(nb) % 8
        cand = jnp.concatenate([v1[a] + v2a[:nb[a]] for a in range(topk)]
                               + [jnp.full((pad, v2a.shape[1]), -jnp.inf, F32)], axis=0)
        cidx = jnp.concatenate([i1[a] * nkeys + i2a[:nb[a]] for a in range(topk)]
                               + [jnp.full((pad, v2a.shape[1]), -1, jnp.int32)], axis=0)
        top, _, eidx = _topk_rows(cand, cidx, topk)
        top = jnp.concatenate(top, axis=0)
        ex = jnp.exp(top - top[0:1])
        gate_rows.append(ex / jnp.sum(ex, axis=0, keepdims=True))
        idx_rows.extend(eidx)
    idx_ref[...] = (jnp.concatenate(idx_rows, axis=0) * HALF_TILE).T
    gate_ref[...] = jnp.concatenate(gate_rows, axis=0).T


def _retrieve(scores, topk):
    nhc, nkeys, n = scores.shape
    slots = (nhc // 2) * topk
    tt = _row_tile(n, RETRIEVE_BLOCK)
    kern = functools.partial(_retrieve_kernel, topk=topk, nkeys=nkeys)
    return pl.pallas_call(
        kern,
        grid=(n // tt,),
        in_specs=[pl.BlockSpec((nhc, nkeys, tt), lambda i: (0, 0, i))],
        out_specs=[pl.BlockSpec((tt, slots), lambda i: (i, 0))] * 2,
        out_shape=[jax.ShapeDtypeStruct((n, slots), jnp.int32),
                   jax.ShapeDtypeStruct((n, slots), F32)],
        compiler_params=_cparams(("parallel",)),
        name="peer_retrieve",
    )(scores)


ROW_TILE = 8
HALF_TILE = ROW_TILE // 2
ACT_TOKENS_PER_STEP = 1
MIX_TOKENS_PER_STEP = 2


def _pack_table(t):
    e, d = t.shape
    assert d == ROW_TILE * LANES
    bits = lax.bitcast_convert_type(t.astype(BF16), jnp.uint16).astype(jnp.uint32)
    bits = bits.reshape(e, 2, HALF_TILE, LANES)
    word = bits[:, 0] | (bits[:, 1] << 16)
    return lax.bitcast_convert_type(word, jnp.int32).reshape(e * HALF_TILE, LANES)


def _gather_rows(tab_ref, off_ref, base, stack_ref, slots, place=lambda j: j):
    tok = off_ref.at[pl.ds(base, slots)]
    for j in range(slots):
        off = pl.multiple_of(tok[j], HALF_TILE)
        p = place(j)
        stack_ref[p * HALF_TILE:(p + 1) * HALF_TILE, :] = tab_ref[pl.ds(off, HALF_TILE), :]


def _fold_pairs(xs, span):
    pos = lax.broadcasted_iota(jnp.int32, (ROW_TILE, LANES), 0)
    keep = (pos % (2 * span)) < span
    out = []
    for a, b in zip(xs[0::2], xs[1::2]):
        other = jnp.where(keep, b, a)
        swapped = jnp.where(keep, pltpu.roll(other, ROW_TILE - span, 0), pltpu.roll(other, span, 0))
        out.append(jnp.where(keep, a, b) + swapped)
    return out


_FOLD_ROW_OF_GROUP = (0, 4, 2, 6, 1, 5, 3, 7)


def _lane_sums_as_row(q):
    ones = jnp.ones((ROW_TILE, LANES), BF16)
    nt = lambda b: lax.dot_general(ones, b, (((1,), (1,)), ((), ())), preferred_element_type=F32)
    hi = q.astype(BF16)
    mid = (q - hi.astype(F32)).astype(BF16)
    return nt(hi) + nt(mid)


def _dot3(x, w01):
    hi = x.astype(BF16)
    r1 = x - hi.astype(F32)
    mid = r1.astype(BF16)
    lo = (r1 - mid.astype(F32)).astype(BF16)
    d = lambda a: jnp.dot(a, w01, preferred_element_type=F32)
    return d(hi) + d(mid) + d(lo)


def _peer_act_kernel(off_ref, x_ref, tab_ref, gate_ref, w_ref, stacks, part_ref, act_ref):
    tb, slots = gate_ref.shape
    place = lambda j: ROW_TILE * (j // ROW_TILE) + _FOLD_ROW_OF_GROUP[j % ROW_TILE]

    def one(t, stack_ref):
        _gather_rows(tab_ref, off_ref, t * slots, stack_ref, slots, place)
        x = x_ref[pl.ds(pl.multiple_of(t * ROW_TILE, ROW_TILE), ROW_TILE), :]
        x_lo = jnp.concatenate([x[:HALF_TILE]] * 2, axis=0)
        x_hi = jnp.concatenate([x[HALF_TILE:]] * 2, axis=0)
        for g in range(slots // ROW_TILE):
            prods = []
            for k in range(4 * g, 4 * g + 4):
                word = stack_ref[k * ROW_TILE:(k + 1) * ROW_TILE, :]
                prods.append(lax.bitcast_convert_type(jnp.left_shift(word, 16), F32) * x_lo
                             + lax.bitcast_convert_type(word & jnp.int32(-65536), F32) * x_hi)
            part_ref[t, g * ROW_TILE:(g + 1) * ROW_TILE, :] = _fold_pairs(_fold_pairs(prods, 2), 1)[0]

    def step(i, carry):
        for u in range(ACT_TOKENS_PER_STEP):
            one(ACT_TOKENS_PER_STEP * i + u, stacks.at[u])
        return carry

    lax.fori_loop(0, tb // ACT_TOKENS_PER_STEP, step, 0)

    for c in range(tb // ROW_TILE):
        rows = part_ref[c * ROW_TILE:(c + 1) * ROW_TILE].reshape(ROW_TILE * slots, LANES)
        sums = _lane_sums_as_row(rows)
        for u in range(ROW_TILE):
            act_ref[c * ROW_TILE + u:c * ROW_TILE + u + 1, :] = sums[:1, u * slots:(u + 1) * slots]
    act = act_ref[...]
    gelu = 0.5 * act * (1.0 + lax.erf(act * math.sqrt(0.5)))
    w_ref[...] = gate_ref[...] * gelu


def _peer_act(off, x8, tab, gate, *, tb):
    n, slots = gate.shape
    stack = pltpu.VMEM((ACT_TOKENS_PER_STEP, slots * HALF_TILE, LANES), jnp.int32)
    return pl.pallas_call(
        _peer_act_kernel,
        grid=(n // tb,),
        in_specs=[pl.BlockSpec((tb * slots,), lambda i: (i,), memory_space=pltpu.SMEM),
                  pl.BlockSpec((tb * ROW_TILE, LANES), lambda i: (i, 0)),
                  pl.BlockSpec(memory_space=pltpu.VMEM),
                  pl.BlockSpec((tb, slots), lambda i: (i, 0))],
        out_specs=pl.BlockSpec((tb, slots), lambda i: (i, 0)),
        out_shape=jax.ShapeDtypeStruct((n, slots), F32),
        scratch_shapes=[stack, pltpu.VMEM((tb, slots, LANES), F32),
                        pltpu.VMEM((tb, slots), F32)],
        compiler_params=_cparams(("arbitrary",)),
        name="peer_expert_act",
    )(off, x8, tab, gate)


def _peer_mix_kernel(off_ref, w_ref, tab_ref, x2_ref, gfin_ref, o_ref, stacks, wbc_all, x3_ref,
                     *, final_norm):
    tb, slots = w_ref.shape
    nacc = 4
    group = 16
    wt = w_ref[...].T
    w_hi = wt.astype(BF16)
    w_mid = (wt - w_hi.astype(F32)).astype(BF16)
    row_tok = lax.broadcasted_iota(jnp.int32, (tb, group * LANES), 0)
    col_tok = lax.broadcasted_iota(jnp.int32, (tb, group * LANES), 1) // LANES

    def broadcast_group(c, carry):
        onehot = (row_tok == col_tok + c * group).astype(BF16)
        res = sum(jnp.dot(term, onehot, preferred_element_type=F32) for term in (w_hi, w_mid))
        for tt in range(group):
            wbc_all[c * group + tt] = res[:, tt * LANES:(tt + 1) * LANES]
        return carry

    lax.fori_loop(0, tb // group, broadcast_group, 0)
    sub = lax.broadcasted_iota(jnp.int32, (ROW_TILE, LANES), 0)

    def one(t, stack_ref):
        _gather_rows(tab_ref, off_ref, t * slots, stack_ref, slots)
        wbc_ref = wbc_all.at[t]
        lo = [jnp.zeros((ROW_TILE, LANES), F32) for _ in range(nacc)]
        hi_acc = [jnp.zeros((ROW_TILE, LANES), F32) for _ in range(nacc)]
        for k in range(slots // 2):
            word = stack_ref[k * ROW_TILE:(k + 1) * ROW_TILE, :]
            wv = jnp.where(sub < HALF_TILE,
                           jnp.broadcast_to(wbc_ref[2 * k:2 * k + 1, :], (ROW_TILE, LANES)),
                           jnp.broadcast_to(wbc_ref[2 * k + 1:2 * k + 2, :], (ROW_TILE, LANES)))
            a = k % nacc
            lo[a] = lo[a] + lax.bitcast_convert_type(jnp.left_shift(word, 16), F32) * wv
            hi_acc[a] = hi_acc[a] + lax.bitcast_convert_type(word & jnp.int32(-65536), F32) * wv
        lo_sum = (lo[0] + lo[1]) + (lo[2] + lo[3])
        hi_sum = (hi_acc[0] + hi_acc[1]) + (hi_acc[2] + hi_acc[3])
        ff = jnp.concatenate([lo_sum[:HALF_TILE] + lo_sum[HALF_TILE:],
                              hi_sum[:HALF_TILE] + hi_sum[HALF_TILE:]], axis=0)
        rows = pl.ds(pl.multiple_of(t * ROW_TILE, ROW_TILE), ROW_TILE)
        x3_ref[rows, :] = x2_ref[rows, :] + ff

    def step(i, carry):
        for u in range(MIX_TOKENS_PER_STEP):
            one(MIX_TOKENS_PER_STEP * i + u, stacks.at[u])
        return carry

    lax.fori_loop(0, tb // MIX_TOKENS_PER_STEP, step, 0)
    x3 = x3_ref[...].reshape(tb, ROW_TILE, LANES)
    if final_norm:
        sq = jnp.sum(jnp.sum(x3 * x3, axis=2, keepdims=True), axis=1, keepdims=True)
        scale = lax.rsqrt(sq * (1.0 / (ROW_TILE * LANES)) + NORM_EPS)
        x3 = x3 * scale * gfin_ref[...][None]
    o_ref[...] = x3.reshape(o_ref.shape)


def _peer_mix(off, wgt, tab, x8, g_final, *, tb, final_norm):
    n, slots = wgt.shape
    d = ROW_TILE * LANES
    g8 = g_final.reshape(ROW_TILE, LANES)
    kern = functools.partial(_peer_mix_kernel, final_norm=final_norm)
    stack = pltpu.VMEM((MIX_TOKENS_PER_STEP, slots * HALF_TILE, LANES), jnp.int32)
    return pl.pallas_call(
        kern,
        grid=(n // tb,),
        in_specs=[pl.BlockSpec((tb * slots,), lambda i: (i,), memory_space=pltpu.SMEM),
                  pl.BlockSpec((tb, slots), lambda i: (i, 0)),
                  pl.BlockSpec(memory_space=pltpu.VMEM),
                  pl.BlockSpec((tb * ROW_TILE, LANES), lambda i: (i, 0)),
                  pl.BlockSpec((ROW_TILE, LANES), lambda i: (0, 0))],
        out_specs=pl.BlockSpec((tb, d), lambda i: (i, 0)),
        out_shape=jax.ShapeDtypeStruct((n, d), F32),
        scratch_shapes=[stack, pltpu.VMEM((tb, slots, LANES), F32),
                        pltpu.VMEM((tb * ROW_TILE, LANES), F32)],
        compiler_params=_cparams(("arbitrary",)),
        name="peer_expert_mix",
    )(off, wgt, tab, x8, g8)


def _state_to_blocks(s):
    b, h, d, _ = s.shape
    st = jnp.swapaxes(s, -1, -2).reshape(b, h // HEADS_PER_LANE_TILE, HEADS_PER_LANE_TILE, d, d)
    eye = jnp.eye(HEADS_PER_LANE_TILE, dtype=s.dtype)
    blk = st[:, :, :, :, None, :] * eye[None, None, :, None, :, None]
    return blk.reshape(b, h // HEADS_PER_LANE_TILE, LANES, LANES)


def _blocks_to_state(blk, heads):
    b, npair = blk.shape[:2]
    x = blk.reshape(b, npair, HEADS_PER_LANE_TILE, HEAD_DIM, HEADS_PER_LANE_TILE, HEAD_DIM)
    diag = jnp.stack([x[:, :, hh, :, hh, :] for hh in range(HEADS_PER_LANE_TILE)], axis=2)
    return jnp.swapaxes(diag.reshape(b, heads, HEAD_DIM, HEAD_DIM), -1, -2)


PEER_TOPK = 16


def _layer(x, k_past, v_past, lf_past, s0, shift0, lp, g_final, final_norm):
    (norm_mix_g, w_in, fox_b_f, mu, w0, w2, a0, a2, g2, k_k, k_a, r_k, lnx_w, lnx_b, w_out,
     norm_ffn_g, peer_w_q, peer_sub_keys, tab_u, tab_v) = lp
    b, t, d = x.shape
    n = b * t
    fox_heads = fox_b_f.shape[0]
    fw = fox_heads * HEAD_DIM
    fox_cols = 3 * fw + fox_heads
    rwkv_heads = r_k.shape[0]
    w = rwkv_heads * HEAD_DIM
    x2d = x.reshape(n, d)
    qb, kt, vt, kb, vb, lf, rw_main, rw_tail = _inproj(x2d, norm_mix_g, w_in, fox_b_f, fox_cols,
                                                       fox_heads, 3 * w, batch=b, seq=t)
    k = jnp.transpose(kt.reshape(b, fox_heads, HEAD_DIM, t), (0, 3, 1, 2))
    v = jnp.transpose(vt.reshape(b, fox_heads, HEAD_DIM, t), (0, 3, 1, 2))
    fox = _fox_stream(qb, kb, vb, lf, k_past, v_past, lf_past, batch=b, q_len=t)

    prm = _rwkv_params(mu, w0, w2, a0, a2, g2, k_k, k_a, r_k.reshape(-1))
    tail = prm["tail"]
    shift_main = shift0[..., :3 * w]
    shift_tail = jnp.pad(shift0[..., 3 * w:], ((0, 0), (0, 0), (0, TAIL_PAD - tail)))
    r, lw, km, vv, kn, bb, g, bonus = _rwkv_pre(rw_main, rw_tail, shift_main, shift_tail, prm,
                                                batch=b, seq=t)
    y, s_blk = _rwkv_scan(r, lw, km, vv, kn, bb, _state_to_blocks(s0), batch=b, seq=t)
    s_t = _blocks_to_state(s_blk, rwkv_heads)
    last = jnp.concatenate([rw_main.reshape(b, t, -1)[:, -1:], rw_tail.reshape(b, t, -1)[:, -1:, :tail]],
                           axis=-1)

    x2, xn, scores = _outproj(x2d, fox, y, bonus, g, lnx_w, lnx_b, w_out, norm_ffn_g, peer_w_q,
                              peer_sub_keys)
    idx, gate = _retrieve(scores, PEER_TOPK)
    tb = _row_tile(n, PEER_BLOCK)
    slots = gate.shape[1]
    off = idx.reshape(n * slots)
    wgt = _peer_act(off, xn, tab_u, gate, tb=tb)
    out = _peer_mix(off, wgt, tab_v, x2, g_final, tb=tb, final_norm=final_norm)
    return (out.reshape(b, t, d), k, v, lf.reshape(b, t, fox_heads), s_t, last)


def kernel(x_prompt, x_sample, cache_fox_k, cache_fox_v, cache_fox_logf, state_rwkv, state_shift,
           norm_mix_g, w_in, fox_b_f, rwkv_mu, rwkv_w0, rwkv_w2, rwkv_a0, rwkv_a2, rwkv_g2,
           rwkv_k_k, rwkv_k_a, rwkv_r_k, rwkv_lnx_w, rwkv_lnx_b, w_out, norm_ffn_g,
           peer_w_q, peer_sub_keys, peer_u, peer_v, norm_final_g):
    depth = w_in.shape[0]
    yp, ys = x_prompt, x_sample
    bp = x_prompt.shape[0]
    dt = x_prompt.dtype
    fox_heads = fox_b_f.shape[1]
    rwkv_heads = rwkv_r_k.shape[1]
    rwkv_cols = rwkv_mu.shape[1]
    outs_p, outs_s = [], []
    for l in range(depth):
        lp = (norm_mix_g[l], w_in[l], fox_b_f[l], rwkv_mu[l], rwkv_w0[l], rwkv_w2[l], rwkv_a0[l],
              rwkv_a2[l], rwkv_g2[l], rwkv_k_k[l], rwkv_k_a[l], rwkv_r_k[l], rwkv_lnx_w[l],
              rwkv_lnx_b[l], w_out[l], norm_ffn_g[l], peer_w_q[l], peer_sub_keys[l],
              _pack_table(peer_u[l]), _pack_table(peer_v[l]))
        last = l == depth - 1
        empty_kv = jnp.zeros((bp, 0, fox_heads, HEAD_DIM), dt)
        empty_lf = jnp.zeros((bp, 0, fox_heads), dt)
        s_zero = jnp.zeros((bp, rwkv_heads, HEAD_DIM, HEAD_DIM), dt)
        sh_zero = jnp.zeros((bp, 1, rwkv_cols), dt)
        yp, *rest_p = _layer(yp, empty_kv, empty_kv, empty_lf, s_zero, sh_zero, lp, norm_final_g, last)
        ys, *rest_s = _layer(ys, cache_fox_k[l], cache_fox_v[l], cache_fox_logf[l], state_rwkv[l],
                             state_shift[l], lp, norm_final_g, last)
        outs_p.append(rest_p)
        outs_s.append(rest_s)
    stack = lambda outs, i: jnp.stack([o[i] for o in outs])
    return ((yp, ys) + tuple(stack(outs_p, i) for i in range(5))
            + tuple(stack(outs_s, i) for i in range(5)))
```
